```python
import jax, jax.numpy as jnp
from jax import lax
import numpy as np

D_MODEL = 1024
BATCH = 2
SEQ = 16384
DEPTH = 2

EPS = 1e-6
ROPE_THETA = 500000.0
POOL_GROUPS = 4
POOL_WINDOWS = (2, 4, 8, 16)
POOL_GROUP_DIM = D_MODEL // 8
POOL_WIDTH = POOL_GROUPS * POOL_GROUP_DIM
NSA_HEADS = D_MODEL // 128
NSA_KV_GROUPS = 2
NSA_HEAD_DIM = 64
NSA_ROT_DIM = NSA_HEAD_DIM // 4
NSA_WIDTH = NSA_HEADS * NSA_HEAD_DIM
NSA_KV_WIDTH = NSA_KV_GROUPS * NSA_HEAD_DIM
CMP_BLOCK = 32
CMP_STRIDE = 16
SEL_BLOCK = 64
SEL_TOPK = 16
WINDOW = 512
Q_BLOCK = 128
GLA_HEADS = 4
GLA_DK = D_MODEL // 16
GLA_DV = D_MODEL // 8
GLA_GATE_RANK = 16
GLA_TAU = 16.0
GLA_CHUNK = 64
GLA_WIDTH = GLA_HEADS * GLA_DV
N_BRANCHES = 3
BRANCH_WIDTH = D_MODEL // 2
D_FF = 4 * D_MODEL
PLE_DIM = 256
IN_SIZES = (POOL_WIDTH, NSA_WIDTH, 6 * NSA_KV_WIDTH, 3 * NSA_HEADS, GLA_HEADS * GLA_DK, GLA_HEADS * GLA_DK, GLA_WIDTH, GLA_GATE_RANK, GLA_WIDTH, N_BRANCHES * D_MODEL)
IN_WIDTH = sum(IN_SIZES)

kernel_name = "hybrid_pool_nsa_gla_gated_merge"


def rmsnorm(x, g):
    xf = x.astype(jnp.float32)
    y = xf * lax.rsqrt(jnp.mean(xf * xf, axis=-1, keepdims=True) + EPS)
    return (y * g.astype(jnp.float32)).astype(x.dtype)


def partial_rope(x, positions):
    half = NSA_ROT_DIM // 2
    inv_freq = jnp.power(ROPE_THETA, -jnp.arange(half, dtype=jnp.float32) * (2.0 / NSA_ROT_DIM))
    ang = positions.astype(jnp.float32)[..., None] * inv_freq
    cos = jnp.cos(ang)[:, :, None, :]
    sin = jnp.sin(ang)[:, :, None, :]
    xr = x[..., :NSA_ROT_DIM].astype(jnp.float32)
    x1, x2 = xr[..., :half], xr[..., half:]
    rot = jnp.concatenate([x1 * cos - x2 * sin, x2 * cos + x1 * sin], axis=-1).astype(x.dtype)
    return jnp.concatenate([rot, x[..., NSA_ROT_DIM:]], axis=-1)


def masked_softmax(s, mask):
    s = jnp.where(mask, s.astype(jnp.float32), -jnp.inf)
    m = jnp.max(s, axis=-1, keepdims=True)
    m = jnp.where(jnp.isfinite(m), m, 0.0)
    e = jnp.exp(s - m)
    return e / jnp.maximum(jnp.sum(e, axis=-1, keepdims=True), 1e-30)


def pool_mixer(u, w_pool, scale):
    B, S, _ = u.shape
    ug = u.reshape(B, S, POOL_GROUPS, POOL_GROUP_DIM).astype(jnp.float32)
    cs = jnp.concatenate([jnp.zeros_like(ug[:, :1]), jnp.cumsum(ug, axis=1)], axis=1)
    t = jnp.arange(S)
    win = jnp.array(POOL_WINDOWS, dtype=jnp.int32)
    start = jnp.maximum(t[:, None] + 1 - win[None, :], 0)
    cnt = (t[:, None] + 1 - start).astype(jnp.float32)
    g_idx = jnp.arange(POOL_GROUPS)[None, :]
    lower = cs[:, start, g_idx, :]
    pooled = (cs[:, 1:] - lower) / cnt[None, :, :, None] - ug
    y = jnp.einsum('bsgc,gcd->bsgd', pooled, w_pool.astype(jnp.float32))
    y = y * scale.astype(jnp.float32).reshape(POOL_GROUPS, POOL_GROUP_DIM)
    return y.reshape(B, S, POOL_WIDTH).astype(u.dtype)


def nsa_mixer(q, kv, gate_logits, positions, cmp_pos_k, cmp_w_k, cmp_pos_v, cmp_w_v):
    B, S, _ = q.shape
    H, G, Dh = NSA_HEADS, NSA_KV_GROUPS, NSA_HEAD_DIM
    HG = H // G
    dt = q.dtype
    q = partial_rope(q.reshape(B, S, H, Dh), positions) * (Dh ** -0.5)
    kv = kv.reshape(B, S, 6, G, Dh)
    kc = partial_rope(kv[:, :, 0], positions)
    vc = kv[:, :, 1]
    ks = partial_rope(kv[:, :, 2], positions)
    vs = kv[:, :, 3]
    kw = partial_rope(kv[:, :, 4], positions)
    vw = kv[:, :, 5]

    n_cmp = (S - CMP_BLOCK) // CMP_STRIDE + 1
    blk_start = jnp.arange(n_cmp) * CMP_STRIDE
    idx = blk_start[:, None] + jnp.arange(CMP_BLOCK)[None, :]

    def compress(t, pos, w):
        blk = t[:, idx] + pos[None, None, :, None, :]
        blk = jnp.moveaxis(blk, 3, 2).reshape(B, n_cmp, G, CMP_BLOCK * Dh)
        return jnp.einsum('bngf,fd->bngd', blk, w)

    kc_cmp = compress(kc, cmp_pos_k, cmp_w_k)
    vc_cmp = compress(vc, cmp_pos_v, cmp_w_v)
    cmp_end = blk_start + CMP_BLOCK - 1

    n_sel = S // SEL_BLOCK
    sel_start = jnp.arange(n_sel) * SEL_BLOCK
    overlap = ((blk_start[:, None] < sel_start[None, :] + SEL_BLOCK) & (blk_start[:, None] + CMP_BLOCK > sel_start[None, :])).astype(jnp.float32)
    top_k = min(SEL_TOPK, n_sel)

    ks_g = ks.transpose(0, 2, 1, 3)
    vs_g = vs.transpose(0, 2, 1, 3)
    kw_pad = jnp.pad(kw, ((0, 0), (WINDOW, 0), (0, 0), (0, 0)))
    vw_pad = jnp.pad(vw, ((0, 0), (WINDOW, 0), (0, 0), (0, 0)))
    gates = jax.nn.sigmoid(gate_logits.astype(jnp.float32)).astype(dt).reshape(B, S, H, 3)
    j_sel = jnp.arange(n_sel)
    n_keys = top_k * SEL_BLOCK

    def query_block(qb):
        t0 = qb * Q_BLOCK
        qblk = lax.dynamic_slice_in_dim(q, t0, Q_BLOCK, axis=1).reshape(B, Q_BLOCK, G, HG, Dh)
        tq = t0 + jnp.arange(Q_BLOCK)
        s = jnp.einsum('bqghd,bngd->bghqn', qblk, kc_cmp)
        p_cmp = masked_softmax(s, cmp_end[None, :] <= tq[:, None])
        o_cmp = jnp.einsum('bghqn,bngd->bqghd', p_cmp.astype(dt), vc_cmp)
        imp = jnp.einsum('bghqn,nj->bgqj', p_cmp, overlap)
        cur = tq // SEL_BLOCK
        valid = j_sel[None, :] <= cur[:, None]
        forced = (j_sel[None, :] == 0) | (j_sel[None, :] == cur[:, None]) | (j_sel[None, :] == cur[:, None] - 1)
        score = jnp.where(forced, jnp.inf, jnp.where(valid, imp, -jnp.inf))
        _, sel = lax.top_k(score, top_k)
        tok = (sel[..., None] * SEL_BLOCK + jnp.arange(SEL_BLOCK)).reshape(B, G, Q_BLOCK * n_keys)
        k_sel = jnp.take_along_axis(ks_g, tok[..., None], axis=2).reshape(B, G, Q_BLOCK, n_keys, Dh)
        v_sel = jnp.take_along_axis(vs_g, tok[..., None], axis=2).reshape(B, G, Q_BLOCK, n_keys, Dh)
        tok_r = tok.reshape(B, G, Q_BLOCK, n_keys)
        s = jnp.einsum('bqghd,bgqkd->bghqk', qblk, k_sel)
        p_slc = masked_softmax(s, (tok_r <= tq[:, None])[:, :, None])
        o_slc = jnp.einsum('bghqk,bgqkd->bqghd', p_slc.astype(dt), v_sel)
        kwb = lax.dynamic_slice_in_dim(kw_pad, t0, Q_BLOCK + WINDOW, axis=1)
        vwb = lax.dynamic_slice_in_dim(vw_pad, t0, Q_BLOCK + WINDOW, axis=1)
        kpos = t0 - WINDOW + jnp.arange(Q_BLOCK + WINDOW)
        diff = tq[:, None] - kpos[None, :]
        wmask = (diff >= 0) & (diff < WINDOW) & (kpos[None, :] >= 0)
        s = jnp.einsum('bqghd,bkgd->bghqk', qblk, kwb)
        o_win = jnp.einsum('bghqk,bkgd->bqghd', masked_softmax(s, wmask).astype(dt), vwb)
        g = lax.dynamic_slice_in_dim(gates, t0, Q_BLOCK, axis=1).reshape(B, Q_BLOCK, G, HG, 3)
        o = g[..., 0:1] * o_cmp + g[..., 1:2] * o_slc + g[..., 2:3] * o_win
        return o.reshape(B, Q_BLOCK, H * Dh)

    out = lax.map(query_block, jnp.arange(S // Q_BLOCK))
    return out.transpose(1, 0, 2, 3).reshape(B, S, NSA_WIDTH)


def gla_mixer(q, k, v, gate_low, r, w_gate2, b_gate, norm_g):
    B, S, _ = q.shape
    H, Dk, Dv, C = GLA_HEADS, GLA_DK, GLA_DV, GLA_CHUNK
    nC = S // C
    f32 = jnp.float32
    z = jnp.einsum('bsr,rd->bsd', gate_low, w_gate2) + b_gate
    log_a = (jax.nn.log_sigmoid(z.astype(f32)) / GLA_TAU).reshape(B, nC, C, H, Dk)
    bcum = jnp.cumsum(log_a, axis=2)
    b_last = bcum[:, :, -1]
    qf = q.astype(f32).reshape(B, nC, C, H, Dk) * (Dk ** -0.5)
    kf = k.astype(f32).reshape(B, nC, C, H, Dk)
    vf = v.astype(f32).reshape(B, nC, C, H, Dv)
    q_s = qf * jnp.exp(bcum)
    k_s = kf * jnp.exp(-bcum)
    k_t = kf * jnp.exp(b_last[:, :, None] - bcum)
    causal = jnp.tril(jnp.ones((C, C), dtype=bool))
    A = jnp.where(causal, jnp.einsum('bnihd,bnjhd->bnhij', q_s, k_s), 0.0)
    o_intra = jnp.einsum('bnhij,bnjhv->bnihv', A, vf)

    def step(state, xs):
        qn, kn, vn, bl = xs
        o = jnp.einsum('bihd,bhdv->bihv', qn, state)
        state = state * jnp.exp(bl)[..., None] + jnp.einsum('bjhd,bjhv->bhdv', kn, vn)
        return state, o

    xs = (jnp.moveaxis(q_s, 1, 0), jnp.moveaxis(k_t, 1, 0), jnp.moveaxis(vf, 1, 0), jnp.moveaxis(b_last, 1, 0))
    _, o_inter = lax.scan(step, jnp.zeros((B, H, Dk, Dv), f32), xs)
    o = (o_intra + jnp.moveaxis(o_inter, 0, 1)).reshape(B, S, H, Dv)
    o = rmsnorm(o, norm_g).reshape(B, S, GLA_WIDTH)
    return (o * jax.nn.silu(r.astype(f32))).astype(q.dtype)


def setup_inputs(seed: int = 0) -> dict:
    key = jax.random.key(seed)
    ks = jax.random.split(key, 24)
    f32 = jnp.float32

    def nrm(k, shape, scale):
        return jax.random.normal(k, shape, f32) * scale

    def gain(k, shape):
        return 1.0 + 0.02 * jax.random.normal(k, shape, f32)

    return {
        "x": nrm(ks[0], (BATCH, SEQ, D_MODEL), 1.0),
        "p": nrm(ks[1], (DEPTH, BATCH, SEQ, PLE_DIM), 1.0),
        "positions": jnp.broadcast_to(jnp.arange(SEQ, dtype=jnp.int32), (BATCH, SEQ)),
        "norm_mix": gain(ks[2], (DEPTH, D_MODEL)),
        "w_in": nrm(ks[3], (DEPTH, D_MODEL, IN_WIDTH), D_MODEL ** -0.5),
        "pool_w": nrm(ks[4], (DEPTH, POOL_GROUPS, POOL_GROUP_DIM, POOL_GROUP_DIM), POOL_GROUP_DIM ** -0.5),
        "pool_scale": gain(ks[5], (DEPTH, POOL_WIDTH)),
        "cmp_pos_k": nrm(ks[6], (DEPTH, CMP_BLOCK, NSA_HEAD_DIM), 0.1),
        "cmp_w_k": nrm(ks[7], (DEPTH, CMP_BLOCK * NSA_HEAD_DIM, NSA_HEAD_DIM), (CMP_BLOCK * NSA_HEAD_DIM) ** -0.5),
        "cmp_pos_v": nrm(ks[8], (DEPTH, CMP_BLOCK, NSA_HEAD_DIM), 0.1),
        "cmp_w_v": nrm(ks[9], (DEPTH, CMP_BLOCK * NSA_HEAD_DIM, NSA_HEAD_DIM), (CMP_BLOCK * NSA_HEAD_DIM) ** -0.5),
        "gla_w_gate": nrm(ks[10], (DEPTH, GLA_GATE_RANK, GLA_HEADS * GLA_DK), GLA_GATE_RANK ** -0.5),
        "gla_b_gate": nrm(ks[11], (DEPTH, GLA_HEADS * GLA_DK), 0.1),
        "gla_norm": gain(ks[12], (DEPTH, GLA_DV)),
        "w_branch": nrm(ks[13], (DEPTH, N_BRANCHES, BRANCH_WIDTH, D_MODEL), BRANCH_WIDTH ** -0.5),
        "w_out": nrm(ks[14], (DEPTH, D_MODEL, D_MODEL), D_MODEL ** -0.5),
        "norm_ffn": gain(ks[15], (DEPTH, D_MODEL)),
        "w_ff1": nrm(ks[16], (DEPTH, D_MODEL, D_FF), D_MODEL ** -0.5),
        "w_ff2": nrm(ks[17], (DEPTH, D_FF, D_MODEL), 0.5 * D_FF ** -0.5),
        "norm_ple": gain(ks[18], (DEPTH, D_MODEL)),
        "w_ple_gate": nrm(ks[19], (DEPTH, D_MODEL, D_MODEL), D_MODEL ** -0.5),
        "w_ple_proj": nrm(ks[20], (DEPTH, PLE_DIM, D_MODEL), PLE_DIM ** -0.5),
        "norm_final": gain(ks[21], (D_MODEL,)),
    }


def reference(x, p, positions, norm_mix, w_in, pool_w, pool_scale, cmp_pos_k, cmp_w_k, cmp_pos_v, cmp_w_v, gla_w_gate, gla_b_gate, gla_norm, w_branch, w_out, norm_ffn, w_ff1, w_ff2, norm_ple, w_ple_gate, w_ple_proj, norm_final):
    B, S, D = x.shape
    h = x
    for i in range(DEPTH):
        a = rmsnorm(h, norm_mix[i])
        proj = jnp.einsum('bsd,df->bsf', a, w_in[i])
        parts = []
        off = 0
        for sz in IN_SIZES:
            parts.append(proj[..., off:off + sz])
            off += sz
        u_pool, q_nsa, kv_nsa, g_nsa, q_gla, k_gla, v_gla, a_gla, r_gla, g_merge = parts

        y_a = pool_mixer(u_pool, pool_w[i], pool_scale[i])
        y_b = nsa_mixer(q_nsa, kv_nsa, g_nsa, positions, cmp_pos_k[i], cmp_w_k[i], cmp_pos_v[i], cmp_w_v[i])
        y_c = gla_mixer(q_gla, k_gla, v_gla, a_gla, r_gla, gla_w_gate[i], gla_b_gate[i], gla_norm[i])

        gm = jax.nn.sigmoid(g_merge.astype(jnp.float32)).astype(x.dtype).reshape(B, S, N_BRANCHES, D)
        merged = gm[:, :, 0] * jnp.einsum('bsw,wd->bsd', y_a, w_branch[i, 0])
        merged = merged + gm[:, :, 1] * jnp.einsum('bsw,wd->bsd', y_b, w_branch[i, 1])
        merged = merged + gm[:, :, 2] * jnp.einsum('bsw,wd->bsd', y_c, w_branch[i, 2])
        h = h + jnp.einsum('bsd,de->bse', merged, w_out[i])

        f = rmsnorm(h, norm_ffn[i])
        f = jnp.square(jax.nn.relu(jnp.einsum('bsd,df->bsf', f, w_ff1[i])))
        h = h + jnp.einsum('bsf,fd->bsd', f, w_ff2[i])

        gate = jax.nn.sigmoid(jnp.einsum('bsd,de->bse', rmsnorm(h, norm_ple[i]), w_ple_gate[i]).astype(jnp.float32)).astype(x.dtype)
        h = h + gate * jnp.einsum('bsk,kd->bsd', p[i], w_ple_proj[i])
    return rmsnorm(h, norm_final)
```

```python
import functools

import jax
import jax.numpy as jnp
from jax import lax
from jax.experimental import pallas as pl
from jax.experimental.pallas import tpu as pltpu

F32, BF16 = jnp.float32, jnp.bfloat16

EPS = 1e-6
ROPE_THETA = 500000.0
POOL_WINDOWS = (2, 4, 8, 16)
POOL_GROUP_DIM = 128
NSA_HEADS = 8
NSA_KV_GROUPS = 2
NSA_HEADS_PER_GROUP = NSA_HEADS // NSA_KV_GROUPS
NSA_HEAD_DIM = 64
NSA_ROT_DIM = 16
CMP_BLOCK = 32
CMP_STRIDE = 16
SEL_BLOCK = 64
SEL_TOPK = 16
WINDOW = 512
Q_BLOCK = 128
GLA_HEADS = 4
GLA_DK = 64
GLA_DV = 128
GLA_GATE_RANK = 16
GLA_TAU = 16.0
GLA_CHUNK = 64
IN_SIZES = (512, 512, 768, 24, 256, 256, 512, 16, 512, 3072)

LANES = 128
MASKED = -1e30
VMEM_LIMIT = 56 * 1024 * 1024

MAIN_U, MAIN_QG, MAIN_KG, MAIN_VG, MAIN_RG, MAIN_GM, MAIN_A, MAIN_WIDTH = 0, 512, 768, 1024, 1536, 2048, 5120, 5376
NSA_Q, NSA_KC, NSA_KS, NSA_KW, NSA_VC, NSA_VS, NSA_VW, NSA_G, NSA_WIDTH = 0, 512, 640, 768, 896, 1024, 1152, 1280, 1536


def _nt(a, b):
    return lax.dot_general(a, b, (((1,), (1,)), ((), ())), preferred_element_type=F32)


def _tn(a, b):
    return lax.dot_general(a, b, (((0,), (0,)), ((), ())), preferred_element_type=F32)


def _dot(a, b):
    return jnp.dot(a, b, preferred_element_type=F32)


def _rms(x, gain):
    return x * lax.rsqrt(jnp.mean(x * x, axis=-1, keepdims=True) + EPS) * gain


def _proj_body(h_ref, g_ref, w_ref, o_ref, a_scr):
    @pl.when(pl.program_id(1) == 0)
    def _():
        a_scr[...] = _rms(h_ref[...], g_ref[...]).astype(BF16)

    o_ref[...] = _dot(a_scr[...], w_ref[...]).astype(o_ref.dtype)


def _norm_proj(h, gain, w, out_dtype, tm=512, tn=768):
    T, D = h.shape
    N = w.shape[1]
    return pl.pallas_call(
        _proj_body,
        grid=(T // tm, N // tn),
        in_specs=[
            pl.BlockSpec((tm, D), lambda i, j: (i, 0)),
            pl.BlockSpec((1, D), lambda i, j: (0, 0)),
            pl.BlockSpec((D, tn), lambda i, j: (0, j)),
        ],
        out_specs=pl.BlockSpec((tm, tn), lambda i, j: (i, j)),
        out_shape=jax.ShapeDtypeStruct((T, N), out_dtype),
        scratch_shapes=[pltpu.VMEM((tm, D), BF16)],
        compiler_params=pltpu.CompilerParams(
            dimension_semantics=("parallel", "arbitrary"), vmem_limit_bytes=VMEM_LIMIT),
        name="norm_proj",
    )(h, gain, w)


def _pool_body(u_ref, halo_ref, w_ref, sc_ref, o_ref, *, ts):
    i = pl.program_id(1)
    cur = u_ref[...].astype(F32)
    halo = jnp.where(i == 0, 0.0, halo_ref[...].astype(F32))
    t = i * ts + lax.broadcasted_iota(jnp.int32, (ts, 1), 0)
    for g, win in enumerate(POOL_WINDOWS):
        lo, hi = g * POOL_GROUP_DIM, (g + 1) * POOL_GROUP_DIM
        x = cur[:, lo:hi]
        e = jnp.concatenate([halo[:, lo:hi], x], axis=0)
        step = 1
        while step < win:
            e = e[step:] + e[:-step]
            step *= 2
        wsum = e[16 - (win - 1):]
        cnt = jnp.minimum(t + 1, win).astype(F32)
        pooled = wsum / cnt - x
        y = _dot(pooled.astype(BF16), w_ref[g]) * sc_ref[:, lo:hi]
        o_ref[:, lo:hi] = y.astype(o_ref.dtype)


def _pool_mixer(main, pool_w, scale, B, S, ts=512):
    ts = min(ts, S)
    halo_blocks = ts // 16
    return pl.pallas_call(
        functools.partial(_pool_body, ts=ts),
        grid=(B, S // ts),
        in_specs=[
            pl.BlockSpec((None, ts, 512), lambda b, i: (b, i, MAIN_U // 512)),
            pl.BlockSpec((None, 16, 512), lambda b, i: (b, jnp.maximum(i * halo_blocks - 1, 0), MAIN_U // 512)),
            pl.BlockSpec((4, 128, 128), lambda b, i: (0, 0, 0)),
            pl.BlockSpec((1, 512), lambda b, i: (0, 0)),
        ],
        out_specs=pl.BlockSpec((None, ts, 512), lambda b, i: (b, i, 0)),
        out_shape=jax.ShapeDtypeStruct((B, S, 512), BF16),
        compiler_params=pltpu.CompilerParams(dimension_semantics=("parallel", "arbitrary")),
        name="pool_mixer",
    )(main, main, pool_w, scale)


def _gla_body(q_ref, k_ref, v_ref, r_ref, a_ref, wg_ref, bg_ref, ng_ref, o_ref, st_ref, *, tc):
    @pl.when(pl.program_id(1) == 0)
    def _():
        st_ref[...] = jnp.zeros_like(st_ref)

    C = GLA_CHUNK
    causal = lax.broadcasted_iota(jnp.int32, (C, C), 0) >= lax.broadcasted_iota(jnp.int32, (C, C), 1)
    width = GLA_HEADS * GLA_DK
    for c in range(tc // C):
        rows = slice(c * C, (c + 1) * C)
        z = _dot(a_ref[rows, :], wg_ref[...]) + bg_ref[...]
        b = jax.nn.log_sigmoid(z) * (1.0 / GLA_TAU)
        step = 1
        while step < C:
            b = b + jnp.concatenate([jnp.zeros((step, width), F32), b[:-step]], axis=0)
            step *= 2
        b_last = b[C - 1:C, :]
        qf = q_ref[rows, :].astype(F32) * (GLA_DK ** -0.5)
        kf = k_ref[rows, :].astype(F32)
        q_s = (qf * jnp.exp(b)).astype(BF16)
        k_s = (kf * jnp.exp(-b)).astype(BF16)
        k_t = (kf * jnp.exp(b_last - b)).astype(BF16)
        decay = jnp.exp(b_last)
        for h in range(GLA_HEADS):
            hk = slice(h * GLA_DK, (h + 1) * GLA_DK)
            hv = slice(h * GLA_DV, (h + 1) * GLA_DV)
            vh = v_ref[rows, hv]
            state_t = st_ref[h]
            a = jnp.where(causal, _nt(q_s[:, hk], k_s[:, hk]), 0.0)
            o = _dot(a.astype(BF16), vh) + _nt(q_s[:, hk], state_t.astype(BF16))
            st_ref[h] = state_t * decay[:, hk] + _tn(vh, k_t[:, hk])
            o = _rms(o, ng_ref[...])
            r = r_ref[rows, hv].astype(F32)
            o_ref[rows, hv] = (o * (r * jax.nn.sigmoid(r))).astype(o_ref.dtype)


def _gla_mixer(main, w_gate, b_gate, norm_g, B, S, tc=256):
    tc = min(tc, S)
    return pl.pallas_call(
        functools.partial(_gla_body, tc=tc),
        grid=(B, S // tc),
        in_specs=[
            pl.BlockSpec((None, tc, 256), lambda b, i: (b, i, MAIN_QG // 256)),
            pl.BlockSpec((None, tc, 256), lambda b, i: (b, i, MAIN_KG // 256)),
            pl.BlockSpec((None, tc, 512), lambda b, i: (b, i, MAIN_VG // 512)),
            pl.BlockSpec((None, tc, 512), lambda b, i: (b, i, MAIN_RG // 512)),
            pl.BlockSpec((None, tc, 128), lambda b, i: (b, i, MAIN_A // 128)),
            pl.BlockSpec((128, 256), lambda b, i: (0, 0)),
            pl.BlockSpec((1, 256), lambda b, i: (0, 0)),
            pl.BlockSpec((1, 128), lambda b, i: (0, 0)),
        ],
        out_specs=pl.BlockSpec((None, tc, 512), lambda b, i: (b, i, 0)),
        out_shape=jax.ShapeDtypeStruct((B, S, 512), BF16),
        scratch_shapes=[pltpu.VMEM((GLA_HEADS, GLA_DV, GLA_DK), F32)],
        compiler_params=pltpu.CompilerParams(dimension_semantics=("parallel", "arbitrary")),
        name="gla_mixer",
    )(main, main, main, main, main, w_gate, b_gate, norm_g)


def _nsa_prep_body(x_ref, pos_ref, invf_ref, qs_ref, kc_ref, vc_ref, ks_ref, kw_ref, vst_ref, vwt_ref, gt_ref):
    ang = pos_ref[...].astype(F32) * invf_ref[...]
    cs, sn = jnp.cos(ang), jnp.sin(ang)
    d = lax.broadcasted_iota(jnp.int32, (1, LANES), 1) % NSA_HEAD_DIM
    half = NSA_ROT_DIM // 2
    s_lo = jnp.where(d < half, -sn, 0.0)
    s_hi = jnp.where((d >= half) & (d < NSA_ROT_DIM), sn, 0.0)

    def rope(x):
        return x * cs + pltpu.roll(x, LANES - half, 1) * s_lo + pltpu.roll(x, half, 1) * s_hi

    def split_groups(x, ref, dtype):
        ref[0] = x[:, :NSA_HEAD_DIM].astype(dtype)
        ref[1] = x[:, NSA_HEAD_DIM:].astype(dtype)

    for pair in range(NSA_HEADS // 2):
        qr = rope(x_ref[:, NSA_Q + pair * LANES:NSA_Q + (pair + 1) * LANES]) * (NSA_HEAD_DIM ** -0.5)
        for sub in range(2):
            head = 2 * pair + sub
            grp, hp = head // NSA_HEADS_PER_GROUP, head % NSA_HEADS_PER_GROUP
            qs_ref[grp, hp * Q_BLOCK:(hp + 1) * Q_BLOCK, :] = (
                qr[:, sub * NSA_HEAD_DIM:(sub + 1) * NSA_HEAD_DIM].astype(BF16))
    split_groups(rope(x_ref[:, NSA_KC:NSA_KC + LANES]), kc_ref, F32)
    split_groups(rope(x_ref[:, NSA_KS:NSA_KS + LANES]), ks_ref, BF16)
    split_groups(rope(x_ref[:, NSA_KW:NSA_KW + LANES]), kw_ref, BF16)
    split_groups(x_ref[:, NSA_VC:NSA_VC + LANES], vc_ref, F32)
    vst = x_ref[:, NSA_VS:NSA_VS + LANES].T
    vwt = x_ref[:, NSA_VW:NSA_VW + LANES].T
    for grp in range(NSA_KV_GROUPS):
        vst_ref[grp] = vst[grp * NSA_HEAD_DIM:(grp + 1) * NSA_HEAD_DIM].astype(BF16)
        vwt_ref[grp] = vwt[grp * NSA_HEAD_DIM:(grp + 1) * NSA_HEAD_DIM].astype(BF16)
    gt_ref[...] = jax.nn.sigmoid(x_ref[:, NSA_G:NSA_G + LANES]).T[:32]


def _nsa_prep(nsa_proj, positions, inv_freq_lanes, B, S):
    G, Dh, nq = NSA_KV_GROUPS, NSA_HEAD_DIM, S // Q_BLOCK
    tok = lambda dtype: jax.ShapeDtypeStruct((B, G, S, Dh), dtype)
    tok_spec = pl.BlockSpec((None, G, Q_BLOCK, Dh), lambda b, i: (b, 0, i, 0))
    tr = jax.ShapeDtypeStruct((B, G, Dh, S), BF16)
    tr_spec = pl.BlockSpec((None, G, Dh, Q_BLOCK), lambda b, i: (b, 0, 0, i))
    return pl.pallas_call(
        _nsa_prep_body,
        grid=(B, nq),
        in_specs=[
            pl.BlockSpec((None, Q_BLOCK, NSA_WIDTH), lambda b, i: (b, i, 0)),
            pl.BlockSpec((None, Q_BLOCK, 1), lambda b, i: (b, i, 0)),
            pl.BlockSpec((1, LANES), lambda b, i: (0, 0)),
        ],
        out_specs=[
            pl.BlockSpec((None, G, NSA_HEADS_PER_GROUP * Q_BLOCK, Dh), lambda b, i: (b, 0, i, 0)),
            tok_spec, tok_spec, tok_spec, tok_spec, tr_spec, tr_spec,
            pl.BlockSpec((None, None, 32, Q_BLOCK), lambda b, i: (b, i, 0, 0)),
        ],
        out_shape=[
            jax.ShapeDtypeStruct((B, G, NSA_HEADS_PER_GROUP * S, Dh), BF16),
            tok(F32), tok(F32), tok(BF16), tok(BF16), tr, tr,
            jax.ShapeDtypeStruct((B, nq, 32, Q_BLOCK), F32),
        ],
        compiler_params=pltpu.CompilerParams(dimension_semantics=("parallel", "parallel")),
        name="nsa_prep",
    )(nsa_proj, positions, inv_freq_lanes)


def _cmp_body(x_ref, w_ref, p_ref, o_ref, *, n_blk, transpose_out):
    half = CMP_BLOCK // 2
    first = jnp.zeros((n_blk, NSA_HEAD_DIM), F32)
    second = jnp.zeros((n_blk, NSA_HEAD_DIM), F32)
    for l in range(half):
        rows = x_ref[pl.ds(l, n_blk, stride=CMP_STRIDE), :]
        first = first + _dot((rows + p_ref[l]).astype(BF16), w_ref[l])
        second = second + _dot((rows + p_ref[half + l]).astype(BF16), w_ref[half + l])
    out = first + jnp.concatenate([second[1:], jnp.zeros((1, NSA_HEAD_DIM), F32)], axis=0)
    o_ref[...] = (out.T if transpose_out else out).astype(o_ref.dtype)


def _compress(x, w, pos, transpose_out):
    B, G, S, Dh = x.shape
    n_blk = S // CMP_STRIDE
    out_block = (None, None, Dh, n_blk) if transpose_out else (None, None, n_blk, Dh)
    out_shape = (B, G, Dh, n_blk) if transpose_out else (B, G, n_blk, Dh)
    return pl.pallas_call(
        functools.partial(_cmp_body, n_blk=n_blk, transpose_out=transpose_out),
        grid=(B, G),
        in_specs=[
            pl.BlockSpec((None, None, S, Dh), lambda b, g: (b, g, 0, 0)),
            pl.BlockSpec((CMP_BLOCK, Dh, Dh), lambda b, g: (0, 0, 0)),
            pl.BlockSpec((CMP_BLOCK, 1, Dh), lambda b, g: (0, 0, 0)),
        ],
        out_specs=pl.BlockSpec(out_block, lambda b, g: (b, g, 0, 0)),
        out_shape=jax.ShapeDtypeStruct(out_shape, BF16),
        compiler_params=pltpu.CompilerParams(
            dimension_semantics=("parallel", "parallel"), vmem_limit_bytes=VMEM_LIMIT),
        name="nsa_compress",
    )(x, w, pos)


def _nsa_body(q_ref, kcc_ref, vcct_ref, ks_ref, vst_ref, kw_ref, vwt_ref, gt_ref, o_ref, imp_scr, sel_scr, *, kt_size):
    grp = pl.program_id(1)
    t0 = pl.program_id(2) * Q_BLOCK
    hpg = NSA_HEADS_PER_GROUP
    n_cmp = kcc_ref.shape[0]
    n_sel = sel_scr.shape[0]
    q = q_ref[...]
    tq1 = t0 + lax.broadcasted_iota(jnp.int32, (1, Q_BLOCK), 1)
    heads = lambda x: jnp.concatenate([x] * hpg, axis=1)

    s = _nt(kcc_ref[...], q)
    cmp_end = lax.broadcasted_iota(jnp.int32, (n_cmp, 1), 0) * CMP_STRIDE + (CMP_BLOCK - 1)
    s = jnp.where(heads(cmp_end <= tq1), s, -jnp.inf)
    m = jnp.max(s, axis=0, keepdims=True)
    m = jnp.where(jnp.isfinite(m), m, 0.0)
    e = jnp.exp(s - m)
    p = e / jnp.maximum(jnp.sum(e, axis=0, keepdims=True), 1e-30)
    o_cmp = _dot(vcct_ref[...], p.astype(BF16))

    p_sum = p[:, :Q_BLOCK]
    for hp in range(1, hpg):
        p_sum = p_sum + p[:, hp * Q_BLOCK:(hp + 1) * Q_BLOCK]
    imp_scr[pl.ds(0, 8), :] = jnp.zeros((8, Q_BLOCK), F32)
    imp_scr[pl.ds(8, n_cmp), :] = p_sum
    ratio = SEL_BLOCK // CMP_STRIDE
    imp = imp_scr[pl.ds(7, n_sel, stride=ratio), :]
    for k in range(1, ratio + 1):
        imp = imp + imp_scr[pl.ds(7 + k, n_sel, stride=ratio), :]
    j = lax.broadcasted_iota(jnp.int32, (n_sel, Q_BLOCK), 0)
    cur = tq1 >> (SEL_BLOCK.bit_length() - 1)
    forced = (j == 0) | (j == cur) | (j == cur - 1)
    score = jnp.where(forced, jnp.inf, jnp.where(j <= cur, imp, -jnp.inf))

    def pick(_, carry):
        sc, bias = carry
        best = jnp.max(sc, axis=0, keepdims=True)
        idx = jnp.min(jnp.where(sc == best, j, n_sel), axis=0, keepdims=True)
        hit = j == idx
        return jnp.where(hit, -jnp.inf, sc), jnp.where(hit, 0.0, bias)

    _, sel_bias = lax.fori_loop(0, min(SEL_TOPK, n_sel), pick, (score, jnp.full((n_sel, Q_BLOCK), MASKED, F32)))
    sel_scr[...] = sel_bias

    blocks_per_tile = kt_size // SEL_BLOCK

    def slc_step(kt, carry):
        m_run, l_run, acc = carry
        k0 = pl.multiple_of(kt * kt_size, kt_size)
        sc = _nt(ks_ref[pl.ds(k0, kt_size), :], q)
        bias = jnp.concatenate(
            [jnp.broadcast_to(sel_scr[pl.ds(kt * blocks_per_tile + jb, 1), :], (SEL_BLOCK, Q_BLOCK))
             for jb in range(blocks_per_tile)], axis=0)
        kpos = k0 + lax.broadcasted_iota(jnp.int32, (kt_size, 1), 0)
        bias = jnp.where(kpos <= tq1, bias, MASKED)
        sc = sc + heads(bias)
        m_new = jnp.maximum(m_run, jnp.max(sc, axis=0, keepdims=True))
        alpha = jnp.exp(m_run - m_new)
        pe = jnp.exp(sc - m_new)
        l_new = alpha * l_run + jnp.sum(pe, axis=0, keepdims=True)
        acc_new = alpha * acc + _dot(vst_ref[:, pl.ds(k0, kt_size)], pe.astype(BF16))
        return m_new, l_new, acc_new

    width = hpg * Q_BLOCK
    n_tiles = (t0 + Q_BLOCK + kt_size - 1) // kt_size
    _, l_slc, acc_slc = lax.fori_loop(
        0, n_tiles, slc_step,
        (jnp.full((1, width), MASKED, F32), jnp.zeros((1, width), F32), jnp.zeros((NSA_HEAD_DIM, width), F32)))
    o_slc = acc_slc / l_slc

    span = WINDOW + Q_BLOCK
    k_start = pl.multiple_of(jnp.maximum(t0 - WINDOW, 0), Q_BLOCK)
    sw = _nt(kw_ref[pl.ds(k_start, span), :], q)
    diff = tq1 - (k_start + lax.broadcasted_iota(jnp.int32, (span, 1), 0))
    sw = jnp.where(heads((diff >= 0) & (diff < WINDOW)), sw, MASKED)
    pw = jnp.exp(sw - jnp.max(sw, axis=0, keepdims=True))
    o_win = _dot(vwt_ref[:, pl.ds(k_start, span)], pw.astype(BF16)) / jnp.sum(pw, axis=0, keepdims=True)

    for hp in range(hpg):
        cols = slice(hp * Q_BLOCK, (hp + 1) * Q_BLOCK)
        row = (grp * hpg + hp) * 3
        o_t = (gt_ref[pl.ds(row, 1), :] * o_cmp[:, cols] + gt_ref[pl.ds(row + 1, 1), :] * o_slc[:, cols]
               + gt_ref[pl.ds(row + 2, 1), :] * o_win[:, cols])
        o_ref[:, hp * NSA_HEAD_DIM:(hp + 1) * NSA_HEAD_DIM] = o_t.T.astype(o_ref.dtype)


def _nsa_attention(qs, kcc, vcct, ks, vst, kw, vwt, gt, B, S, kt_size=512):
    G, Dh, nq, hpg = NSA_KV_GROUPS, NSA_HEAD_DIM, S // Q_BLOCK, NSA_HEADS_PER_GROUP
    kt_size = min(kt_size, S)
    n_cmp, n_sel = S // CMP_STRIDE, S // SEL_BLOCK
    per_group = lambda shape: pl.BlockSpec((None, None) + shape, lambda b, g, i: (b, g, 0, 0))
    return pl.pallas_call(
        functools.partial(_nsa_body, kt_size=kt_size),
        grid=(B, G, nq),
        in_specs=[
            pl.BlockSpec((None, None, hpg * Q_BLOCK, Dh), lambda b, g, i: (b, g, i, 0)),
            per_group((n_cmp, Dh)), per_group((Dh, n_cmp)),
            per_group((S, Dh)), per_group((Dh, S)),
            per_group((S, Dh)), per_group((Dh, S)),
            pl.BlockSpec((None, None, 32, Q_BLOCK), lambda b, g, i: (b, i, 0, 0)),
        ],
        out_specs=pl.BlockSpec((None, Q_BLOCK, hpg * Dh), lambda b, g, i: (b, i, g)),
        out_shape=jax.ShapeDtypeStruct((B, S, NSA_HEADS * Dh), BF16),
        scratch_shapes=[pltpu.VMEM((n_cmp + 16, Q_BLOCK), F32), pltpu.VMEM((n_sel, Q_BLOCK), F32)],
        compiler_params=pltpu.CompilerParams(
            dimension_semantics=("parallel", "parallel", "arbitrary"), vmem_limit_bytes=VMEM_LIMIT),
        name="nsa_attention",
    )(qs, kcc, vcct, ks, vst, kw, vwt, gt)


def _merge_body(ya_ref, yb_ref, yc_ref, ga_ref, gb_ref, gc_ref, h_ref, wb_ref, wo_ref, o_ref):
    merged = jax.nn.sigmoid(ga_ref[...].astype(F32)) * _dot(ya_ref[...], wb_ref[0])
    merged = merged + jax.nn.sigmoid(gb_ref[...].astype(F32)) * _dot(yb_ref[...], wb_ref[1])
    merged = merged + jax.nn.sigmoid(gc_ref[...].astype(F32)) * _dot(yc_ref[...], wb_ref[2])
    o_ref[...] = h_ref[...] + _dot(merged.astype(BF16), wo_ref[...])


def _merge(ya, yb, yc, main, h, w_branch, w_out, tm=512):
    T, D = h.shape
    y_spec = pl.BlockSpec((tm, 512), lambda i: (i, 0))
    gate_spec = lambda k: pl.BlockSpec((tm, D), lambda i: (i, MAIN_GM // D + k))
    return pl.pallas_call(
        _merge_body,
        grid=(T // tm,),
        in_specs=[
            y_spec, y_spec, y_spec, gate_spec(0), gate_spec(1), gate_spec(2),
            pl.BlockSpec((tm, D), lambda i: (i, 0)),
            pl.BlockSpec((3, 512, D), lambda i: (0, 0, 0)),
            pl.BlockSpec((D, D), lambda i: (0, 0)),
        ],
        out_specs=pl.BlockSpec((tm, D), lambda i: (i, 0)),
        out_shape=jax.ShapeDtypeStruct((T, D), F32),
        compiler_params=pltpu.CompilerParams(dimension_semantics=("parallel",), vmem_limit_bytes=VMEM_LIMIT),
        name="merge_out",
    )(ya, yb, yc, main, main, main, h, w_branch, w_out)


def _ffn_body(h_ref, gf_ref, w1_ref, w2_ref, gp_ref, wg_ref, p_ref, wp_ref, gl_ref, o_ref, f_scr, acc_scr, *, final_norm):
    j = pl.program_id(1)

    @pl.when(j == 0)
    def _():
        f_scr[...] = _rms(h_ref[...], gf_ref[...]).astype(BF16)
        acc_scr[...] = jnp.zeros_like(acc_scr)

    a = jnp.maximum(_dot(f_scr[...], w1_ref[...]), 0.0)
    acc_scr[...] += _dot((a * a).astype(BF16), w2_ref[...])

    @pl.when(j == pl.num_programs(1) - 1)
    def _():
        h2 = h_ref[...] + acc_scr[...]
        gate = jax.nn.sigmoid(_dot(_rms(h2, gp_ref[...]).astype(BF16), wg_ref[...]))
        h3 = h2 + gate * _dot(p_ref[...].astype(BF16), wp_ref[...])
        o_ref[...] = _rms(h3, gl_ref[...]) if final_norm else h3


def _ffn_ple(h, norm_ffn, w1, w2, norm_ple, w_gate, p, w_proj, norm_last, final_norm, tm=512, tf=1024):
    T, D = h.shape
    Fdim = w1.shape[1]
    Pdim = p.shape[1]
    vec = pl.BlockSpec((1, D), lambda i, j: (0, 0))
    return pl.pallas_call(
        functools.partial(_ffn_body, final_norm=final_norm),
        grid=(T // tm, Fdim // tf),
        in_specs=[
            pl.BlockSpec((tm, D), lambda i, j: (i, 0)), vec,
            pl.BlockSpec((D, tf), lambda i, j: (0, j)),
            pl.BlockSpec((tf, D), lambda i, j: (j, 0)),
            vec,
            pl.BlockSpec((D, D), lambda i, j: (0, 0)),
            pl.BlockSpec((tm, Pdim), lambda i, j: (i, 0)),
            pl.BlockSpec((Pdim, D), lambda i, j: (0, 0)),
            vec,
        ],
        out_specs=pl.BlockSpec((tm, D), lambda i, j: (i, 0)),
        out_shape=jax.ShapeDtypeStruct((T, D), F32),
        scratch_shapes=[pltpu.VMEM((tm, D), BF16), pltpu.VMEM((tm, D), F32)],
        compiler_params=pltpu.CompilerParams(
            dimension_semantics=("parallel", "arbitrary"), vmem_limit_bytes=VMEM_LIMIT),
        name="ffn_ple",
    )(h, norm_ffn, w1, w2, norm_ple, w_gate, p, w_proj, norm_last)


def _split_w_in(w):
    parts, off = [], 0
    for sz in IN_SIZES:
        parts.append(w[:, off:off + sz])
        off += sz
    u, q_nsa, kv_nsa, g_nsa, q_gla, k_gla, v_gla, a_gla, r_gla, g_merge = parts
    D = w.shape[0]
    pad = lambda n: jnp.zeros((D, n), w.dtype)
    main = jnp.concatenate([u, q_gla, k_gla, v_gla, r_gla, g_merge, a_gla, pad(MAIN_WIDTH - MAIN_A - GLA_GATE_RANK)], axis=1)
    kv = [kv_nsa[:, s * LANES:(s + 1) * LANES] for s in range(6)]
    nsa = jnp.concatenate([q_nsa, kv[0], kv[2], kv[4], kv[1], kv[3], kv[5], g_nsa, pad(NSA_WIDTH - NSA_G - 24)], axis=1)
    return main.astype(BF16), nsa.astype(BF16)


def kernel(x, p, positions, norm_mix, w_in, pool_w, pool_scale, cmp_pos_k, cmp_w_k, cmp_pos_v, cmp_w_v, gla_w_gate, gla_b_gate, gla_norm, w_branch, w_out, norm_ffn, w_ff1, w_ff2, norm_ple, w_ple_gate, w_ple_proj, norm_final):
    B, S, D = x.shape
    depth = w_in.shape[0]
    T = B * S
    Dh = NSA_HEAD_DIM
    row = lambda v: v.reshape(1, -1).astype(F32)

    half = NSA_ROT_DIM // 2
    inv_freq = jnp.power(ROPE_THETA, -jnp.arange(half, dtype=F32) * (2.0 / NSA_ROT_DIM))
    d = jnp.arange(LANES) % Dh
    inv_freq_lanes = jnp.where(d < NSA_ROT_DIM, inv_freq[d % half], 0.0).reshape(1, LANES)
    pos3 = positions.reshape(B, S, 1)

    h = x.reshape(T, D)
    for i in range(depth):
        w_main, w_nsa = _split_w_in(w_in[i])
        main = _norm_proj(h, row(norm_mix[i]), w_main, BF16).reshape(B, S, MAIN_WIDTH)
        nsa_proj = _norm_proj(h, row(norm_mix[i]), w_nsa, F32).reshape(B, S, NSA_WIDTH)

        y_a = _pool_mixer(main, pool_w[i].astype(BF16), row(pool_scale[i]), B, S)

        wg = jnp.zeros((LANES, GLA_HEADS * GLA_DK), BF16).at[:GLA_GATE_RANK].set(gla_w_gate[i].astype(BF16))
        y_c = _gla_mixer(main, wg, row(gla_b_gate[i]), row(gla_norm[i]), B, S)

        qs, kc, vc, ks, kw, vst, vwt, gt = _nsa_prep(nsa_proj, pos3, inv_freq_lanes, B, S)
        kcc = _compress(kc, cmp_w_k[i].reshape(CMP_BLOCK, Dh, Dh).astype(BF16),
                        cmp_pos_k[i].reshape(CMP_BLOCK, 1, Dh).astype(F32), transpose_out=False)
        vcct = _compress(vc, cmp_w_v[i].reshape(CMP_BLOCK, Dh, Dh).astype(BF16),
                         cmp_pos_v[i].reshape(CMP_BLOCK, 1, Dh).astype(F32), transpose_out=True)
        y_b = _nsa_attention(qs, kcc, vcct, ks, vst, kw, vwt, gt, B, S)

        h = _merge(y_a.reshape(T, -1), y_b.reshape(T, -1), y_c.reshape(T, -1), main.reshape(T, MAIN_WIDTH), h,
                   w_branch[i].astype(BF16), w_out[i].astype(BF16))
        h = _ffn_ple(h, row(norm_ffn[i]), w_ff1[i].astype(BF16), w_ff2[i].astype(BF16), row(norm_ple[i]),
                     w_ple_gate[i].astype(BF16), p[i].reshape(T, -1), w_ple_proj[i].astype(BF16),
                     row(norm_final), final_norm=(i == depth - 1))
    return h.reshape(B, S, D)
```

```python
import functools

import jax
import jax.numpy as jnp
from jax import lax
from jax.experimental import pallas as pl
from jax.experimental.pallas import tpu as pltpu

F32, BF16 = jnp.float32, jnp.bfloat16

EPS = 1e-6
ROPE_THETA = 500000.0
POOL_WINDOWS = (2, 4, 8, 16)
POOL_GROUP_DIM = 128
NSA_HEADS = 8
NSA_KV_GROUPS = 2
NSA_HEADS_PER_GROUP = NSA_HEADS // NSA_KV_GROUPS
NSA_HEAD_DIM = 64
NSA_ROT_DIM = 16
CMP_BLOCK = 32
CMP_STRIDE = 16
SEL_BLOCK = 64
SEL_TOPK = 16
WINDOW = 512
Q_BLOCK = 128
GLA_HEADS = 4
GLA_DK = 64
GLA_DV = 128
GLA_GATE_RANK = 16
GLA_TAU = 16.0
GLA_CHUNK = 64
IN_SIZES = (512, 512, 768, 24, 256, 256, 512, 16, 512, 3072)

LANES = 128
MASKED = -1e30
LOG2E = 1.4426950408889634
Q_SCALE = NSA_HEAD_DIM ** -0.5 * LOG2E
V_ROWS = NSA_HEAD_DIM + 16
VMEM_LIMIT = 56 * 1024 * 1024

MAIN_U, MAIN_QG, MAIN_KG, MAIN_VG, MAIN_RG, MAIN_GM, MAIN_A, MAIN_WIDTH = 0, 512, 768, 1024, 1536, 2048, 5120, 5376
NSA_Q, NSA_KC, NSA_KS, NSA_KW, NSA_VC, NSA_VS, NSA_VW, NSA_G, NSA_WIDTH = 0, 512, 640, 768, 896, 1024, 1152, 1280, 1536


def _nt(a, b):
    return lax.dot_general(a, b, (((1,), (1,)), ((), ())), preferred_element_type=F32)


def _tn(a, b):
    return lax.dot_general(a, b, (((0,), (0,)), ((), ())), preferred_element_type=F32)


def _dot(a, b):
    return jnp.dot(a, b, preferred_element_type=F32)


def _rms(x, gain):
    return x * lax.rsqrt(jnp.mean(x * x, axis=-1, keepdims=True) + EPS) * gain


def _proj_body(h_ref, g_ref, w_ref, o_ref, a_scr):
    @pl.when(pl.program_id(1) == 0)
    def _():
        a_scr[...] = _rms(h_ref[...], g_ref[...]).astype(BF16)

    o_ref[...] = _dot(a_scr[...], w_ref[...]).astype(o_ref.dtype)


def _norm_proj(h, gain, w, out_dtype, tm=512, tn=768):
    T, D = h.shape
    N = w.shape[1]
    return pl.pallas_call(
        _proj_body,
        grid=(T // tm, N // tn),
        in_specs=[
            pl.BlockSpec((tm, D), lambda i, j: (i, 0)),
            pl.BlockSpec((1, D), lambda i, j: (0, 0)),
            pl.BlockSpec((D, tn), lambda i, j: (0, j)),
        ],
        out_specs=pl.BlockSpec((tm, tn), lambda i, j: (i, j)),
        out_shape=jax.ShapeDtypeStruct((T, N), out_dtype),
        scratch_shapes=[pltpu.VMEM((tm, D), BF16)],
        compiler_params=pltpu.CompilerParams(
            dimension_semantics=("parallel", "arbitrary"), vmem_limit_bytes=VMEM_LIMIT),
        name="norm_proj",
    )(h, gain, w)


def _pool_body(u_ref, halo_ref, w_ref, sc_ref, o_ref, *, ts):
    i = pl.program_id(1)
    cur = u_ref[...].astype(F32)
    halo = jnp.where(i == 0, 0.0, halo_ref[...].astype(F32))
    t = i * ts + lax.broadcasted_iota(jnp.int32, (ts, 1), 0)
    for g, win in enumerate(POOL_WINDOWS):
        lo, hi = g * POOL_GROUP_DIM, (g + 1) * POOL_GROUP_DIM
        x = cur[:, lo:hi]
        e = jnp.concatenate([halo[:, lo:hi], x], axis=0)
        step = 1
        while step < win:
            e = e[step:] + e[:-step]
            step *= 2
        wsum = e[16 - (win - 1):]
        cnt = jnp.minimum(t + 1, win).astype(F32)
        pooled = wsum / cnt - x
        y = _dot(pooled.astype(BF16), w_ref[g]) * sc_ref[:, lo:hi]
        o_ref[:, lo:hi] = y.astype(o_ref.dtype)


def _pool_mixer(main, pool_w, scale, B, S, ts=512):
    ts = min(ts, S)
    halo_blocks = ts // 16
    return pl.pallas_call(
        functools.partial(_pool_body, ts=ts),
        grid=(B, S // ts),
        in_specs=[
            pl.BlockSpec((None, ts, 512), lambda b, i: (b, i, MAIN_U // 512)),
            pl.BlockSpec((None, 16, 512), lambda b, i: (b, jnp.maximum(i * halo_blocks - 1, 0), MAIN_U // 512)),
            pl.BlockSpec((4, 128, 128), lambda b, i: (0, 0, 0)),
            pl.BlockSpec((1, 512), lambda b, i: (0, 0)),
        ],
        out_specs=pl.BlockSpec((None, ts, 512), lambda b, i: (b, i, 0)),
        out_shape=jax.ShapeDtypeStruct((B, S, 512), BF16),
        compiler_params=pltpu.CompilerParams(dimension_semantics=("parallel", "arbitrary")),
        name="pool_mixer",
    )(main, main, pool_w, scale)


def _gla_body(q_ref, k_ref, v_ref, r_ref, a_ref, wg_ref, bg_ref, ng_ref, o_ref, st_ref, *, tc):
    @pl.when(pl.program_id(1) == 0)
    def _():
        st_ref[...] = jnp.zeros_like(st_ref)

    C = GLA_CHUNK
    causal = lax.broadcasted_iota(jnp.int32, (C, C), 0) >= lax.broadcasted_iota(jnp.int32, (C, C), 1)
    width = GLA_HEADS * GLA_DK
    for c in range(tc // C):
        rows = slice(c * C, (c + 1) * C)
        z = _dot(a_ref[rows, :], wg_ref[...]) + bg_ref[...]
        b = jax.nn.log_sigmoid(z) * (1.0 / GLA_TAU)
        step = 1
        while step < C:
            b = b + jnp.concatenate([jnp.zeros((step, width), F32), b[:-step]], axis=0)
            step *= 2
        b_last = b[C - 1:C, :]
        qf = q_ref[rows, :].astype(F32) * (GLA_DK ** -0.5)
        kf = k_ref[rows, :].astype(F32)
        q_s = (qf * jnp.exp(b)).astype(BF16)
        k_s = (kf * jnp.exp(-b)).astype(BF16)
        k_t = (kf * jnp.exp(b_last - b)).astype(BF16)
        decay = jnp.exp(b_last)
        for h in range(GLA_HEADS):
            hk = slice(h * GLA_DK, (h + 1) * GLA_DK)
            hv = slice(h * GLA_DV, (h + 1) * GLA_DV)
            vh = v_ref[rows, hv]
            state_t = st_ref[h]
            a = jnp.where(causal, _nt(q_s[:, hk], k_s[:, hk]), 0.0)
            o = _dot(a.astype(BF16), vh) + _nt(q_s[:, hk], state_t.astype(BF16))
            st_ref[h] = state_t * decay[:, hk] + _tn(vh, k_t[:, hk])
            o = _rms(o, ng_ref[...])
            r = r_ref[rows, hv].astype(F32)
            o_ref[rows, hv] = (o * (r * jax.nn.sigmoid(r))).astype(o_ref.dtype)


def _gla_mixer(main, w_gate, b_gate, norm_g, B, S, tc=256):
    tc = min(tc, S)
    return pl.pallas_call(
        functools.partial(_gla_body, tc=tc),
        grid=(B, S // tc),
        in_specs=[
            pl.BlockSpec((None, tc, 256), lambda b, i: (b, i, MAIN_QG // 256)),
            pl.BlockSpec((None, tc, 256), lambda b, i: (b, i, MAIN_KG // 256)),
            pl.BlockSpec((None, tc, 512), lambda b, i: (b, i, MAIN_VG // 512)),
            pl.BlockSpec((None, tc, 512), lambda b, i: (b, i, MAIN_RG // 512)),
            pl.BlockSpec((None, tc, 128), lambda b, i: (b, i, MAIN_A // 128)),
            pl.BlockSpec((128, 256), lambda b, i: (0, 0)),
            pl.BlockSpec((1, 256), lambda b, i: (0, 0)),
            pl.BlockSpec((1, 128), lambda b, i: (0, 0)),
        ],
        out_specs=pl.BlockSpec((None, tc, 512), lambda b, i: (b, i, 0)),
        out_shape=jax.ShapeDtypeStruct((B, S, 512), BF16),
        scratch_shapes=[pltpu.VMEM((GLA_HEADS, GLA_DV, GLA_DK), F32)],
        compiler_params=pltpu.CompilerParams(dimension_semantics=("parallel", "arbitrary")),
        name="gla_mixer",
    )(main, main, main, main, main, w_gate, b_gate, norm_g)


def _nsa_prep_body(x_ref, pos_ref, invf_ref, qs_ref, kc_ref, vc_ref, ks_ref, kw_ref, vst_ref, vwt_ref, gt_ref):
    ang = pos_ref[...].astype(F32) * invf_ref[...]
    cs, sn = jnp.cos(ang), jnp.sin(ang)
    d = lax.broadcasted_iota(jnp.int32, (1, LANES), 1) % NSA_HEAD_DIM
    half = NSA_ROT_DIM // 2
    s_lo = jnp.where(d < half, -sn, 0.0)
    s_hi = jnp.where((d >= half) & (d < NSA_ROT_DIM), sn, 0.0)

    def rope(x):
        return x * cs + pltpu.roll(x, LANES - half, 1) * s_lo + pltpu.roll(x, half, 1) * s_hi

    def split_groups(x, ref, dtype):
        ref[0] = x[:, :NSA_HEAD_DIM].astype(dtype)
        ref[1] = x[:, NSA_HEAD_DIM:].astype(dtype)

    for pair in range(NSA_HEADS // 2):
        qr = rope(x_ref[:, NSA_Q + pair * LANES:NSA_Q + (pair + 1) * LANES]) * Q_SCALE
        for sub in range(2):
            head = 2 * pair + sub
            grp, hp = head // NSA_HEADS_PER_GROUP, head % NSA_HEADS_PER_GROUP
            qs_ref[grp, hp * Q_BLOCK:(hp + 1) * Q_BLOCK, :] = (
                qr[:, sub * NSA_HEAD_DIM:(sub + 1) * NSA_HEAD_DIM].astype(BF16))
    split_groups(rope(x_ref[:, NSA_KC:NSA_KC + LANES]), kc_ref, F32)
    split_groups(rope(x_ref[:, NSA_KS:NSA_KS + LANES]), ks_ref, BF16)
    split_groups(rope(x_ref[:, NSA_KW:NSA_KW + LANES]), kw_ref, BF16)
    split_groups(x_ref[:, NSA_VC:NSA_VC + LANES], vc_ref, F32)
    vst = x_ref[:, NSA_VS:NSA_VS + LANES].T
    vwt = x_ref[:, NSA_VW:NSA_VW + LANES].T
    ones_row = (lax.broadcasted_iota(jnp.int32, (V_ROWS - NSA_HEAD_DIM, Q_BLOCK), 0) == 0).astype(BF16)
    for grp in range(NSA_KV_GROUPS):
        for ref, v in ((vst_ref, vst), (vwt_ref, vwt)):
            ref[grp, :NSA_HEAD_DIM, :] = v[grp * NSA_HEAD_DIM:(grp + 1) * NSA_HEAD_DIM].astype(BF16)
            ref[grp, NSA_HEAD_DIM:, :] = ones_row
    gt_ref[...] = jax.nn.sigmoid(x_ref[:, NSA_G:NSA_G + LANES]).T[:32]


def _nsa_prep(nsa_proj, positions, inv_freq_lanes, B, S):
    G, Dh, nq = NSA_KV_GROUPS, NSA_HEAD_DIM, S // Q_BLOCK
    tok = lambda dtype: jax.ShapeDtypeStruct((B, G, S, Dh), dtype)
    tok_spec = pl.BlockSpec((None, G, Q_BLOCK, Dh), lambda b, i: (b, 0, i, 0))
    tr = jax.ShapeDtypeStruct((B, G, V_ROWS, S), BF16)
    tr_spec = pl.BlockSpec((None, G, V_ROWS, Q_BLOCK), lambda b, i: (b, 0, 0, i))
    return pl.pallas_call(
        _nsa_prep_body,
        grid=(B, nq),
        in_specs=[
            pl.BlockSpec((None, Q_BLOCK, NSA_WIDTH), lambda b, i: (b, i, 0)),
            pl.BlockSpec((None, Q_BLOCK, 1), lambda b, i: (b, i, 0)),
            pl.BlockSpec((1, LANES), lambda b, i: (0, 0)),
        ],
        out_specs=[
            pl.BlockSpec((None, G, NSA_HEADS_PER_GROUP * Q_BLOCK, Dh), lambda b, i: (b, 0, i, 0)),
            tok_spec, tok_spec, tok_spec, tok_spec, tr_spec, tr_spec,
            pl.BlockSpec((None, None, 32, Q_BLOCK), lambda b, i: (b, i, 0, 0)),
        ],
        out_shape=[
            jax.ShapeDtypeStruct((B, G, NSA_HEADS_PER_GROUP * S, Dh), BF16),
            tok(F32), tok(F32), tok(BF16), tok(BF16), tr, tr,
            jax.ShapeDtypeStruct((B, nq, 32, Q_BLOCK), F32),
        ],
        compiler_params=pltpu.CompilerParams(dimension_semantics=("parallel", "parallel")),
        name="nsa_prep",
    )(nsa_proj, positions, inv_freq_lanes)


def _cmp_body(x_ref, w_ref, p_ref, o_ref, *, n_blk, transpose_out):
    half = CMP_BLOCK // 2
    first = jnp.zeros((n_blk, NSA_HEAD_DIM), F32)
    second = jnp.zeros((n_blk, NSA_HEAD_DIM), F32)
    for l in range(half):
        rows = x_ref[pl.ds(l, n_blk, stride=CMP_STRIDE), :]
        first = first + _dot((rows + p_ref[l]).astype(BF16), w_ref[l])
        second = second + _dot((rows + p_ref[half + l]).astype(BF16), w_ref[half + l])
    out = first + jnp.concatenate([second[1:], jnp.zeros((1, NSA_HEAD_DIM), F32)], axis=0)
    o_ref[...] = (out.T if transpose_out else out).astype(o_ref.dtype)


def _compress(x, w, pos, transpose_out):
    B, G, S, Dh = x.shape
    n_blk = S // CMP_STRIDE
    out_block = (None, None, Dh, n_blk) if transpose_out else (None, None, n_blk, Dh)
    out_shape = (B, G, Dh, n_blk) if transpose_out else (B, G, n_blk, Dh)
    return pl.pallas_call(
        functools.partial(_cmp_body, n_blk=n_blk, transpose_out=transpose_out),
        grid=(B, G),
        in_specs=[
            pl.BlockSpec((None, None, S, Dh), lambda b, g: (b, g, 0, 0)),
            pl.BlockSpec((CMP_BLOCK, Dh, Dh), lambda b, g: (0, 0, 0)),
            pl.BlockSpec((CMP_BLOCK, 1, Dh), lambda b, g: (0, 0, 0)),
        ],
        out_specs=pl.BlockSpec(out_block, lambda b, g: (b, g, 0, 0)),
        out_shape=jax.ShapeDtypeStruct(out_shape, BF16),
        compiler_params=pltpu.CompilerParams(
            dimension_semantics=("parallel", "parallel"), vmem_limit_bytes=VMEM_LIMIT),
        name="nsa_compress",
    )(x, w, pos)


def _nsa_body(q_ref, kcc_ref, vcct_ref, ks_ref, vst_ref, kw_ref, vwt_ref, gt_ref, o_ref,
              imp_scr, sel_scr, e_scr, mrow_scr, *, kt_size, sub_size, cmp_chunk):
    grp = pl.program_id(1)
    t0 = pl.program_id(2) * Q_BLOCK
    hpg = NSA_HEADS_PER_GROUP
    n_cmp = kcc_ref.shape[0]
    n_sel = sel_scr.shape[0]
    q = q_ref[...]
    tq1 = t0 + lax.broadcasted_iota(jnp.int32, (1, Q_BLOCK), 1)
    heads = lambda x: jnp.concatenate([x] * hpg, axis=1)

    width = hpg * Q_BLOCK
    neg_inf = -jnp.inf

    ch = cmp_chunk
    n_chunks = (((t0 + Q_BLOCK - CMP_BLOCK) >> (CMP_STRIDE.bit_length() - 1)) + ch) // ch

    def cmp_step(c, carry):
        m_run, l_run, acc = carry
        r0 = pl.multiple_of(c * ch, ch)
        s = _nt(kcc_ref[pl.ds(r0, ch), :], q)
        cmp_end = (r0 + lax.broadcasted_iota(jnp.int32, (ch, 1), 0)) * CMP_STRIDE + (CMP_BLOCK - 1)
        s = jnp.where(heads(cmp_end <= tq1), s, neg_inf)
        m_new = jnp.maximum(m_run, jnp.max(s, axis=0, keepdims=True))
        m_safe = jnp.where(m_new == neg_inf, 0.0, m_new)
        alpha = jnp.exp2(jnp.where(m_run == neg_inf, 0.0, m_run) - m_safe)
        e = jnp.exp2(s - m_safe)
        e_scr[pl.ds(r0, ch), :] = e
        mrow_scr[pl.ds(c, 1), :] = m_safe
        l_new = alpha * l_run + jnp.sum(e, axis=0, keepdims=True)
        return m_new, l_new, alpha * acc + _dot(vcct_ref[:, pl.ds(r0, ch)], e.astype(BF16))

    m_cmp, l_cmp, acc_cmp = lax.fori_loop(
        0, n_chunks, cmp_step,
        (jnp.full((1, width), neg_inf, F32), jnp.zeros((1, width), F32), jnp.zeros((NSA_HEAD_DIM, width), F32)))
    inv_l = 1.0 / jnp.maximum(l_cmp, 1e-30)
    o_cmp = acc_cmp * inv_l
    m_cmp = jnp.where(m_cmp == neg_inf, 0.0, m_cmp)

    imp_scr[...] = jnp.zeros(imp_scr.shape, F32)

    def imp_step(c, _):
        r0 = pl.multiple_of(c * ch, ch)
        pc = e_scr[pl.ds(r0, ch), :] * (jnp.exp2(mrow_scr[pl.ds(c, 1), :] - m_cmp) * inv_l)
        p_sum = pc[:, :Q_BLOCK]
        for hp in range(1, hpg):
            p_sum = p_sum + pc[:, hp * Q_BLOCK:(hp + 1) * Q_BLOCK]
        imp_scr[pl.ds(pl.multiple_of(r0 + 8, 8), ch), :] = p_sum
        return 0

    lax.fori_loop(0, n_chunks, imp_step, 0)
    ratio = SEL_BLOCK // CMP_STRIDE
    imp = imp_scr[pl.ds(7, n_sel, stride=ratio), :]
    for k in range(1, ratio + 1):
        imp = imp + imp_scr[pl.ds(7 + k, n_sel, stride=ratio), :]
    j = lax.broadcasted_iota(jnp.int32, (n_sel, Q_BLOCK), 0)
    cur = tq1 >> (SEL_BLOCK.bit_length() - 1)
    forced = (j == 0) | (j == cur) | (j == cur - 1)
    valid = j <= cur

    def pick(_, sc):
        best = jnp.max(sc, axis=0, keepdims=True)
        idx = jnp.min(jnp.where(sc == best, j, n_sel), axis=0, keepdims=True)
        return jnp.where(j == idx, neg_inf, sc)

    n_forced = 3
    left = lax.fori_loop(0, max(min(SEL_TOPK, n_sel) - n_forced, 0), pick,
                         jnp.where(valid & ~forced, imp, neg_inf), unroll=True)
    sel_scr[...] = jnp.where(forced | (valid & (left == neg_inf)), 0.0, MASKED)

    blocks_per_tile = kt_size // SEL_BLOCK
    eye = (lax.broadcasted_iota(jnp.int32, (Q_BLOCK, Q_BLOCK), 0)
           == lax.broadcasted_iota(jnp.int32, (Q_BLOCK, Q_BLOCK), 1)).astype(BF16)
    q_aug = jnp.concatenate([jnp.concatenate([eye] * hpg, axis=0), q], axis=1)

    def slc_tile(kt, carry, causal):
        m_run, acc = carry
        scores = []
        for sub in range(kt_size // sub_size):
            k0 = pl.multiple_of(kt * kt_size + sub * sub_size, sub_size)
            blk0 = kt * blocks_per_tile + sub * (sub_size // SEL_BLOCK)
            bias = jnp.concatenate(
                [jnp.broadcast_to(sel_scr[pl.ds(blk0 + jb, 1), :], (SEL_BLOCK, Q_BLOCK))
                 for jb in range(sub_size // SEL_BLOCK)], axis=0)
            if causal:
                kpos = k0 + lax.broadcasted_iota(jnp.int32, (sub_size, 1), 0)
                bias = jnp.where(kpos <= tq1, bias, MASKED)
            k_aug = jnp.concatenate([bias.astype(BF16), ks_ref[pl.ds(k0, sub_size), :]], axis=1)
            scores.append((k0, _nt(k_aug, q_aug)))
        parts = []
        for k0, sc in scores:
            m_sub = jnp.max(sc, axis=0, keepdims=True)
            pe = jnp.exp2(sc - m_sub).astype(BF16)
            parts.append((m_sub, _dot(vst_ref[:, pl.ds(k0, sub_size)], pe)))
        m_new = m_run
        for m_sub, _ in parts:
            m_new = jnp.maximum(m_new, m_sub)
        acc = jnp.exp2(m_run - m_new) * acc
        for m_sub, acc_sub in parts:
            acc = acc + jnp.exp2(m_sub - m_new) * acc_sub
        return m_new, acc

    n_full = t0 // kt_size
    carry = lax.fori_loop(0, n_full, functools.partial(slc_tile, causal=False),
                          (jnp.full((1, width), MASKED, F32), jnp.zeros((V_ROWS, width), F32)))
    _, acc_slc = slc_tile(n_full, carry, causal=True)
    o_slc = acc_slc[:NSA_HEAD_DIM] / acc_slc[NSA_HEAD_DIM:NSA_HEAD_DIM + 1]

    span = WINDOW + Q_BLOCK
    k_start = pl.multiple_of(jnp.maximum(t0 - WINDOW, 0), Q_BLOCK)
    sw = _nt(kw_ref[pl.ds(k_start, span), :], q)
    diff = tq1 - (k_start + lax.broadcasted_iota(jnp.int32, (span, 1), 0))
    sw = jnp.where(heads((diff >= 0) & (diff < WINDOW)), sw, MASKED)
    pw = jnp.exp2(sw - jnp.max(sw, axis=0, keepdims=True)).astype(BF16)
    acc_win = _dot(vwt_ref[:, pl.ds(k_start, span)], pw)
    o_win = acc_win[:NSA_HEAD_DIM] / acc_win[NSA_HEAD_DIM:NSA_HEAD_DIM + 1]

    for hp in range(hpg):
        cols = slice(hp * Q_BLOCK, (hp + 1) * Q_BLOCK)
        row = (grp * hpg + hp) * 3
        o_t = (gt_ref[pl.ds(row, 1), :] * o_cmp[:, cols] + gt_ref[pl.ds(row + 1, 1), :] * o_slc[:, cols]
               + gt_ref[pl.ds(row + 2, 1), :] * o_win[:, cols])
        o_ref[:, hp * NSA_HEAD_DIM:(hp + 1) * NSA_HEAD_DIM] = o_t.T.astype(o_ref.dtype)


def _nsa_attention(qs, kcc, vcct, ks, vst, kw, vwt, gt, B, S, kt_size=2048, sub_size=256, cmp_chunk=256):
    G, Dh, nq, hpg = NSA_KV_GROUPS, NSA_HEAD_DIM, S // Q_BLOCK, NSA_HEADS_PER_GROUP
    kt_size = min(kt_size, S)
    n_cmp, n_sel = S // CMP_STRIDE, S // SEL_BLOCK
    cmp_chunk = min(cmp_chunk, n_cmp)
    per_group = lambda shape: pl.BlockSpec((None, None) + shape, lambda b, g, i: (b, g, 0, 0))
    return pl.pallas_call(
        functools.partial(_nsa_body, kt_size=kt_size, sub_size=min(sub_size, kt_size), cmp_chunk=cmp_chunk),
        grid=(B, G, nq),
        in_specs=[
            pl.BlockSpec((None, None, hpg * Q_BLOCK, Dh), lambda b, g, i: (b, g, i, 0)),
            per_group((n_cmp, Dh)), per_group((Dh, n_cmp)),
            per_group((S, Dh)), per_group((V_ROWS, S)),
            per_group((S, Dh)), per_group((V_ROWS, S)),
            pl.BlockSpec((None, None, 32, Q_BLOCK), lambda b, g, i: (b, i, 0, 0)),
        ],
        out_specs=pl.BlockSpec((None, Q_BLOCK, hpg * Dh), lambda b, g, i: (b, i, g)),
        out_shape=jax.ShapeDtypeStruct((B, S, NSA_HEADS * Dh), BF16),
        scratch_shapes=[
            pltpu.VMEM((n_cmp + 16, Q_BLOCK), F32),
            pltpu.VMEM((n_sel, Q_BLOCK), F32),
            pltpu.VMEM((n_cmp, hpg * Q_BLOCK), F32),
            pltpu.VMEM((max(n_cmp // cmp_chunk, 8), hpg * Q_BLOCK), F32),
        ],
        compiler_params=pltpu.CompilerParams(
            dimension_semantics=("parallel", "parallel", "arbitrary"), vmem_limit_bytes=VMEM_LIMIT),
        name="nsa_attention",
    )(qs, kcc, vcct, ks, vst, kw, vwt, gt)


def _merge_body(ya_ref, yb_ref, yc_ref, ga_ref, gb_ref, gc_ref, h_ref, wb_ref, wo_ref, o_ref):
    merged = jax.nn.sigmoid(ga_ref[...].astype(F32)) * _dot(ya_ref[...], wb_ref[0])
    merged = merged + jax.nn.sigmoid(gb_ref[...].astype(F32)) * _dot(yb_ref[...], wb_ref[1])
    merged = merged + jax.nn.sigmoid(gc_ref[...].astype(F32)) * _dot(yc_ref[...], wb_ref[2])
    o_ref[...] = h_ref[...] + _dot(merged.astype(BF16), wo_ref[...])


def _merge(ya, yb, yc, main, h, w_branch, w_out, tm=512):
    T, D = h.shape
    y_spec = pl.BlockSpec((tm, 512), lambda i: (i, 0))
    gate_spec = lambda k: pl.BlockSpec((tm, D), lambda i: (i, MAIN_GM // D + k))
    return pl.pallas_call(
        _merge_body,
        grid=(T // tm,),
        in_specs=[
            y_spec, y_spec, y_spec, gate_spec(0), gate_spec(1), gate_spec(2),
            pl.BlockSpec((tm, D), lambda i: (i, 0)),
            pl.BlockSpec((3, 512, D), lambda i: (0, 0, 0)),
            pl.BlockSpec((D, D), lambda i: (0, 0)),
        ],
        out_specs=pl.BlockSpec((tm, D), lambda i: (i, 0)),
        out_shape=jax.ShapeDtypeStruct((T, D), F32),
        compiler_params=pltpu.CompilerParams(dimension_semantics=("parallel",), vmem_limit_bytes=VMEM_LIMIT),
        name="merge_out",
    )(ya, yb, yc, main, main, main, h, w_branch, w_out)


def _ffn_body(h_ref, gf_ref, w1_ref, w2_ref, gp_ref, wg_ref, p_ref, wp_ref, gl_ref, o_ref, f_scr, acc_scr, *, final_norm):
    j = pl.program_id(1)

    @pl.when(j == 0)
    def _():
        f_scr[...] = _rms(h_ref[...], gf_ref[...]).astype(BF16)
        acc_scr[...] = jnp.zeros_like(acc_scr)

    a = jnp.maximum(_dot(f_scr[...], w1_ref[...]), 0.0)
    acc_scr[...] += _dot((a * a).astype(BF16), w2_ref[...])

    @pl.when(j == pl.num_programs(1) - 1)
    def _():
        h2 = h_ref[...] + acc_scr[...]
        gate = jax.nn.sigmoid(_dot(_rms(h2, gp_ref[...]).astype(BF16), wg_ref[...]))
        h3 = h2 + gate * _dot(p_ref[...].astype(BF16), wp_ref[...])
        o_ref[...] = _rms(h3, gl_ref[...]) if final_norm else h3


def _ffn_ple(h, norm_ffn, w1, w2, norm_ple, w_gate, p, w_proj, norm_last, final_norm, tm=512, tf=1024):
    T, D = h.shape
    Fdim = w1.shape[1]
    Pdim = p.shape[1]
    vec = pl.BlockSpec((1, D), lambda i, j: (0, 0))
    return pl.pallas_call(
        functools.partial(_ffn_body, final_norm=final_norm),
        grid=(T // tm, Fdim // tf),
        in_specs=[
            pl.BlockSpec((tm, D), lambda i, j: (i, 0)), vec,
            pl.BlockSpec((D, tf), lambda i, j: (0, j)),
            pl.BlockSpec((tf, D), lambda i, j: (j, 0)),
            vec,
            pl.BlockSpec((D, D), lambda i, j: (0, 0)),
            pl.BlockSpec((tm, Pdim), lambda i, j: (i, 0)),
            pl.BlockSpec((Pdim, D), lambda i, j: (0, 0)),
            vec,
        ],
        out_specs=pl.BlockSpec((tm, D), lambda i, j: (i, 0)),
        out_shape=jax.ShapeDtypeStruct((T, D), F32),
        scratch_shapes=[pltpu.VMEM((tm, D), BF16), pltpu.VMEM((tm, D), F32)],
        compiler_params=pltpu.CompilerParams(
            dimension_semantics=("parallel", "arbitrary"), vmem_limit_bytes=VMEM_LIMIT),
        name="ffn_ple",
    )(h, norm_ffn, w1, w2, norm_ple, w_gate, p, w_proj, norm_last)


def _split_w_in(w):
    parts, off = [], 0
    for sz in IN_SIZES:
        parts.append(w[:, off:off + sz])
        off += sz
    u, q_nsa, kv_nsa, g_nsa, q_gla, k_gla, v_gla, a_gla, r_gla, g_merge = parts
    D = w.shape[0]
    pad = lambda n: jnp.zeros((D, n), w.dtype)
    main = jnp.concatenate([u, q_gla, k_gla, v_gla, r_gla, g_merge, a_gla, pad(MAIN_WIDTH - MAIN_A - GLA_GATE_RANK)], axis=1)
    kv = [kv_nsa[:, s * LANES:(s + 1) * LANES] for s in range(6)]
    nsa = jnp.concatenate([q_nsa, kv[0], kv[2], kv[4], kv[1], kv[3], kv[5], g_nsa, pad(NSA_WIDTH - NSA_G - 24)], axis=1)
    return main.astype(BF16), nsa.astype(BF16)


def kernel(x, p, positions, norm_mix, w_in, pool_w, pool_scale, cmp_pos_k, cmp_w_k, cmp_pos_v, cmp_w_v, gla_w_gate, gla_b_gate, gla_norm, w_branch, w_out, norm_ffn, w_ff1, w_ff2, norm_ple, w_ple_gate, w_ple_proj, norm_final):
    B, S, D = x.shape
    depth = w_in.shape[0]
    T = B * S
    Dh = NSA_HEAD_DIM
    row = lambda v: v.reshape(1, -1).astype(F32)

    half = NSA_ROT_DIM // 2
    inv_freq = jnp.power(ROPE_THETA, -jnp.arange(half, dtype=F32) * (2.0 / NSA_ROT_DIM))
    d = jnp.arange(LANES) % Dh
    inv_freq_lanes = jnp.where(d < NSA_ROT_DIM, inv_freq[d % half], 0.0).reshape(1, LANES)
    pos3 = positions.reshape(B, S, 1)

    h = x.reshape(T, D)
    for i in range(depth):
        w_main, w_nsa = _split_w_in(w_in[i])
        main = _norm_proj(h, row(norm_mix[i]), w_main, BF16).reshape(B, S, MAIN_WIDTH)
        nsa_proj = _norm_proj(h, row(norm_mix[i]), w_nsa, F32).reshape(B, S, NSA_WIDTH)

        y_a = _pool_mixer(main, pool_w[i].astype(BF16), row(pool_scale[i]), B, S)

        wg = jnp.zeros((LANES, GLA_HEADS * GLA_DK), BF16).at[:GLA_GATE_RANK].set(gla_w_gate[i].astype(BF16))
        y_c = _gla_mixer(main, wg, row(gla_b_gate[i]), row(gla_norm[i]), B, S)

        qs, kc, vc, ks, kw, vst, vwt, gt = _nsa_prep(nsa_proj, pos3, inv_freq_lanes, B, S)
        kcc = _compress(kc, cmp_w_k[i].reshape(CMP_BLOCK, Dh, Dh).astype(BF16),
                        cmp_pos_k[i].reshape(CMP_BLOCK, 1, Dh).astype(F32), transpose_out=False)
        vcct = _compress(vc, cmp_w_v[i].reshape(CMP_BLOCK, Dh, Dh).astype(BF16),
                         cmp_pos_v[i].reshape(CMP_BLOCK, 1, Dh).astype(F32), transpose_out=True)
        y_b = _nsa_attention(qs, kcc, vcct, ks, vst, kw, vwt, gt, B, S)

        h = _merge(y_a.reshape(T, -1), y_b.reshape(T, -1), y_c.reshape(T, -1), main.reshape(T, MAIN_WIDTH), h,
                   w_branch[i].astype(BF16), w_out[i].astype(BF16))
        h = _ffn_ple(h, row(norm_ffn[i]), w_ff1[i].astype(BF16), w_ff2[i].astype(BF16), row(norm_ple[i]),
                     w_ple_gate[i].astype(BF16), p[i].reshape(T, -1), w_ple_proj[i].astype(BF16),
                     row(norm_final), final_norm=(i == depth - 1))
    return h.reshape(B, S, D)
```

```python
import functools

import jax
import jax.numpy as jnp
from jax import lax
from jax.experimental import pallas as pl
from jax.experimental.pallas import tpu as pltpu

F32, BF16 = jnp.float32, jnp.bfloat16

EPS = 1e-6
ROPE_THETA = 500000.0
POOL_WINDOWS = (2, 4, 8, 16)
POOL_GROUP_DIM = 128
NSA_HEADS = 8
NSA_KV_GROUPS = 2
NSA_HEADS_PER_GROUP = NSA_HEADS // NSA_KV_GROUPS
NSA_HEAD_DIM = 64
NSA_ROT_DIM = 16
CMP_BLOCK = 32
CMP_STRIDE = 16
SEL_BLOCK = 64
SEL_TOPK = 16
WINDOW = 512
Q_BLOCK = 128
GLA_HEADS = 4
GLA_DK = 64
GLA_DV = 128
GLA_GATE_RANK = 16
GLA_TAU = 16.0
GLA_CHUNK = 64
IN_SIZES = (512, 512, 768, 24, 256, 256, 512, 16, 512, 3072)

LANES = 128
MASKED = -1e30
LOG2E = 1.4426950408889634
Q_SCALE = NSA_HEAD_DIM ** -0.5 * LOG2E
V_ROWS = NSA_HEAD_DIM + 16
VMEM_LIMIT = 56 * 1024 * 1024

MAIN_U, MAIN_QG, MAIN_KG, MAIN_VG, MAIN_RG, MAIN_GM, MAIN_A, MAIN_WIDTH = 0, 512, 768, 1024, 1536, 2048, 5120, 5376
NSA_Q, NSA_KC, NSA_KS, NSA_KW, NSA_VC, NSA_VS, NSA_VW, NSA_G, NSA_WIDTH = 0, 512, 640, 768, 896, 1024, 1152, 1280, 1536


def _nt(a, b):
    return lax.dot_general(a, b, (((1,), (1,)), ((), ())), preferred_element_type=F32)


def _tn(a, b):
    return lax.dot_general(a, b, (((0,), (0,)), ((), ())), preferred_element_type=F32)


def _dot(a, b):
    return jnp.dot(a, b, preferred_element_type=F32)


def _rms(x, gain):
    return x * lax.rsqrt(jnp.mean(x * x, axis=-1, keepdims=True) + EPS) * gain


def _proj_body(h_ref, g_ref, w_ref, o_ref, a_scr):
    @pl.when(pl.program_id(1) == 0)
    def _():
        a_scr[...] = _rms(h_ref[...], g_ref[...]).astype(BF16)

    o_ref[...] = _dot(a_scr[...], w_ref[...]).astype(o_ref.dtype)


def _norm_proj(h, gain, w, out_dtype, tn, tm=1024):
    T, D = h.shape
    N = w.shape[1]
    tm = min(tm, T)
    return pl.pallas_call(
        _proj_body,
        grid=(T // tm, N // tn),
        in_specs=[
            pl.BlockSpec((tm, D), lambda i, j: (i, 0)),
            pl.BlockSpec((1, D), lambda i, j: (0, 0)),
            pl.BlockSpec((D, tn), lambda i, j: (0, j)),
        ],
        out_specs=pl.BlockSpec((tm, tn), lambda i, j: (i, j)),
        out_shape=jax.ShapeDtypeStruct((T, N), out_dtype),
        scratch_shapes=[pltpu.VMEM((tm, D), BF16)],
        compiler_params=pltpu.CompilerParams(
            dimension_semantics=("parallel", "arbitrary"), vmem_limit_bytes=VMEM_LIMIT),
        name="norm_proj",
    )(h, gain, w)


def _pool_body(u_ref, halo_ref, w_ref, sc_ref, o_ref, *, ts):
    i = pl.program_id(1)
    cur = u_ref[...].astype(F32)
    halo = jnp.where(i == 0, 0.0, halo_ref[...].astype(F32))
    t = i * ts + lax.broadcasted_iota(jnp.int32, (ts, 1), 0)
    for g, win in enumerate(POOL_WINDOWS):
        lo, hi = g * POOL_GROUP_DIM, (g + 1) * POOL_GROUP_DIM
        x = cur[:, lo:hi]
        e = jnp.concatenate([halo[:, lo:hi], x], axis=0)
        step = 1
        while step < win:
            e = e[step:] + e[:-step]
            step *= 2
        wsum = e[16 - (win - 1):]
        cnt = jnp.minimum(t + 1, win).astype(F32)
        pooled = wsum / cnt - x
        y = _dot(pooled.astype(BF16), w_ref[g]) * sc_ref[:, lo:hi]
        o_ref[:, lo:hi] = y.astype(o_ref.dtype)


def _pool_mixer(main, pool_w, scale, B, S, ts=512):
    ts = min(ts, S)
    halo_blocks = ts // 16
    return pl.pallas_call(
        functools.partial(_pool_body, ts=ts),
        grid=(B, S // ts),
        in_specs=[
            pl.BlockSpec((None, ts, 512), lambda b, i: (b, i, MAIN_U // 512)),
            pl.BlockSpec((None, 16, 512), lambda b, i: (b, jnp.maximum(i * halo_blocks - 1, 0), MAIN_U // 512)),
            pl.BlockSpec((4, 128, 128), lambda b, i: (0, 0, 0)),
            pl.BlockSpec((1, 512), lambda b, i: (0, 0)),
        ],
        out_specs=pl.BlockSpec((None, ts, 512), lambda b, i: (b, i, 0)),
        out_shape=jax.ShapeDtypeStruct((B, S, 512), BF16),
        compiler_params=pltpu.CompilerParams(dimension_semantics=("parallel", "arbitrary")),
        name="pool_mixer",
    )(main, main, pool_w, scale)


def _gla_body(q_ref, k_ref, v_ref, r_ref, a_ref, wg_ref, bg_ref, ng_ref, o_ref, st_ref, *, tc):
    @pl.when(pl.program_id(1) == 0)
    def _():
        st_ref[...] = jnp.zeros_like(st_ref)

    C = GLA_CHUNK
    n_chunks = tc // C
    causal = lax.broadcasted_iota(jnp.int32, (C, C), 0) >= lax.broadcasted_iota(jnp.int32, (C, C), 1)
    width = GLA_HEADS * GLA_DK
    hk = [slice(h * GLA_DK, (h + 1) * GLA_DK) for h in range(GLA_HEADS)]
    hv = [slice(h * GLA_DV, (h + 1) * GLA_DV) for h in range(GLA_HEADS)]
    rows = [slice(c * C, (c + 1) * C) for c in range(n_chunks)]

    z = _dot(a_ref[...], wg_ref[...]) + bg_ref[...]
    b = jax.nn.log_sigmoid(z) * (1.0 / GLA_TAU)
    row_in_chunk = lax.broadcasted_iota(jnp.int32, (tc, 1), 0) % C
    step = 1
    while step < C:
        shifted = jnp.concatenate([jnp.zeros((step, width), F32), b[:-step]], axis=0)
        b = b + jnp.where(row_in_chunk >= step, shifted, 0.0)
        step *= 2
    b_last = [b[(c + 1) * C - 1:(c + 1) * C, :] for c in range(n_chunks)]
    b_last_rows = jnp.concatenate([jnp.broadcast_to(bl, (C, width)) for bl in b_last], axis=0)
    qf = q_ref[...].astype(F32) * (GLA_DK ** -0.5)
    kf = k_ref[...].astype(F32)
    q_s = (qf * jnp.exp(b)).astype(BF16)
    k_s = (kf * jnp.exp(-b)).astype(BF16)
    k_t = (kf * jnp.exp(b_last_rows - b)).astype(BF16)

    att = [[jnp.where(causal, _nt(q_s[rows[c], hk[h]], k_s[rows[c], hk[h]]), 0.0).astype(BF16)
            for h in range(GLA_HEADS)] for c in range(n_chunks)]
    o_intra = [[_dot(att[c][h], v_ref[rows[c], hv[h]]) for h in range(GLA_HEADS)] for c in range(n_chunks)]
    kv = [[_tn(v_ref[rows[c], hv[h]], k_t[rows[c], hk[h]]) for h in range(GLA_HEADS)] for c in range(n_chunks)]

    state_t = [st_ref[h] for h in range(GLA_HEADS)]
    for c in range(n_chunks):
        decay = jnp.exp(b_last[c])
        for h in range(GLA_HEADS):
            o = o_intra[c][h] + _nt(q_s[rows[c], hk[h]], state_t[h].astype(BF16))
            state_t[h] = state_t[h] * decay[:, hk[h]] + kv[c][h]
            o = _rms(o, ng_ref[...])
            r = r_ref[rows[c], hv[h]].astype(F32)
            o_ref[rows[c], hv[h]] = (o * (r * jax.nn.sigmoid(r))).astype(o_ref.dtype)
    for h in range(GLA_HEADS):
        st_ref[h] = state_t[h]


def _gla_mixer(main, w_gate, b_gate, norm_g, B, S, tc=256):
    tc = min(tc, S)
    return pl.pallas_call(
        functools.partial(_gla_body, tc=tc),
        grid=(B, S // tc),
        in_specs=[
            pl.BlockSpec((None, tc, 256), lambda b, i: (b, i, MAIN_QG // 256)),
            pl.BlockSpec((None, tc, 256), lambda b, i: (b, i, MAIN_KG // 256)),
            pl.BlockSpec((None, tc, 512), lambda b, i: (b, i, MAIN_VG // 512)),
            pl.BlockSpec((None, tc, 512), lambda b, i: (b, i, MAIN_RG // 512)),
            pl.BlockSpec((None, tc, 128), lambda b, i: (b, i, MAIN_A // 128)),
            pl.BlockSpec((128, 256), lambda b, i: (0, 0)),
            pl.BlockSpec((1, 256), lambda b, i: (0, 0)),
            pl.BlockSpec((1, 128), lambda b, i: (0, 0)),
        ],
        out_specs=pl.BlockSpec((None, tc, 512), lambda b, i: (b, i, 0)),
        out_shape=jax.ShapeDtypeStruct((B, S, 512), BF16),
        scratch_shapes=[pltpu.VMEM((GLA_HEADS, GLA_DV, GLA_DK), F32)],
        compiler_params=pltpu.CompilerParams(dimension_semantics=("parallel", "arbitrary")),
        name="gla_mixer",
    )(main, main, main, main, main, w_gate, b_gate, norm_g)


def _nsa_prep_body(x_ref, pos_ref, invf_ref, qs_ref, kc_ref, vc_ref, ks_ref, kw_ref, vst_ref, vwt_ref, gt_ref):
    ang = pos_ref[...].astype(F32) * invf_ref[...]
    cs, sn = jnp.cos(ang), jnp.sin(ang)
    d = lax.broadcasted_iota(jnp.int32, (1, LANES), 1) % NSA_HEAD_DIM
    half = NSA_ROT_DIM // 2
    s_lo = jnp.where(d < half, -sn, 0.0)
    s_hi = jnp.where((d >= half) & (d < NSA_ROT_DIM), sn, 0.0)

    def rope(x):
        return x * cs + pltpu.roll(x, LANES - half, 1) * s_lo + pltpu.roll(x, half, 1) * s_hi

    def split_groups(x, ref, dtype):
        ref[0] = x[:, :NSA_HEAD_DIM].astype(dtype)
        ref[1] = x[:, NSA_HEAD_DIM:].astype(dtype)

    for pair in range(NSA_HEADS // 2):
        qr = rope(x_ref[:, NSA_Q + pair * LANES:NSA_Q + (pair + 1) * LANES]) * Q_SCALE
        for sub in range(2):
            head = 2 * pair + sub
            grp, hp = head // NSA_HEADS_PER_GROUP, head % NSA_HEADS_PER_GROUP
            qs_ref[grp, hp * Q_BLOCK:(hp + 1) * Q_BLOCK, :] = (
                qr[:, sub * NSA_HEAD_DIM:(sub + 1) * NSA_HEAD_DIM].astype(BF16))
    split_groups(rope(x_ref[:, NSA_KC:NSA_KC + LANES]), kc_ref, F32)
    split_groups(rope(x_ref[:, NSA_KS:NSA_KS + LANES]), ks_ref, BF16)
    split_groups(rope(x_ref[:, NSA_KW:NSA_KW + LANES]), kw_ref, BF16)
    split_groups(x_ref[:, NSA_VC:NSA_VC + LANES], vc_ref, F32)
    vst = x_ref[:, NSA_VS:NSA_VS + LANES].T
    vwt = x_ref[:, NSA_VW:NSA_VW + LANES].T
    ones_row = (lax.broadcasted_iota(jnp.int32, (V_ROWS - NSA_HEAD_DIM, Q_BLOCK), 0) == 0).astype(BF16)
    for grp in range(NSA_KV_GROUPS):
        for ref, v in ((vst_ref, vst), (vwt_ref, vwt)):
            ref[grp, :NSA_HEAD_DIM, :] = v[grp * NSA_HEAD_DIM:(grp + 1) * NSA_HEAD_DIM].astype(BF16)
            ref[grp, NSA_HEAD_DIM:, :] = ones_row
    gt_ref[...] = jax.nn.sigmoid(x_ref[:, NSA_G:NSA_G + LANES]).T[:32]


def _nsa_prep(nsa_proj, positions, inv_freq_lanes, B, S):
    G, Dh, nq = NSA_KV_GROUPS, NSA_HEAD_DIM, S // Q_BLOCK
    tok = lambda dtype: jax.ShapeDtypeStruct((B, G, S, Dh), dtype)
    tok_spec = pl.BlockSpec((None, G, Q_BLOCK, Dh), lambda b, i: (b, 0, i, 0))
    tr = jax.ShapeDtypeStruct((B, G, V_ROWS, S), BF16)
    tr_spec = pl.BlockSpec((None, G, V_ROWS, Q_BLOCK), lambda b, i: (b, 0, 0, i))
    return pl.pallas_call(
        _nsa_prep_body,
        grid=(B, nq),
        in_specs=[
            pl.BlockSpec((None, Q_BLOCK, NSA_WIDTH), lambda b, i: (b, i, 0)),
            pl.BlockSpec((None, Q_BLOCK, 1), lambda b, i: (b, i, 0)),
            pl.BlockSpec((1, LANES), lambda b, i: (0, 0)),
        ],
        out_specs=[
            pl.BlockSpec((None, G, NSA_HEADS_PER_GROUP * Q_BLOCK, Dh), lambda b, i: (b, 0, i, 0)),
            tok_spec, tok_spec, tok_spec, tok_spec, tr_spec, tr_spec,
            pl.BlockSpec((None, None, 32, Q_BLOCK), lambda b, i: (b, i, 0, 0)),
        ],
        out_shape=[
            jax.ShapeDtypeStruct((B, G, NSA_HEADS_PER_GROUP * S, Dh), BF16),
            tok(F32), tok(F32), tok(BF16), tok(BF16), tr, tr,
            jax.ShapeDtypeStruct((B, nq, 32, Q_BLOCK), F32),
        ],
        compiler_params=pltpu.CompilerParams(dimension_semantics=("parallel", "parallel")),
        name="nsa_prep",
    )(nsa_proj, positions, inv_freq_lanes)


def _cmp_body(x_ref, w_ref, p_ref, o_ref, *, n_blk, transpose_out):
    half = CMP_BLOCK // 2
    first = jnp.zeros((n_blk, NSA_HEAD_DIM), F32)
    second = jnp.zeros((n_blk, NSA_HEAD_DIM), F32)
    for l in range(half):
        rows = x_ref[pl.ds(l, n_blk, stride=CMP_STRIDE), :]
        first = first + _dot((rows + p_ref[l]).astype(BF16), w_ref[l])
        second = second + _dot((rows + p_ref[half + l]).astype(BF16), w_ref[half + l])
    out = first + jnp.concatenate([second[1:], jnp.zeros((1, NSA_HEAD_DIM), F32)], axis=0)
    o_ref[...] = (out.T if transpose_out else out).astype(o_ref.dtype)


def _compress(x, w, pos, transpose_out):
    B, G, S, Dh = x.shape
    n_blk = S // CMP_STRIDE
    out_block = (None, None, Dh, n_blk) if transpose_out else (None, None, n_blk, Dh)
    out_shape = (B, G, Dh, n_blk) if transpose_out else (B, G, n_blk, Dh)
    return pl.pallas_call(
        functools.partial(_cmp_body, n_blk=n_blk, transpose_out=transpose_out),
        grid=(B, G),
        in_specs=[
            pl.BlockSpec((None, None, S, Dh), lambda b, g: (b, g, 0, 0)),
            pl.BlockSpec((CMP_BLOCK, Dh, Dh), lambda b, g: (0, 0, 0)),
            pl.BlockSpec((CMP_BLOCK, 1, Dh), lambda b, g: (0, 0, 0)),
        ],
        out_specs=pl.BlockSpec(out_block, lambda b, g: (b, g, 0, 0)),
        out_shape=jax.ShapeDtypeStruct(out_shape, BF16),
        compiler_params=pltpu.CompilerParams(
            dimension_semantics=("parallel", "parallel"), vmem_limit_bytes=VMEM_LIMIT),
        name="nsa_compress",
    )(x, w, pos)


def _nsa_body(q_ref, kcc_ref, vcct_ref, ks_ref, vst_ref, kw_ref, vwt_ref, gt_ref, o_ref,
              imp_scr, sel_scr, e_scr, mrow_scr, *, kt_size, sub_size, cmp_chunk):
    grp = pl.program_id(1)
    t0 = pl.program_id(2) * Q_BLOCK
    hpg = NSA_HEADS_PER_GROUP
    n_cmp = kcc_ref.shape[0]
    n_sel = sel_scr.shape[0]
    q = q_ref[...]
    tq1 = t0 + lax.broadcasted_iota(jnp.int32, (1, Q_BLOCK), 1)
    heads = lambda x: jnp.concatenate([x] * hpg, axis=1)

    width = hpg * Q_BLOCK
    neg_inf = -jnp.inf

    ch = cmp_chunk
    n_chunks = (((t0 + Q_BLOCK - CMP_BLOCK) >> (CMP_STRIDE.bit_length() - 1)) + ch) // ch

    def cmp_step(c, carry):
        m_run, l_run, acc = carry
        r0 = pl.multiple_of(c * ch, ch)
        s = _nt(kcc_ref[pl.ds(r0, ch), :], q)
        cmp_end = (r0 + lax.broadcasted_iota(jnp.int32, (ch, 1), 0)) * CMP_STRIDE + (CMP_BLOCK - 1)
        s = jnp.where(heads(cmp_end <= tq1), s, neg_inf)
        m_new = jnp.maximum(m_run, jnp.max(s, axis=0, keepdims=True))
        m_safe = jnp.where(m_new == neg_inf, 0.0, m_new)
        alpha = jnp.exp2(jnp.where(m_run == neg_inf, 0.0, m_run) - m_safe)
        e = jnp.exp2(s - m_safe)
        e_scr[pl.ds(r0, ch), :] = e
        mrow_scr[pl.ds(c, 1), :] = m_safe
        l_new = alpha * l_run + jnp.sum(e, axis=0, keepdims=True)
        return m_new, l_new, alpha * acc + _dot(vcct_ref[:, pl.ds(r0, ch)], e.astype(BF16))

    m_cmp, l_cmp, acc_cmp = lax.fori_loop(
        0, n_chunks, cmp_step,
        (jnp.full((1, width), neg_inf, F32), jnp.zeros((1, width), F32), jnp.zeros((NSA_HEAD_DIM, width), F32)))
    inv_l = 1.0 / jnp.maximum(l_cmp, 1e-30)
    o_cmp = acc_cmp * inv_l
    m_cmp = jnp.where(m_cmp == neg_inf, 0.0, m_cmp)

    imp_scr[...] = jnp.zeros(imp_scr.shape, F32)

    def imp_step(c, _):
        r0 = pl.multiple_of(c * ch, ch)
        pc = e_scr[pl.ds(r0, ch), :] * (jnp.exp2(mrow_scr[pl.ds(c, 1), :] - m_cmp) * inv_l)
        p_sum = pc[:, :Q_BLOCK]
        for hp in range(1, hpg):
            p_sum = p_sum + pc[:, hp * Q_BLOCK:(hp + 1) * Q_BLOCK]
        imp_scr[pl.ds(pl.multiple_of(r0 + 8, 8), ch), :] = p_sum
        return 0

    lax.fori_loop(0, n_chunks, imp_step, 0)
    ratio = SEL_BLOCK // CMP_STRIDE
    imp = imp_scr[pl.ds(7, n_sel, stride=ratio), :]
    for k in range(1, ratio + 1):
        imp = imp + imp_scr[pl.ds(7 + k, n_sel, stride=ratio), :]
    j = lax.broadcasted_iota(jnp.int32, (n_sel, Q_BLOCK), 0)
    cur = tq1 >> (SEL_BLOCK.bit_length() - 1)
    forced = (j == 0) | (j == cur) | (j == cur - 1)
    valid = j <= cur

    def pick(_, sc):
        best = jnp.max(sc, axis=0, keepdims=True)
        idx = jnp.min(jnp.where(sc == best, j, n_sel), axis=0, keepdims=True)
        return jnp.where(j == idx, neg_inf, sc)

    n_forced = 3
    left = lax.fori_loop(0, max(min(SEL_TOPK, n_sel) - n_forced, 0), pick,
                         jnp.where(valid & ~forced, imp, neg_inf), unroll=True)
    sel_scr[...] = jnp.where(forced | (valid & (left == neg_inf)), 0.0, MASKED)

    span = WINDOW + Q_BLOCK
    k_start = pl.multiple_of(jnp.maximum(t0 - WINDOW, 0), Q_BLOCK)
    sw = _nt(kw_ref[pl.ds(k_start, span), :], q)
    diff = tq1 - (k_start + lax.broadcasted_iota(jnp.int32, (span, 1), 0))
    sw = jnp.where(heads((diff >= 0) & (diff < WINDOW)), sw, MASKED)
    pw = jnp.exp2(sw - jnp.max(sw, axis=0, keepdims=True)).astype(BF16)
    acc_win = _dot(vwt_ref[:, pl.ds(k_start, span)], pw)
    o_win = acc_win[:NSA_HEAD_DIM] / acc_win[NSA_HEAD_DIM:NSA_HEAD_DIM + 1]

    blocks_per_tile = kt_size // SEL_BLOCK
    eye = (lax.broadcasted_iota(jnp.int32, (Q_BLOCK, Q_BLOCK), 0)
           == lax.broadcasted_iota(jnp.int32, (Q_BLOCK, Q_BLOCK), 1)).astype(BF16)
    q_aug = jnp.concatenate([jnp.concatenate([eye] * hpg, axis=0), q], axis=1)

    def slc_tile(kt, carry, causal):
        m_run, acc = carry
        scores = []
        for sub in range(kt_size // sub_size):
            k0 = pl.multiple_of(kt * kt_size + sub * sub_size, sub_size)
            blk0 = kt * blocks_per_tile + sub * (sub_size // SEL_BLOCK)
            bias = jnp.concatenate(
                [jnp.broadcast_to(sel_scr[pl.ds(blk0 + jb, 1), :], (SEL_BLOCK, Q_BLOCK))
                 for jb in range(sub_size // SEL_BLOCK)], axis=0)
            if causal:
                kpos = k0 + lax.broadcasted_iota(jnp.int32, (sub_size, 1), 0)
                bias = jnp.where(kpos <= tq1, bias, MASKED)
            k_aug = jnp.concatenate([bias.astype(BF16), ks_ref[pl.ds(k0, sub_size), :]], axis=1)
            scores.append((k0, _nt(k_aug, q_aug)))
        parts = []
        for k0, sc in scores:
            m_sub = jnp.max(sc, axis=0, keepdims=True)
            pe = jnp.exp2(sc - m_sub).astype(BF16)
            parts.append((m_sub, _dot(vst_ref[:, pl.ds(k0, sub_size)], pe)))
        m_new = m_run
        for m_sub, _ in parts:
            m_new = jnp.maximum(m_new, m_sub)
        acc = jnp.exp2(m_run - m_new) * acc
        for m_sub, acc_sub in parts:
            acc = acc + jnp.exp2(m_sub - m_new) * acc_sub
        return m_new, acc

    n_full = t0 // kt_size
    carry = lax.fori_loop(0, n_full, functools.partial(slc_tile, causal=False),
                          (jnp.full((1, width), MASKED, F32), jnp.zeros((V_ROWS, width), F32)))
    _, acc_slc = slc_tile(n_full, carry, causal=True)
    o_slc = acc_slc[:NSA_HEAD_DIM] / acc_slc[NSA_HEAD_DIM:NSA_HEAD_DIM + 1]

    for hp in range(hpg):
        cols = slice(hp * Q_BLOCK, (hp + 1) * Q_BLOCK)
        row = (grp * hpg + hp) * 3
        o_t = (gt_ref[pl.ds(row, 1), :] * o_cmp[:, cols] + gt_ref[pl.ds(row + 1, 1), :] * o_slc[:, cols]
               + gt_ref[pl.ds(row + 2, 1), :] * o_win[:, cols])
        o_ref[:, hp * NSA_HEAD_DIM:(hp + 1) * NSA_HEAD_DIM] = o_t.T.astype(o_ref.dtype)


def _nsa_attention(qs, kcc, vcct, ks, vst, kw, vwt, gt, B, S, kt_size=2048, sub_size=256, cmp_chunk=256):
    G, Dh, nq, hpg = NSA_KV_GROUPS, NSA_HEAD_DIM, S // Q_BLOCK, NSA_HEADS_PER_GROUP
    kt_size = min(kt_size, S)
    n_cmp, n_sel = S // CMP_STRIDE, S // SEL_BLOCK
    cmp_chunk = min(cmp_chunk, n_cmp)
    per_group = lambda shape: pl.BlockSpec((None, None) + shape, lambda b, g, i: (b, g, 0, 0))
    return pl.pallas_call(
        functools.partial(_nsa_body, kt_size=kt_size, sub_size=min(sub_size, kt_size), cmp_chunk=cmp_chunk),
        grid=(B, G, nq),
        in_specs=[
            pl.BlockSpec((None, None, hpg * Q_BLOCK, Dh), lambda b, g, i: (b, g, i, 0)),
            per_group((n_cmp, Dh)), per_group((Dh, n_cmp)),
            per_group((S, Dh)), per_group((V_ROWS, S)),
            per_group((S, Dh)), per_group((V_ROWS, S)),
            pl.BlockSpec((None, None, 32, Q_BLOCK), lambda b, g, i: (b, i, 0, 0)),
        ],
        out_specs=pl.BlockSpec((None, Q_BLOCK, hpg * Dh), lambda b, g, i: (b, i, g)),
        out_shape=jax.ShapeDtypeStruct((B, S, NSA_HEADS * Dh), BF16),
        scratch_shapes=[
            pltpu.VMEM((n_cmp + 16, Q_BLOCK), F32),
            pltpu.VMEM((n_sel, Q_BLOCK), F32),
            pltpu.VMEM((n_cmp, hpg * Q_BLOCK), F32),
            pltpu.VMEM((max(n_cmp // cmp_chunk, 8), hpg * Q_BLOCK), F32),
        ],
        compiler_params=pltpu.CompilerParams(
            dimension_semantics=("parallel", "parallel", "arbitrary"), vmem_limit_bytes=VMEM_LIMIT),
        name="nsa_attention",
    )(qs, kcc, vcct, ks, vst, kw, vwt, gt)


def _merge_body(ya_ref, yb_ref, yc_ref, ga_ref, gb_ref, gc_ref, h_ref, wb_ref, wo_ref, o_ref):
    merged = jax.nn.sigmoid(ga_ref[...].astype(F32)) * _dot(ya_ref[...], wb_ref[0])
    merged = merged + jax.nn.sigmoid(gb_ref[...].astype(F32)) * _dot(yb_ref[...], wb_ref[1])
    merged = merged + jax.nn.sigmoid(gc_ref[...].astype(F32)) * _dot(yc_ref[...], wb_ref[2])
    o_ref[...] = h_ref[...] + _dot(merged.astype(BF16), wo_ref[...])


def _merge(ya, yb, yc, main, h, w_branch, w_out, tm=512):
    T, D = h.shape
    y_spec = pl.BlockSpec((tm, 512), lambda i: (i, 0))
    gate_spec = lambda k: pl.BlockSpec((tm, D), lambda i: (i, MAIN_GM // D + k))
    return pl.pallas_call(
        _merge_body,
        grid=(T // tm,),
        in_specs=[
            y_spec, y_spec, y_spec, gate_spec(0), gate_spec(1), gate_spec(2),
            pl.BlockSpec((tm, D), lambda i: (i, 0)),
            pl.BlockSpec((3, 512, D), lambda i: (0, 0, 0)),
            pl.BlockSpec((D, D), lambda i: (0, 0)),
        ],
        out_specs=pl.BlockSpec((tm, D), lambda i: (i, 0)),
        out_shape=jax.ShapeDtypeStruct((T, D), F32),
        compiler_params=pltpu.CompilerParams(dimension_semantics=("parallel",), vmem_limit_bytes=VMEM_LIMIT),
        name="merge_out",
    )(ya, yb, yc, main, main, main, h, w_branch, w_out)


def _ffn_body(h_ref, gf_ref, w1_ref, w2_ref, gp_ref, wg_ref, p_ref, wp_ref, gl_ref, o_ref, f_scr, acc_scr, *, final_norm):
    j = pl.program_id(1)

    @pl.when(j == 0)
    def _():
        f_scr[...] = _rms(h_ref[...], gf_ref[...]).astype(BF16)
        acc_scr[...] = jnp.zeros_like(acc_scr)

    a = jnp.maximum(_dot(f_scr[...], w1_ref[...]), 0.0)
    acc_scr[...] += _dot((a * a).astype(BF16), w2_ref[...])

    @pl.when(j == pl.num_programs(1) - 1)
    def _():
        h2 = h_ref[...] + acc_scr[...]
        gate = jax.nn.sigmoid(_dot(_rms(h2, gp_ref[...]).astype(BF16), wg_ref[...]))
        h3 = h2 + gate * _dot(p_ref[...].astype(BF16), wp_ref[...])
        o_ref[...] = _rms(h3, gl_ref[...]) if final_norm else h3


def _ffn_ple(h, norm_ffn, w1, w2, norm_ple, w_gate, p, w_proj, norm_last, final_norm, tm=1024, tf=512):
    T, D = h.shape
    tm = min(tm, T)
    Fdim = w1.shape[1]
    Pdim = p.shape[1]
    vec = pl.BlockSpec((1, D), lambda i, j: (0, 0))
    return pl.pallas_call(
        functools.partial(_ffn_body, final_norm=final_norm),
        grid=(T // tm, Fdim // tf),
        in_specs=[
            pl.BlockSpec((tm, D), lambda i, j: (i, 0)), vec,
            pl.BlockSpec((D, tf), lambda i, j: (0, j)),
            pl.BlockSpec((tf, D), lambda i, j: (j, 0)),
            vec,
            pl.BlockSpec((D, D), lambda i, j: (0, 0)),
            pl.BlockSpec((tm, Pdim), lambda i, j: (i, 0)),
            pl.BlockSpec((Pdim, D), lambda i, j: (0, 0)),
            vec,
        ],
        out_specs=pl.BlockSpec((tm, D), lambda i, j: (i, 0)),
        out_shape=jax.ShapeDtypeStruct((T, D), F32),
        scratch_shapes=[pltpu.VMEM((tm, D), BF16), pltpu.VMEM((tm, D), F32)],
        compiler_params=pltpu.CompilerParams(
            dimension_semantics=("parallel", "arbitrary"), vmem_limit_bytes=VMEM_LIMIT),
        name="ffn_ple",
    )(h, norm_ffn, w1, w2, norm_ple, w_gate, p, w_proj, norm_last)


def _split_w_in(w):
    parts, off = [], 0
    for sz in IN_SIZES:
        parts.append(w[:, off:off + sz])
        off += sz
    u, q_nsa, kv_nsa, g_nsa, q_gla, k_gla, v_gla, a_gla, r_gla, g_merge = parts
    D = w.shape[0]
    pad = lambda n: jnp.zeros((D, n), w.dtype)
    main = jnp.concatenate([u, q_gla, k_gla, v_gla, r_gla, g_merge, a_gla, pad(MAIN_WIDTH - MAIN_A - GLA_GATE_RANK)], axis=1)
    kv = [kv_nsa[:, s * LANES:(s + 1) * LANES] for s in range(6)]
    nsa = jnp.concatenate([q_nsa, kv[0], kv[2], kv[4], kv[1], kv[3], kv[5], g_nsa, pad(NSA_WIDTH - NSA_G - 24)], axis=1)
    return main.astype(BF16), nsa.astype(BF16)


def kernel(x, p, positions, norm_mix, w_in, pool_w, pool_scale, cmp_pos_k, cmp_w_k, cmp_pos_v, cmp_w_v, gla_w_gate, gla_b_gate, gla_norm, w_branch, w_out, norm_ffn, w_ff1, w_ff2, norm_ple, w_ple_gate, w_ple_proj, norm_final):
    B, S, D = x.shape
    depth = w_in.shape[0]
    T = B * S
    Dh = NSA_HEAD_DIM
    row = lambda v: v.reshape(1, -1).astype(F32)

    half = NSA_ROT_DIM // 2
    inv_freq = jnp.power(ROPE_THETA, -jnp.arange(half, dtype=F32) * (2.0 / NSA_ROT_DIM))
    d = jnp.arange(LANES) % Dh
    inv_freq_lanes = jnp.where(d < NSA_ROT_DIM, inv_freq[d % half], 0.0).reshape(1, LANES)
    pos3 = positions.reshape(B, S, 1)

    h = x.reshape(T, D)
    for i in range(depth):
        w_main, w_nsa = _split_w_in(w_in[i])
        main = _norm_proj(h, row(norm_mix[i]), w_main, BF16, tn=MAIN_WIDTH // 3).reshape(B, S, MAIN_WIDTH)
        nsa_proj = _norm_proj(h, row(norm_mix[i]), w_nsa, F32, tn=NSA_WIDTH).reshape(B, S, NSA_WIDTH)

        y_a = _pool_mixer(main, pool_w[i].astype(BF16), row(pool_scale[i]), B, S)

        wg = jnp.zeros((LANES, GLA_HEADS * GLA_DK), BF16).at[:GLA_GATE_RANK].set(gla_w_gate[i].astype(BF16))
        y_c = _gla_mixer(main, wg, row(gla_b_gate[i]), row(gla_norm[i]), B, S)

        qs, kc, vc, ks, kw, vst, vwt, gt = _nsa_prep(nsa_proj, pos3, inv_freq_lanes, B, S)
        kcc = _compress(kc, cmp_w_k[i].reshape(CMP_BLOCK, Dh, Dh).astype(BF16),
                        cmp_pos_k[i].reshape(CMP_BLOCK, 1, Dh).astype(F32), transpose_out=False)
        vcct = _compress(vc, cmp_w_v[i].reshape(CMP_BLOCK, Dh, Dh).astype(BF16),
                         cmp_pos_v[i].reshape(CMP_BLOCK, 1, Dh).astype(F32), transpose_out=True)
        y_b = _nsa_attention(qs, kcc, vcct, ks, vst, kw, vwt, gt, B, S)

        h = _merge(y_a.reshape(T, -1), y_b.reshape(T, -1), y_c.reshape(T, -1), main.reshape(T, MAIN_WIDTH), h,
                   w_branch[i].astype(BF16), w_out[i].astype(BF16))
        h = _ffn_ple(h, row(norm_ffn[i]), w_ff1[i].astype(BF16), w_ff2[i].astype(BF16), row(norm_ple[i]),
                     w_ple_gate[i].astype(BF16), p[i].reshape(T, -1), w_ple_proj[i].astype(BF16),
                     row(norm_final), final_norm=(i == depth - 1))
    return h.reshape(B, S, D)
```

```python
import functools

import jax
import jax.numpy as jnp
from jax import lax
from jax.experimental import pallas as pl
from jax.experimental.pallas import tpu as pltpu

F32, BF16 = jnp.float32, jnp.bfloat16

EPS = 1e-6
ROPE_THETA = 500000.0
POOL_WINDOWS = (2, 4, 8, 16)
POOL_GROUP_DIM = 128
NSA_HEADS = 8
NSA_KV_GROUPS = 2
NSA_HEADS_PER_GROUP = NSA_HEADS // NSA_KV_GROUPS
NSA_HEAD_DIM = 64
NSA_ROT_DIM = 16
CMP_BLOCK = 32
CMP_STRIDE = 16
SEL_BLOCK = 64
SEL_TOPK = 16
WINDOW = 512
Q_BLOCK = 128
GLA_HEADS = 4
GLA_DK = 64
GLA_DV = 128
GLA_GATE_RANK = 16
GLA_TAU = 16.0
GLA_CHUNK = 64
IN_SIZES = (512, 512, 768, 24, 256, 256, 512, 16, 512, 3072)

LANES = 128
MASKED = -1e30
LOG2E = 1.4426950408889634
Q_SCALE = NSA_HEAD_DIM ** -0.5 * LOG2E
V_ROWS = NSA_HEAD_DIM + 16
VMEM_LIMIT = 56 * 1024 * 1024

MAIN_U, MAIN_QG, MAIN_KG, MAIN_VG, MAIN_RG, MAIN_GM, MAIN_A, MAIN_WIDTH = 0, 512, 768, 1024, 1536, 2048, 5120, 5376
NSA_Q, NSA_KC, NSA_KS, NSA_KW, NSA_VC, NSA_VS, NSA_VW, NSA_G, NSA_WIDTH = 0, 512, 640, 768, 896, 1024, 1152, 1280, 1536


def _nt(a, b):
    return lax.dot_general(a, b, (((1,), (1,)), ((), ())), preferred_element_type=F32)


def _tn(a, b):
    return lax.dot_general(a, b, (((0,), (0,)), ((), ())), preferred_element_type=F32)


def _dot(a, b):
    return jnp.dot(a, b, preferred_element_type=F32)


def _rms(x, gain):
    return x * lax.rsqrt(jnp.mean(x * x, axis=-1, keepdims=True) + EPS) * gain


def _proj_body(h_ref, g_ref, w_ref, o_ref, a_scr):
    @pl.when(pl.program_id(1) == 0)
    def _():
        a_scr[...] = _rms(h_ref[...], g_ref[...]).astype(BF16)

    o_ref[...] = _dot(a_scr[...], w_ref[...]).astype(o_ref.dtype)


def _norm_proj(h, gain, w, out_dtype, tn, tm=1024):
    T, D = h.shape
    N = w.shape[1]
    tm = min(tm, T)
    return pl.pallas_call(
        _proj_body,
        grid=(T // tm, N // tn),
        in_specs=[
            pl.BlockSpec((tm, D), lambda i, j: (i, 0)),
            pl.BlockSpec((1, D), lambda i, j: (0, 0)),
            pl.BlockSpec((D, tn), lambda i, j: (0, j)),
        ],
        out_specs=pl.BlockSpec((tm, tn), lambda i, j: (i, j)),
        out_shape=jax.ShapeDtypeStruct((T, N), out_dtype),
        scratch_shapes=[pltpu.VMEM((tm, D), BF16)],
        compiler_params=pltpu.CompilerParams(
            dimension_semantics=("parallel", "arbitrary"), vmem_limit_bytes=VMEM_LIMIT),
        name="norm_proj",
    )(h, gain, w)


def _pool_body(u_ref, halo_ref, w_ref, sc_ref, o_ref, *, ts):
    i = pl.program_id(1)
    cur = u_ref[...].astype(F32)
    halo = jnp.where(i == 0, 0.0, halo_ref[...].astype(F32))
    t = i * ts + lax.broadcasted_iota(jnp.int32, (ts, 1), 0)
    for g, win in enumerate(POOL_WINDOWS):
        lo, hi = g * POOL_GROUP_DIM, (g + 1) * POOL_GROUP_DIM
        x = cur[:, lo:hi]
        e = jnp.concatenate([halo[:, lo:hi], x], axis=0)
        step = 1
        while step < win:
            e = e[step:] + e[:-step]
            step *= 2
        wsum = e[16 - (win - 1):]
        cnt = jnp.minimum(t + 1, win).astype(F32)
        pooled = wsum / cnt - x
        y = _dot(pooled.astype(BF16), w_ref[g]) * sc_ref[:, lo:hi]
        o_ref[:, lo:hi] = y.astype(o_ref.dtype)


def _pool_mixer(main, pool_w, scale, B, S, ts=512):
    ts = min(ts, S)
    halo_blocks = ts // 16
    return pl.pallas_call(
        functools.partial(_pool_body, ts=ts),
        grid=(B, S // ts),
        in_specs=[
            pl.BlockSpec((None, ts, 512), lambda b, i: (b, i, MAIN_U // 512)),
            pl.BlockSpec((None, 16, 512), lambda b, i: (b, jnp.maximum(i * halo_blocks - 1, 0), MAIN_U // 512)),
            pl.BlockSpec((4, 128, 128), lambda b, i: (0, 0, 0)),
            pl.BlockSpec((1, 512), lambda b, i: (0, 0)),
        ],
        out_specs=pl.BlockSpec((None, ts, 512), lambda b, i: (b, i, 0)),
        out_shape=jax.ShapeDtypeStruct((B, S, 512), BF16),
        compiler_params=pltpu.CompilerParams(dimension_semantics=("parallel", "arbitrary")),
        name="pool_mixer",
    )(main, main, pool_w, scale)


def _gla_body(q_ref, k_ref, v_ref, r_ref, a_ref, wg_ref, bg_ref, ng_ref, o_ref, st_ref, *, tc):
    @pl.when(pl.program_id(1) == 0)
    def _():
        st_ref[...] = jnp.zeros_like(st_ref)

    C = GLA_CHUNK
    n_chunks = tc // C
    causal = lax.broadcasted_iota(jnp.int32, (C, C), 0) >= lax.broadcasted_iota(jnp.int32, (C, C), 1)
    width = GLA_HEADS * GLA_DK
    hk = [slice(h * GLA_DK, (h + 1) * GLA_DK) for h in range(GLA_HEADS)]
    hv = [slice(h * GLA_DV, (h + 1) * GLA_DV) for h in range(GLA_HEADS)]
    rows = [slice(c * C, (c + 1) * C) for c in range(n_chunks)]

    z = _dot(a_ref[...], wg_ref[...]) + bg_ref[...]
    b = jax.nn.log_sigmoid(z) * (1.0 / GLA_TAU)
    row_in_chunk = lax.broadcasted_iota(jnp.int32, (tc, 1), 0) % C
    step = 1
    while step < C:
        shifted = jnp.concatenate([jnp.zeros((step, width), F32), b[:-step]], axis=0)
        b = b + jnp.where(row_in_chunk >= step, shifted, 0.0)
        step *= 2
    b_last = [b[(c + 1) * C - 1:(c + 1) * C, :] for c in range(n_chunks)]
    b_last_rows = jnp.concatenate([jnp.broadcast_to(bl, (C, width)) for bl in b_last], axis=0)
    qf = q_ref[...].astype(F32) * (GLA_DK ** -0.5)
    kf = k_ref[...].astype(F32)
    q_s = (qf * jnp.exp(b)).astype(BF16)
    k_s = (kf * jnp.exp(-b)).astype(BF16)
    k_t = (kf * jnp.exp(b_last_rows - b)).astype(BF16)

    att = [[jnp.where(causal, _nt(q_s[rows[c], hk[h]], k_s[rows[c], hk[h]]), 0.0).astype(BF16)
            for h in range(GLA_HEADS)] for c in range(n_chunks)]
    o_intra = [[_dot(att[c][h], v_ref[rows[c], hv[h]]) for h in range(GLA_HEADS)] for c in range(n_chunks)]
    kv = [[_tn(v_ref[rows[c], hv[h]], k_t[rows[c], hk[h]]) for h in range(GLA_HEADS)] for c in range(n_chunks)]

    state_t = [st_ref[h] for h in range(GLA_HEADS)]
    for c in range(n_chunks):
        decay = jnp.exp(b_last[c])
        for h in range(GLA_HEADS):
            o = o_intra[c][h] + _nt(q_s[rows[c], hk[h]], state_t[h].astype(BF16))
            state_t[h] = state_t[h] * decay[:, hk[h]] + kv[c][h]
            o = _rms(o, ng_ref[...])
            r = r_ref[rows[c], hv[h]].astype(F32)
            o_ref[rows[c], hv[h]] = (o * (r * jax.nn.sigmoid(r))).astype(o_ref.dtype)
    for h in range(GLA_HEADS):
        st_ref[h] = state_t[h]


def _gla_mixer(main, w_gate, b_gate, norm_g, B, S, tc=256):
    tc = min(tc, S)
    return pl.pallas_call(
        functools.partial(_gla_body, tc=tc),
        grid=(B, S // tc),
        in_specs=[
            pl.BlockSpec((None, tc, 256), lambda b, i: (b, i, MAIN_QG // 256)),
            pl.BlockSpec((None, tc, 256), lambda b, i: (b, i, MAIN_KG // 256)),
            pl.BlockSpec((None, tc, 512), lambda b, i: (b, i, MAIN_VG // 512)),
            pl.BlockSpec((None, tc, 512), lambda b, i: (b, i, MAIN_RG // 512)),
            pl.BlockSpec((None, tc, 128), lambda b, i: (b, i, MAIN_A // 128)),
            pl.BlockSpec((128, 256), lambda b, i: (0, 0)),
            pl.BlockSpec((1, 256), lambda b, i: (0, 0)),
            pl.BlockSpec((1, 128), lambda b, i: (0, 0)),
        ],
        out_specs=pl.BlockSpec((None, tc, 512), lambda b, i: (b, i, 0)),
        out_shape=jax.ShapeDtypeStruct((B, S, 512), BF16),
        scratch_shapes=[pltpu.VMEM((GLA_HEADS, GLA_DV, GLA_DK), F32)],
        compiler_params=pltpu.CompilerParams(dimension_semantics=("parallel", "arbitrary")),
        name="gla_mixer",
    )(main, main, main, main, main, w_gate, b_gate, norm_g)


def _nsa_prep_body(x_ref, pos_ref, invf_ref, qs_ref, kc_ref, vc_ref, ks_ref, kw_ref, vst_ref, vwt_ref, gt_ref):
    ang = pos_ref[...].astype(F32) * invf_ref[...]
    cs, sn = jnp.cos(ang), jnp.sin(ang)
    d = lax.broadcasted_iota(jnp.int32, (1, LANES), 1) % NSA_HEAD_DIM
    half = NSA_ROT_DIM // 2
    s_lo = jnp.where(d < half, -sn, 0.0)
    s_hi = jnp.where((d >= half) & (d < NSA_ROT_DIM), sn, 0.0)

    def rope(x):
        return x * cs + pltpu.roll(x, LANES - half, 1) * s_lo + pltpu.roll(x, half, 1) * s_hi

    def split_groups(x, ref, dtype):
        ref[0] = x[:, :NSA_HEAD_DIM].astype(dtype)
        ref[1] = x[:, NSA_HEAD_DIM:].astype(dtype)

    for pair in range(NSA_HEADS // 2):
        qr = rope(x_ref[:, NSA_Q + pair * LANES:NSA_Q + (pair + 1) * LANES]) * Q_SCALE
        for sub in range(2):
            head = 2 * pair + sub
            grp, hp = head // NSA_HEADS_PER_GROUP, head % NSA_HEADS_PER_GROUP
            qs_ref[grp, hp * Q_BLOCK:(hp + 1) * Q_BLOCK, :] = (
                qr[:, sub * NSA_HEAD_DIM:(sub + 1) * NSA_HEAD_DIM].astype(BF16))
    split_groups(rope(x_ref[:, NSA_KC:NSA_KC + LANES]), kc_ref, F32)
    split_groups(rope(x_ref[:, NSA_KS:NSA_KS + LANES]), ks_ref, BF16)
    split_groups(rope(x_ref[:, NSA_KW:NSA_KW + LANES]), kw_ref, BF16)
    split_groups(x_ref[:, NSA_VC:NSA_VC + LANES], vc_ref, F32)
    vst = x_ref[:, NSA_VS:NSA_VS + LANES].T
    vwt = x_ref[:, NSA_VW:NSA_VW + LANES].T
    ones_row = (lax.broadcasted_iota(jnp.int32, (V_ROWS - NSA_HEAD_DIM, Q_BLOCK), 0) == 0).astype(BF16)
    for grp in range(NSA_KV_GROUPS):
        for ref, v in ((vst_ref, vst), (vwt_ref, vwt)):
            ref[grp, :NSA_HEAD_DIM, :] = v[grp * NSA_HEAD_DIM:(grp + 1) * NSA_HEAD_DIM].astype(BF16)
            ref[grp, NSA_HEAD_DIM:, :] = ones_row
    gt_ref[...] = jax.nn.sigmoid(x_ref[:, NSA_G:NSA_G + LANES]).T[:32]


def _nsa_prep(nsa_proj, positions, inv_freq_lanes, B, S):
    G, Dh, nq = NSA_KV_GROUPS, NSA_HEAD_DIM, S // Q_BLOCK
    tok = lambda dtype: jax.ShapeDtypeStruct((B, G, S, Dh), dtype)
    tok_spec = pl.BlockSpec((None, G, Q_BLOCK, Dh), lambda b, i: (b, 0, i, 0))
    tr = jax.ShapeDtypeStruct((B, G, V_ROWS, S), BF16)
    tr_spec = pl.BlockSpec((None, G, V_ROWS, Q_BLOCK), lambda b, i: (b, 0, 0, i))
    return pl.pallas_call(
        _nsa_prep_body,
        grid=(B, nq),
        in_specs=[
            pl.BlockSpec((None, Q_BLOCK, NSA_WIDTH), lambda b, i: (b, i, 0)),
            pl.BlockSpec((None, Q_BLOCK, 1), lambda b, i: (b, i, 0)),
            pl.BlockSpec((1, LANES), lambda b, i: (0, 0)),
        ],
        out_specs=[
            pl.BlockSpec((None, G, NSA_HEADS_PER_GROUP * Q_BLOCK, Dh), lambda b, i: (b, 0, i, 0)),
            tok_spec, tok_spec, tok_spec, tok_spec, tr_spec, tr_spec,
            pl.BlockSpec((None, None, 32, Q_BLOCK), lambda b, i: (b, i, 0, 0)),
        ],
        out_shape=[
            jax.ShapeDtypeStruct((B, G, NSA_HEADS_PER_GROUP * S, Dh), BF16),
            tok(F32), tok(F32), tok(BF16), tok(BF16), tr, tr,
            jax.ShapeDtypeStruct((B, nq, 32, Q_BLOCK), F32),
        ],
        compiler_params=pltpu.CompilerParams(dimension_semantics=("parallel", "parallel")),
        name="nsa_prep",
    )(nsa_proj, positions, inv_freq_lanes)


def _cmp_body(x_ref, w_ref, p_ref, o_ref, *, n_blk, transpose_out):
    half = CMP_BLOCK // 2
    first = jnp.zeros((n_blk, NSA_HEAD_DIM), F32)
    second = jnp.zeros((n_blk, NSA_HEAD_DIM), F32)
    for l in range(half):
        rows = x_ref[pl.ds(l, n_blk, stride=CMP_STRIDE), :]
        first = first + _dot((rows + p_ref[l]).astype(BF16), w_ref[l])
        second = second + _dot((rows + p_ref[half + l]).astype(BF16), w_ref[half + l])
    out = first + jnp.concatenate([second[1:], jnp.zeros((1, NSA_HEAD_DIM), F32)], axis=0)
    o_ref[...] = (out.T if transpose_out else out).astype(o_ref.dtype)


def _compress(x, w, pos, transpose_out):
    B, G, S, Dh = x.shape
    n_blk = S // CMP_STRIDE
    out_block = (None, None, Dh, n_blk) if transpose_out else (None, None, n_blk, Dh)
    out_shape = (B, G, Dh, n_blk) if transpose_out else (B, G, n_blk, Dh)
    return pl.pallas_call(
        functools.partial(_cmp_body, n_blk=n_blk, transpose_out=transpose_out),
        grid=(B, G),
        in_specs=[
            pl.BlockSpec((None, None, S, Dh), lambda b, g: (b, g, 0, 0)),
            pl.BlockSpec((CMP_BLOCK, Dh, Dh), lambda b, g: (0, 0, 0)),
            pl.BlockSpec((CMP_BLOCK, 1, Dh), lambda b, g: (0, 0, 0)),
        ],
        out_specs=pl.BlockSpec(out_block, lambda b, g: (b, g, 0, 0)),
        out_shape=jax.ShapeDtypeStruct(out_shape, BF16),
        compiler_params=pltpu.CompilerParams(
            dimension_semantics=("parallel", "parallel"), vmem_limit_bytes=VMEM_LIMIT),
        name="nsa_compress",
    )(x, w, pos)


def _nsa_body(q_ref, kcc_ref, vcct_ref, ks_ref, vst_ref, kw_ref, vwt_ref, gt_ref, o_ref,
              imp_scr, sel_scr, ocmp_scr, oslc_scr, *, kt_size, sub_size, cmp_chunk):
    grp = pl.program_id(1)
    t0 = pl.program_id(2) * Q_BLOCK
    hpg = NSA_HEADS_PER_GROUP
    n_cmp = kcc_ref.shape[0]
    n_sel = sel_scr.shape[0]
    q = q_ref[...]
    tq1 = t0 + lax.broadcasted_iota(jnp.int32, (1, Q_BLOCK), 1)
    heads = lambda x: jnp.concatenate([x] * hpg, axis=1)

    width = hpg * Q_BLOCK
    neg_inf = -jnp.inf

    ch = cmp_chunk
    n_chunks = (((t0 + Q_BLOCK - CMP_BLOCK) >> (CMP_STRIDE.bit_length() - 1)) + ch) // ch
    imp_scr[...] = jnp.zeros(imp_scr.shape, F32)

    def compressed(n_sub):
        scores = []
        for sub in range(n_sub):
            s = _nt(kcc_ref[sub * ch:(sub + 1) * ch, :], q)
            if sub >= n_sub - 2:
                cmp_end = (sub * ch + lax.broadcasted_iota(jnp.int32, (ch, 1), 0)) * CMP_STRIDE + (CMP_BLOCK - 1)
                s = jnp.where(heads(cmp_end <= tq1), s, neg_inf)
            scores.append(s)
        m = jnp.max(scores[0], axis=0, keepdims=True)
        for s in scores[1:]:
            m = jnp.maximum(m, jnp.max(s, axis=0, keepdims=True))
        m = jnp.where(m == neg_inf, 0.0, m)
        es = [jnp.exp2(s - m) for s in scores]
        l = jnp.sum(es[0], axis=0, keepdims=True)
        for e in es[1:]:
            l = l + jnp.sum(e, axis=0, keepdims=True)
        inv_l = 1.0 / jnp.maximum(l, 1e-30)
        acc = _dot(vcct_ref[:, 0:ch], es[0].astype(BF16))
        for sub in range(1, n_sub):
            acc = acc + _dot(vcct_ref[:, sub * ch:(sub + 1) * ch], es[sub].astype(BF16))
        ocmp_scr[...] = acc * inv_l
        for sub in range(n_sub):
            pc = es[sub] * inv_l
            p_sum = pc[:, :Q_BLOCK]
            for hp in range(1, hpg):
                p_sum = p_sum + pc[:, hp * Q_BLOCK:(hp + 1) * Q_BLOCK]
            imp_scr[8 + sub * ch:8 + (sub + 1) * ch, :] = p_sum

    for variant in range(n_cmp // ch):
        pl.when(n_chunks == variant + 1)(functools.partial(compressed, variant + 1))
    o_cmp = ocmp_scr[...]

    ratio = SEL_BLOCK // CMP_STRIDE
    imp = imp_scr[pl.ds(7, n_sel, stride=ratio), :]
    for k in range(1, ratio + 1):
        imp = imp + imp_scr[pl.ds(7 + k, n_sel, stride=ratio), :]
    j = lax.broadcasted_iota(jnp.int32, (n_sel, Q_BLOCK), 0)
    cur = tq1 >> (SEL_BLOCK.bit_length() - 1)
    forced = (j == 0) | (j == cur) | (j == cur - 1)
    valid = j <= cur

    j_f = j.astype(F32)

    def pick(_, sc):
        best = jnp.max(sc, axis=0, keepdims=True)
        idx = jnp.min(jnp.where(sc == best, j_f, float(n_sel)), axis=0, keepdims=True)
        return jnp.where(j_f == idx, neg_inf, sc)

    n_forced = 3
    left = lax.fori_loop(0, max(min(SEL_TOPK, n_sel) - n_forced, 0), pick,
                         jnp.where(valid & ~forced, imp, neg_inf), unroll=True)
    sel_scr[...] = jnp.where(forced | (valid & (left == neg_inf)), 0.0, MASKED)

    span = WINDOW + Q_BLOCK
    k_start = pl.multiple_of(jnp.maximum(t0 - WINDOW, 0), Q_BLOCK)
    sw = _nt(kw_ref[pl.ds(k_start, span), :], q)
    diff = tq1 - (k_start + lax.broadcasted_iota(jnp.int32, (span, 1), 0))
    sw = jnp.where(heads((diff >= 0) & (diff < WINDOW)), sw, MASKED)
    pw = jnp.exp2(sw - jnp.max(sw, axis=0, keepdims=True)).astype(BF16)
    acc_win = _dot(vwt_ref[:, pl.ds(k_start, span)], pw)
    o_win = acc_win[:NSA_HEAD_DIM] / acc_win[NSA_HEAD_DIM:NSA_HEAD_DIM + 1]

    blocks_per_tile = kt_size // SEL_BLOCK
    eye = (lax.broadcasted_iota(jnp.int32, (Q_BLOCK, Q_BLOCK), 0)
           == lax.broadcasted_iota(jnp.int32, (Q_BLOCK, Q_BLOCK), 1)).astype(BF16)
    q_aug = jnp.concatenate([jnp.concatenate([eye] * hpg, axis=0), q], axis=1)

    def slc_tile(kt, carry, n_sub, causal_from):
        m_run, acc = carry
        scores = []
        for sub in range(n_sub):
            k0 = pl.multiple_of(kt * kt_size + sub * sub_size, sub_size)
            blk0 = kt * blocks_per_tile + sub * (sub_size // SEL_BLOCK)
            bias = jnp.concatenate(
                [jnp.broadcast_to(sel_scr[pl.ds(blk0 + jb, 1), :], (SEL_BLOCK, Q_BLOCK))
                 for jb in range(sub_size // SEL_BLOCK)], axis=0)
            if sub >= causal_from:
                kpos = k0 + lax.broadcasted_iota(jnp.int32, (sub_size, 1), 0)
                bias = jnp.where(kpos <= tq1, bias, MASKED)
            k_aug = jnp.concatenate([bias.astype(BF16), ks_ref[pl.ds(k0, sub_size), :]], axis=1)
            scores.append((k0, _nt(k_aug, q_aug)))
        parts = []
        for k0, sc in scores:
            m_sub = jnp.max(sc, axis=0, keepdims=True)
            pe = jnp.exp2(sc - m_sub).astype(BF16)
            parts.append((m_sub, _dot(vst_ref[:, pl.ds(k0, sub_size)], pe)))
        m_new = m_run
        for m_sub, _ in parts:
            m_new = jnp.maximum(m_new, m_sub)
        acc = jnp.exp2(m_run - m_new) * acc
        for m_sub, acc_sub in parts:
            acc = acc + jnp.exp2(m_sub - m_new) * acc_sub
        return m_new, acc

    subs_per_tile = kt_size // sub_size
    n_full = t0 // kt_size
    carry = lax.fori_loop(
        0, n_full, functools.partial(slc_tile, n_sub=subs_per_tile, causal_from=subs_per_tile),
        (jnp.full((1, width), MASKED, F32), jnp.zeros((V_ROWS, width), F32)))
    tail_step = min(2, subs_per_tile)
    tail_len = (t0 - n_full * kt_size) // (tail_step * sub_size)
    for variant in range(subs_per_tile // tail_step):
        n_sub = tail_step * (variant + 1)

        @pl.when(tail_len == variant)
        def _(n_sub=n_sub):
            _, acc_slc = slc_tile(n_full, carry, n_sub=n_sub, causal_from=n_sub - tail_step)
            oslc_scr[...] = acc_slc[:NSA_HEAD_DIM] / acc_slc[NSA_HEAD_DIM:NSA_HEAD_DIM + 1]

    o_slc = oslc_scr[...]

    for hp in range(hpg):
        cols = slice(hp * Q_BLOCK, (hp + 1) * Q_BLOCK)
        row = (grp * hpg + hp) * 3
        o_t = (gt_ref[pl.ds(row, 1), :] * o_cmp[:, cols] + gt_ref[pl.ds(row + 1, 1), :] * o_slc[:, cols]
               + gt_ref[pl.ds(row + 2, 1), :] * o_win[:, cols])
        o_ref[:, hp * NSA_HEAD_DIM:(hp + 1) * NSA_HEAD_DIM] = o_t.T.astype(o_ref.dtype)


def _nsa_attention(qs, kcc, vcct, ks, vst, kw, vwt, gt, B, S, kt_size=4096, sub_size=256, cmp_chunk=256):
    G, Dh, nq, hpg = NSA_KV_GROUPS, NSA_HEAD_DIM, S // Q_BLOCK, NSA_HEADS_PER_GROUP
    kt_size = min(kt_size, S)
    n_cmp, n_sel = S // CMP_STRIDE, S // SEL_BLOCK
    cmp_chunk = min(cmp_chunk, n_cmp)
    per_group = lambda shape: pl.BlockSpec((None, None) + shape, lambda b, g, i: (b, g, 0, 0))
    return pl.pallas_call(
        functools.partial(_nsa_body, kt_size=kt_size, sub_size=min(sub_size, kt_size), cmp_chunk=cmp_chunk),
        grid=(B, G, nq),
        in_specs=[
            pl.BlockSpec((None, None, hpg * Q_BLOCK, Dh), lambda b, g, i: (b, g, i, 0)),
            per_group((n_cmp, Dh)), per_group((Dh, n_cmp)),
            per_group((S, Dh)), per_group((V_ROWS, S)),
            per_group((S, Dh)), per_group((V_ROWS, S)),
            pl.BlockSpec((None, None, 32, Q_BLOCK), lambda b, g, i: (b, i, 0, 0)),
        ],
        out_specs=pl.BlockSpec((None, Q_BLOCK, hpg * Dh), lambda b, g, i: (b, i, g)),
        out_shape=jax.ShapeDtypeStruct((B, S, NSA_HEADS * Dh), BF16),
        scratch_shapes=[
            pltpu.VMEM((n_cmp + 16, Q_BLOCK), F32),
            pltpu.VMEM((n_sel, Q_BLOCK), F32),
            pltpu.VMEM((Dh, hpg * Q_BLOCK), F32),
            pltpu.VMEM((Dh, hpg * Q_BLOCK), F32),
        ],
        compiler_params=pltpu.CompilerParams(
            dimension_semantics=("parallel", "parallel", "arbitrary"), vmem_limit_bytes=VMEM_LIMIT),
        name="nsa_attention",
    )(qs, kcc, vcct, ks, vst, kw, vwt, gt)


def _merge_body(ya_ref, yb_ref, yc_ref, ga_ref, gb_ref, gc_ref, h_ref, wb_ref, wo_ref, o_ref):
    merged = jax.nn.sigmoid(ga_ref[...].astype(F32)) * _dot(ya_ref[...], wb_ref[0])
    merged = merged + jax.nn.sigmoid(gb_ref[...].astype(F32)) * _dot(yb_ref[...], wb_ref[1])
    merged = merged + jax.nn.sigmoid(gc_ref[...].astype(F32)) * _dot(yc_ref[...], wb_ref[2])
    o_ref[...] = h_ref[...] + _dot(merged.astype(BF16), wo_ref[...])


def _merge(ya, yb, yc, main, h, w_branch, w_out, tm=512):
    T, D = h.shape
    y_spec = pl.BlockSpec((tm, 512), lambda i: (i, 0))
    gate_spec = lambda k: pl.BlockSpec((tm, D), lambda i: (i, MAIN_GM // D + k))
    return pl.pallas_call(
        _merge_body,
        grid=(T // tm,),
        in_specs=[
            y_spec, y_spec, y_spec, gate_spec(0), gate_spec(1), gate_spec(2),
            pl.BlockSpec((tm, D), lambda i: (i, 0)),
            pl.BlockSpec((3, 512, D), lambda i: (0, 0, 0)),
            pl.BlockSpec((D, D), lambda i: (0, 0)),
        ],
        out_specs=pl.BlockSpec((tm, D), lambda i: (i, 0)),
        out_shape=jax.ShapeDtypeStruct((T, D), F32),
        compiler_params=pltpu.CompilerParams(dimension_semantics=("parallel",), vmem_limit_bytes=VMEM_LIMIT),
        name="merge_out",
    )(ya, yb, yc, main, main, main, h, w_branch, w_out)


def _ffn_body(h_ref, gf_ref, w1_ref, w2_ref, gp_ref, wg_ref, p_ref, wp_ref, gl_ref, o_ref, f_scr, acc_scr, *, final_norm):
    j = pl.program_id(1)

    @pl.when(j == 0)
    def _():
        f_scr[...] = _rms(h_ref[...], gf_ref[...]).astype(BF16)
        acc_scr[...] = jnp.zeros_like(acc_scr)

    a = jnp.maximum(_dot(f_scr[...], w1_ref[...]), 0.0)
    acc_scr[...] += _dot((a * a).astype(BF16), w2_ref[...])

    @pl.when(j == pl.num_programs(1) - 1)
    def _():
        h2 = h_ref[...] + acc_scr[...]
        gate = jax.nn.sigmoid(_dot(_rms(h2, gp_ref[...]).astype(BF16), wg_ref[...]))
        h3 = h2 + gate * _dot(p_ref[...].astype(BF16), wp_ref[...])
        o_ref[...] = _rms(h3, gl_ref[...]) if final_norm else h3


def _ffn_ple(h, norm_ffn, w1, w2, norm_ple, w_gate, p, w_proj, norm_last, final_norm, tm=1024, tf=512):
    T, D = h.shape
    tm = min(tm, T)
    Fdim = w1.shape[1]
    Pdim = p.shape[1]
    vec = pl.BlockSpec((1, D), lambda i, j: (0, 0))
    return pl.pallas_call(
        functools.partial(_ffn_body, final_norm=final_norm),
        grid=(T // tm, Fdim // tf),
        in_specs=[
            pl.BlockSpec((tm, D), lambda i, j: (i, 0)), vec,
            pl.BlockSpec((D, tf), lambda i, j: (0, j)),
            pl.BlockSpec((tf, D), lambda i, j: (j, 0)),
            vec,
            pl.BlockSpec((D, D), lambda i, j: (0, 0)),
            pl.BlockSpec((tm, Pdim), lambda i, j: (i, 0)),
            pl.BlockSpec((Pdim, D), lambda i, j: (0, 0)),
            vec,
        ],
        out_specs=pl.BlockSpec((tm, D), lambda i, j: (i, 0)),
        out_shape=jax.ShapeDtypeStruct((T, D), F32),
        scratch_shapes=[pltpu.VMEM((tm, D), BF16), pltpu.VMEM((tm, D), F32)],
        compiler_params=pltpu.CompilerParams(
            dimension_semantics=("parallel", "arbitrary"), vmem_limit_bytes=VMEM_LIMIT),
        name="ffn_ple",
    )(h, norm_ffn, w1, w2, norm_ple, w_gate, p, w_proj, norm_last)


def _split_w_in(w):
    parts, off = [], 0
    for sz in IN_SIZES:
        parts.append(w[:, off:off + sz])
        off += sz
    u, q_nsa, kv_nsa, g_nsa, q_gla, k_gla, v_gla, a_gla, r_gla, g_merge = parts
    D = w.shape[0]
    pad = lambda n: jnp.zeros((D, n), w.dtype)
    main = jnp.concatenate([u, q_gla, k_gla, v_gla, r_gla, g_merge, a_gla, pad(MAIN_WIDTH - MAIN_A - GLA_GATE_RANK)], axis=1)
    kv = [kv_nsa[:, s * LANES:(s + 1) * LANES] for s in range(6)]
    nsa = jnp.concatenate([q_nsa, kv[0], kv[2], kv[4], kv[1], kv[3], kv[5], g_nsa, pad(NSA_WIDTH - NSA_G - 24)], axis=1)
    return main.astype(BF16), nsa.astype(BF16)


def kernel(x, p, positions, norm_mix, w_in, pool_w, pool_scale, cmp_pos_k, cmp_w_k, cmp_pos_v, cmp_w_v, gla_w_gate, gla_b_gate, gla_norm, w_branch, w_out, norm_ffn, w_ff1, w_ff2, norm_ple, w_ple_gate, w_ple_proj, norm_final):
    B, S, D = x.shape
    depth = w_in.shape[0]
    T = B * S
    Dh = NSA_HEAD_DIM
    row = lambda v: v.reshape(1, -1).astype(F32)

    half = NSA_ROT_DIM // 2
    inv_freq = jnp.power(ROPE_THETA, -jnp.arange(half, dtype=F32) * (2.0 / NSA_ROT_DIM))
    d = jnp.arange(LANES) % Dh
    inv_freq_lanes = jnp.where(d < NSA_ROT_DIM, inv_freq[d % half], 0.0).reshape(1, LANES)
    pos3 = positions.reshape(B, S, 1)

    h = x.reshape(T, D)
    for i in range(depth):
        w_main, w_nsa = _split_w_in(w_in[i])
        main = _norm_proj(h, row(norm_mix[i]), w_main, BF16, tn=MAIN_WIDTH // 3).reshape(B, S, MAIN_WIDTH)
        nsa_proj = _norm_proj(h, row(norm_mix[i]), w_nsa, F32, tn=NSA_WIDTH).reshape(B, S, NSA_WIDTH)

        y_a = _pool_mixer(main, pool_w[i].astype(BF16), row(pool_scale[i]), B, S)

        wg = jnp.zeros((LANES, GLA_HEADS * GLA_DK), BF16).at[:GLA_GATE_RANK].set(gla_w_gate[i].astype(BF16))
        y_c = _gla_mixer(main, wg, row(gla_b_gate[i]), row(gla_norm[i]), B, S)

        qs, kc, vc, ks, kw, vst, vwt, gt = _nsa_prep(nsa_proj, pos3, inv_freq_lanes, B, S)
        kcc = _compress(kc, cmp_w_k[i].reshape(CMP_BLOCK, Dh, Dh).astype(BF16),
                        cmp_pos_k[i].reshape(CMP_BLOCK, 1, Dh).astype(F32), transpose_out=False)
        vcct = _compress(vc, cmp_w_v[i].reshape(CMP_BLOCK, Dh, Dh).astype(BF16),
                         cmp_pos_v[i].reshape(CMP_BLOCK, 1, Dh).astype(F32), transpose_out=True)
        y_b = _nsa_attention(qs, kcc, vcct, ks, vst, kw, vwt, gt, B, S)

        h = _merge(y_a.reshape(T, -1), y_b.reshape(T, -1), y_c.reshape(T, -1), main.reshape(T, MAIN_WIDTH), h,
                   w_branch[i].astype(BF16), w_out[i].astype(BF16))
        h = _ffn_ple(h, row(norm_ffn[i]), w_ff1[i].astype(BF16), w_ff2[i].astype(BF16), row(norm_ple[i]),
                     w_ple_gate[i].astype(BF16), p[i].reshape(T, -1), w_ple_proj[i].astype(BF16),
                     row(norm_final), final_norm=(i == depth - 1))
    return h.reshape(B, S, D)
```

```python
import functools

import jax
import jax.numpy as jnp
import numpy as np
from jax import lax
from jax.experimental import pallas as pl
from jax.experimental.pallas import tpu as pltpu

F32, BF16 = jnp.float32, jnp.bfloat16

EPS = 1e-6
ROPE_THETA = 500000.0
POOL_WINDOWS = (2, 4, 8, 16)
POOL_GROUP_DIM = 128
NSA_HEADS = 8
NSA_KV_GROUPS = 2
NSA_HEADS_PER_GROUP = NSA_HEADS // NSA_KV_GROUPS
NSA_HEAD_DIM = 64
NSA_ROT_DIM = 16
CMP_BLOCK = 32
CMP_STRIDE = 16
SEL_BLOCK = 64
SEL_TOPK = 16
WINDOW = 512
Q_BLOCK = 128
GLA_HEADS = 4
GLA_DK = 64
GLA_DV = 128
GLA_GATE_RANK = 16
GLA_TAU = 16.0
GLA_CHUNK = 64
IN_SIZES = (512, 512, 768, 24, 256, 256, 512, 16, 512, 3072)

LANES = 128
MASKED = -1e30
LOG2E = 1.4426950408889634
Q_SCALE = NSA_HEAD_DIM ** -0.5 * LOG2E
V_ROWS = NSA_HEAD_DIM + 16
VMEM_LIMIT = 56 * 1024 * 1024

MAIN_U, MAIN_QG, MAIN_KG, MAIN_VG, MAIN_RG, MAIN_GM, MAIN_A, MAIN_WIDTH = 0, 512, 768, 1024, 1536, 2048, 5120, 5376
NSA_Q, NSA_KC, NSA_KS, NSA_KW, NSA_VC, NSA_VS, NSA_VW, NSA_G, NSA_WIDTH = 0, 512, 640, 768, 896, 1024, 1152, 1280, 1536


def _nt(a, b):
    return lax.dot_general(a, b, (((1,), (1,)), ((), ())), preferred_element_type=F32)


def _tn(a, b):
    return lax.dot_general(a, b, (((0,), (0,)), ((), ())), preferred_element_type=F32)


def _dot(a, b):
    return jnp.dot(a, b, preferred_element_type=F32)


def _rms(x, gain):
    return x * lax.rsqrt(jnp.mean(x * x, axis=-1, keepdims=True) + EPS) * gain


def _proj_body(h_ref, g_ref, w_ref, o_ref, a_scr):
    @pl.when(pl.program_id(1) == 0)
    def _():
        a_scr[...] = _rms(h_ref[...], g_ref[...]).astype(BF16)

    o_ref[...] = _dot(a_scr[...], w_ref[...]).astype(o_ref.dtype)


def _norm_proj(h, gain, w, out_dtype, tn, tm=1024):
    T, D = h.shape
    N = w.shape[1]
    tm = min(tm, T)
    return pl.pallas_call(
        _proj_body,
        grid=(T // tm, N // tn),
        in_specs=[
            pl.BlockSpec((tm, D), lambda i, j: (i, 0)),
            pl.BlockSpec((1, D), lambda i, j: (0, 0)),
            pl.BlockSpec((D, tn), lambda i, j: (0, j)),
        ],
        out_specs=pl.BlockSpec((tm, tn), lambda i, j: (i, j)),
        out_shape=jax.ShapeDtypeStruct((T, N), out_dtype),
        scratch_shapes=[pltpu.VMEM((tm, D), BF16)],
        compiler_params=pltpu.CompilerParams(
            dimension_semantics=("parallel", "arbitrary"), vmem_limit_bytes=VMEM_LIMIT),
        name="norm_proj",
    )(h, gain, w)


def _pool_body(u_ref, halo_ref, w_ref, sc_ref, o_ref, *, ts):
    i = pl.program_id(1)
    cur = u_ref[...].astype(F32)
    halo = jnp.where(i == 0, 0.0, halo_ref[...].astype(F32))
    t = i * ts + lax.broadcasted_iota(jnp.int32, (ts, 1), 0)
    for g, win in enumerate(POOL_WINDOWS):
        lo, hi = g * POOL_GROUP_DIM, (g + 1) * POOL_GROUP_DIM
        x = cur[:, lo:hi]
        e = jnp.concatenate([halo[:, lo:hi], x], axis=0)
        step = 1
        while step < win:
            e = e[step:] + e[:-step]
            step *= 2
        wsum = e[16 - (win - 1):]
        cnt = jnp.minimum(t + 1, win).astype(F32)
        pooled = wsum / cnt - x
        y = _dot(pooled.astype(BF16), w_ref[g]) * sc_ref[:, lo:hi]
        o_ref[:, lo:hi] = y.astype(o_ref.dtype)


def _pool_mixer(main, pool_w, scale, B, S, ts=512):
    ts = min(ts, S)
    halo_blocks = ts // 16
    return pl.pallas_call(
        functools.partial(_pool_body, ts=ts),
        grid=(B, S // ts),
        in_specs=[
            pl.BlockSpec((None, ts, 512), lambda b, i: (b, i, MAIN_U // 512)),
            pl.BlockSpec((None, 16, 512), lambda b, i: (b, jnp.maximum(i * halo_blocks - 1, 0), MAIN_U // 512)),
            pl.BlockSpec((4, 128, 128), lambda b, i: (0, 0, 0)),
            pl.BlockSpec((1, 512), lambda b, i: (0, 0)),
        ],
        out_specs=pl.BlockSpec((None, ts, 512), lambda b, i: (b, i, 0)),
        out_shape=jax.ShapeDtypeStruct((B, S, 512), BF16),
        compiler_params=pltpu.CompilerParams(dimension_semantics=("parallel", "arbitrary")),
        name="pool_mixer",
    )(main, main, pool_w, scale)


def _gla_body(q_ref, k_ref, v_ref, r_ref, a_ref, wg_ref, bg_ref, ng_ref, o_ref, st_ref, *, tc):
    @pl.when(pl.program_id(1) == 0)
    def _():
        st_ref[...] = jnp.zeros_like(st_ref)

    C = GLA_CHUNK
    n_chunks = tc // C
    causal = lax.broadcasted_iota(jnp.int32, (C, C), 0) >= lax.broadcasted_iota(jnp.int32, (C, C), 1)
    width = GLA_HEADS * GLA_DK
    hk = [slice(h * GLA_DK, (h + 1) * GLA_DK) for h in range(GLA_HEADS)]
    hv = [slice(h * GLA_DV, (h + 1) * GLA_DV) for h in range(GLA_HEADS)]
    rows = [slice(c * C, (c + 1) * C) for c in range(n_chunks)]

    z = _dot(a_ref[...], wg_ref[...]) + bg_ref[...]
    b = jax.nn.log_sigmoid(z) * (1.0 / GLA_TAU)
    row_in_chunk = lax.broadcasted_iota(jnp.int32, (tc, 1), 0) % C
    step = 1
    while step < C:
        shifted = jnp.concatenate([jnp.zeros((step, width), F32), b[:-step]], axis=0)
        b = b + jnp.where(row_in_chunk >= step, shifted, 0.0)
        step *= 2
    b_last = [b[(c + 1) * C - 1:(c + 1) * C, :] for c in range(n_chunks)]
    b_last_rows = jnp.concatenate([jnp.broadcast_to(bl, (C, width)) for bl in b_last], axis=0)
    qf = q_ref[...].astype(F32) * (GLA_DK ** -0.5)
    kf = k_ref[...].astype(F32)
    q_s = (qf * jnp.exp(b)).astype(BF16)
    k_s = (kf * jnp.exp(-b)).astype(BF16)
    k_t = (kf * jnp.exp(b_last_rows - b)).astype(BF16)

    att = [[jnp.where(causal, _nt(q_s[rows[c], hk[h]], k_s[rows[c], hk[h]]), 0.0).astype(BF16)
            for h in range(GLA_HEADS)] for c in range(n_chunks)]
    o_intra = [[_dot(att[c][h], v_ref[rows[c], hv[h]]) for h in range(GLA_HEADS)] for c in range(n_chunks)]
    kv = [[_tn(v_ref[rows[c], hv[h]], k_t[rows[c], hk[h]]) for h in range(GLA_HEADS)] for c in range(n_chunks)]

    state_t = [st_ref[h] for h in range(GLA_HEADS)]
    for c in range(n_chunks):
        decay = jnp.exp(b_last[c])
        for h in range(GLA_HEADS):
            o = o_intra[c][h] + _nt(q_s[rows[c], hk[h]], state_t[h].astype(BF16))
            state_t[h] = state_t[h] * decay[:, hk[h]] + kv[c][h]
            o = _rms(o, ng_ref[...])
            r = r_ref[rows[c], hv[h]].astype(F32)
            o_ref[rows[c], hv[h]] = (o * (r * jax.nn.sigmoid(r))).astype(o_ref.dtype)
    for h in range(GLA_HEADS):
        st_ref[h] = state_t[h]


def _gla_mixer(main, w_gate, b_gate, norm_g, B, S, tc=512):
    tc = min(tc, S)
    return pl.pallas_call(
        functools.partial(_gla_body, tc=tc),
        grid=(B, S // tc),
        in_specs=[
            pl.BlockSpec((None, tc, 256), lambda b, i: (b, i, MAIN_QG // 256)),
            pl.BlockSpec((None, tc, 256), lambda b, i: (b, i, MAIN_KG // 256)),
            pl.BlockSpec((None, tc, 512), lambda b, i: (b, i, MAIN_VG // 512)),
            pl.BlockSpec((None, tc, 512), lambda b, i: (b, i, MAIN_RG // 512)),
            pl.BlockSpec((None, tc, 128), lambda b, i: (b, i, MAIN_A // 128)),
            pl.BlockSpec((128, 256), lambda b, i: (0, 0)),
            pl.BlockSpec((1, 256), lambda b, i: (0, 0)),
            pl.BlockSpec((1, 128), lambda b, i: (0, 0)),
        ],
        out_specs=pl.BlockSpec((None, tc, 512), lambda b, i: (b, i, 0)),
        out_shape=jax.ShapeDtypeStruct((B, S, 512), BF16),
        scratch_shapes=[pltpu.VMEM((GLA_HEADS, GLA_DV, GLA_DK), F32)],
        compiler_params=pltpu.CompilerParams(dimension_semantics=("parallel", "arbitrary")),
        name="gla_mixer",
    )(main, main, main, main, main, w_gate, b_gate, norm_g)


def _nsa_prep_body(x_ref, pos_ref, invf_ref, spread_ref, qs_ref, kc_ref, vc_ref, ks_ref, kw_ref, vst_ref, vwt_ref,
                   gt_ref):
    half = NSA_ROT_DIM // 2
    ang = invf_ref[...] * pos_ref[...].astype(F32)

    def pieces(a):
        p1 = a.astype(BF16).astype(F32)
        p2 = (a - p1).astype(BF16).astype(F32)
        return [p1, p2, (a - p1 - p2).astype(BF16).astype(F32)]

    lhs = jnp.concatenate(pieces(jnp.cos(ang)) + pieces(jnp.sin(ang)) + [jnp.ones((2 * half, Q_BLOCK), F32)], axis=0)
    tables = _tn(lhs.astype(BF16), spread_ref[...])
    cs = tables[:, :LANES]
    s_lo = tables[:, LANES:2 * LANES]
    s_hi = tables[:, 2 * LANES:]

    def rope(x):
        return x * cs + pltpu.roll(x, LANES - half, 1) * s_lo + pltpu.roll(x, half, 1) * s_hi

    def split_groups(x, ref, dtype):
        ref[0] = x[:, :NSA_HEAD_DIM].astype(dtype)
        ref[1] = x[:, NSA_HEAD_DIM:].astype(dtype)

    for pair in range(NSA_HEADS // 2):
        qr = rope(x_ref[:, NSA_Q + pair * LANES:NSA_Q + (pair + 1) * LANES]) * Q_SCALE
        for sub in range(2):
            head = 2 * pair + sub
            grp, hp = head // NSA_HEADS_PER_GROUP, head % NSA_HEADS_PER_GROUP
            qs_ref[grp, hp * Q_BLOCK:(hp + 1) * Q_BLOCK, :] = (
                qr[:, sub * NSA_HEAD_DIM:(sub + 1) * NSA_HEAD_DIM].astype(BF16))
    split_groups(rope(x_ref[:, NSA_KC:NSA_KC + LANES]), kc_ref, F32)
    split_groups(rope(x_ref[:, NSA_KS:NSA_KS + LANES]), ks_ref, BF16)
    split_groups(rope(x_ref[:, NSA_KW:NSA_KW + LANES]), kw_ref, BF16)
    split_groups(x_ref[:, NSA_VC:NSA_VC + LANES], vc_ref, F32)
    vst = x_ref[:, NSA_VS:NSA_VS + LANES].T
    vwt = x_ref[:, NSA_VW:NSA_VW + LANES].T
    ones_row = (lax.broadcasted_iota(jnp.int32, (V_ROWS - NSA_HEAD_DIM, Q_BLOCK), 0) == 0).astype(BF16)
    for grp in range(NSA_KV_GROUPS):
        for ref, v in ((vst_ref, vst), (vwt_ref, vwt)):
            ref[grp, :NSA_HEAD_DIM, :] = v[grp * NSA_HEAD_DIM:(grp + 1) * NSA_HEAD_DIM].astype(BF16)
            ref[grp, NSA_HEAD_DIM:, :] = ones_row
    gt_ref[...] = jax.nn.sigmoid(x_ref[:, NSA_G:NSA_G + LANES]).T[:32]


def _rope_spread_matrix():
    half = NSA_ROT_DIM // 2
    lane = np.arange(LANES)
    d = lane % NSA_HEAD_DIM
    m = np.zeros((8 * half, 3 * LANES), np.float32)
    for piece in range(3):
        for f in range(half):
            hit = d % half == f
            m[piece * half + f, :LANES] = hit & (d < NSA_ROT_DIM)
            m[(3 + piece) * half + f, LANES:2 * LANES] = -1.0 * (hit & (d < half))
            m[(3 + piece) * half + f, 2 * LANES:] = hit & (d >= half) & (d < NSA_ROT_DIM)
    m[6 * half, :LANES] = d >= NSA_ROT_DIM
    return jnp.asarray(m, BF16)


def _nsa_prep(nsa_proj, positions, inv_freq, rope_spread, B, S):
    G, Dh, nq = NSA_KV_GROUPS, NSA_HEAD_DIM, S // Q_BLOCK
    tok = lambda dtype: jax.ShapeDtypeStruct((B, G, S, Dh), dtype)
    tok_spec = pl.BlockSpec((None, G, Q_BLOCK, Dh), lambda b, i: (b, 0, i, 0))
    tr = jax.ShapeDtypeStruct((B, G, V_ROWS, S), BF16)
    tr_spec = pl.BlockSpec((None, G, V_ROWS, Q_BLOCK), lambda b, i: (b, 0, 0, i))
    return pl.pallas_call(
        _nsa_prep_body,
        grid=(B, nq),
        in_specs=[
            pl.BlockSpec((None, Q_BLOCK, NSA_WIDTH), lambda b, i: (b, i, 0)),
            pl.BlockSpec((None, None, 1, Q_BLOCK), lambda b, i: (b, i, 0, 0)),
            pl.BlockSpec(inv_freq.shape, lambda b, i: (0, 0)),
            pl.BlockSpec(rope_spread.shape, lambda b, i: (0, 0)),
        ],
        out_specs=[
            pl.BlockSpec((None, G, NSA_HEADS_PER_GROUP * Q_BLOCK, Dh), lambda b, i: (b, 0, i, 0)),
            tok_spec, tok_spec, tok_spec, tok_spec, tr_spec, tr_spec,
            pl.BlockSpec((None, None, 32, Q_BLOCK), lambda b, i: (b, i, 0, 0)),
        ],
        out_shape=[
            jax.ShapeDtypeStruct((B, G, NSA_HEADS_PER_GROUP * S, Dh), BF16),
            tok(F32), tok(F32), tok(BF16), tok(BF16), tr, tr,
            jax.ShapeDtypeStruct((B, nq, 32, Q_BLOCK), F32),
        ],
        compiler_params=pltpu.CompilerParams(dimension_semantics=("parallel", "parallel")),
        name="nsa_prep",
    )(nsa_proj, positions, inv_freq, rope_spread)


def _cmp_body(x_ref, w_ref, p_ref, o_ref, *, n_blk, transpose_out):
    half = CMP_BLOCK // 2
    first = jnp.zeros((n_blk, NSA_HEAD_DIM), F32)
    second = jnp.zeros((n_blk, NSA_HEAD_DIM), F32)
    for l in range(half):
        rows = x_ref[pl.ds(l, n_blk, stride=CMP_STRIDE), :]
        first = first + _dot((rows + p_ref[l]).astype(BF16), w_ref[l])
        second = second + _dot((rows + p_ref[half + l]).astype(BF16), w_ref[half + l])
    out = first + jnp.concatenate([second[1:], jnp.zeros((1, NSA_HEAD_DIM), F32)], axis=0)
    o_ref[...] = (out.T if transpose_out else out).astype(o_ref.dtype)


def _compress(x, w, pos, transpose_out):
    B, G, S, Dh = x.shape
    n_blk = S // CMP_STRIDE
    out_block = (None, None, Dh, n_blk) if transpose_out else (None, None, n_blk, Dh)
    out_shape = (B, G, Dh, n_blk) if transpose_out else (B, G, n_blk, Dh)
    return pl.pallas_call(
        functools.partial(_cmp_body, n_blk=n_blk, transpose_out=transpose_out),
        grid=(B, G),
        in_specs=[
            pl.BlockSpec((None, None, S, Dh), lambda b, g: (b, g, 0, 0)),
            pl.BlockSpec((CMP_BLOCK, Dh, Dh), lambda b, g: (0, 0, 0)),
            pl.BlockSpec((CMP_BLOCK, 1, Dh), lambda b, g: (0, 0, 0)),
        ],
        out_specs=pl.BlockSpec(out_block, lambda b, g: (b, g, 0, 0)),
        out_shape=jax.ShapeDtypeStruct(out_shape, BF16),
        compiler_params=pltpu.CompilerParams(
            dimension_semantics=("parallel", "parallel"), vmem_limit_bytes=VMEM_LIMIT),
        name="nsa_compress",
    )(x, w, pos)


def _nsa_body(q_ref, kcc_ref, vcct_ref, ks_ref, vst_ref, kw_ref, vwt_ref, gt_ref, o_ref,
              imp_scr, sel_scr, ocmp_scr, oslc_scr, *, kt_size, sub_size, cmp_chunk):
    grp = pl.program_id(1)
    t0 = pl.program_id(2) * Q_BLOCK
    hpg = NSA_HEADS_PER_GROUP
    n_cmp = kcc_ref.shape[0]
    n_sel = sel_scr.shape[0]
    q = q_ref[...]
    tq1 = t0 + lax.broadcasted_iota(jnp.int32, (1, Q_BLOCK), 1)
    heads = lambda x: jnp.concatenate([x] * hpg, axis=1)

    width = hpg * Q_BLOCK
    neg_inf = -jnp.inf

    ch = cmp_chunk
    n_chunks = (((t0 + Q_BLOCK - CMP_BLOCK) >> (CMP_STRIDE.bit_length() - 1)) + ch) // ch
    imp_scr[...] = jnp.zeros(imp_scr.shape, F32)

    def compressed(n_sub):
        scores = []
        for sub in range(n_sub):
            s = _nt(kcc_ref[sub * ch:(sub + 1) * ch, :], q)
            if sub >= n_sub - 2:
                cmp_end = (sub * ch + lax.broadcasted_iota(jnp.int32, (ch, 1), 0)) * CMP_STRIDE + (CMP_BLOCK - 1)
                s = jnp.where(heads(cmp_end <= tq1), s, neg_inf)
            scores.append(s)
        m = jnp.max(scores[0], axis=0, keepdims=True)
        for s in scores[1:]:
            m = jnp.maximum(m, jnp.max(s, axis=0, keepdims=True))
        m = jnp.where(m == neg_inf, 0.0, m)
        es = [jnp.exp2(s - m) for s in scores]
        l = jnp.sum(es[0], axis=0, keepdims=True)
        for e in es[1:]:
            l = l + jnp.sum(e, axis=0, keepdims=True)
        inv_l = 1.0 / jnp.maximum(l, 1e-30)
        acc = _dot(vcct_ref[:, 0:ch], es[0].astype(BF16))
        for sub in range(1, n_sub):
            acc = acc + _dot(vcct_ref[:, sub * ch:(sub + 1) * ch], es[sub].astype(BF16))
        ocmp_scr[...] = acc * inv_l
        for sub in range(n_sub):
            pc = es[sub] * inv_l
            p_sum = pc[:, :Q_BLOCK]
            for hp in range(1, hpg):
                p_sum = p_sum + pc[:, hp * Q_BLOCK:(hp + 1) * Q_BLOCK]
            imp_scr[8 + sub * ch:8 + (sub + 1) * ch, :] = p_sum

    for variant in range(n_cmp // ch):
        pl.when(n_chunks == variant + 1)(functools.partial(compressed, variant + 1))
    o_cmp = ocmp_scr[...]

    ratio = SEL_BLOCK // CMP_STRIDE
    imp = imp_scr[pl.ds(7, n_sel, stride=ratio), :]
    for k in range(1, ratio + 1):
        imp = imp + imp_scr[pl.ds(7 + k, n_sel, stride=ratio), :]
    j = lax.broadcasted_iota(jnp.int32, (n_sel, Q_BLOCK), 0)
    cur = tq1 >> (SEL_BLOCK.bit_length() - 1)
    forced = (j == 0) | (j == cur) | (j == cur - 1)
    valid = j <= cur

    j_f = j.astype(F32)

    def pick(_, sc):
        best = jnp.max(sc, axis=0, keepdims=True)
        idx = jnp.min(jnp.where(sc == best, j_f, float(n_sel)), axis=0, keepdims=True)
        return jnp.where(j_f == idx, neg_inf, sc)

    n_forced = 3
    left = lax.fori_loop(0, max(min(SEL_TOPK, n_sel) - n_forced, 0), pick,
                         jnp.where(valid & ~forced, imp, neg_inf), unroll=True)
    sel_scr[...] = jnp.where(forced | (valid & (left == neg_inf)), 0.0, MASKED)

    span = WINDOW + Q_BLOCK
    k_start = pl.multiple_of(jnp.maximum(t0 - WINDOW, 0), Q_BLOCK)
    sw = _nt(kw_ref[pl.ds(k_start, span), :], q)
    diff = tq1 - (k_start + lax.broadcasted_iota(jnp.int32, (span, 1), 0))
    sw = jnp.where(heads((diff >= 0) & (diff < WINDOW)), sw, MASKED)
    pw = jnp.exp2(sw - jnp.max(sw, axis=0, keepdims=True)).astype(BF16)
    acc_win = _dot(vwt_ref[:, pl.ds(k_start, span)], pw)
    o_win = acc_win[:NSA_HEAD_DIM] / acc_win[NSA_HEAD_DIM:NSA_HEAD_DIM + 1]

    blocks_per_tile = kt_size // SEL_BLOCK
    eye = (lax.broadcasted_iota(jnp.int32, (Q_BLOCK, Q_BLOCK), 0)
           == lax.broadcasted_iota(jnp.int32, (Q_BLOCK, Q_BLOCK), 1)).astype(BF16)
    q_aug = jnp.concatenate([jnp.concatenate([eye] * hpg, axis=0), q], axis=1)

    def slc_tile(kt, carry, n_sub, causal_from):
        m_run, acc = carry
        scores = []
        for sub in range(n_sub):
            k0 = pl.multiple_of(kt * kt_size + sub * sub_size, sub_size)
            blk0 = kt * blocks_per_tile + sub * (sub_size // SEL_BLOCK)
            bias = jnp.concatenate(
                [jnp.broadcast_to(sel_scr[pl.ds(blk0 + jb, 1), :], (SEL_BLOCK, Q_BLOCK))
                 for jb in range(sub_size // SEL_BLOCK)], axis=0)
            if sub >= causal_from:
                kpos = k0 + lax.broadcasted_iota(jnp.int32, (sub_size, 1), 0)
                bias = jnp.where(kpos <= tq1, bias, MASKED)
            k_aug = jnp.concatenate([bias.astype(BF16), ks_ref[pl.ds(k0, sub_size), :]], axis=1)
            scores.append((k0, _nt(k_aug, q_aug)))
        parts = []
        for k0, sc in scores:
            m_sub = jnp.max(sc, axis=0, keepdims=True)
            pe = jnp.exp2(sc - m_sub).astype(BF16)
            parts.append((m_sub, _dot(vst_ref[:, pl.ds(k0, sub_size)], pe)))
        m_new = m_run
        for m_sub, _ in parts:
            m_new = jnp.maximum(m_new, m_sub)
        acc = jnp.exp2(m_run - m_new) * acc
        for m_sub, acc_sub in parts:
            acc = acc + jnp.exp2(m_sub - m_new) * acc_sub
        return m_new, acc

    subs_per_tile = kt_size // sub_size
    n_full = t0 // kt_size
    carry = lax.fori_loop(
        0, n_full, functools.partial(slc_tile, n_sub=subs_per_tile, causal_from=subs_per_tile),
        (jnp.full((1, width), MASKED, F32), jnp.zeros((V_ROWS, width), F32)))
    tail_step = min(2, subs_per_tile)
    tail_len = (t0 - n_full * kt_size) // (tail_step * sub_size)
    for variant in range(subs_per_tile // tail_step):
        n_sub = tail_step * (variant + 1)

        @pl.when(tail_len == variant)
        def _(n_sub=n_sub):
            _, acc_slc = slc_tile(n_full, carry, n_sub=n_sub, causal_from=n_sub - tail_step)
            oslc_scr[...] = acc_slc[:NSA_HEAD_DIM] / acc_slc[NSA_HEAD_DIM:NSA_HEAD_DIM + 1]

    o_slc = oslc_scr[...]

    for hp in range(hpg):
        cols = slice(hp * Q_BLOCK, (hp + 1) * Q_BLOCK)
        row = (grp * hpg + hp) * 3
        o_t = (gt_ref[pl.ds(row, 1), :] * o_cmp[:, cols] + gt_ref[pl.ds(row + 1, 1), :] * o_slc[:, cols]
               + gt_ref[pl.ds(row + 2, 1), :] * o_win[:, cols])
        o_ref[:, hp * NSA_HEAD_DIM:(hp + 1) * NSA_HEAD_DIM] = o_t.T.astype(o_ref.dtype)


def _nsa_attention(qs, kcc, vcct, ks, vst, kw, vwt, gt, B, S, kt_size=4096, sub_size=256, cmp_chunk=256):
    G, Dh, nq, hpg = NSA_KV_GROUPS, NSA_HEAD_DIM, S // Q_BLOCK, NSA_HEADS_PER_GROUP
    kt_size = min(kt_size, S)
    n_cmp, n_sel = S // CMP_STRIDE, S // SEL_BLOCK
    cmp_chunk = min(cmp_chunk, n_cmp)
    per_group = lambda shape: pl.BlockSpec((None, None) + shape, lambda b, g, i: (b, g, 0, 0))
    return pl.pallas_call(
        functools.partial(_nsa_body, kt_size=kt_size, sub_size=min(sub_size, kt_size), cmp_chunk=cmp_chunk),
        grid=(B, G, nq),
        in_specs=[
            pl.BlockSpec((None, None, hpg * Q_BLOCK, Dh), lambda b, g, i: (b, g, i, 0)),
            per_group((n_cmp, Dh)), per_group((Dh, n_cmp)),
            per_group((S, Dh)), per_group((V_ROWS, S)),
            per_group((S, Dh)), per_group((V_ROWS, S)),
            pl.BlockSpec((None, None, 32, Q_BLOCK), lambda b, g, i: (b, i, 0, 0)),
        ],
        out_specs=pl.BlockSpec((None, Q_BLOCK, hpg * Dh), lambda b, g, i: (b, i, g)),
        out_shape=jax.ShapeDtypeStruct((B, S, NSA_HEADS * Dh), BF16),
        scratch_shapes=[
            pltpu.VMEM((n_cmp + 16, Q_BLOCK), F32),
            pltpu.VMEM((n_sel, Q_BLOCK), F32),
            pltpu.VMEM((Dh, hpg * Q_BLOCK), F32),
            pltpu.VMEM((Dh, hpg * Q_BLOCK), F32),
        ],
        compiler_params=pltpu.CompilerParams(
            dimension_semantics=("parallel", "parallel", "arbitrary"), vmem_limit_bytes=VMEM_LIMIT),
        name="nsa_attention",
    )(qs, kcc, vcct, ks, vst, kw, vwt, gt)


def _merge_body(ya_ref, yb_ref, yc_ref, ga_ref, gb_ref, gc_ref, h_ref, wb_ref, wo_ref, o_ref):
    merged = jax.nn.sigmoid(ga_ref[...].astype(F32)) * _dot(ya_ref[...], wb_ref[0])
    merged = merged + jax.nn.sigmoid(gb_ref[...].astype(F32)) * _dot(yb_ref[...], wb_ref[1])
    merged = merged + jax.nn.sigmoid(gc_ref[...].astype(F32)) * _dot(yc_ref[...], wb_ref[2])
    o_ref[...] = h_ref[...] + _dot(merged.astype(BF16), wo_ref[...])


def _merge(ya, yb, yc, main, h, w_branch, w_out, tm=1024):
    T, D = h.shape
    y_spec = pl.BlockSpec((tm, 512), lambda i: (i, 0))
    gate_spec = lambda k: pl.BlockSpec((tm, D), lambda i: (i, MAIN_GM // D + k))
    return pl.pallas_call(
        _merge_body,
        grid=(T // tm,),
        in_specs=[
            y_spec, y_spec, y_spec, gate_spec(0), gate_spec(1), gate_spec(2),
            pl.BlockSpec((tm, D), lambda i: (i, 0)),
            pl.BlockSpec((3, 512, D), lambda i: (0, 0, 0)),
            pl.BlockSpec((D, D), lambda i: (0, 0)),
        ],
        out_specs=pl.BlockSpec((tm, D), lambda i: (i, 0)),
        out_shape=jax.ShapeDtypeStruct((T, D), F32),
        compiler_params=pltpu.CompilerParams(dimension_semantics=("parallel",), vmem_limit_bytes=VMEM_LIMIT),
        name="merge_out",
    )(ya, yb, yc, main, main, main, h, w_branch, w_out)


def _ffn_body(h_ref, gf_ref, w1_ref, w2_ref, gp_ref, wg_ref, p_ref, wp_ref, gl_ref, o_ref, f_scr, acc_scr, *, final_norm):
    j = pl.program_id(1)

    @pl.when(j == 0)
    def _():
        f_scr[...] = _rms(h_ref[...], gf_ref[...]).astype(BF16)
        acc_scr[...] = jnp.zeros_like(acc_scr)

    a = jnp.maximum(_dot(f_scr[...], w1_ref[...]), 0.0)
    acc_scr[...] += _dot((a * a).astype(BF16), w2_ref[...])

    @pl.when(j == pl.num_programs(1) - 1)
    def _():
        h2 = h_ref[...] + acc_scr[...]
        gate = jax.nn.sigmoid(_dot(_rms(h2, gp_ref[...]).astype(BF16), wg_ref[...]))
        h3 = h2 + gate * _dot(p_ref[...].astype(BF16), wp_ref[...])
        o_ref[...] = _rms(h3, gl_ref[...]) if final_norm else h3


def _ffn_ple(h, norm_ffn, w1, w2, norm_ple, w_gate, p, w_proj, norm_last, final_norm, tm=1024, tf=1024):
    T, D = h.shape
    tm = min(tm, T)
    Fdim = w1.shape[1]
    Pdim = p.shape[1]
    vec = pl.BlockSpec((1, D), lambda i, j: (0, 0))
    return pl.pallas_call(
        functools.partial(_ffn_body, final_norm=final_norm),
        grid=(T // tm, Fdim // tf),
        in_specs=[
            pl.BlockSpec((tm, D), lambda i, j: (i, 0)), vec,
            pl.BlockSpec((D, tf), lambda i, j: (0, j)),
            pl.BlockSpec((tf, D), lambda i, j: (j, 0)),
            vec,
            pl.BlockSpec((D, D), lambda i, j: (0, 0)),
            pl.BlockSpec((tm, Pdim), lambda i, j: (i, 0)),
            pl.BlockSpec((Pdim, D), lambda i, j: (0, 0)),
            vec,
        ],
        out_specs=pl.BlockSpec((tm, D), lambda i, j: (i, 0)),
        out_shape=jax.ShapeDtypeStruct((T, D), F32),
        scratch_shapes=[pltpu.VMEM((tm, D), BF16), pltpu.VMEM((tm, D), F32)],
        compiler_params=pltpu.CompilerParams(
            dimension_semantics=("parallel", "arbitrary"), vmem_limit_bytes=VMEM_LIMIT),
        name="ffn_ple",
    )(h, norm_ffn, w1, w2, norm_ple, w_gate, p, w_proj, norm_last)


def _split_w_in(w):
    parts, off = [], 0
    for sz in IN_SIZES:
        parts.append(w[:, off:off + sz])
        off += sz
    u, q_nsa, kv_nsa, g_nsa, q_gla, k_gla, v_gla, a_gla, r_gla, g_merge = parts
    D = w.shape[0]
    pad = lambda n: jnp.zeros((D, n), w.dtype)
    main = jnp.concatenate([u, q_gla, k_gla, v_gla, r_gla, g_merge, a_gla, pad(MAIN_WIDTH - MAIN_A - GLA_GATE_RANK)], axis=1)
    kv = [kv_nsa[:, s * LANES:(s + 1) * LANES] for s in range(6)]
    nsa = jnp.concatenate([q_nsa, kv[0], kv[2], kv[4], kv[1], kv[3], kv[5], g_nsa, pad(NSA_WIDTH - NSA_G - 24)], axis=1)
    return main.astype(BF16), nsa.astype(BF16)


def kernel(x, p, positions, norm_mix, w_in, pool_w, pool_scale, cmp_pos_k, cmp_w_k, cmp_pos_v, cmp_w_v, gla_w_gate, gla_b_gate, gla_norm, w_branch, w_out, norm_ffn, w_ff1, w_ff2, norm_ple, w_ple_gate, w_ple_proj, norm_final):
    B, S, D = x.shape
    depth = w_in.shape[0]
    T = B * S
    Dh = NSA_HEAD_DIM
    row = lambda v: v.reshape(1, -1).astype(F32)

    half = NSA_ROT_DIM // 2
    inv_freq = jnp.power(ROPE_THETA, -jnp.arange(half, dtype=F32) * (2.0 / NSA_ROT_DIM)).reshape(half, 1)
    pos_rows = positions.reshape(B, S // Q_BLOCK, 1, Q_BLOCK)
    rope_spread = _rope_spread_matrix()

    h = x.reshape(T, D)
    for i in range(depth):
        w_main, w_nsa = _split_w_in(w_in[i])
        main = _norm_proj(h, row(norm_mix[i]), w_main, BF16, tn=MAIN_WIDTH // 3).reshape(B, S, MAIN_WIDTH)
        nsa_proj = _norm_proj(h, row(norm_mix[i]), w_nsa, F32, tn=NSA_WIDTH).reshape(B, S, NSA_WIDTH)

        y_a = _pool_mixer(main, pool_w[i].astype(BF16), row(pool_scale[i]), B, S)

        wg = jnp.zeros((LANES, GLA_HEADS * GLA_DK), BF16).at[:GLA_GATE_RANK].set(gla_w_gate[i].astype(BF16))
        y_c = _gla_mixer(main, wg, row(gla_b_gate[i]), row(gla_norm[i]), B, S)

        qs, kc, vc, ks, kw, vst, vwt, gt = _nsa_prep(nsa_proj, pos_rows, inv_freq, rope_spread, B, S)
        kcc = _compress(kc, cmp_w_k[i].reshape(CMP_BLOCK, Dh, Dh).astype(BF16),
                        cmp_pos_k[i].reshape(CMP_BLOCK, 1, Dh).astype(F32), transpose_out=False)
        vcct = _compress(vc, cmp_w_v[i].reshape(CMP_BLOCK, Dh, Dh).astype(BF16),
                         cmp_pos_v[i].reshape(CMP_BLOCK, 1, Dh).astype(F32), transpose_out=True)
        y_b = _nsa_attention(qs, kcc, vcct, ks, vst, kw, vwt, gt, B, S)

        h = _merge(y_a.reshape(T, -1), y_b.reshape(T, -1), y_c.reshape(T, -1), main.reshape(T, MAIN_WIDTH), h,
                   w_branch[i].astype(BF16), w_out[i].astype(BF16))
        h = _ffn_ple(h, row(norm_ffn[i]), w_ff1[i].astype(BF16), w_ff2[i].astype(BF16), row(norm_ple[i]),
                     w_ple_gate[i].astype(BF16), p[i].reshape(T, -1), w_ple_proj[i].astype(BF16),
                     row(norm_final), final_norm=(i == depth - 1))
    return h.reshape(B, S, D)
```

```python
import functools

import jax
import jax.numpy as jnp
import numpy as np
from jax import lax
from jax.experimental import pallas as pl
from jax.experimental.pallas import tpu as pltpu

F32, BF16 = jnp.float32, jnp.bfloat16

EPS = 1e-6
ROPE_THETA = 500000.0
POOL_WINDOWS = (2, 4, 8, 16)
POOL_GROUP_DIM = 128
NSA_HEADS = 8
NSA_KV_GROUPS = 2
NSA_HEADS_PER_GROUP = NSA_HEADS // NSA_KV_GROUPS
NSA_HEAD_DIM = 64
NSA_ROT_DIM = 16
CMP_BLOCK = 32
CMP_STRIDE = 16
SEL_BLOCK = 64
SEL_TOPK = 16
WINDOW = 512
Q_BLOCK = 128
GLA_HEADS = 4
GLA_DK = 64
GLA_DV = 128
GLA_GATE_RANK = 16
GLA_TAU = 16.0
GLA_CHUNK = 64
IN_SIZES = (512, 512, 768, 24, 256, 256, 512, 16, 512, 3072)

LANES = 128
MASKED = -1e30
LOG2E = 1.4426950408889634
Q_SCALE = NSA_HEAD_DIM ** -0.5 * LOG2E
V_ROWS = NSA_HEAD_DIM + 16
VMEM_LIMIT = 56 * 1024 * 1024

MAIN_U, MAIN_QG, MAIN_KG, MAIN_VG, MAIN_RG, MAIN_GM, MAIN_A, MAIN_WIDTH = 0, 512, 768, 1024, 1536, 2048, 5120, 5376
NSA_Q, NSA_KC, NSA_KS, NSA_KW, NSA_VC, NSA_VS, NSA_VW, NSA_G, NSA_WIDTH = 0, 512, 640, 768, 896, 1024, 1152, 1280, 1536


def _nt(a, b):
    return lax.dot_general(a, b, (((1,), (1,)), ((), ())), preferred_element_type=F32)


def _tn(a, b):
    return lax.dot_general(a, b, (((0,), (0,)), ((), ())), preferred_element_type=F32)


def _dot(a, b):
    return jnp.dot(a, b, preferred_element_type=F32)


def _rms(x, gain):
    return x * lax.rsqrt(jnp.mean(x * x, axis=-1, keepdims=True) + EPS) * gain


def _proj_body(h_ref, g_ref, w_ref, o_ref, a_scr):
    @pl.when(pl.program_id(1) == 0)
    def _():
        a_scr[...] = _rms(h_ref[...], g_ref[...]).astype(BF16)

    o_ref[...] = _dot(a_scr[...], w_ref[...]).astype(o_ref.dtype)


def _norm_proj(h, gain, w, out_dtype, tn, tm=1024):
    T, D = h.shape
    N = w.shape[1]
    tm = min(tm, T)
    return pl.pallas_call(
        _proj_body,
        grid=(T // tm, N // tn),
        in_specs=[
            pl.BlockSpec((tm, D), lambda i, j: (i, 0)),
            pl.BlockSpec((1, D), lambda i, j: (0, 0)),
            pl.BlockSpec((D, tn), lambda i, j: (0, j)),
        ],
        out_specs=pl.BlockSpec((tm, tn), lambda i, j: (i, j)),
        out_shape=jax.ShapeDtypeStruct((T, N), out_dtype),
        scratch_shapes=[pltpu.VMEM((tm, D), BF16)],
        compiler_params=pltpu.CompilerParams(
            dimension_semantics=("parallel", "arbitrary"), vmem_limit_bytes=VMEM_LIMIT),
        name="norm_proj",
    )(h, gain, w)


def _pool_body(u_ref, halo_ref, w_ref, sc_ref, o_ref, *, ts):
    i = pl.program_id(1)
    cur = u_ref[...].astype(F32)
    halo = jnp.where(i == 0, 0.0, halo_ref[...].astype(F32))
    t = i * ts + lax.broadcasted_iota(jnp.int32, (ts, 1), 0)
    for g, win in enumerate(POOL_WINDOWS):
        lo, hi = g * POOL_GROUP_DIM, (g + 1) * POOL_GROUP_DIM
        x = cur[:, lo:hi]
        e = jnp.concatenate([halo[:, lo:hi], x], axis=0)
        step = 1
        while step < win:
            e = e[step:] + e[:-step]
            step *= 2
        wsum = e[16 - (win - 1):]
        cnt = jnp.minimum(t + 1, win).astype(F32)
        pooled = wsum / cnt - x
        y = _dot(pooled.astype(BF16), w_ref[g]) * sc_ref[:, lo:hi]
        o_ref[:, lo:hi] = y.astype(o_ref.dtype)


def _pool_mixer(main, pool_w, scale, B, S, ts=512):
    ts = min(ts, S)
    halo_blocks = ts // 16
    return pl.pallas_call(
        functools.partial(_pool_body, ts=ts),
        grid=(B, S // ts),
        in_specs=[
            pl.BlockSpec((None, ts, 512), lambda b, i: (b, i, MAIN_U // 512)),
            pl.BlockSpec((None, 16, 512), lambda b, i: (b, jnp.maximum(i * halo_blocks - 1, 0), MAIN_U // 512)),
            pl.BlockSpec((4, 128, 128), lambda b, i: (0, 0, 0)),
            pl.BlockSpec((1, 512), lambda b, i: (0, 0)),
        ],
        out_specs=pl.BlockSpec((None, ts, 512), lambda b, i: (b, i, 0)),
        out_shape=jax.ShapeDtypeStruct((B, S, 512), BF16),
        compiler_params=pltpu.CompilerParams(dimension_semantics=("parallel", "arbitrary")),
        name="pool_mixer",
    )(main, main, pool_w, scale)


def _gla_body(q_ref, k_ref, v_ref, r_ref, a_ref, wg_ref, bg_ref, ng_ref, o_ref, st_ref, *, tc):
    @pl.when(pl.program_id(1) == 0)
    def _():
        st_ref[...] = jnp.zeros_like(st_ref)

    C = GLA_CHUNK
    n_chunks = tc // C
    causal = lax.broadcasted_iota(jnp.int32, (C, C), 0) >= lax.broadcasted_iota(jnp.int32, (C, C), 1)
    width = GLA_HEADS * GLA_DK
    hk = [slice(h * GLA_DK, (h + 1) * GLA_DK) for h in range(GLA_HEADS)]
    hv = [slice(h * GLA_DV, (h + 1) * GLA_DV) for h in range(GLA_HEADS)]
    rows = [slice(c * C, (c + 1) * C) for c in range(n_chunks)]

    z = _dot(a_ref[...], wg_ref[...]) + bg_ref[...]
    b = jax.nn.log_sigmoid(z) * (1.0 / GLA_TAU)
    row_in_chunk = lax.broadcasted_iota(jnp.int32, (tc, 1), 0) % C
    step = 1
    while step < C:
        shifted = jnp.concatenate([jnp.zeros((step, width), F32), b[:-step]], axis=0)
        b = b + jnp.where(row_in_chunk >= step, shifted, 0.0)
        step *= 2
    b_last = [b[(c + 1) * C - 1:(c + 1) * C, :] for c in range(n_chunks)]
    b_last_rows = jnp.concatenate([jnp.broadcast_to(bl, (C, width)) for bl in b_last], axis=0)
    qf = q_ref[...].astype(F32) * (GLA_DK ** -0.5)
    kf = k_ref[...].astype(F32)
    q_s = (qf * jnp.exp(b)).astype(BF16)
    k_s = (kf * jnp.exp(-b)).astype(BF16)
    k_t = (kf * jnp.exp(b_last_rows - b)).astype(BF16)

    att = [[jnp.where(causal, _nt(q_s[rows[c], hk[h]], k_s[rows[c], hk[h]]), 0.0).astype(BF16)
            for h in range(GLA_HEADS)] for c in range(n_chunks)]
    o_intra = [[_dot(att[c][h], v_ref[rows[c], hv[h]]) for h in range(GLA_HEADS)] for c in range(n_chunks)]
    kv = [[_tn(v_ref[rows[c], hv[h]], k_t[rows[c], hk[h]]) for h in range(GLA_HEADS)] for c in range(n_chunks)]

    state_t = [st_ref[h] for h in range(GLA_HEADS)]
    for c in range(n_chunks):
        decay = jnp.exp(b_last[c])
        for h in range(GLA_HEADS):
            o = o_intra[c][h] + _nt(q_s[rows[c], hk[h]], state_t[h].astype(BF16))
            state_t[h] = state_t[h] * decay[:, hk[h]] + kv[c][h]
            o = _rms(o, ng_ref[...])
            r = r_ref[rows[c], hv[h]].astype(F32)
            o_ref[rows[c], hv[h]] = (o * (r * jax.nn.sigmoid(r))).astype(o_ref.dtype)
    for h in range(GLA_HEADS):
        st_ref[h] = state_t[h]


def _gla_mixer(main, w_gate, b_gate, norm_g, B, S, tc=512):
    tc = min(tc, S)
    return pl.pallas_call(
        functools.partial(_gla_body, tc=tc),
        grid=(B, S // tc),
        in_specs=[
            pl.BlockSpec((None, tc, 256), lambda b, i: (b, i, MAIN_QG // 256)),
            pl.BlockSpec((None, tc, 256), lambda b, i: (b, i, MAIN_KG // 256)),
            pl.BlockSpec((None, tc, 512), lambda b, i: (b, i, MAIN_VG // 512)),
            pl.BlockSpec((None, tc, 512), lambda b, i: (b, i, MAIN_RG // 512)),
            pl.BlockSpec((None, tc, 128), lambda b, i: (b, i, MAIN_A // 128)),
            pl.BlockSpec((128, 256), lambda b, i: (0, 0)),
            pl.BlockSpec((1, 256), lambda b, i: (0, 0)),
            pl.BlockSpec((1, 128), lambda b, i: (0, 0)),
        ],
        out_specs=pl.BlockSpec((None, tc, 512), lambda b, i: (b, i, 0)),
        out_shape=jax.ShapeDtypeStruct((B, S, 512), BF16),
        scratch_shapes=[pltpu.VMEM((GLA_HEADS, GLA_DV, GLA_DK), F32)],
        compiler_params=pltpu.CompilerParams(dimension_semantics=("parallel", "arbitrary")),
        name="gla_mixer",
    )(main, main, main, main, main, w_gate, b_gate, norm_g)


def _nsa_prep_body(x_ref, pos_ref, invf_ref, spread_ref, qs_ref, kc_ref, vc_ref, ks_ref, kw_ref, vst_ref, vwt_ref,
                   gt_ref):
    half = NSA_ROT_DIM // 2
    ang = invf_ref[...] * pos_ref[...].astype(F32)

    def pieces(a):
        p1 = a.astype(BF16).astype(F32)
        p2 = (a - p1).astype(BF16).astype(F32)
        return [p1, p2, (a - p1 - p2).astype(BF16).astype(F32)]

    lhs = jnp.concatenate(pieces(jnp.cos(ang)) + pieces(jnp.sin(ang)) + [jnp.ones((2 * half, Q_BLOCK), F32)], axis=0)
    tables = _tn(lhs.astype(BF16), spread_ref[...])
    cs = tables[:, :LANES]
    s_lo = tables[:, LANES:2 * LANES]
    s_hi = tables[:, 2 * LANES:]

    def rope(x):
        return x * cs + pltpu.roll(x, LANES - half, 1) * s_lo + pltpu.roll(x, half, 1) * s_hi

    def split_groups(x, ref, dtype):
        ref[0] = x[:, :NSA_HEAD_DIM].astype(dtype)
        ref[1] = x[:, NSA_HEAD_DIM:].astype(dtype)

    for pair in range(NSA_HEADS // 2):
        qr = rope(x_ref[:, NSA_Q + pair * LANES:NSA_Q + (pair + 1) * LANES]) * Q_SCALE
        for sub in range(2):
            head = 2 * pair + sub
            grp, hp = head // NSA_HEADS_PER_GROUP, head % NSA_HEADS_PER_GROUP
            qs_ref[grp, hp * Q_BLOCK:(hp + 1) * Q_BLOCK, :] = (
                qr[:, sub * NSA_HEAD_DIM:(sub + 1) * NSA_HEAD_DIM].astype(BF16))
    split_groups(rope(x_ref[:, NSA_KC:NSA_KC + LANES]), kc_ref, F32)
    split_groups(rope(x_ref[:, NSA_KS:NSA_KS + LANES]), ks_ref, BF16)
    split_groups(rope(x_ref[:, NSA_KW:NSA_KW + LANES]), kw_ref, BF16)
    split_groups(x_ref[:, NSA_VC:NSA_VC + LANES], vc_ref, F32)
    vst = x_ref[:, NSA_VS:NSA_VS + LANES].T
    vwt = x_ref[:, NSA_VW:NSA_VW + LANES].T
    ones_row = (lax.broadcasted_iota(jnp.int32, (V_ROWS - NSA_HEAD_DIM, Q_BLOCK), 0) == 0).astype(BF16)
    for grp in range(NSA_KV_GROUPS):
        for ref, v in ((vst_ref, vst), (vwt_ref, vwt)):
            ref[grp, :NSA_HEAD_DIM, :] = v[grp * NSA_HEAD_DIM:(grp + 1) * NSA_HEAD_DIM].astype(BF16)
            ref[grp, NSA_HEAD_DIM:, :] = ones_row
    gt_ref[...] = jax.nn.sigmoid(x_ref[:, NSA_G:NSA_G + LANES]).T[:32]


def _rope_spread_matrix():
    half = NSA_ROT_DIM // 2
    lane = np.arange(LANES)
    d = lane % NSA_HEAD_DIM
    m = np.zeros((8 * half, 3 * LANES), np.float32)
    for piece in range(3):
        for f in range(half):
            hit = d % half == f
            m[piece * half + f, :LANES] = hit & (d < NSA_ROT_DIM)
            m[(3 + piece) * half + f, LANES:2 * LANES] = -1.0 * (hit & (d < half))
            m[(3 + piece) * half + f, 2 * LANES:] = hit & (d >= half) & (d < NSA_ROT_DIM)
    m[6 * half, :LANES] = d >= NSA_ROT_DIM
    return jnp.asarray(m, BF16)


def _nsa_prep(nsa_proj, positions, inv_freq, rope_spread, B, S):
    G, Dh, nq = NSA_KV_GROUPS, NSA_HEAD_DIM, S // Q_BLOCK
    tok = lambda dtype: jax.ShapeDtypeStruct((B, G, S, Dh), dtype)
    tok_spec = pl.BlockSpec((None, G, Q_BLOCK, Dh), lambda b, i: (b, 0, i, 0))
    tr = jax.ShapeDtypeStruct((B, G, V_ROWS, S), BF16)
    tr_spec = pl.BlockSpec((None, G, V_ROWS, Q_BLOCK), lambda b, i: (b, 0, 0, i))
    return pl.pallas_call(
        _nsa_prep_body,
        grid=(B, nq),
        in_specs=[
            pl.BlockSpec((None, Q_BLOCK, NSA_WIDTH), lambda b, i: (b, i, 0)),
            pl.BlockSpec((None, None, 1, Q_BLOCK), lambda b, i: (b, i, 0, 0)),
            pl.BlockSpec(inv_freq.shape, lambda b, i: (0, 0)),
            pl.BlockSpec(rope_spread.shape, lambda b, i: (0, 0)),
        ],
        out_specs=[
            pl.BlockSpec((None, G, NSA_HEADS_PER_GROUP * Q_BLOCK, Dh), lambda b, i: (b, 0, i, 0)),
            tok_spec, tok_spec, tok_spec, tok_spec, tr_spec, tr_spec,
            pl.BlockSpec((None, None, 32, Q_BLOCK), lambda b, i: (b, i, 0, 0)),
        ],
        out_shape=[
            jax.ShapeDtypeStruct((B, G, NSA_HEADS_PER_GROUP * S, Dh), BF16),
            tok(F32), tok(F32), tok(BF16), tok(BF16), tr, tr,
            jax.ShapeDtypeStruct((B, nq, 32, Q_BLOCK), F32),
        ],
        compiler_params=pltpu.CompilerParams(dimension_semantics=("parallel", "parallel")),
        name="nsa_prep",
    )(nsa_proj, positions, inv_freq, rope_spread)


def _cmp_body(x_ref, w_ref, p_ref, o_ref, *, n_blk, transpose_out):
    half = CMP_BLOCK // 2
    first = jnp.zeros((n_blk, NSA_HEAD_DIM), F32)
    second = jnp.zeros((n_blk, NSA_HEAD_DIM), F32)
    for l in range(half):
        rows = x_ref[pl.ds(l, n_blk, stride=CMP_STRIDE), :]
        first = first + _dot((rows + p_ref[l]).astype(BF16), w_ref[l])
        second = second + _dot((rows + p_ref[half + l]).astype(BF16), w_ref[half + l])
    out = first + jnp.concatenate([second[1:], jnp.zeros((1, NSA_HEAD_DIM), F32)], axis=0)
    o_ref[...] = (out.T if transpose_out else out).astype(o_ref.dtype)


def _compress(x, w, pos, transpose_out):
    B, G, S, Dh = x.shape
    n_blk = S // CMP_STRIDE
    out_block = (None, None, Dh, n_blk) if transpose_out else (None, None, n_blk, Dh)
    out_shape = (B, G, Dh, n_blk) if transpose_out else (B, G, n_blk, Dh)
    return pl.pallas_call(
        functools.partial(_cmp_body, n_blk=n_blk, transpose_out=transpose_out),
        grid=(B, G),
        in_specs=[
            pl.BlockSpec((None, None, S, Dh), lambda b, g: (b, g, 0, 0)),
            pl.BlockSpec((CMP_BLOCK, Dh, Dh), lambda b, g: (0, 0, 0)),
            pl.BlockSpec((CMP_BLOCK, 1, Dh), lambda b, g: (0, 0, 0)),
        ],
        out_specs=pl.BlockSpec(out_block, lambda b, g: (b, g, 0, 0)),
        out_shape=jax.ShapeDtypeStruct(out_shape, BF16),
        compiler_params=pltpu.CompilerParams(
            dimension_semantics=("parallel", "parallel"), vmem_limit_bytes=VMEM_LIMIT),
        name="nsa_compress",
    )(x, w, pos)


def _nsa_body(q_ref, kcc_ref, vcct_ref, ks_ref, vst_ref, kw_ref, vwt_ref, gt_ref, o_ref,
              imp_scr, sel_scr, ocmp_scr, owin_scr, oslc_scr, *, kt_size, sub_size, cmp_chunk):
    grp = pl.program_id(1)
    t0 = pl.program_id(2) * Q_BLOCK
    hpg = NSA_HEADS_PER_GROUP
    n_cmp = kcc_ref.shape[0]
    n_sel = sel_scr.shape[0]
    q = q_ref[...]
    tq1 = t0 + lax.broadcasted_iota(jnp.int32, (1, Q_BLOCK), 1)
    heads = lambda x: jnp.concatenate([x] * hpg, axis=1)

    width = hpg * Q_BLOCK
    neg_inf = -jnp.inf
    eye = (lax.broadcasted_iota(jnp.int32, (Q_BLOCK, Q_BLOCK), 0)
           == lax.broadcasted_iota(jnp.int32, (Q_BLOCK, Q_BLOCK), 1)).astype(BF16)
    q_aug = jnp.concatenate([jnp.concatenate([eye] * hpg, axis=0), q], axis=1)

    ch = cmp_chunk
    n_chunks = (((t0 + Q_BLOCK - CMP_BLOCK) >> (CMP_STRIDE.bit_length() - 1)) + ch) // ch
    imp_scr[...] = jnp.zeros(imp_scr.shape, F32)

    def compressed(n_sub):
        scores = []
        for sub in range(n_sub):
            rows = kcc_ref[sub * ch:(sub + 1) * ch, :]
            if sub >= n_sub - 2:
                cmp_end = (sub * ch + lax.broadcasted_iota(jnp.int32, (ch, 1), 0)) * CMP_STRIDE + (CMP_BLOCK - 1)
                bias = jnp.where(cmp_end <= tq1, 0.0, MASKED).astype(BF16)
                scores.append(_nt(jnp.concatenate([bias, rows], axis=1), q_aug))
            else:
                scores.append(_nt(rows, q))
        m = jnp.max(scores[0], axis=0, keepdims=True)
        for s in scores[1:]:
            m = jnp.maximum(m, jnp.max(s, axis=0, keepdims=True))
        es = [jnp.exp2(s - m) for s in scores]
        l = jnp.sum(es[0], axis=0, keepdims=True)
        for e in es[1:]:
            l = l + jnp.sum(e, axis=0, keepdims=True)
        inv_l = jnp.where(heads(tq1 >= CMP_BLOCK - 1), 1.0 / l, 0.0)
        acc = _dot(vcct_ref[:, 0:ch], es[0].astype(BF16))
        for sub in range(1, n_sub):
            acc = acc + _dot(vcct_ref[:, sub * ch:(sub + 1) * ch], es[sub].astype(BF16))
        ocmp_scr[...] = acc * inv_l
        for sub in range(n_sub):
            pc = es[sub] * inv_l
            p_sum = pc[:, :Q_BLOCK]
            for hp in range(1, hpg):
                p_sum = p_sum + pc[:, hp * Q_BLOCK:(hp + 1) * Q_BLOCK]
            imp_scr[8 + sub * ch:8 + (sub + 1) * ch, :] = p_sum

    for variant in range(n_cmp // ch):
        pl.when(n_chunks == variant + 1)(functools.partial(compressed, variant + 1))
    o_cmp = ocmp_scr[...]

    ratio = SEL_BLOCK // CMP_STRIDE
    sel_shift = SEL_BLOCK.bit_length() - 1
    cur = tq1 >> sel_shift
    n_forced = 3

    def window_branch():
        span = WINDOW + Q_BLOCK
        k_start = pl.multiple_of(jnp.maximum(t0 - WINDOW, 0), Q_BLOCK)
        diff = tq1 - (k_start + lax.broadcasted_iota(jnp.int32, (span, 1), 0))
        inside = pltpu.bitcast(diff, jnp.uint32) < jnp.uint32(WINDOW)
        bias = jnp.where(inside, 0.0, MASKED).astype(BF16)
        sw = _nt(jnp.concatenate([bias, kw_ref[pl.ds(k_start, span), :]], axis=1), q_aug)
        pw = jnp.exp2(sw - jnp.max(sw, axis=0, keepdims=True)).astype(BF16)
        acc_win = _dot(vwt_ref[:, pl.ds(k_start, span)], pw)
        owin_scr[...] = acc_win[:NSA_HEAD_DIM] / acc_win[NSA_HEAD_DIM:NSA_HEAD_DIM + 1]

    def select_blocks(rows):
        window_branch()
        imp = imp_scr[pl.ds(7, rows, stride=ratio), :]
        for k in range(1, ratio + 1):
            imp = imp + imp_scr[pl.ds(7 + k, rows, stride=ratio), :]
        j = lax.broadcasted_iota(jnp.int32, (rows, Q_BLOCK), 0)
        j_f = j.astype(F32)
        forced = (j == 0) | (j == cur) | (j == cur - 1)
        valid = j <= cur

        def pick(_, sc):
            best = jnp.max(sc, axis=0, keepdims=True)
            idx = jnp.min(jnp.where(sc == best, j_f, float(rows)), axis=0, keepdims=True)
            return jnp.where(j_f == idx, neg_inf, sc)

        left = lax.fori_loop(0, max(min(SEL_TOPK, n_sel) - n_forced, 0), pick,
                             jnp.where(valid & ~forced, imp, neg_inf), unroll=True)
        sel_scr[0:rows, :] = jnp.where(forced | (valid & (left == neg_inf)), 0.0, MASKED)
        if rows < n_sel:
            sel_scr[rows:n_sel, :] = jnp.full((n_sel - rows, Q_BLOCK), MASKED, F32)

    row_step = min(64, n_sel)
    last_block = (t0 + Q_BLOCK - 1) >> sel_shift
    for variant in range(n_sel // row_step):
        pl.when(last_block // row_step == variant)(functools.partial(select_blocks, (variant + 1) * row_step))

    o_win = owin_scr[...]

    blocks_per_tile = kt_size // SEL_BLOCK

    def slc_tile(kt, carry, n_sub, causal_from):
        m_run, acc = carry
        base = pl.multiple_of(kt * kt_size, kt_size)
        biases = []
        for sub in range(n_sub):
            blk0 = kt * blocks_per_tile + sub * (sub_size // SEL_BLOCK)
            bias = jnp.concatenate(
                [jnp.broadcast_to(sel_scr[pl.ds(blk0 + jb, 1), :], (SEL_BLOCK, Q_BLOCK))
                 for jb in range(sub_size // SEL_BLOCK)], axis=0)
            if sub >= causal_from:
                kpos = base + sub * sub_size + lax.broadcasted_iota(jnp.int32, (sub_size, 1), 0)
                bias = jnp.where(kpos <= tq1, bias, MASKED)
            biases.append(bias.astype(BF16))
        k_aug = jnp.concatenate([jnp.concatenate(biases, axis=0), ks_ref[pl.ds(base, n_sub * sub_size), :]], axis=1)
        scores = _nt(k_aug, q_aug)
        parts = []
        for sub in range(n_sub):
            sc = scores[sub * sub_size:(sub + 1) * sub_size]
            m_sub = jnp.max(sc, axis=0, keepdims=True)
            pe = jnp.exp2(sc - m_sub).astype(BF16)
            k0 = pl.multiple_of(base + sub * sub_size, sub_size)
            parts.append((m_sub, _dot(vst_ref[:, pl.ds(k0, sub_size)], pe)))
        m_new = m_run
        for m_sub, _ in parts:
            m_new = jnp.maximum(m_new, m_sub)
        acc = jnp.exp2(m_run - m_new) * acc
        for m_sub, acc_sub in parts:
            acc = acc + jnp.exp2(m_sub - m_new) * acc_sub
        return m_new, acc

    subs_per_tile = kt_size // sub_size
    n_full = t0 // kt_size
    carry = lax.fori_loop(
        0, n_full, functools.partial(slc_tile, n_sub=subs_per_tile, causal_from=subs_per_tile),
        (jnp.full((1, width), MASKED, F32), jnp.zeros((V_ROWS, width), F32)))
    tail_step = min(2, subs_per_tile)
    tail_len = (t0 - n_full * kt_size) // (tail_step * sub_size)
    for variant in range(subs_per_tile // tail_step):
        n_sub = tail_step * (variant + 1)

        @pl.when(tail_len == variant)
        def _(n_sub=n_sub):
            _, acc_slc = slc_tile(n_full, carry, n_sub=n_sub, causal_from=n_sub - tail_step)
            oslc_scr[...] = acc_slc[:NSA_HEAD_DIM] / acc_slc[NSA_HEAD_DIM:NSA_HEAD_DIM + 1]

    o_slc = oslc_scr[...]

    for hp in range(hpg):
        cols = slice(hp * Q_BLOCK, (hp + 1) * Q_BLOCK)
        row = (grp * hpg + hp) * 3
        o_t = (gt_ref[pl.ds(row, 1), :] * o_cmp[:, cols] + gt_ref[pl.ds(row + 1, 1), :] * o_slc[:, cols]
               + gt_ref[pl.ds(row + 2, 1), :] * o_win[:, cols])
        o_ref[:, hp * NSA_HEAD_DIM:(hp + 1) * NSA_HEAD_DIM] = o_t.T.astype(o_ref.dtype)


def _nsa_attention(qs, kcc, vcct, ks, vst, kw, vwt, gt, B, S, kt_size=4096, sub_size=256, cmp_chunk=256):
    G, Dh, nq, hpg = NSA_KV_GROUPS, NSA_HEAD_DIM, S // Q_BLOCK, NSA_HEADS_PER_GROUP
    kt_size = min(kt_size, S)
    n_cmp, n_sel = S // CMP_STRIDE, S // SEL_BLOCK
    cmp_chunk = min(cmp_chunk, n_cmp)
    per_group = lambda shape: pl.BlockSpec((None, None) + shape, lambda b, g, i: (b, g, 0, 0))
    return pl.pallas_call(
        functools.partial(_nsa_body, kt_size=kt_size, sub_size=min(sub_size, kt_size), cmp_chunk=cmp_chunk),
        grid=(B, G, nq),
        in_specs=[
            pl.BlockSpec((None, None, hpg * Q_BLOCK, Dh), lambda b, g, i: (b, g, i, 0)),
            per_group((n_cmp, Dh)), per_group((Dh, n_cmp)),
            per_group((S, Dh)), per_group((V_ROWS, S)),
            per_group((S, Dh)), per_group((V_ROWS, S)),
            pl.BlockSpec((None, None, 32, Q_BLOCK), lambda b, g, i: (b, i, 0, 0)),
        ],
        out_specs=pl.BlockSpec((None, Q_BLOCK, hpg * Dh), lambda b, g, i: (b, i, g)),
        out_shape=jax.ShapeDtypeStruct((B, S, NSA_HEADS * Dh), BF16),
        scratch_shapes=[
            pltpu.VMEM((n_cmp + 16, Q_BLOCK), F32),
            pltpu.VMEM((n_sel, Q_BLOCK), F32),
            pltpu.VMEM((Dh, hpg * Q_BLOCK), F32),
            pltpu.VMEM((Dh, hpg * Q_BLOCK), F32),
            pltpu.VMEM((Dh, hpg * Q_BLOCK), F32),
        ],
        compiler_params=pltpu.CompilerParams(
            dimension_semantics=("parallel", "parallel", "arbitrary"), vmem_limit_bytes=VMEM_LIMIT),
        name="nsa_attention",
    )(qs, kcc, vcct, ks, vst, kw, vwt, gt)


def _merge_body(ya_ref, yb_ref, yc_ref, ga_ref, gb_ref, gc_ref, h_ref, wb_ref, wo_ref, o_ref):
    merged = jax.nn.sigmoid(ga_ref[...].astype(F32)) * _dot(ya_ref[...], wb_ref[0])
    merged = merged + jax.nn.sigmoid(gb_ref[...].astype(F32)) * _dot(yb_ref[...], wb_ref[1])
    merged = merged + jax.nn.sigmoid(gc_ref[...].astype(F32)) * _dot(yc_ref[...], wb_ref[2])
    o_ref[...] = h_ref[...] + _dot(merged.astype(BF16), wo_ref[...])


def _merge(ya, yb, yc, main, h, w_branch, w_out, tm=1024):
    T, D = h.shape
    y_spec = pl.BlockSpec((tm, 512), lambda i: (i, 0))
    gate_spec = lambda k: pl.BlockSpec((tm, D), lambda i: (i, MAIN_GM // D + k))
    return pl.pallas_call(
        _merge_body,
        grid=(T // tm,),
        in_specs=[
            y_spec, y_spec, y_spec, gate_spec(0), gate_spec(1), gate_spec(2),
            pl.BlockSpec((tm, D), lambda i: (i, 0)),
            pl.BlockSpec((3, 512, D), lambda i: (0, 0, 0)),
            pl.BlockSpec((D, D), lambda i: (0, 0)),
        ],
        out_specs=pl.BlockSpec((tm, D), lambda i: (i, 0)),
        out_shape=jax.ShapeDtypeStruct((T, D), F32),
        compiler_params=pltpu.CompilerParams(dimension_semantics=("parallel",), vmem_limit_bytes=VMEM_LIMIT),
        name="merge_out",
    )(ya, yb, yc, main, main, main, h, w_branch, w_out)


def _ffn_body(h_ref, gf_ref, w1_ref, w2_ref, gp_ref, wg_ref, p_ref, wp_ref, gl_ref, o_ref, f_scr, acc_scr, *, final_norm):
    j = pl.program_id(1)

    @pl.when(j == 0)
    def _():
        f_scr[...] = _rms(h_ref[...], gf_ref[...]).astype(BF16)
        acc_scr[...] = jnp.zeros_like(acc_scr)

    a = jnp.maximum(_dot(f_scr[...], w1_ref[...]), 0.0)
    acc_scr[...] += _dot((a * a).astype(BF16), w2_ref[...])

    @pl.when(j == pl.num_programs(1) - 1)
    def _():
        h2 = h_ref[...] + acc_scr[...]
        gate = jax.nn.sigmoid(_dot(_rms(h2, gp_ref[...]).astype(BF16), wg_ref[...]))
        h3 = h2 + gate * _dot(p_ref[...].astype(BF16), wp_ref[...])
        o_ref[...] = _rms(h3, gl_ref[...]) if final_norm else h3


def _ffn_ple(h, norm_ffn, w1, w2, norm_ple, w_gate, p, w_proj, norm_last, final_norm, tm=1024, tf=1024):
    T, D = h.shape
    tm = min(tm, T)
    Fdim = w1.shape[1]
    Pdim = p.shape[1]
    vec = pl.BlockSpec((1, D), lambda i, j: (0, 0))
    return pl.pallas_call(
        functools.partial(_ffn_body, final_norm=final_norm),
        grid=(T // tm, Fdim // tf),
        in_specs=[
            pl.BlockSpec((tm, D), lambda i, j: (i, 0)), vec,
            pl.BlockSpec((D, tf), lambda i, j: (0, j)),
            pl.BlockSpec((tf, D), lambda i, j: (j, 0)),
            vec,
            pl.BlockSpec((D, D), lambda i, j: (0, 0)),
            pl.BlockSpec((tm, Pdim), lambda i, j: (i, 0)),
            pl.BlockSpec((Pdim, D), lambda i, j: (0, 0)),
            vec,
        ],
        out_specs=pl.BlockSpec((tm, D), lambda i, j: (i, 0)),
        out_shape=jax.ShapeDtypeStruct((T, D), F32),
        scratch_shapes=[pltpu.VMEM((tm, D), BF16), pltpu.VMEM((tm, D), F32)],
        compiler_params=pltpu.CompilerParams(
            dimension_semantics=("parallel", "arbitrary"), vmem_limit_bytes=VMEM_LIMIT),
        name="ffn_ple",
    )(h, norm_ffn, w1, w2, norm_ple, w_gate, p, w_proj, norm_last)


def _split_w_in(w):
    parts, off = [], 0
    for sz in IN_SIZES:
        parts.append(w[:, off:off + sz])
        off += sz
    u, q_nsa, kv_nsa, g_nsa, q_gla, k_gla, v_gla, a_gla, r_gla, g_merge = parts
    D = w.shape[0]
    pad = lambda n: jnp.zeros((D, n), w.dtype)
    main = jnp.concatenate([u, q_gla, k_gla, v_gla, r_gla, g_merge, a_gla, pad(MAIN_WIDTH - MAIN_A - GLA_GATE_RANK)], axis=1)
    kv = [kv_nsa[:, s * LANES:(s + 1) * LANES] for s in range(6)]
    nsa = jnp.concatenate([q_nsa, kv[0], kv[2], kv[4], kv[1], kv[3], kv[5], g_nsa, pad(NSA_WIDTH - NSA_G - 24)], axis=1)
    return main.astype(BF16), nsa.astype(BF16)


def kernel(x, p, positions, norm_mix, w_in, pool_w, pool_scale, cmp_pos_k, cmp_w_k, cmp_pos_v, cmp_w_v, gla_w_gate, gla_b_gate, gla_norm, w_branch, w_out, norm_ffn, w_ff1, w_ff2, norm_ple, w_ple_gate, w_ple_proj, norm_final):
    B, S, D = x.shape
    depth = w_in.shape[0]
    T = B * S
    Dh = NSA_HEAD_DIM
    row = lambda v: v.reshape(1, -1).astype(F32)

    half = NSA_ROT_DIM // 2
    inv_freq = jnp.power(ROPE_THETA, -jnp.arange(half, dtype=F32) * (2.0 / NSA_ROT_DIM)).reshape(half, 1)
    pos_rows = positions.reshape(B, S // Q_BLOCK, 1, Q_BLOCK)
    rope_spread = _rope_spread_matrix()

    h = x.reshape(T, D)
    for i in range(depth):
        w_main, w_nsa = _split_w_in(w_in[i])
        main = _norm_proj(h, row(norm_mix[i]), w_main, BF16, tn=MAIN_WIDTH // 3).reshape(B, S, MAIN_WIDTH)
        nsa_proj = _norm_proj(h, row(norm_mix[i]), w_nsa, F32, tn=NSA_WIDTH).reshape(B, S, NSA_WIDTH)

        y_a = _pool_mixer(main, pool_w[i].astype(BF16), row(pool_scale[i]), B, S)

        wg = jnp.zeros((LANES, GLA_HEADS * GLA_DK), BF16).at[:GLA_GATE_RANK].set(gla_w_gate[i].astype(BF16))
        y_c = _gla_mixer(main, wg, row(gla_b_gate[i]), row(gla_norm[i]), B, S)

        qs, kc, vc, ks, kw, vst, vwt, gt = _nsa_prep(nsa_proj, pos_rows, inv_freq, rope_spread, B, S)
        kcc = _compress(kc, cmp_w_k[i].reshape(CMP_BLOCK, Dh, Dh).astype(BF16),
                        cmp_pos_k[i].reshape(CMP_BLOCK, 1, Dh).astype(F32), transpose_out=False)
        vcct = _compress(vc, cmp_w_v[i].reshape(CMP_BLOCK, Dh, Dh).astype(BF16),
                         cmp_pos_v[i].reshape(CMP_BLOCK, 1, Dh).astype(F32), transpose_out=True)
        y_b = _nsa_attention(qs, kcc, vcct, ks, vst, kw, vwt, gt, B, S)

        h = _merge(y_a.reshape(T, -1), y_b.reshape(T, -1), y_c.reshape(T, -1), main.reshape(T, MAIN_WIDTH), h,
                   w_branch[i].astype(BF16), w_out[i].astype(BF16))
        h = _ffn_ple(h, row(norm_ffn[i]), w_ff1[i].astype(BF16), w_ff2[i].astype(BF16), row(norm_ple[i]),
                     w_ple_gate[i].astype(BF16), p[i].reshape(T, -1), w_ple_proj[i].astype(BF16),
                     row(norm_final), final_norm=(i == depth - 1))
    return h.reshape(B, S, D)
```

```python
import functools

import jax
import jax.numpy as jnp
import numpy as np
from jax import lax
from jax.experimental import pallas as pl
from jax.experimental.pallas import tpu as pltpu

F32, BF16 = jnp.float32, jnp.bfloat16

EPS = 1e-6
ROPE_THETA = 500000.0
POOL_WINDOWS = (2, 4, 8, 16)
POOL_GROUP_DIM = 128
NSA_HEADS = 8
NSA_KV_GROUPS = 2
NSA_HEADS_PER_GROUP = NSA_HEADS // NSA_KV_GROUPS
NSA_HEAD_DIM = 64
NSA_ROT_DIM = 16
CMP_BLOCK = 32
CMP_STRIDE = 16
SEL_BLOCK = 64
SEL_TOPK = 16
WINDOW = 512
Q_BLOCK = 128
GLA_HEADS = 4
GLA_DK = 64
GLA_DV = 128
GLA_GATE_RANK = 16
GLA_TAU = 16.0
GLA_CHUNK = 64
IN_SIZES = (512, 512, 768, 24, 256, 256, 512, 16, 512, 3072)

LANES = 128
MASKED = -1e30
LOG2E = 1.4426950408889634
Q_SCALE = NSA_HEAD_DIM ** -0.5 * LOG2E
V_ROWS = NSA_HEAD_DIM + 16
VMEM_LIMIT = 56 * 1024 * 1024

MAIN_U, MAIN_QG, MAIN_KG, MAIN_VG, MAIN_RG, MAIN_GM, MAIN_A, MAIN_WIDTH = 0, 512, 768, 1024, 1536, 2048, 5120, 5376
NSA_Q, NSA_KC, NSA_KS, NSA_KW, NSA_VC, NSA_VS, NSA_VW, NSA_G, NSA_WIDTH = 0, 512, 640, 768, 896, 1024, 1152, 1280, 1536


def _nt(a, b):
    return lax.dot_general(a, b, (((1,), (1,)), ((), ())), preferred_element_type=F32)


def _tn(a, b):
    return lax.dot_general(a, b, (((0,), (0,)), ((), ())), preferred_element_type=F32)


def _dot(a, b):
    return jnp.dot(a, b, preferred_element_type=F32)


def _rms(x, gain):
    return x * lax.rsqrt(jnp.mean(x * x, axis=-1, keepdims=True) + EPS) * gain


def _proj_body(h_ref, g_ref, w_ref, o_ref, a_scr):
    @pl.when(pl.program_id(1) == 0)
    def _():
        a_scr[...] = _rms(h_ref[...], g_ref[...]).astype(BF16)

    o_ref[...] = _dot(a_scr[...], w_ref[...]).astype(o_ref.dtype)


def _norm_proj(h, gain, w, out_dtype, tn, tm=1024):
    T, D = h.shape
    N = w.shape[1]
    tm = min(tm, T)
    return pl.pallas_call(
        _proj_body,
        grid=(T // tm, N // tn),
        in_specs=[
            pl.BlockSpec((tm, D), lambda i, j: (i, 0)),
            pl.BlockSpec((1, D), lambda i, j: (0, 0)),
            pl.BlockSpec((D, tn), lambda i, j: (0, j)),
        ],
        out_specs=pl.BlockSpec((tm, tn), lambda i, j: (i, j)),
        out_shape=jax.ShapeDtypeStruct((T, N), out_dtype),
        scratch_shapes=[pltpu.VMEM((tm, D), BF16)],
        compiler_params=pltpu.CompilerParams(
            dimension_semantics=("parallel", "arbitrary"), vmem_limit_bytes=VMEM_LIMIT),
        name="norm_proj",
    )(h, gain, w)


def _pool_body(u_ref, halo_ref, w_ref, sc_ref, o_ref, *, ts):
    i = pl.program_id(1)
    cur = u_ref[...].astype(F32)
    halo = jnp.where(i == 0, 0.0, halo_ref[...].astype(F32))
    t = i * ts + lax.broadcasted_iota(jnp.int32, (ts, 1), 0)
    for g, win in enumerate(POOL_WINDOWS):
        lo, hi = g * POOL_GROUP_DIM, (g + 1) * POOL_GROUP_DIM
        x = cur[:, lo:hi]
        e = jnp.concatenate([halo[:, lo:hi], x], axis=0)
        step = 1
        while step < win:
            e = e[step:] + e[:-step]
            step *= 2
        wsum = e[16 - (win - 1):]
        cnt = jnp.minimum(t + 1, win).astype(F32)
        pooled = wsum / cnt - x
        y = _dot(pooled.astype(BF16), w_ref[g]) * sc_ref[:, lo:hi]
        o_ref[:, lo:hi] = y.astype(o_ref.dtype)


def _pool_mixer(main, pool_w, scale, B, S, ts=512):
    ts = min(ts, S)
    halo_blocks = ts // 16
    return pl.pallas_call(
        functools.partial(_pool_body, ts=ts),
        grid=(B, S // ts),
        in_specs=[
            pl.BlockSpec((None, ts, 512), lambda b, i: (b, i, MAIN_U // 512)),
            pl.BlockSpec((None, 16, 512), lambda b, i: (b, jnp.maximum(i * halo_blocks - 1, 0), MAIN_U // 512)),
            pl.BlockSpec((4, 128, 128), lambda b, i: (0, 0, 0)),
            pl.BlockSpec((1, 512), lambda b, i: (0, 0)),
        ],
        out_specs=pl.BlockSpec((None, ts, 512), lambda b, i: (b, i, 0)),
        out_shape=jax.ShapeDtypeStruct((B, S, 512), BF16),
        compiler_params=pltpu.CompilerParams(dimension_semantics=("parallel", "arbitrary")),
        name="pool_mixer",
    )(main, main, pool_w, scale)


def _gla_body(q_ref, k_ref, v_ref, r_ref, a_ref, wg_ref, bg_ref, ng_ref, o_ref, st_ref, *, tc):
    @pl.when(pl.program_id(1) == 0)
    def _():
        st_ref[...] = jnp.zeros_like(st_ref)

    C = GLA_CHUNK
    n_chunks = tc // C
    causal = lax.broadcasted_iota(jnp.int32, (C, C), 0) >= lax.broadcasted_iota(jnp.int32, (C, C), 1)
    width = GLA_HEADS * GLA_DK
    hk = [slice(h * GLA_DK, (h + 1) * GLA_DK) for h in range(GLA_HEADS)]
    hv = [slice(h * GLA_DV, (h + 1) * GLA_DV) for h in range(GLA_HEADS)]
    rows = [slice(c * C, (c + 1) * C) for c in range(n_chunks)]

    z = _dot(a_ref[...], wg_ref[...]) + bg_ref[...]
    b = jax.nn.log_sigmoid(z) * (1.0 / GLA_TAU)
    row_in_chunk = lax.broadcasted_iota(jnp.int32, (tc, 1), 0) % C
    step = 1
    while step < C:
        shifted = jnp.concatenate([jnp.zeros((step, width), F32), b[:-step]], axis=0)
        b = b + jnp.where(row_in_chunk >= step, shifted, 0.0)
        step *= 2
    b_last = [b[(c + 1) * C - 1:(c + 1) * C, :] for c in range(n_chunks)]
    b_last_rows = jnp.concatenate([jnp.broadcast_to(bl, (C, width)) for bl in b_last], axis=0)
    qf = q_ref[...].astype(F32) * (GLA_DK ** -0.5)
    kf = k_ref[...].astype(F32)
    q_s = (qf * jnp.exp(b)).astype(BF16)
    k_s = (kf * jnp.exp(-b)).astype(BF16)
    k_t = (kf * jnp.exp(b_last_rows - b)).astype(BF16)

    att = [[jnp.where(causal, _nt(q_s[rows[c], hk[h]], k_s[rows[c], hk[h]]), 0.0).astype(BF16)
            for h in range(GLA_HEADS)] for c in range(n_chunks)]
    o_intra = [[_dot(att[c][h], v_ref[rows[c], hv[h]]) for h in range(GLA_HEADS)] for c in range(n_chunks)]
    kv = [[_tn(v_ref[rows[c], hv[h]], k_t[rows[c], hk[h]]) for h in range(GLA_HEADS)] for c in range(n_chunks)]

    state_t = [st_ref[h] for h in range(GLA_HEADS)]
    for c in range(n_chunks):
        decay = jnp.exp(b_last[c])
        for h in range(GLA_HEADS):
            o = o_intra[c][h] + _nt(q_s[rows[c], hk[h]], state_t[h].astype(BF16))
            state_t[h] = state_t[h] * decay[:, hk[h]] + kv[c][h]
            o = _rms(o, ng_ref[...])
            r = r_ref[rows[c], hv[h]].astype(F32)
            o_ref[rows[c], hv[h]] = (o * (r * jax.nn.sigmoid(r))).astype(o_ref.dtype)
    for h in range(GLA_HEADS):
        st_ref[h] = state_t[h]


def _gla_mixer(main, w_gate, b_gate, norm_g, B, S, tc=512):
    tc = min(tc, S)
    return pl.pallas_call(
        functools.partial(_gla_body, tc=tc),
        grid=(B, S // tc),
        in_specs=[
            pl.BlockSpec((None, tc, 256), lambda b, i: (b, i, MAIN_QG // 256)),
            pl.BlockSpec((None, tc, 256), lambda b, i: (b, i, MAIN_KG // 256)),
            pl.BlockSpec((None, tc, 512), lambda b, i: (b, i, MAIN_VG // 512)),
            pl.BlockSpec((None, tc, 512), lambda b, i: (b, i, MAIN_RG // 512)),
            pl.BlockSpec((None, tc, 128), lambda b, i: (b, i, MAIN_A // 128)),
            pl.BlockSpec((128, 256), lambda b, i: (0, 0)),
            pl.BlockSpec((1, 256), lambda b, i: (0, 0)),
            pl.BlockSpec((1, 128), lambda b, i: (0, 0)),
        ],
        out_specs=pl.BlockSpec((None, tc, 512), lambda b, i: (b, i, 0)),
        out_shape=jax.ShapeDtypeStruct((B, S, 512), BF16),
        scratch_shapes=[pltpu.VMEM((GLA_HEADS, GLA_DV, GLA_DK), F32)],
        compiler_params=pltpu.CompilerParams(dimension_semantics=("parallel", "arbitrary")),
        name="gla_mixer",
    )(main, main, main, main, main, w_gate, b_gate, norm_g)


def _nsa_prep_body(x_ref, pos_ref, invf_ref, spread_ref, qs_ref, kc_ref, vc_ref, ks_ref, kw_ref, vst_ref, vwt_ref,
                   gt_ref, *, blocks):
    half = NSA_ROT_DIM // 2
    hpg = NSA_HEADS_PER_GROUP
    ones_row = (lax.broadcasted_iota(jnp.int32, (V_ROWS - NSA_HEAD_DIM, Q_BLOCK), 0) == 0).astype(BF16)

    def pieces(a):
        p1 = a.astype(BF16).astype(F32)
        p2 = (a - p1).astype(BF16).astype(F32)
        return [p1, p2, (a - p1 - p2).astype(BF16).astype(F32)]

    for blk in range(blocks):
        tok = slice(blk * Q_BLOCK, (blk + 1) * Q_BLOCK)
        ang = invf_ref[...] * pos_ref[blk].astype(F32)
        lhs = jnp.concatenate(
            pieces(jnp.cos(ang)) + pieces(jnp.sin(ang)) + [jnp.ones((2 * half, Q_BLOCK), F32)], axis=0)
        tables = _tn(lhs.astype(BF16), spread_ref[...])
        cs = tables[:, :LANES]
        s_lo = tables[:, LANES:2 * LANES]
        s_hi = tables[:, 2 * LANES:]

        def rope(x):
            return x * cs + pltpu.roll(x, LANES - half, 1) * s_lo + pltpu.roll(x, half, 1) * s_hi

        def split_groups(x, ref, dtype):
            ref[0, tok, :] = x[:, :NSA_HEAD_DIM].astype(dtype)
            ref[1, tok, :] = x[:, NSA_HEAD_DIM:].astype(dtype)

        for pair in range(NSA_HEADS // 2):
            cols = slice(NSA_Q + pair * LANES, NSA_Q + (pair + 1) * LANES)
            qr_t = (rope(x_ref[tok, cols]) * Q_SCALE).T
            for sub in range(2):
                head = 2 * pair + sub
                grp, hp = head // hpg, head % hpg
                col0 = (blk * hpg + hp) * Q_BLOCK
                qs_ref[grp, :, col0:col0 + Q_BLOCK] = qr_t[sub * NSA_HEAD_DIM:(sub + 1) * NSA_HEAD_DIM].astype(BF16)
        split_groups(rope(x_ref[tok, NSA_KC:NSA_KC + LANES]), kc_ref, F32)
        split_groups(rope(x_ref[tok, NSA_KS:NSA_KS + LANES]), ks_ref, BF16)
        split_groups(rope(x_ref[tok, NSA_KW:NSA_KW + LANES]), kw_ref, BF16)
        split_groups(x_ref[tok, NSA_VC:NSA_VC + LANES], vc_ref, F32)
        vst = x_ref[tok, NSA_VS:NSA_VS + LANES].T
        vwt = x_ref[tok, NSA_VW:NSA_VW + LANES].T
        for grp in range(NSA_KV_GROUPS):
            for ref, v in ((vst_ref, vst), (vwt_ref, vwt)):
                ref[grp, :NSA_HEAD_DIM, tok] = v[grp * NSA_HEAD_DIM:(grp + 1) * NSA_HEAD_DIM].astype(BF16)
                ref[grp, NSA_HEAD_DIM:, tok] = ones_row
        gt_ref[blk] = jax.nn.sigmoid(x_ref[tok, NSA_G:NSA_G + LANES]).T[:32]


def _rope_spread_matrix():
    half = NSA_ROT_DIM // 2
    lane = np.arange(LANES)
    d = lane % NSA_HEAD_DIM
    m = np.zeros((8 * half, 3 * LANES), np.float32)
    for piece in range(3):
        for f in range(half):
            hit = d % half == f
            m[piece * half + f, :LANES] = hit & (d < NSA_ROT_DIM)
            m[(3 + piece) * half + f, LANES:2 * LANES] = -1.0 * (hit & (d < half))
            m[(3 + piece) * half + f, 2 * LANES:] = hit & (d >= half) & (d < NSA_ROT_DIM)
    m[6 * half, :LANES] = d >= NSA_ROT_DIM
    return jnp.asarray(m, BF16)


def _nsa_prep(nsa_proj, positions, inv_freq, rope_spread, B, S, blocks=4):
    G, Dh, nq = NSA_KV_GROUPS, NSA_HEAD_DIM, S // Q_BLOCK
    blocks = min(blocks, nq)
    ts = blocks * Q_BLOCK
    tok = lambda dtype: jax.ShapeDtypeStruct((B, G, S, Dh), dtype)
    tok_spec = pl.BlockSpec((None, G, ts, Dh), lambda b, i: (b, 0, i, 0))
    tr = jax.ShapeDtypeStruct((B, G, V_ROWS, S), BF16)
    tr_spec = pl.BlockSpec((None, G, V_ROWS, ts), lambda b, i: (b, 0, 0, i))
    return pl.pallas_call(
        functools.partial(_nsa_prep_body, blocks=blocks),
        grid=(B, nq // blocks),
        in_specs=[
            pl.BlockSpec((None, ts, NSA_WIDTH), lambda b, i: (b, i, 0)),
            pl.BlockSpec((None, blocks, 1, Q_BLOCK), lambda b, i: (b, i, 0, 0)),
            pl.BlockSpec(inv_freq.shape, lambda b, i: (0, 0)),
            pl.BlockSpec(rope_spread.shape, lambda b, i: (0, 0)),
        ],
        out_specs=[
            pl.BlockSpec((None, G, Dh, NSA_HEADS_PER_GROUP * ts), lambda b, i: (b, 0, 0, i)),
            tok_spec, tok_spec, tok_spec, tok_spec, tr_spec, tr_spec,
            pl.BlockSpec((None, blocks, 32, Q_BLOCK), lambda b, i: (b, i, 0, 0)),
        ],
        out_shape=[
            jax.ShapeDtypeStruct((B, G, Dh, NSA_HEADS_PER_GROUP * S), BF16),
            tok(F32), tok(F32), tok(BF16), tok(BF16), tr, tr,
            jax.ShapeDtypeStruct((B, nq, 32, Q_BLOCK), F32),
        ],
        compiler_params=pltpu.CompilerParams(dimension_semantics=("parallel", "parallel")),
        name="nsa_prep",
    )(nsa_proj, positions, inv_freq, rope_spread)


def _cmp_body(x_ref, w_ref, p_ref, o_ref, *, n_blk, transpose_out):
    half = CMP_BLOCK // 2
    first = jnp.zeros((n_blk, NSA_HEAD_DIM), F32)
    second = jnp.zeros((n_blk, NSA_HEAD_DIM), F32)
    for l in range(half):
        rows = x_ref[pl.ds(l, n_blk, stride=CMP_STRIDE), :]
        first = first + _dot((rows + p_ref[l]).astype(BF16), w_ref[l])
        second = second + _dot((rows + p_ref[half + l]).astype(BF16), w_ref[half + l])
    out = first + jnp.concatenate([second[1:], jnp.zeros((1, NSA_HEAD_DIM), F32)], axis=0)
    o_ref[...] = (out.T if transpose_out else out).astype(o_ref.dtype)


def _compress(x, w, pos, transpose_out):
    B, G, S, Dh = x.shape
    n_blk = S // CMP_STRIDE
    out_block = (None, None, Dh, n_blk) if transpose_out else (None, None, n_blk, Dh)
    out_shape = (B, G, Dh, n_blk) if transpose_out else (B, G, n_blk, Dh)
    return pl.pallas_call(
        functools.partial(_cmp_body, n_blk=n_blk, transpose_out=transpose_out),
        grid=(B, G),
        in_specs=[
            pl.BlockSpec((None, None, S, Dh), lambda b, g: (b, g, 0, 0)),
            pl.BlockSpec((CMP_BLOCK, Dh, Dh), lambda b, g: (0, 0, 0)),
            pl.BlockSpec((CMP_BLOCK, 1, Dh), lambda b, g: (0, 0, 0)),
        ],
        out_specs=pl.BlockSpec(out_block, lambda b, g: (b, g, 0, 0)),
        out_shape=jax.ShapeDtypeStruct(out_shape, BF16),
        compiler_params=pltpu.CompilerParams(
            dimension_semantics=("parallel", "parallel"), vmem_limit_bytes=VMEM_LIMIT),
        name="nsa_compress",
    )(x, w, pos)


def _nsa_body(q_ref, kcc_ref, vcct_ref, ks_ref, vst_ref, kw_ref, vwt_ref, gt_ref, o_ref,
              imp_scr, sel_scr, ocmp_scr, owin_scr, oslc_scr, *, kt_size, sub_size, cmp_chunk):
    grp = pl.program_id(1)
    t0 = pl.program_id(2) * Q_BLOCK
    hpg = NSA_HEADS_PER_GROUP
    n_cmp = kcc_ref.shape[0]
    n_sel = sel_scr.shape[0]
    q_t = q_ref[...]
    tq1 = t0 + lax.broadcasted_iota(jnp.int32, (1, Q_BLOCK), 1)
    heads = lambda x: jnp.concatenate([x] * hpg, axis=1)

    width = hpg * Q_BLOCK
    neg_inf = -jnp.inf
    eye = (lax.broadcasted_iota(jnp.int32, (Q_BLOCK, Q_BLOCK), 0)
           == lax.broadcasted_iota(jnp.int32, (Q_BLOCK, Q_BLOCK), 1)).astype(BF16)
    q_aug_t = jnp.concatenate([jnp.concatenate([eye] * hpg, axis=1), q_t], axis=0)

    ch = cmp_chunk
    n_chunks = (((t0 + Q_BLOCK - CMP_BLOCK) >> (CMP_STRIDE.bit_length() - 1)) + ch) // ch
    imp_scr[...] = jnp.zeros(imp_scr.shape, F32)

    def compressed(n_sub):
        scores = []
        for sub in range(n_sub):
            rows = kcc_ref[sub * ch:(sub + 1) * ch, :]
            if sub >= n_sub - 2:
                cmp_end = (sub * ch + lax.broadcasted_iota(jnp.int32, (ch, 1), 0)) * CMP_STRIDE + (CMP_BLOCK - 1)
                bias = jnp.where(cmp_end <= tq1, 0.0, MASKED).astype(BF16)
                scores.append(_dot(jnp.concatenate([bias, rows], axis=1), q_aug_t))
            else:
                scores.append(_dot(rows, q_t))
        m = jnp.max(scores[0], axis=0, keepdims=True)
        for s in scores[1:]:
            m = jnp.maximum(m, jnp.max(s, axis=0, keepdims=True))
        es = [jnp.exp2(s - m) for s in scores]
        l = jnp.sum(es[0], axis=0, keepdims=True)
        for e in es[1:]:
            l = l + jnp.sum(e, axis=0, keepdims=True)
        inv_l = jnp.where(heads(tq1 >= CMP_BLOCK - 1), 1.0 / l, 0.0)
        acc = _dot(vcct_ref[:, 0:ch], es[0].astype(BF16))
        for sub in range(1, n_sub):
            acc = acc + _dot(vcct_ref[:, sub * ch:(sub + 1) * ch], es[sub].astype(BF16))
        ocmp_scr[...] = acc * inv_l
        for sub in range(n_sub):
            pc = es[sub] * inv_l
            p_sum = pc[:, :Q_BLOCK]
            for hp in range(1, hpg):
                p_sum = p_sum + pc[:, hp * Q_BLOCK:(hp + 1) * Q_BLOCK]
            imp_scr[8 + sub * ch:8 + (sub + 1) * ch, :] = p_sum

    for variant in range(n_cmp // ch):
        pl.when(n_chunks == variant + 1)(functools.partial(compressed, variant + 1))
    o_cmp = ocmp_scr[...]

    ratio = SEL_BLOCK // CMP_STRIDE
    sel_shift = SEL_BLOCK.bit_length() - 1
    cur = tq1 >> sel_shift
    n_forced = 3

    def window_branch():
        span = WINDOW + Q_BLOCK
        k_start = pl.multiple_of(jnp.maximum(t0 - WINDOW, 0), Q_BLOCK)
        diff = tq1 - (k_start + lax.broadcasted_iota(jnp.int32, (span, 1), 0))
        inside = pltpu.bitcast(diff, jnp.uint32) < jnp.uint32(WINDOW)
        bias = jnp.where(inside, 0.0, MASKED).astype(BF16)
        sw = _dot(jnp.concatenate([bias, kw_ref[pl.ds(k_start, span), :]], axis=1), q_aug_t)
        pw = jnp.exp2(sw - jnp.max(sw, axis=0, keepdims=True)).astype(BF16)
        acc_win = _dot(vwt_ref[:, pl.ds(k_start, span)], pw)
        owin_scr[...] = acc_win[:NSA_HEAD_DIM] / acc_win[NSA_HEAD_DIM:NSA_HEAD_DIM + 1]

    def select_blocks(rows):
        window_branch()
        imp = imp_scr[pl.ds(7, rows, stride=ratio), :]
        for k in range(1, ratio + 1):
            imp = imp + imp_scr[pl.ds(7 + k, rows, stride=ratio), :]
        j = lax.broadcasted_iota(jnp.int32, (rows, Q_BLOCK), 0)
        j_f = j.astype(F32)
        forced = (j == 0) | (j == cur) | (j == cur - 1)
        valid = j <= cur

        def pick(_, sc):
            best = jnp.max(sc, axis=0, keepdims=True)
            idx = jnp.min(jnp.where(sc == best, j_f, float(rows)), axis=0, keepdims=True)
            return jnp.where(j_f == idx, neg_inf, sc)

        left = lax.fori_loop(0, max(min(SEL_TOPK, n_sel) - n_forced, 0), pick,
                             jnp.where(valid & ~forced, imp, neg_inf), unroll=True)
        sel_scr[0:rows, :] = jnp.where(forced | (valid & (left == neg_inf)), 0.0, MASKED)
        if rows < n_sel:
            sel_scr[rows:n_sel, :] = jnp.full((n_sel - rows, Q_BLOCK), MASKED, F32)

    row_step = min(64, n_sel)
    last_block = (t0 + Q_BLOCK - 1) >> sel_shift
    for variant in range(n_sel // row_step):
        pl.when(last_block // row_step == variant)(functools.partial(select_blocks, (variant + 1) * row_step))

    o_win = owin_scr[...]

    blocks_per_tile = kt_size // SEL_BLOCK

    def slc_tile(kt, carry, n_sub, causal_from):
        m_run, acc = carry
        base = pl.multiple_of(kt * kt_size, kt_size)
        biases = []
        for sub in range(n_sub):
            blk0 = kt * blocks_per_tile + sub * (sub_size // SEL_BLOCK)
            bias = jnp.concatenate(
                [jnp.broadcast_to(sel_scr[pl.ds(blk0 + jb, 1), :], (SEL_BLOCK, Q_BLOCK))
                 for jb in range(sub_size // SEL_BLOCK)], axis=0)
            if sub >= causal_from:
                kpos = base + sub * sub_size + lax.broadcasted_iota(jnp.int32, (sub_size, 1), 0)
                bias = jnp.where(kpos <= tq1, bias, MASKED)
            biases.append(bias.astype(BF16))
        k_aug = jnp.concatenate([jnp.concatenate(biases, axis=0), ks_ref[pl.ds(base, n_sub * sub_size), :]], axis=1)
        scores = _dot(k_aug, q_aug_t)
        parts = []
        for sub in range(n_sub):
            sc = scores[sub * sub_size:(sub + 1) * sub_size]
            m_sub = jnp.max(sc, axis=0, keepdims=True)
            pe = jnp.exp2(sc - m_sub).astype(BF16)
            k0 = pl.multiple_of(base + sub * sub_size, sub_size)
            parts.append((m_sub, _dot(vst_ref[:, pl.ds(k0, sub_size)], pe)))
        m_new = m_run
        for m_sub, _ in parts:
            m_new = jnp.maximum(m_new, m_sub)
        acc = jnp.exp2(m_run - m_new) * acc
        for m_sub, acc_sub in parts:
            acc = acc + jnp.exp2(m_sub - m_new) * acc_sub
        return m_new, acc

    subs_per_tile = kt_size // sub_size
    n_full = t0 // kt_size
    carry = lax.fori_loop(
        0, n_full, functools.partial(slc_tile, n_sub=subs_per_tile, causal_from=subs_per_tile),
        (jnp.full((1, width), MASKED, F32), jnp.zeros((V_ROWS, width), F32)))
    tail_step = min(2, subs_per_tile)
    tail_len = (t0 - n_full * kt_size) // (tail_step * sub_size)
    for variant in range(subs_per_tile // tail_step):
        n_sub = tail_step * (variant + 1)

        @pl.when(tail_len == variant)
        def _(n_sub=n_sub):
            _, acc_slc = slc_tile(n_full, carry, n_sub=n_sub, causal_from=n_sub - tail_step)
            oslc_scr[...] = acc_slc[:NSA_HEAD_DIM] / acc_slc[NSA_HEAD_DIM:NSA_HEAD_DIM + 1]

    o_slc = oslc_scr[...]

    for hp in range(hpg):
        cols = slice(hp * Q_BLOCK, (hp + 1) * Q_BLOCK)
        row = (grp * hpg + hp) * 3
        o_t = (gt_ref[pl.ds(row, 1), :] * o_cmp[:, cols] + gt_ref[pl.ds(row + 1, 1), :] * o_slc[:, cols]
               + gt_ref[pl.ds(row + 2, 1), :] * o_win[:, cols])
        o_ref[:, hp * NSA_HEAD_DIM:(hp + 1) * NSA_HEAD_DIM] = o_t.T.astype(o_ref.dtype)


def _nsa_attention(qs, kcc, vcct, ks, vst, kw, vwt, gt, B, S, kt_size=4096, sub_size=256, cmp_chunk=256):
    G, Dh, nq, hpg = NSA_KV_GROUPS, NSA_HEAD_DIM, S // Q_BLOCK, NSA_HEADS_PER_GROUP
    kt_size = min(kt_size, S)
    n_cmp, n_sel = S // CMP_STRIDE, S // SEL_BLOCK
    cmp_chunk = min(cmp_chunk, n_cmp)
    per_group = lambda shape: pl.BlockSpec((None, None) + shape, lambda b, g, i: (b, g, 0, 0))
    return pl.pallas_call(
        functools.partial(_nsa_body, kt_size=kt_size, sub_size=min(sub_size, kt_size), cmp_chunk=cmp_chunk),
        grid=(B, G, nq),
        in_specs=[
            pl.BlockSpec((None, None, Dh, hpg * Q_BLOCK), lambda b, g, i: (b, g, 0, i)),
            per_group((n_cmp, Dh)), per_group((Dh, n_cmp)),
            per_group((S, Dh)), per_group((V_ROWS, S)),
            per_group((S, Dh)), per_group((V_ROWS, S)),
            pl.BlockSpec((None, None, 32, Q_BLOCK), lambda b, g, i: (b, i, 0, 0)),
        ],
        out_specs=pl.BlockSpec((None, Q_BLOCK, hpg * Dh), lambda b, g, i: (b, i, g)),
        out_shape=jax.ShapeDtypeStruct((B, S, NSA_HEADS * Dh), BF16),
        scratch_shapes=[
            pltpu.VMEM((n_cmp + 16, Q_BLOCK), F32),
            pltpu.VMEM((n_sel, Q_BLOCK), F32),
            pltpu.VMEM((Dh, hpg * Q_BLOCK), F32),
            pltpu.VMEM((Dh, hpg * Q_BLOCK), F32),
            pltpu.VMEM((Dh, hpg * Q_BLOCK), F32),
        ],
        compiler_params=pltpu.CompilerParams(
            dimension_semantics=("parallel", "parallel", "arbitrary"), vmem_limit_bytes=VMEM_LIMIT),
        name="nsa_attention",
    )(qs, kcc, vcct, ks, vst, kw, vwt, gt)


def _merge_body(ya_ref, yb_ref, yc_ref, ga_ref, gb_ref, gc_ref, h_ref, wb_ref, wo_ref, o_ref):
    merged = jax.nn.sigmoid(ga_ref[...].astype(F32)) * _dot(ya_ref[...], wb_ref[0])
    merged = merged + jax.nn.sigmoid(gb_ref[...].astype(F32)) * _dot(yb_ref[...], wb_ref[1])
    merged = merged + jax.nn.sigmoid(gc_ref[...].astype(F32)) * _dot(yc_ref[...], wb_ref[2])
    o_ref[...] = h_ref[...] + _dot(merged.astype(BF16), wo_ref[...])


def _merge(ya, yb, yc, main, h, w_branch, w_out, tm=1024):
    T, D = h.shape
    y_spec = pl.BlockSpec((tm, 512), lambda i: (i, 0))
    gate_spec = lambda k: pl.BlockSpec((tm, D), lambda i: (i, MAIN_GM // D + k))
    return pl.pallas_call(
        _merge_body,
        grid=(T // tm,),
        in_specs=[
            y_spec, y_spec, y_spec, gate_spec(0), gate_spec(1), gate_spec(2),
            pl.BlockSpec((tm, D), lambda i: (i, 0)),
            pl.BlockSpec((3, 512, D), lambda i: (0, 0, 0)),
            pl.BlockSpec((D, D), lambda i: (0, 0)),
        ],
        out_specs=pl.BlockSpec((tm, D), lambda i: (i, 0)),
        out_shape=jax.ShapeDtypeStruct((T, D), F32),
        compiler_params=pltpu.CompilerParams(dimension_semantics=("parallel",), vmem_limit_bytes=VMEM_LIMIT),
        name="merge_out",
    )(ya, yb, yc, main, main, main, h, w_branch, w_out)


def _ffn_body(h_ref, gf_ref, w1_ref, w2_ref, gp_ref, wg_ref, p_ref, wp_ref, gl_ref, o_ref, f_scr, acc_scr, *, final_norm):
    j = pl.program_id(1)

    @pl.when(j == 0)
    def _():
        f_scr[...] = _rms(h_ref[...], gf_ref[...]).astype(BF16)
        acc_scr[...] = jnp.zeros_like(acc_scr)

    a = jnp.maximum(_dot(f_scr[...], w1_ref[...]), 0.0)
    acc_scr[...] += _dot((a * a).astype(BF16), w2_ref[...])

    @pl.when(j == pl.num_programs(1) - 1)
    def _():
        h2 = h_ref[...] + acc_scr[...]
        gate = jax.nn.sigmoid(_dot(_rms(h2, gp_ref[...]).astype(BF16), wg_ref[...]))
        h3 = h2 + gate * _dot(p_ref[...].astype(BF16), wp_ref[...])
        o_ref[...] = _rms(h3, gl_ref[...]) if final_norm else h3


def _ffn_ple(h, norm_ffn, w1, w2, norm_ple, w_gate, p, w_proj, norm_last, final_norm, tm=1024, tf=1024):
    T, D = h.shape
    tm = min(tm, T)
    Fdim = w1.shape[1]
    Pdim = p.shape[1]
    vec = pl.BlockSpec((1, D), lambda i, j: (0, 0))
    return pl.pallas_call(
        functools.partial(_ffn_body, final_norm=final_norm),
        grid=(T // tm, Fdim // tf),
        in_specs=[
            pl.BlockSpec((tm, D), lambda i, j: (i, 0)), vec,
            pl.BlockSpec((D, tf), lambda i, j: (0, j)),
            pl.BlockSpec((tf, D), lambda i, j: (j, 0)),
            vec,
            pl.BlockSpec((D, D), lambda i, j: (0, 0)),
            pl.BlockSpec((tm, Pdim), lambda i, j: (i, 0)),
            pl.BlockSpec((Pdim, D), lambda i, j: (0, 0)),
            vec,
        ],
        out_specs=pl.BlockSpec((tm, D), lambda i, j: (i, 0)),
        out_shape=jax.ShapeDtypeStruct((T, D), F32),
        scratch_shapes=[pltpu.VMEM((tm, D), BF16), pltpu.VMEM((tm, D), F32)],
        compiler_params=pltpu.CompilerParams(
            dimension_semantics=("parallel", "arbitrary"), vmem_limit_bytes=VMEM_LIMIT),
        name="ffn_ple",
    )(h, norm_ffn, w1, w2, norm_ple, w_gate, p, w_proj, norm_last)


def _split_w_in(w):
    parts, off = [], 0
    for sz in IN_SIZES:
        parts.append(w[:, off:off + sz])
        off += sz
    u, q_nsa, kv_nsa, g_nsa, q_gla, k_gla, v_gla, a_gla, r_gla, g_merge = parts
    D = w.shape[0]
    pad = lambda n: jnp.zeros((D, n), w.dtype)
    main = jnp.concatenate([u, q_gla, k_gla, v_gla, r_gla, g_merge, a_gla, pad(MAIN_WIDTH - MAIN_A - GLA_GATE_RANK)], axis=1)
    kv = [kv_nsa[:, s * LANES:(s + 1) * LANES] for s in range(6)]
    nsa = jnp.concatenate([q_nsa, kv[0], kv[2], kv[4], kv[1], kv[3], kv[5], g_nsa, pad(NSA_WIDTH - NSA_G - 24)], axis=1)
    return main.astype(BF16), nsa.astype(BF16)


def kernel(x, p, positions, norm_mix, w_in, pool_w, pool_scale, cmp_pos_k, cmp_w_k, cmp_pos_v, cmp_w_v, gla_w_gate, gla_b_gate, gla_norm, w_branch, w_out, norm_ffn, w_ff1, w_ff2, norm_ple, w_ple_gate, w_ple_proj, norm_final):
    B, S, D = x.shape
    depth = w_in.shape[0]
    T = B * S
    Dh = NSA_HEAD_DIM
    row = lambda v: v.reshape(1, -1).astype(F32)

    half = NSA_ROT_DIM // 2
    inv_freq = jnp.power(ROPE_THETA, -jnp.arange(half, dtype=F32) * (2.0 / NSA_ROT_DIM)).reshape(half, 1)
    pos_rows = positions.reshape(B, S // Q_BLOCK, 1, Q_BLOCK)
    rope_spread = _rope_spread_matrix()

    h = x.reshape(T, D)
    for i in range(depth):
        w_main, w_nsa = _split_w_in(w_in[i])
        main = _norm_proj(h, row(norm_mix[i]), w_main, BF16, tn=MAIN_WIDTH // 3).reshape(B, S, MAIN_WIDTH)
        nsa_proj = _norm_proj(h, row(norm_mix[i]), w_nsa, F32, tn=NSA_WIDTH).reshape(B, S, NSA_WIDTH)

        y_a = _pool_mixer(main, pool_w[i].astype(BF16), row(pool_scale[i]), B, S)

        wg = jnp.zeros((LANES, GLA_HEADS * GLA_DK), BF16).at[:GLA_GATE_RANK].set(gla_w_gate[i].astype(BF16))
        y_c = _gla_mixer(main, wg, row(gla_b_gate[i]), row(gla_norm[i]), B, S)

        qs, kc, vc, ks, kw, vst, vwt, gt = _nsa_prep(nsa_proj, pos_rows, inv_freq, rope_spread, B, S)
        kcc = _compress(kc, cmp_w_k[i].reshape(CMP_BLOCK, Dh, Dh).astype(BF16),
                        cmp_pos_k[i].reshape(CMP_BLOCK, 1, Dh).astype(F32), transpose_out=False)
        vcct = _compress(vc, cmp_w_v[i].reshape(CMP_BLOCK, Dh, Dh).astype(BF16),
                         cmp_pos_v[i].reshape(CMP_BLOCK, 1, Dh).astype(F32), transpose_out=True)
        y_b = _nsa_attention(qs, kcc, vcct, ks, vst, kw, vwt, gt, B, S)

        h = _merge(y_a.reshape(T, -1), y_b.reshape(T, -1), y_c.reshape(T, -1), main.reshape(T, MAIN_WIDTH), h,
                   w_branch[i].astype(BF16), w_out[i].astype(BF16))
        h = _ffn_ple(h, row(norm_ffn[i]), w_ff1[i].astype(BF16), w_ff2[i].astype(BF16), row(norm_ple[i]),
                     w_ple_gate[i].astype(BF16), p[i].reshape(T, -1), w_ple_proj[i].astype(BF16),
                     row(norm_final), final_norm=(i == depth - 1))
    return h.reshape(B, S, D)
```

```python
import functools

import jax
import jax.numpy as jnp
import numpy as np
from jax import lax
from jax.experimental import pallas as pl
from jax.experimental.pallas import tpu as pltpu

F32, BF16 = jnp.float32, jnp.bfloat16

EPS = 1e-6
ROPE_THETA = 500000.0
POOL_WINDOWS = (2, 4, 8, 16)
POOL_GROUP_DIM = 128
NSA_HEADS = 8
NSA_KV_GROUPS = 2
NSA_HEADS_PER_GROUP = NSA_HEADS // NSA_KV_GROUPS
NSA_HEAD_DIM = 64
NSA_ROT_DIM = 16
CMP_BLOCK = 32
CMP_STRIDE = 16
SEL_BLOCK = 64
SEL_TOPK = 16
WINDOW = 512
Q_BLOCK = 128
GLA_HEADS = 4
GLA_DK = 64
GLA_DV = 128
GLA_GATE_RANK = 16
GLA_TAU = 16.0
GLA_CHUNK = 64
IN_SIZES = (512, 512, 768, 24, 256, 256, 512, 16, 512, 3072)

LANES = 128
MASKED = -1e30
LOG2E = 1.4426950408889634
Q_SCALE = NSA_HEAD_DIM ** -0.5 * LOG2E
V_ROWS = NSA_HEAD_DIM + 16
VMEM_LIMIT = 56 * 1024 * 1024

MAIN_U, MAIN_QG, MAIN_KG, MAIN_VG, MAIN_RG, MAIN_GM, MAIN_A, MAIN_WIDTH = 0, 512, 768, 1024, 1536, 2048, 5120, 5376
NSA_Q, NSA_KC, NSA_KS, NSA_KW, NSA_VC, NSA_VS, NSA_VW, NSA_G, NSA_WIDTH = 0, 512, 640, 768, 896, 1024, 1152, 1280, 1536


def _nt(a, b):
    return lax.dot_general(a, b, (((1,), (1,)), ((), ())), preferred_element_type=F32)


def _tn(a, b):
    return lax.dot_general(a, b, (((0,), (0,)), ((), ())), preferred_element_type=F32)


def _dot(a, b):
    return jnp.dot(a, b, preferred_element_type=F32)


def _rms(x, gain):
    return x * lax.rsqrt(jnp.mean(x * x, axis=-1, keepdims=True) + EPS) * gain


def _proj_body(h_ref, g_ref, w_ref, o_ref, a_scr):
    @pl.when(pl.program_id(1) == 0)
    def _():
        a_scr[...] = _rms(h_ref[...], g_ref[...]).astype(BF16)

    o_ref[...] = _dot(a_scr[...], w_ref[...]).astype(o_ref.dtype)


def _norm_proj(h, gain, w, out_dtype, tn, tm=1024):
    T, D = h.shape
    N = w.shape[1]
    tm = min(tm, T)
    return pl.pallas_call(
        _proj_body,
        grid=(T // tm, N // tn),
        in_specs=[
            pl.BlockSpec((tm, D), lambda i, j: (i, 0)),
            pl.BlockSpec((1, D), lambda i, j: (0, 0)),
            pl.BlockSpec((D, tn), lambda i, j: (0, j)),
        ],
        out_specs=pl.BlockSpec((tm, tn), lambda i, j: (i, j)),
        out_shape=jax.ShapeDtypeStruct((T, N), out_dtype),
        scratch_shapes=[pltpu.VMEM((tm, D), BF16)],
        compiler_params=pltpu.CompilerParams(
            dimension_semantics=("parallel", "arbitrary"), vmem_limit_bytes=VMEM_LIMIT),
        name="norm_proj",
    )(h, gain, w)


def _pool_body(u_ref, halo_ref, w_ref, sc_ref, o_ref, *, ts):
    i = pl.program_id(1)
    cur = u_ref[...].astype(F32)
    halo = jnp.where(i == 0, 0.0, halo_ref[...].astype(F32))
    t = i * ts + lax.broadcasted_iota(jnp.int32, (ts, 1), 0)
    for g, win in enumerate(POOL_WINDOWS):
        lo, hi = g * POOL_GROUP_DIM, (g + 1) * POOL_GROUP_DIM
        x = cur[:, lo:hi]
        e = jnp.concatenate([halo[:, lo:hi], x], axis=0)
        step = 1
        while step < win:
            e = e[step:] + e[:-step]
            step *= 2
        wsum = e[16 - (win - 1):]
        cnt = jnp.minimum(t + 1, win).astype(F32)
        pooled = wsum / cnt - x
        y = _dot(pooled.astype(BF16), w_ref[g]) * sc_ref[:, lo:hi]
        o_ref[:, lo:hi] = y.astype(o_ref.dtype)


def _pool_mixer(main, pool_w, scale, B, S, ts=512):
    ts = min(ts, S)
    halo_blocks = ts // 16
    return pl.pallas_call(
        functools.partial(_pool_body, ts=ts),
        grid=(B, S // ts),
        in_specs=[
            pl.BlockSpec((None, ts, 512), lambda b, i: (b, i, MAIN_U // 512)),
            pl.BlockSpec((None, 16, 512), lambda b, i: (b, jnp.maximum(i * halo_blocks - 1, 0), MAIN_U // 512)),
            pl.BlockSpec((4, 128, 128), lambda b, i: (0, 0, 0)),
            pl.BlockSpec((1, 512), lambda b, i: (0, 0)),
        ],
        out_specs=pl.BlockSpec((None, ts, 512), lambda b, i: (b, i, 0)),
        out_shape=jax.ShapeDtypeStruct((B, S, 512), BF16),
        compiler_params=pltpu.CompilerParams(dimension_semantics=("parallel", "arbitrary")),
        name="pool_mixer",
    )(main, main, pool_w, scale)


def _gla_body(q_ref, k_ref, v_ref, r_ref, a_ref, wg_ref, bg_ref, ng_ref, o_ref, st_ref, *, tc):
    @pl.when(pl.program_id(1) == 0)
    def _():
        st_ref[...] = jnp.zeros_like(st_ref)

    C = GLA_CHUNK
    n_chunks = tc // C
    causal = lax.broadcasted_iota(jnp.int32, (C, C), 0) >= lax.broadcasted_iota(jnp.int32, (C, C), 1)
    width = GLA_HEADS * GLA_DK
    hk = [slice(h * GLA_DK, (h + 1) * GLA_DK) for h in range(GLA_HEADS)]
    hv = [slice(h * GLA_DV, (h + 1) * GLA_DV) for h in range(GLA_HEADS)]
    rows = [slice(c * C, (c + 1) * C) for c in range(n_chunks)]

    z = _dot(a_ref[...], wg_ref[...]) + bg_ref[...]
    b = jax.nn.log_sigmoid(z) * (1.0 / GLA_TAU)
    row_in_chunk = lax.broadcasted_iota(jnp.int32, (tc, 1), 0) % C
    step = 1
    while step < C:
        shifted = jnp.concatenate([jnp.zeros((step, width), F32), b[:-step]], axis=0)
        b = b + jnp.where(row_in_chunk >= step, shifted, 0.0)
        step *= 2
    b_last = [b[(c + 1) * C - 1:(c + 1) * C, :] for c in range(n_chunks)]
    b_last_rows = jnp.concatenate([jnp.broadcast_to(bl, (C, width)) for bl in b_last], axis=0)
    qf = q_ref[...].astype(F32) * (GLA_DK ** -0.5)
    kf = k_ref[...].astype(F32)
    q_s = (qf * jnp.exp(b)).astype(BF16)
    k_s = (kf * jnp.exp(-b)).astype(BF16)
    k_t = (kf * jnp.exp(b_last_rows - b)).astype(BF16)

    att = [[jnp.where(causal, _nt(q_s[rows[c], hk[h]], k_s[rows[c], hk[h]]), 0.0).astype(BF16)
            for h in range(GLA_HEADS)] for c in range(n_chunks)]
    o_intra = [[_dot(att[c][h], v_ref[rows[c], hv[h]]) for h in range(GLA_HEADS)] for c in range(n_chunks)]
    kv = [[_tn(v_ref[rows[c], hv[h]], k_t[rows[c], hk[h]]) for h in range(GLA_HEADS)] for c in range(n_chunks)]

    state_t = [st_ref[h] for h in range(GLA_HEADS)]
    for c in range(n_chunks):
        decay = jnp.exp(b_last[c])
        for h in range(GLA_HEADS):
            o = o_intra[c][h] + _nt(q_s[rows[c], hk[h]], state_t[h].astype(BF16))
            state_t[h] = state_t[h] * decay[:, hk[h]] + kv[c][h]
            o = _rms(o, ng_ref[...])
            r = r_ref[rows[c], hv[h]].astype(F32)
            o_ref[rows[c], hv[h]] = (o * (r * jax.nn.sigmoid(r))).astype(o_ref.dtype)
    for h in range(GLA_HEADS):
        st_ref[h] = state_t[h]


def _gla_mixer(main, w_gate, b_gate, norm_g, B, S, tc=512):
    tc = min(tc, S)
    return pl.pallas_call(
        functools.partial(_gla_body, tc=tc),
        grid=(B, S // tc),
        in_specs=[
            pl.BlockSpec((None, tc, 256), lambda b, i: (b, i, MAIN_QG // 256)),
            pl.BlockSpec((None, tc, 256), lambda b, i: (b, i, MAIN_KG // 256)),
            pl.BlockSpec((None, tc, 512), lambda b, i: (b, i, MAIN_VG // 512)),
            pl.BlockSpec((None, tc, 512), lambda b, i: (b, i, MAIN_RG // 512)),
            pl.BlockSpec((None, tc, 128), lambda b, i: (b, i, MAIN_A // 128)),
            pl.BlockSpec((128, 256), lambda b, i: (0, 0)),
            pl.BlockSpec((1, 256), lambda b, i: (0, 0)),
            pl.BlockSpec((1, 128), lambda b, i: (0, 0)),
        ],
        out_specs=pl.BlockSpec((None, tc, 512), lambda b, i: (b, i, 0)),
        out_shape=jax.ShapeDtypeStruct((B, S, 512), BF16),
        scratch_shapes=[pltpu.VMEM((GLA_HEADS, GLA_DV, GLA_DK), F32)],
        compiler_params=pltpu.CompilerParams(dimension_semantics=("parallel", "arbitrary")),
        name="gla_mixer",
    )(main, main, main, main, main, w_gate, b_gate, norm_g)


def _nsa_prep_body(x_ref, pos_ref, invf_ref, spread_ref, qs_ref, kc_ref, vc_ref, ks_ref, kw_ref, vst_ref, vwt_ref,
                   gt_ref, *, blocks):
    half = NSA_ROT_DIM // 2
    hpg = NSA_HEADS_PER_GROUP
    ones_row = (lax.broadcasted_iota(jnp.int32, (V_ROWS - NSA_HEAD_DIM, Q_BLOCK), 0) == 0).astype(BF16)

    def pieces(a):
        p1 = a.astype(BF16).astype(F32)
        p2 = (a - p1).astype(BF16).astype(F32)
        return [p1, p2, (a - p1 - p2).astype(BF16).astype(F32)]

    for blk in range(blocks):
        tok = slice(blk * Q_BLOCK, (blk + 1) * Q_BLOCK)
        ang = invf_ref[...] * pos_ref[blk].astype(F32)
        lhs = jnp.concatenate(
            pieces(jnp.cos(ang)) + pieces(jnp.sin(ang)) + [jnp.ones((2 * half, Q_BLOCK), F32)], axis=0)
        tables = _tn(lhs.astype(BF16), spread_ref[...])
        cs = tables[:, :LANES]
        s_lo = tables[:, LANES:2 * LANES]
        s_hi = tables[:, 2 * LANES:]

        def rope(x):
            return x * cs + pltpu.roll(x, LANES - half, 1) * s_lo + pltpu.roll(x, half, 1) * s_hi

        def split_groups(x, ref, dtype):
            ref[0, tok, :] = x[:, :NSA_HEAD_DIM].astype(dtype)
            ref[1, tok, :] = x[:, NSA_HEAD_DIM:].astype(dtype)

        for pair in range(NSA_HEADS // 2):
            cols = slice(NSA_Q + pair * LANES, NSA_Q + (pair + 1) * LANES)
            qr_t = (rope(x_ref[tok, cols]) * Q_SCALE).T
            for sub in range(2):
                head = 2 * pair + sub
                grp, hp = head // hpg, head % hpg
                col0 = (blk * hpg + hp) * Q_BLOCK
                qs_ref[grp, :, col0:col0 + Q_BLOCK] = qr_t[sub * NSA_HEAD_DIM:(sub + 1) * NSA_HEAD_DIM].astype(BF16)
        split_groups(rope(x_ref[tok, NSA_KC:NSA_KC + LANES]), kc_ref, F32)
        split_groups(rope(x_ref[tok, NSA_KS:NSA_KS + LANES]), ks_ref, BF16)
        split_groups(rope(x_ref[tok, NSA_KW:NSA_KW + LANES]), kw_ref, BF16)
        split_groups(x_ref[tok, NSA_VC:NSA_VC + LANES], vc_ref, F32)
        vst = x_ref[tok, NSA_VS:NSA_VS + LANES].T
        vwt = x_ref[tok, NSA_VW:NSA_VW + LANES].T
        for grp in range(NSA_KV_GROUPS):
            for ref, v in ((vst_ref, vst), (vwt_ref, vwt)):
                ref[grp, :NSA_HEAD_DIM, tok] = v[grp * NSA_HEAD_DIM:(grp + 1) * NSA_HEAD_DIM].astype(BF16)
                ref[grp, NSA_HEAD_DIM:, tok] = ones_row
        gt_ref[blk] = jax.nn.sigmoid(x_ref[tok, NSA_G:NSA_G + LANES]).T[:32]


def _rope_spread_matrix():
    half = NSA_ROT_DIM // 2
    lane = np.arange(LANES)
    d = lane % NSA_HEAD_DIM
    m = np.zeros((8 * half, 3 * LANES), np.float32)
    for piece in range(3):
        for f in range(half):
            hit = d % half == f
            m[piece * half + f, :LANES] = hit & (d < NSA_ROT_DIM)
            m[(3 + piece) * half + f, LANES:2 * LANES] = -1.0 * (hit & (d < half))
            m[(3 + piece) * half + f, 2 * LANES:] = hit & (d >= half) & (d < NSA_ROT_DIM)
    m[6 * half, :LANES] = d >= NSA_ROT_DIM
    return jnp.asarray(m, BF16)


def _nsa_prep(nsa_proj, positions, inv_freq, rope_spread, B, S, blocks=4):
    G, Dh, nq = NSA_KV_GROUPS, NSA_HEAD_DIM, S // Q_BLOCK
    blocks = min(blocks, nq)
    ts = blocks * Q_BLOCK
    tok = lambda dtype: jax.ShapeDtypeStruct((B, G, S, Dh), dtype)
    tok_spec = pl.BlockSpec((None, G, ts, Dh), lambda b, i: (b, 0, i, 0))
    tr = jax.ShapeDtypeStruct((B, G, V_ROWS, S), BF16)
    tr_spec = pl.BlockSpec((None, G, V_ROWS, ts), lambda b, i: (b, 0, 0, i))
    return pl.pallas_call(
        functools.partial(_nsa_prep_body, blocks=blocks),
        grid=(B, nq // blocks),
        in_specs=[
            pl.BlockSpec((None, ts, NSA_WIDTH), lambda b, i: (b, i, 0)),
            pl.BlockSpec((None, blocks, 1, Q_BLOCK), lambda b, i: (b, i, 0, 0)),
            pl.BlockSpec(inv_freq.shape, lambda b, i: (0, 0)),
            pl.BlockSpec(rope_spread.shape, lambda b, i: (0, 0)),
        ],
        out_specs=[
            pl.BlockSpec((None, G, Dh, NSA_HEADS_PER_GROUP * ts), lambda b, i: (b, 0, 0, i)),
            tok_spec, tok_spec, tok_spec, tok_spec, tr_spec, tr_spec,
            pl.BlockSpec((None, blocks, 32, Q_BLOCK), lambda b, i: (b, i, 0, 0)),
        ],
        out_shape=[
            jax.ShapeDtypeStruct((B, G, Dh, NSA_HEADS_PER_GROUP * S), BF16),
            tok(F32), tok(F32), tok(BF16), tok(BF16), tr, tr,
            jax.ShapeDtypeStruct((B, nq, 32, Q_BLOCK), F32),
        ],
        compiler_params=pltpu.CompilerParams(dimension_semantics=("parallel", "parallel")),
        name="nsa_prep",
    )(nsa_proj, positions, inv_freq, rope_spread)


def _cmp_body(x_ref, w_ref, p_ref, o_ref, *, n_blk, transpose_out):
    half = CMP_BLOCK // 2
    first = jnp.zeros((n_blk, NSA_HEAD_DIM), F32)
    second = jnp.zeros((n_blk, NSA_HEAD_DIM), F32)
    for l in range(half):
        rows = x_ref[pl.ds(l, n_blk, stride=CMP_STRIDE), :]
        first = first + _dot((rows + p_ref[l]).astype(BF16), w_ref[l])
        second = second + _dot((rows + p_ref[half + l]).astype(BF16), w_ref[half + l])
    out = first + jnp.concatenate([second[1:], jnp.zeros((1, NSA_HEAD_DIM), F32)], axis=0)
    o_ref[...] = (out.T if transpose_out else out).astype(o_ref.dtype)


def _compress(x, w, pos, transpose_out):
    B, G, S, Dh = x.shape
    n_blk = S // CMP_STRIDE
    out_block = (None, None, Dh, n_blk) if transpose_out else (None, None, n_blk, Dh)
    out_shape = (B, G, Dh, n_blk) if transpose_out else (B, G, n_blk, Dh)
    return pl.pallas_call(
        functools.partial(_cmp_body, n_blk=n_blk, transpose_out=transpose_out),
        grid=(B, G),
        in_specs=[
            pl.BlockSpec((None, None, S, Dh), lambda b, g: (b, g, 0, 0)),
            pl.BlockSpec((CMP_BLOCK, Dh, Dh), lambda b, g: (0, 0, 0)),
            pl.BlockSpec((CMP_BLOCK, 1, Dh), lambda b, g: (0, 0, 0)),
        ],
        out_specs=pl.BlockSpec(out_block, lambda b, g: (b, g, 0, 0)),
        out_shape=jax.ShapeDtypeStruct(out_shape, BF16),
        compiler_params=pltpu.CompilerParams(
            dimension_semantics=("parallel", "parallel"), vmem_limit_bytes=VMEM_LIMIT),
        name="nsa_compress",
    )(x, w, pos)


def _nsa_body(*refs, blocks, **tiles):
    def one_block(qi, carry):
        _nsa_block(qi, *refs, blocks=blocks, **tiles)
        return carry

    lax.fori_loop(0, blocks, one_block, 0)


def _nsa_block(qi, q_ref, kcc_ref, vcct_ref, ks_ref, vst_ref, kw_ref, vwt_ref, gt_ref, o_ref,
               imp_scr, sel_scr, ocmp_scr, owin_scr, oslc_scr, *, blocks, kt_size, sub_size, cmp_chunk):
    grp = pl.program_id(1)
    t0 = (pl.program_id(2) * blocks + qi) * Q_BLOCK
    hpg = NSA_HEADS_PER_GROUP
    n_cmp = kcc_ref.shape[0]
    n_sel = sel_scr.shape[0]
    q_t = q_ref[:, pl.ds(pl.multiple_of(qi * (hpg * Q_BLOCK), hpg * Q_BLOCK), hpg * Q_BLOCK)]
    tq1 = t0 + lax.broadcasted_iota(jnp.int32, (1, Q_BLOCK), 1)
    heads = lambda x: jnp.concatenate([x] * hpg, axis=1)

    width = hpg * Q_BLOCK
    neg_inf = -jnp.inf
    eye = (lax.broadcasted_iota(jnp.int32, (Q_BLOCK, Q_BLOCK), 0)
           == lax.broadcasted_iota(jnp.int32, (Q_BLOCK, Q_BLOCK), 1)).astype(BF16)
    q_aug_t = jnp.concatenate([jnp.concatenate([eye] * hpg, axis=1), q_t], axis=0)

    ch = cmp_chunk
    n_chunks = (((t0 + Q_BLOCK - CMP_BLOCK) >> (CMP_STRIDE.bit_length() - 1)) + ch) // ch
    imp_scr[...] = jnp.zeros(imp_scr.shape, F32)

    def compressed(n_sub):
        scores = []
        for sub in range(n_sub):
            rows = kcc_ref[sub * ch:(sub + 1) * ch, :]
            if sub >= n_sub - 2:
                cmp_end = (sub * ch + lax.broadcasted_iota(jnp.int32, (ch, 1), 0)) * CMP_STRIDE + (CMP_BLOCK - 1)
                bias = jnp.where(cmp_end <= tq1, 0.0, MASKED).astype(BF16)
                scores.append(_dot(jnp.concatenate([bias, rows], axis=1), q_aug_t))
            else:
                scores.append(_dot(rows, q_t))
        m = jnp.max(scores[0], axis=0, keepdims=True)
        for s in scores[1:]:
            m = jnp.maximum(m, jnp.max(s, axis=0, keepdims=True))
        es = [jnp.exp2(s - m) for s in scores]
        l = jnp.sum(es[0], axis=0, keepdims=True)
        for e in es[1:]:
            l = l + jnp.sum(e, axis=0, keepdims=True)
        inv_l = jnp.where(heads(tq1 >= CMP_BLOCK - 1), 1.0 / l, 0.0)
        acc = _dot(vcct_ref[:, 0:ch], es[0].astype(BF16))
        for sub in range(1, n_sub):
            acc = acc + _dot(vcct_ref[:, sub * ch:(sub + 1) * ch], es[sub].astype(BF16))
        ocmp_scr[...] = acc * inv_l
        for sub in range(n_sub):
            pc = es[sub] * inv_l
            p_sum = pc[:, :Q_BLOCK]
            for hp in range(1, hpg):
                p_sum = p_sum + pc[:, hp * Q_BLOCK:(hp + 1) * Q_BLOCK]
            imp_scr[8 + sub * ch:8 + (sub + 1) * ch, :] = p_sum

    for variant in range(n_cmp // ch):
        pl.when(n_chunks == variant + 1)(functools.partial(compressed, variant + 1))
    o_cmp = ocmp_scr[...]

    ratio = SEL_BLOCK // CMP_STRIDE
    sel_shift = SEL_BLOCK.bit_length() - 1
    cur = tq1 >> sel_shift
    n_forced = 3

    def window_branch():
        span = WINDOW + Q_BLOCK
        k_start = pl.multiple_of(jnp.maximum(t0 - WINDOW, 0), Q_BLOCK)
        diff = tq1 - (k_start + lax.broadcasted_iota(jnp.int32, (span, 1), 0))
        inside = pltpu.bitcast(diff, jnp.uint32) < jnp.uint32(WINDOW)
        bias = jnp.where(inside, 0.0, MASKED).astype(BF16)
        sw = _dot(jnp.concatenate([bias, kw_ref[pl.ds(k_start, span), :]], axis=1), q_aug_t)
        pw = jnp.exp2(sw - jnp.max(sw, axis=0, keepdims=True)).astype(BF16)
        acc_win = _dot(vwt_ref[:, pl.ds(k_start, span)], pw)
        owin_scr[...] = acc_win[:NSA_HEAD_DIM] / acc_win[NSA_HEAD_DIM:NSA_HEAD_DIM + 1]

    def select_blocks(rows):
        window_branch()
        imp = imp_scr[pl.ds(7, rows, stride=ratio), :]
        for k in range(1, ratio + 1):
            imp = imp + imp_scr[pl.ds(7 + k, rows, stride=ratio), :]
        j = lax.broadcasted_iota(jnp.int32, (rows, Q_BLOCK), 0)
        j_f = j.astype(F32)
        forced = (j == 0) | (j == cur) | (j == cur - 1)
        valid = j <= cur

        def pick(_, sc):
            best = jnp.max(sc, axis=0, keepdims=True)
            idx = jnp.min(jnp.where(sc == best, j_f, float(rows)), axis=0, keepdims=True)
            return jnp.where(j_f == idx, neg_inf, sc)

        left = lax.fori_loop(0, max(min(SEL_TOPK, n_sel) - n_forced, 0), pick,
                             jnp.where(valid & ~forced, imp, neg_inf), unroll=True)
        sel_scr[0:rows, :] = jnp.where(forced | (valid & (left == neg_inf)), 0.0, MASKED)
        if rows < n_sel:
            sel_scr[rows:n_sel, :] = jnp.full((n_sel - rows, Q_BLOCK), MASKED, F32)

    row_step = min(64, n_sel)
    last_block = (t0 + Q_BLOCK - 1) >> sel_shift
    for variant in range(n_sel // row_step):
        pl.when(last_block // row_step == variant)(functools.partial(select_blocks, (variant + 1) * row_step))

    o_win = owin_scr[...]

    blocks_per_tile = kt_size // SEL_BLOCK

    def slc_tile(kt, carry, n_sub, causal_from):
        m_run, acc = carry
        base = pl.multiple_of(kt * kt_size, kt_size)
        biases = []
        for sub in range(n_sub):
            blk0 = kt * blocks_per_tile + sub * (sub_size // SEL_BLOCK)
            bias = jnp.concatenate(
                [jnp.broadcast_to(sel_scr[pl.ds(blk0 + jb, 1), :], (SEL_BLOCK, Q_BLOCK))
                 for jb in range(sub_size // SEL_BLOCK)], axis=0)
            if sub >= causal_from:
                kpos = base + sub * sub_size + lax.broadcasted_iota(jnp.int32, (sub_size, 1), 0)
                bias = jnp.where(kpos <= tq1, bias, MASKED)
            biases.append(bias.astype(BF16))
        k_aug = jnp.concatenate([jnp.concatenate(biases, axis=0), ks_ref[pl.ds(base, n_sub * sub_size), :]], axis=1)
        scores = _dot(k_aug, q_aug_t)
        parts = []
        for sub in range(n_sub):
            sc = scores[sub * sub_size:(sub + 1) * sub_size]
            m_sub = jnp.max(sc, axis=0, keepdims=True)
            pe = jnp.exp2(sc - m_sub).astype(BF16)
            k0 = pl.multiple_of(base + sub * sub_size, sub_size)
            parts.append((m_sub, _dot(vst_ref[:, pl.ds(k0, sub_size)], pe)))
        m_new = m_run
        for m_sub, _ in parts:
            m_new = jnp.maximum(m_new, m_sub)
        acc = jnp.exp2(m_run - m_new) * acc
        for m_sub, acc_sub in parts:
            acc = acc + jnp.exp2(m_sub - m_new) * acc_sub
        return m_new, acc

    subs_per_tile = kt_size // sub_size
    n_full = t0 // kt_size
    carry = lax.fori_loop(
        0, n_full, functools.partial(slc_tile, n_sub=subs_per_tile, causal_from=subs_per_tile),
        (jnp.full((1, width), MASKED, F32), jnp.zeros((V_ROWS, width), F32)))
    tail_step = min(2, subs_per_tile)
    tail_len = (t0 - n_full * kt_size) // (tail_step * sub_size)
    for variant in range(subs_per_tile // tail_step):
        n_sub = tail_step * (variant + 1)

        @pl.when(tail_len == variant)
        def _(n_sub=n_sub):
            _, acc_slc = slc_tile(n_full, carry, n_sub=n_sub, causal_from=n_sub - tail_step)
            oslc_scr[...] = acc_slc[:NSA_HEAD_DIM] / acc_slc[NSA_HEAD_DIM:NSA_HEAD_DIM + 1]

    o_slc = oslc_scr[...]

    for hp in range(hpg):
        cols = slice(hp * Q_BLOCK, (hp + 1) * Q_BLOCK)
        row = (grp * hpg + hp) * 3
        o_t = (gt_ref[qi, pl.ds(row, 1), :] * o_cmp[:, cols] + gt_ref[qi, pl.ds(row + 1, 1), :] * o_slc[:, cols]
               + gt_ref[qi, pl.ds(row + 2, 1), :] * o_win[:, cols])
        o_ref[pl.ds(pl.multiple_of(qi * Q_BLOCK, Q_BLOCK), Q_BLOCK), hp * NSA_HEAD_DIM:(hp + 1) * NSA_HEAD_DIM] = (
            o_t.T.astype(o_ref.dtype))


def _nsa_attention(qs, kcc, vcct, ks, vst, kw, vwt, gt, B, S, blocks=4, kt_size=4096, sub_size=256, cmp_chunk=256):
    G, Dh, nq, hpg = NSA_KV_GROUPS, NSA_HEAD_DIM, S // Q_BLOCK, NSA_HEADS_PER_GROUP
    kt_size = min(kt_size, S)
    blocks = min(blocks, nq)
    n_cmp, n_sel = S // CMP_STRIDE, S // SEL_BLOCK
    cmp_chunk = min(cmp_chunk, n_cmp)
    per_group = lambda shape: pl.BlockSpec((None, None) + shape, lambda b, g, i: (b, g, 0, 0))
    return pl.pallas_call(
        functools.partial(_nsa_body, blocks=blocks, kt_size=kt_size, sub_size=min(sub_size, kt_size),
                          cmp_chunk=cmp_chunk),
        grid=(B, G, nq // blocks),
        in_specs=[
            pl.BlockSpec((None, None, Dh, blocks * hpg * Q_BLOCK), lambda b, g, i: (b, g, 0, i)),
            per_group((n_cmp, Dh)), per_group((Dh, n_cmp)),
            per_group((S, Dh)), per_group((V_ROWS, S)),
            per_group((S, Dh)), per_group((V_ROWS, S)),
            pl.BlockSpec((None, blocks, 32, Q_BLOCK), lambda b, g, i: (b, i, 0, 0)),
        ],
        out_specs=pl.BlockSpec((None, blocks * Q_BLOCK, hpg * Dh), lambda b, g, i: (b, i, g)),
        out_shape=jax.ShapeDtypeStruct((B, S, NSA_HEADS * Dh), BF16),
        scratch_shapes=[
            pltpu.VMEM((n_cmp + 16, Q_BLOCK), F32),
            pltpu.VMEM((n_sel, Q_BLOCK), F32),
            pltpu.VMEM((Dh, hpg * Q_BLOCK), F32),
            pltpu.VMEM((Dh, hpg * Q_BLOCK), F32),
            pltpu.VMEM((Dh, hpg * Q_BLOCK), F32),
        ],
        compiler_params=pltpu.CompilerParams(
            dimension_semantics=("parallel", "parallel", "arbitrary"), vmem_limit_bytes=VMEM_LIMIT),
        name="nsa_attention",
    )(qs, kcc, vcct, ks, vst, kw, vwt, gt)


def _merge_body(ya_ref, yb_ref, yc_ref, ga_ref, gb_ref, gc_ref, h_ref, wb_ref, wo_ref, o_ref):
    merged = jax.nn.sigmoid(ga_ref[...].astype(F32)) * _dot(ya_ref[...], wb_ref[0])
    merged = merged + jax.nn.sigmoid(gb_ref[...].astype(F32)) * _dot(yb_ref[...], wb_ref[1])
    merged = merged + jax.nn.sigmoid(gc_ref[...].astype(F32)) * _dot(yc_ref[...], wb_ref[2])
    o_ref[...] = h_ref[...] + _dot(merged.astype(BF16), wo_ref[...])


def _merge(ya, yb, yc, main, h, w_branch, w_out, tm=1024):
    T, D = h.shape
    y_spec = pl.BlockSpec((tm, 512), lambda i: (i, 0))
    gate_spec = lambda k: pl.BlockSpec((tm, D), lambda i: (i, MAIN_GM // D + k))
    return pl.pallas_call(
        _merge_body,
        grid=(T // tm,),
        in_specs=[
            y_spec, y_spec, y_spec, gate_spec(0), gate_spec(1), gate_spec(2),
            pl.BlockSpec((tm, D), lambda i: (i, 0)),
            pl.BlockSpec((3, 512, D), lambda i: (0, 0, 0)),
            pl.BlockSpec((D, D), lambda i: (0, 0)),
        ],
        out_specs=pl.BlockSpec((tm, D), lambda i: (i, 0)),
        out_shape=jax.ShapeDtypeStruct((T, D), F32),
        compiler_params=pltpu.CompilerParams(dimension_semantics=("parallel",), vmem_limit_bytes=VMEM_LIMIT),
        name="merge_out",
    )(ya, yb, yc, main, main, main, h, w_branch, w_out)


def _ffn_body(h_ref, gf_ref, w1_ref, w2_ref, gp_ref, wg_ref, p_ref, wp_ref, gl_ref, o_ref, f_scr, acc_scr, *, final_norm):
    j = pl.program_id(1)

    @pl.when(j == 0)
    def _():
        f_scr[...] = _rms(h_ref[...], gf_ref[...]).astype(BF16)
        acc_scr[...] = jnp.zeros_like(acc_scr)

    a = jnp.maximum(_dot(f_scr[...], w1_ref[...]), 0.0)
    acc_scr[...] += _dot((a * a).astype(BF16), w2_ref[...])

    @pl.when(j == pl.num_programs(1) - 1)
    def _():
        h2 = h_ref[...] + acc_scr[...]
        gate = jax.nn.sigmoid(_dot(_rms(h2, gp_ref[...]).astype(BF16), wg_ref[...]))
        h3 = h2 + gate * _dot(p_ref[...].astype(BF16), wp_ref[...])
        o_ref[...] = _rms(h3, gl_ref[...]) if final_norm else h3


def _ffn_ple(h, norm_ffn, w1, w2, norm_ple, w_gate, p, w_proj, norm_last, final_norm, tm=1024, tf=1024):
    T, D = h.shape
    tm = min(tm, T)
    Fdim = w1.shape[1]
    Pdim = p.shape[1]
    vec = pl.BlockSpec((1, D), lambda i, j: (0, 0))
    return pl.pallas_call(
        functools.partial(_ffn_body, final_norm=final_norm),
        grid=(T // tm, Fdim // tf),
        in_specs=[
            pl.BlockSpec((tm, D), lambda i, j: (i, 0)), vec,
            pl.BlockSpec((D, tf), lambda i, j: (0, j)),
            pl.BlockSpec((tf, D), lambda i, j: (j, 0)),
            vec,
            pl.BlockSpec((D, D), lambda i, j: (0, 0)),
            pl.BlockSpec((tm, Pdim), lambda i, j: (i, 0)),
            pl.BlockSpec((Pdim, D), lambda i, j: (0, 0)),
            vec,
        ],
        out_specs=pl.BlockSpec((tm, D), lambda i, j: (i, 0)),
        out_shape=jax.ShapeDtypeStruct((T, D), F32),
        scratch_shapes=[pltpu.VMEM((tm, D), BF16), pltpu.VMEM((tm, D), F32)],
        compiler_params=pltpu.CompilerParams(
            dimension_semantics=("parallel", "arbitrary"), vmem_limit_bytes=VMEM_LIMIT),
        name="ffn_ple",
    )(h, norm_ffn, w1, w2, norm_ple, w_gate, p, w_proj, norm_last)


def _split_w_in(w):
    parts, off = [], 0
    for sz in IN_SIZES:
        parts.append(w[:, off:off + sz])
        off += sz
    u, q_nsa, kv_nsa, g_nsa, q_gla, k_gla, v_gla, a_gla, r_gla, g_merge = parts
    D = w.shape[0]
    pad = lambda n: jnp.zeros((D, n), w.dtype)
    main = jnp.concatenate([u, q_gla, k_gla, v_gla, r_gla, g_merge, a_gla, pad(MAIN_WIDTH - MAIN_A - GLA_GATE_RANK)], axis=1)
    kv = [kv_nsa[:, s * LANES:(s + 1) * LANES] for s in range(6)]
    nsa = jnp.concatenate([q_nsa, kv[0], kv[2], kv[4], kv[1], kv[3], kv[5], g_nsa, pad(NSA_WIDTH - NSA_G - 24)], axis=1)
    return main.astype(BF16), nsa.astype(BF16)


def kernel(x, p, positions, norm_mix, w_in, pool_w, pool_scale, cmp_pos_k, cmp_w_k, cmp_pos_v, cmp_w_v, gla_w_gate, gla_b_gate, gla_norm, w_branch, w_out, norm_ffn, w_ff1, w_ff2, norm_ple, w_ple_gate, w_ple_proj, norm_final):
    B, S, D = x.shape
    depth = w_in.shape[0]
    T = B * S
    Dh = NSA_HEAD_DIM
    row = lambda v: v.reshape(1, -1).astype(F32)

    half = NSA_ROT_DIM // 2
    inv_freq = jnp.power(ROPE_THETA, -jnp.arange(half, dtype=F32) * (2.0 / NSA_ROT_DIM)).reshape(half, 1)
    pos_rows = positions.reshape(B, S // Q_BLOCK, 1, Q_BLOCK)
    rope_spread = _rope_spread_matrix()

    h = x.reshape(T, D)
    for i in range(depth):
        w_main, w_nsa = _split_w_in(w_in[i])
        main = _norm_proj(h, row(norm_mix[i]), w_main, BF16, tn=MAIN_WIDTH // 3).reshape(B, S, MAIN_WIDTH)
        nsa_proj = _norm_proj(h, row(norm_mix[i]), w_nsa, F32, tn=NSA_WIDTH).reshape(B, S, NSA_WIDTH)

        y_a = _pool_mixer(main, pool_w[i].astype(BF16), row(pool_scale[i]), B, S)

        wg = jnp.zeros((LANES, GLA_HEADS * GLA_DK), BF16).at[:GLA_GATE_RANK].set(gla_w_gate[i].astype(BF16))
        y_c = _gla_mixer(main, wg, row(gla_b_gate[i]), row(gla_norm[i]), B, S)

        qs, kc, vc, ks, kw, vst, vwt, gt = _nsa_prep(nsa_proj, pos_rows, inv_freq, rope_spread, B, S)
        kcc = _compress(kc, cmp_w_k[i].reshape(CMP_BLOCK, Dh, Dh).astype(BF16),
                        cmp_pos_k[i].reshape(CMP_BLOCK, 1, Dh).astype(F32), transpose_out=False)
        vcct = _compress(vc, cmp_w_v[i].reshape(CMP_BLOCK, Dh, Dh).astype(BF16),
                         cmp_pos_v[i].reshape(CMP_BLOCK, 1, Dh).astype(F32), transpose_out=True)
        y_b = _nsa_attention(qs, kcc, vcct, ks, vst, kw, vwt, gt, B, S)

        h = _merge(y_a.reshape(T, -1), y_b.reshape(T, -1), y_c.reshape(T, -1), main.reshape(T, MAIN_WIDTH), h,
                   w_branch[i].astype(BF16), w_out[i].astype(BF16))
        h = _ffn_ple(h, row(norm_ffn[i]), w_ff1[i].astype(BF16), w_ff2[i].astype(BF16), row(norm_ple[i]),
                     w_ple_gate[i].astype(BF16), p[i].reshape(T, -1), w_ple_proj[i].astype(BF16),
                     row(norm_final), final_norm=(i == depth - 1))
    return h.reshape(B, S, D)
```

```python
import functools

import jax
import jax.numpy as jnp
import numpy as np
from jax import lax
from jax.experimental import pallas as pl
from jax.experimental.pallas import tpu as pltpu

F32, BF16 = jnp.float32, jnp.bfloat16

EPS = 1e-6
ROPE_THETA = 500000.0
POOL_WINDOWS = (2, 4, 8, 16)
POOL_GROUP_DIM = 128
POOL_HALO = max(POOL_WINDOWS)
NSA_HEADS = 8
NSA_KV_GROUPS = 2
NSA_HEADS_PER_GROUP = NSA_HEADS // NSA_KV_GROUPS
NSA_HEAD_DIM = 64
NSA_ROT_DIM = 16
CMP_BLOCK = 32
CMP_STRIDE = 16
SEL_BLOCK = 64
SEL_TOPK = 16
WINDOW = 512
Q_BLOCK = 128
GLA_HEADS = 4
GLA_DK = 64
GLA_DV = 128
GLA_GATE_RANK = 16
GLA_TAU = 16.0
GLA_CHUNK = 64
IN_SIZES = (512, 512, 768, 24, 256, 256, 512, 16, 512, 3072)

LANES = 128
MASKED = -1e30
LOG2E = 1.4426950408889634
Q_SCALE = NSA_HEAD_DIM ** -0.5 * LOG2E
V_ROWS = NSA_HEAD_DIM + 16
IMP_PAD = 8
GATE_ROWS = 32
TOPK_ROW_STEP = 64
VMEM_LIMIT = 56 * 1024 * 1024

MAIN_U, MAIN_QG, MAIN_KG, MAIN_VG, MAIN_RG, MAIN_GM, MAIN_A, MAIN_WIDTH = 0, 512, 768, 1024, 1536, 2048, 5120, 5376
NSA_Q, NSA_KC, NSA_KS, NSA_KW, NSA_VC, NSA_VS, NSA_VW, NSA_G, NSA_WIDTH = 0, 512, 640, 768, 896, 1024, 1152, 1280, 1536


def _nt(a, b):
    return lax.dot_general(a, b, (((1,), (1,)), ((), ())), preferred_element_type=F32)


def _tn(a, b):
    return lax.dot_general(a, b, (((0,), (0,)), ((), ())), preferred_element_type=F32)


def _dot(a, b):
    return jnp.dot(a, b, preferred_element_type=F32)


def _rms(x, gain):
    return x * lax.rsqrt(jnp.mean(x * x, axis=-1, keepdims=True) + EPS) * gain


def _proj_body(h_ref, g_ref, w_ref, o_ref, a_scr):
    @pl.when(pl.program_id(1) == 0)
    def _():
        a_scr[...] = _rms(h_ref[...], g_ref[...]).astype(BF16)

    o_ref[...] = _dot(a_scr[...], w_ref[...]).astype(o_ref.dtype)


def _norm_proj(h, gain, w, out_dtype, tn, tm=1024):
    T, D = h.shape
    N = w.shape[1]
    tm = min(tm, T)
    return pl.pallas_call(
        _proj_body,
        grid=(T // tm, N // tn),
        in_specs=[
            pl.BlockSpec((tm, D), lambda i, j: (i, 0)),
            pl.BlockSpec((1, D), lambda i, j: (0, 0)),
            pl.BlockSpec((D, tn), lambda i, j: (0, j)),
        ],
        out_specs=pl.BlockSpec((tm, tn), lambda i, j: (i, j)),
        out_shape=jax.ShapeDtypeStruct((T, N), out_dtype),
        scratch_shapes=[pltpu.VMEM((tm, D), BF16)],
        compiler_params=pltpu.CompilerParams(
            dimension_semantics=("parallel", "arbitrary"), vmem_limit_bytes=VMEM_LIMIT),
        name="norm_proj",
    )(h, gain, w)


def _pool_body(u_ref, halo_ref, w_ref, sc_ref, o_ref, *, ts):
    i = pl.program_id(1)
    cur = u_ref[...].astype(F32)
    halo = jnp.where(i == 0, 0.0, halo_ref[...].astype(F32))
    t = i * ts + lax.broadcasted_iota(jnp.int32, (ts, 1), 0)
    for g, win in enumerate(POOL_WINDOWS):
        lo, hi = g * POOL_GROUP_DIM, (g + 1) * POOL_GROUP_DIM
        x = cur[:, lo:hi]
        e = jnp.concatenate([halo[:, lo:hi], x], axis=0)
        step = 1
        while step < win:
            e = e[step:] + e[:-step]
            step *= 2
        wsum = e[POOL_HALO - (win - 1):]
        cnt = jnp.minimum(t + 1, win).astype(F32)
        pooled = wsum / cnt - x
        y = _dot(pooled.astype(BF16), w_ref[g]) * sc_ref[:, lo:hi]
        o_ref[:, lo:hi] = y.astype(o_ref.dtype)


def _pool_mixer(main, pool_w, scale, B, S, ts=512):
    ts = min(ts, S)
    halo_blocks = ts // POOL_HALO
    return pl.pallas_call(
        functools.partial(_pool_body, ts=ts),
        grid=(B, S // ts),
        in_specs=[
            pl.BlockSpec((None, ts, 512), lambda b, i: (b, i, MAIN_U // 512)),
            pl.BlockSpec((None, POOL_HALO, 512), lambda b, i: (b, jnp.maximum(i * halo_blocks - 1, 0), MAIN_U // 512)),
            pl.BlockSpec((4, 128, 128), lambda b, i: (0, 0, 0)),
            pl.BlockSpec((1, 512), lambda b, i: (0, 0)),
        ],
        out_specs=pl.BlockSpec((None, ts, 512), lambda b, i: (b, i, 0)),
        out_shape=jax.ShapeDtypeStruct((B, S, 512), BF16),
        compiler_params=pltpu.CompilerParams(dimension_semantics=("parallel", "arbitrary")),
        name="pool_mixer",
    )(main, main, pool_w, scale)


def _gla_body(q_ref, k_ref, v_ref, r_ref, a_ref, wg_ref, bg_ref, ng_ref, o_ref, st_ref, *, tc):
    @pl.when(pl.program_id(1) == 0)
    def _():
        st_ref[...] = jnp.zeros_like(st_ref)

    C = GLA_CHUNK
    n_chunks = tc // C
    causal = lax.broadcasted_iota(jnp.int32, (C, C), 0) >= lax.broadcasted_iota(jnp.int32, (C, C), 1)
    width = GLA_HEADS * GLA_DK
    hk = [slice(h * GLA_DK, (h + 1) * GLA_DK) for h in range(GLA_HEADS)]
    hv = [slice(h * GLA_DV, (h + 1) * GLA_DV) for h in range(GLA_HEADS)]
    rows = [slice(c * C, (c + 1) * C) for c in range(n_chunks)]

    z = _dot(a_ref[...], wg_ref[...]) + bg_ref[...]
    b = jax.nn.log_sigmoid(z) * (1.0 / GLA_TAU)
    row_in_chunk = lax.broadcasted_iota(jnp.int32, (tc, 1), 0) % C
    step = 1
    while step < C:
        shifted = jnp.concatenate([jnp.zeros((step, width), F32), b[:-step]], axis=0)
        b = b + jnp.where(row_in_chunk >= step, shifted, 0.0)
        step *= 2
    b_last = [b[(c + 1) * C - 1:(c + 1) * C, :] for c in range(n_chunks)]
    b_last_rows = jnp.concatenate([jnp.broadcast_to(bl, (C, width)) for bl in b_last], axis=0)
    qf = q_ref[...].astype(F32) * (GLA_DK ** -0.5)
    kf = k_ref[...].astype(F32)
    q_s = (qf * jnp.exp(b)).astype(BF16)
    k_s = (kf * jnp.exp(-b)).astype(BF16)
    k_t = (kf * jnp.exp(b_last_rows - b)).astype(BF16)

    att = [[jnp.where(causal, _nt(q_s[rows[c], hk[h]], k_s[rows[c], hk[h]]), 0.0).astype(BF16)
            for h in range(GLA_HEADS)] for c in range(n_chunks)]
    o_intra = [[_dot(att[c][h], v_ref[rows[c], hv[h]]) for h in range(GLA_HEADS)] for c in range(n_chunks)]
    kv = [[_tn(v_ref[rows[c], hv[h]], k_t[rows[c], hk[h]]) for h in range(GLA_HEADS)] for c in range(n_chunks)]

    state_t = [st_ref[h] for h in range(GLA_HEADS)]
    for c in range(n_chunks):
        decay = jnp.exp(b_last[c])
        for h in range(GLA_HEADS):
            o = o_intra[c][h] + _nt(q_s[rows[c], hk[h]], state_t[h].astype(BF16))
            state_t[h] = state_t[h] * decay[:, hk[h]] + kv[c][h]
            o = _rms(o, ng_ref[...])
            r = r_ref[rows[c], hv[h]].astype(F32)
            o_ref[rows[c], hv[h]] = (o * (r * jax.nn.sigmoid(r))).astype(o_ref.dtype)
    for h in range(GLA_HEADS):
        st_ref[h] = state_t[h]


def _gla_mixer(main, w_gate, b_gate, norm_g, B, S, tc=512):
    tc = min(tc, S)
    return pl.pallas_call(
        functools.partial(_gla_body, tc=tc),
        grid=(B, S // tc),
        in_specs=[
            pl.BlockSpec((None, tc, 256), lambda b, i: (b, i, MAIN_QG // 256)),
            pl.BlockSpec((None, tc, 256), lambda b, i: (b, i, MAIN_KG // 256)),
            pl.BlockSpec((None, tc, 512), lambda b, i: (b, i, MAIN_VG // 512)),
            pl.BlockSpec((None, tc, 512), lambda b, i: (b, i, MAIN_RG // 512)),
            pl.BlockSpec((None, tc, 128), lambda b, i: (b, i, MAIN_A // 128)),
            pl.BlockSpec((128, 256), lambda b, i: (0, 0)),
            pl.BlockSpec((1, 256), lambda b, i: (0, 0)),
            pl.BlockSpec((1, 128), lambda b, i: (0, 0)),
        ],
        out_specs=pl.BlockSpec((None, tc, 512), lambda b, i: (b, i, 0)),
        out_shape=jax.ShapeDtypeStruct((B, S, 512), BF16),
        scratch_shapes=[pltpu.VMEM((GLA_HEADS, GLA_DV, GLA_DK), F32)],
        compiler_params=pltpu.CompilerParams(dimension_semantics=("parallel", "arbitrary")),
        name="gla_mixer",
    )(main, main, main, main, main, w_gate, b_gate, norm_g)


def _nsa_prep_body(x_ref, pos_ref, invf_ref, spread_ref, qs_ref, kc_ref, vc_ref, ks_ref, kw_ref, vst_ref, vwt_ref,
                   gt_ref, *, blocks):
    half = NSA_ROT_DIM // 2
    hpg = NSA_HEADS_PER_GROUP
    ones_row = (lax.broadcasted_iota(jnp.int32, (V_ROWS - NSA_HEAD_DIM, Q_BLOCK), 0) == 0).astype(BF16)

    def pieces(a):
        p1 = a.astype(BF16).astype(F32)
        p2 = (a - p1).astype(BF16).astype(F32)
        return [p1, p2, (a - p1 - p2).astype(BF16).astype(F32)]

    for blk in range(blocks):
        tok = slice(blk * Q_BLOCK, (blk + 1) * Q_BLOCK)
        ang = invf_ref[...] * pos_ref[blk].astype(F32)
        lhs = jnp.concatenate(
            pieces(jnp.cos(ang)) + pieces(jnp.sin(ang)) + [jnp.ones((2 * half, Q_BLOCK), F32)], axis=0)
        tables = _tn(lhs.astype(BF16), spread_ref[...])
        cs = tables[:, :LANES]
        s_lo = tables[:, LANES:2 * LANES]
        s_hi = tables[:, 2 * LANES:]

        def rope(x):
            return x * cs + pltpu.roll(x, LANES - half, 1) * s_lo + pltpu.roll(x, half, 1) * s_hi

        def split_groups(x, ref, dtype):
            ref[0, tok, :] = x[:, :NSA_HEAD_DIM].astype(dtype)
            ref[1, tok, :] = x[:, NSA_HEAD_DIM:].astype(dtype)

        for pair in range(NSA_HEADS // 2):
            cols = slice(NSA_Q + pair * LANES, NSA_Q + (pair + 1) * LANES)
            qr_t = (rope(x_ref[tok, cols]) * Q_SCALE).T
            for sub in range(2):
                head = 2 * pair + sub
                grp, hp = head // hpg, head % hpg
                col0 = (blk * hpg + hp) * Q_BLOCK
                qs_ref[grp, :, col0:col0 + Q_BLOCK] = qr_t[sub * NSA_HEAD_DIM:(sub + 1) * NSA_HEAD_DIM].astype(BF16)
        split_groups(rope(x_ref[tok, NSA_KC:NSA_KC + LANES]), kc_ref, F32)
        split_groups(rope(x_ref[tok, NSA_KS:NSA_KS + LANES]), ks_ref, BF16)
        split_groups(rope(x_ref[tok, NSA_KW:NSA_KW + LANES]), kw_ref, BF16)
        split_groups(x_ref[tok, NSA_VC:NSA_VC + LANES], vc_ref, F32)
        vst = x_ref[tok, NSA_VS:NSA_VS + LANES].T
        vwt = x_ref[tok, NSA_VW:NSA_VW + LANES].T
        for grp in range(NSA_KV_GROUPS):
            for ref, v in ((vst_ref, vst), (vwt_ref, vwt)):
                ref[grp, :NSA_HEAD_DIM, tok] = v[grp * NSA_HEAD_DIM:(grp + 1) * NSA_HEAD_DIM].astype(BF16)
                ref[grp, NSA_HEAD_DIM:, tok] = ones_row
        gt_ref[blk] = jax.nn.sigmoid(x_ref[tok, NSA_G:NSA_G + LANES]).T[:GATE_ROWS]


def _rope_spread_matrix():
    half = NSA_ROT_DIM // 2
    lane = np.arange(LANES)
    d = lane % NSA_HEAD_DIM
    m = np.zeros((8 * half, 3 * LANES), np.float32)
    for piece in range(3):
        for f in range(half):
            hit = d % half == f
            m[piece * half + f, :LANES] = hit & (d < NSA_ROT_DIM)
            m[(3 + piece) * half + f, LANES:2 * LANES] = -1.0 * (hit & (d < half))
            m[(3 + piece) * half + f, 2 * LANES:] = hit & (d >= half) & (d < NSA_ROT_DIM)
    m[6 * half, :LANES] = d >= NSA_ROT_DIM
    return jnp.asarray(m, BF16)


def _nsa_prep(nsa_proj, positions, inv_freq, rope_spread, B, S, blocks=4):
    G, Dh, nq = NSA_KV_GROUPS, NSA_HEAD_DIM, S // Q_BLOCK
    blocks = min(blocks, nq)
    ts = blocks * Q_BLOCK
    tok = lambda dtype: jax.ShapeDtypeStruct((B, G, S, Dh), dtype)
    tok_spec = pl.BlockSpec((None, G, ts, Dh), lambda b, i: (b, 0, i, 0))
    tr = jax.ShapeDtypeStruct((B, G, V_ROWS, S), BF16)
    tr_spec = pl.BlockSpec((None, G, V_ROWS, ts), lambda b, i: (b, 0, 0, i))
    return pl.pallas_call(
        functools.partial(_nsa_prep_body, blocks=blocks),
        grid=(B, nq // blocks),
        in_specs=[
            pl.BlockSpec((None, ts, NSA_WIDTH), lambda b, i: (b, i, 0)),
            pl.BlockSpec((None, blocks, 1, Q_BLOCK), lambda b, i: (b, i, 0, 0)),
            pl.BlockSpec(inv_freq.shape, lambda b, i: (0, 0)),
            pl.BlockSpec(rope_spread.shape, lambda b, i: (0, 0)),
        ],
        out_specs=[
            pl.BlockSpec((None, G, Dh, NSA_HEADS_PER_GROUP * ts), lambda b, i: (b, 0, 0, i)),
            tok_spec, tok_spec, tok_spec, tok_spec, tr_spec, tr_spec,
            pl.BlockSpec((None, blocks, GATE_ROWS, Q_BLOCK), lambda b, i: (b, i, 0, 0)),
        ],
        out_shape=[
            jax.ShapeDtypeStruct((B, G, Dh, NSA_HEADS_PER_GROUP * S), BF16),
            tok(F32), tok(F32), tok(BF16), tok(BF16), tr, tr,
            jax.ShapeDtypeStruct((B, nq, GATE_ROWS, Q_BLOCK), F32),
        ],
        compiler_params=pltpu.CompilerParams(dimension_semantics=("parallel", "parallel")),
        name="nsa_prep",
    )(nsa_proj, positions, inv_freq, rope_spread)


def _cmp_body(x_ref, w_ref, p_ref, o_ref, *, n_blk, transpose_out):
    half = CMP_BLOCK // 2
    first = jnp.zeros((n_blk, NSA_HEAD_DIM), F32)
    second = jnp.zeros((n_blk, NSA_HEAD_DIM), F32)
    for l in range(half):
        rows = x_ref[pl.ds(l, n_blk, stride=CMP_STRIDE), :]
        first = first + _dot((rows + p_ref[l]).astype(BF16), w_ref[l])
        second = second + _dot((rows + p_ref[half + l]).astype(BF16), w_ref[half + l])
    out = first + jnp.concatenate([second[1:], jnp.zeros((1, NSA_HEAD_DIM), F32)], axis=0)
    o_ref[...] = (out.T if transpose_out else out).astype(o_ref.dtype)


def _compress(x, w, pos, transpose_out):
    B, G, S, Dh = x.shape
    n_blk = S // CMP_STRIDE
    out_block = (None, None, Dh, n_blk) if transpose_out else (None, None, n_blk, Dh)
    out_shape = (B, G, Dh, n_blk) if transpose_out else (B, G, n_blk, Dh)
    return pl.pallas_call(
        functools.partial(_cmp_body, n_blk=n_blk, transpose_out=transpose_out),
        grid=(B, G),
        in_specs=[
            pl.BlockSpec((None, None, S, Dh), lambda b, g: (b, g, 0, 0)),
            pl.BlockSpec((CMP_BLOCK, Dh, Dh), lambda b, g: (0, 0, 0)),
            pl.BlockSpec((CMP_BLOCK, 1, Dh), lambda b, g: (0, 0, 0)),
        ],
        out_specs=pl.BlockSpec(out_block, lambda b, g: (b, g, 0, 0)),
        out_shape=jax.ShapeDtypeStruct(out_shape, BF16),
        compiler_params=pltpu.CompilerParams(
            dimension_semantics=("parallel", "parallel"), vmem_limit_bytes=VMEM_LIMIT),
        name="nsa_compress",
    )(x, w, pos)


def _nsa_body(*refs, blocks, **tiles):
    def one_block(qi, carry):
        _nsa_block(qi, *refs, blocks=blocks, **tiles)
        return carry

    lax.fori_loop(0, blocks, one_block, 0)


def _nsa_block(qi, q_ref, kcc_ref, vcct_ref, ks_ref, vst_ref, kw_ref, vwt_ref, gt_ref, o_ref,
               imp_scr, sel_scr, ocmp_scr, owin_scr, oslc_scr, *, blocks, kt_size, sub_size, cmp_chunk):
    grp = pl.program_id(1)
    t0 = (pl.program_id(2) * blocks + qi) * Q_BLOCK
    hpg = NSA_HEADS_PER_GROUP
    n_cmp = kcc_ref.shape[0]
    n_sel = sel_scr.shape[0]
    q_t = q_ref[:, pl.ds(pl.multiple_of(qi * (hpg * Q_BLOCK), hpg * Q_BLOCK), hpg * Q_BLOCK)]
    tq1 = t0 + lax.broadcasted_iota(jnp.int32, (1, Q_BLOCK), 1)
    heads = lambda x: jnp.concatenate([x] * hpg, axis=1)

    width = hpg * Q_BLOCK
    neg_inf = -jnp.inf
    eye = (lax.broadcasted_iota(jnp.int32, (Q_BLOCK, Q_BLOCK), 0)
           == lax.broadcasted_iota(jnp.int32, (Q_BLOCK, Q_BLOCK), 1)).astype(BF16)
    q_aug_t = jnp.concatenate([jnp.concatenate([eye] * hpg, axis=1), q_t], axis=0)

    ch = cmp_chunk
    n_chunks = (((t0 + Q_BLOCK - CMP_BLOCK) >> (CMP_STRIDE.bit_length() - 1)) + ch) // ch

    def compressed(n_sub):
        imp_scr[0:IMP_PAD, :] = jnp.zeros((IMP_PAD, Q_BLOCK), F32)
        imp_scr[IMP_PAD + n_sub * ch:, :] = jnp.zeros((n_cmp + IMP_PAD - n_sub * ch, Q_BLOCK), F32)
        biases = []
        for sub in range(n_sub):
            if sub >= n_sub - 2:
                cmp_end = (sub * ch + lax.broadcasted_iota(jnp.int32, (ch, 1), 0)) * CMP_STRIDE + (CMP_BLOCK - 1)
                biases.append(jnp.where(cmp_end <= tq1, 0.0, MASKED).astype(BF16))
            else:
                biases.append(jnp.zeros((ch, Q_BLOCK), BF16))
        k_aug = jnp.concatenate([jnp.concatenate(biases, axis=0), kcc_ref[0:n_sub * ch, :]], axis=1)
        scores = _dot(k_aug, q_aug_t)
        parts = []
        for sub in range(n_sub):
            s = scores[sub * ch:(sub + 1) * ch]
            m_sub = jnp.max(s, axis=0, keepdims=True)
            e = jnp.exp2(s - m_sub)
            parts.append((m_sub, jnp.sum(e, axis=0, keepdims=True),
                          _dot(vcct_ref[:, sub * ch:(sub + 1) * ch], e.astype(BF16)), e))
        m = parts[0][0]
        for m_sub, _, _, _ in parts[1:]:
            m = jnp.maximum(m, m_sub)
        scales = [jnp.exp2(m_sub - m) for m_sub, _, _, _ in parts]
        l = scales[0] * parts[0][1]
        acc = scales[0] * parts[0][2]
        for scale, (_, l_sub, acc_sub, _) in zip(scales[1:], parts[1:]):
            l = l + scale * l_sub
            acc = acc + scale * acc_sub
        inv_l = jnp.where(heads(tq1 >= CMP_BLOCK - 1), 1.0 / l, 0.0)
        ocmp_scr[...] = acc * inv_l
        for sub in range(n_sub):
            pc = parts[sub][3] * (scales[sub] * inv_l)
            p_sum = pc[:, :Q_BLOCK]
            for hp in range(1, hpg):
                p_sum = p_sum + pc[:, hp * Q_BLOCK:(hp + 1) * Q_BLOCK]
            imp_scr[IMP_PAD + sub * ch:IMP_PAD + (sub + 1) * ch, :] = p_sum

    for variant in range(n_cmp // ch):
        pl.when(n_chunks == variant + 1)(functools.partial(compressed, variant + 1))
    o_cmp = ocmp_scr[...]

    ratio = SEL_BLOCK // CMP_STRIDE
    sel_shift = SEL_BLOCK.bit_length() - 1
    cur = tq1 >> sel_shift
    n_forced = 3

    def window_branch():
        span = WINDOW + Q_BLOCK
        k_start = pl.multiple_of(jnp.maximum(t0 - WINDOW, 0), Q_BLOCK)
        diff = tq1 - (k_start + lax.broadcasted_iota(jnp.int32, (span, 1), 0))
        inside = pltpu.bitcast(diff, jnp.uint32) < jnp.uint32(WINDOW)
        bias = jnp.where(inside, 0.0, MASKED).astype(BF16)
        sw = _dot(jnp.concatenate([bias, kw_ref[pl.ds(k_start, span), :]], axis=1), q_aug_t)
        pw = jnp.exp2(sw - jnp.max(sw, axis=0, keepdims=True)).astype(BF16)
        acc_win = _dot(vwt_ref[:, pl.ds(k_start, span)], pw)
        owin_scr[...] = acc_win[:NSA_HEAD_DIM] / acc_win[NSA_HEAD_DIM:NSA_HEAD_DIM + 1]

    def select_blocks(rows):
        window_branch()
        imp = imp_scr[pl.ds(IMP_PAD - 1, rows, stride=ratio), :]
        for k in range(1, ratio + 1):
            imp = imp + imp_scr[pl.ds(IMP_PAD - 1 + k, rows, stride=ratio), :]
        j = lax.broadcasted_iota(jnp.int32, (rows, Q_BLOCK), 0)
        j_f = j.astype(F32)
        forced = (j == 0) | (j == cur) | (j == cur - 1)
        valid = j <= cur

        def pick(_, sc):
            best = jnp.max(sc, axis=0, keepdims=True)
            idx = jnp.min(jnp.where(sc == best, j_f, float(rows)), axis=0, keepdims=True)
            return jnp.where(j_f == idx, neg_inf, sc)

        left = lax.fori_loop(0, max(min(SEL_TOPK, n_sel) - n_forced, 0), pick,
                             jnp.where(valid & ~forced, imp, neg_inf), unroll=True)
        sel_scr[0:rows, :] = jnp.where(forced | (valid & (left == neg_inf)), 0.0, MASKED)
        if rows < n_sel:
            sel_scr[rows:n_sel, :] = jnp.full((n_sel - rows, Q_BLOCK), MASKED, F32)

    row_step = min(TOPK_ROW_STEP, n_sel)
    last_block = (t0 + Q_BLOCK - 1) >> sel_shift
    for variant in range(n_sel // row_step):
        pl.when(last_block // row_step == variant)(functools.partial(select_blocks, (variant + 1) * row_step))

    o_win = owin_scr[...]

    blocks_per_tile = kt_size // SEL_BLOCK

    def slc_tile(kt, carry, n_sub, causal_from):
        m_run, acc = carry
        base = pl.multiple_of(kt * kt_size, kt_size)
        biases = []
        for sub in range(n_sub):
            blk0 = kt * blocks_per_tile + sub * (sub_size // SEL_BLOCK)
            bias = jnp.concatenate(
                [jnp.broadcast_to(sel_scr[pl.ds(blk0 + jb, 1), :], (SEL_BLOCK, Q_BLOCK))
                 for jb in range(sub_size // SEL_BLOCK)], axis=0)
            if sub >= causal_from:
                kpos = base + sub * sub_size + lax.broadcasted_iota(jnp.int32, (sub_size, 1), 0)
                bias = jnp.where(kpos <= tq1, bias, MASKED)
            biases.append(bias.astype(BF16))
        k_aug = jnp.concatenate([jnp.concatenate(biases, axis=0), ks_ref[pl.ds(base, n_sub * sub_size), :]], axis=1)
        scores = _dot(k_aug, q_aug_t)
        parts = []
        for sub in range(n_sub):
            sc = scores[sub * sub_size:(sub + 1) * sub_size]
            m_sub = jnp.max(sc, axis=0, keepdims=True)
            pe = jnp.exp2(sc - m_sub).astype(BF16)
            k0 = pl.multiple_of(base + sub * sub_size, sub_size)
            parts.append((m_sub, _dot(vst_ref[:, pl.ds(k0, sub_size)], pe)))
        m_new = m_run
        for m_sub, _ in parts:
            m_new = jnp.maximum(m_new, m_sub)
        acc = jnp.exp2(m_run - m_new) * acc
        for m_sub, acc_sub in parts:
            acc = acc + jnp.exp2(m_sub - m_new) * acc_sub
        return m_new, acc

    subs_per_tile = kt_size // sub_size
    n_full = t0 // kt_size
    carry = lax.fori_loop(
        0, n_full, functools.partial(slc_tile, n_sub=subs_per_tile, causal_from=subs_per_tile),
        (jnp.full((1, width), MASKED, F32), jnp.zeros((V_ROWS, width), F32)))
    tail_step = min(2, subs_per_tile)
    tail_len = (t0 - n_full * kt_size) // (tail_step * sub_size)
    for variant in range(subs_per_tile // tail_step):
        n_sub = tail_step * (variant + 1)

        @pl.when(tail_len == variant)
        def _(n_sub=n_sub):
            _, acc_slc = slc_tile(n_full, carry, n_sub=n_sub, causal_from=n_sub - tail_step)
            oslc_scr[...] = acc_slc[:NSA_HEAD_DIM] / acc_slc[NSA_HEAD_DIM:NSA_HEAD_DIM + 1]

    o_slc = oslc_scr[...]

    for hp in range(hpg):
        cols = slice(hp * Q_BLOCK, (hp + 1) * Q_BLOCK)
        row = (grp * hpg + hp) * 3
        o_t = (gt_ref[qi, pl.ds(row, 1), :] * o_cmp[:, cols] + gt_ref[qi, pl.ds(row + 1, 1), :] * o_slc[:, cols]
               + gt_ref[qi, pl.ds(row + 2, 1), :] * o_win[:, cols])
        o_ref[pl.ds(pl.multiple_of(qi * Q_BLOCK, Q_BLOCK), Q_BLOCK), hp * NSA_HEAD_DIM:(hp + 1) * NSA_HEAD_DIM] = (
            o_t.T.astype(o_ref.dtype))


def _nsa_attention(qs, kcc, vcct, ks, vst, kw, vwt, gt, B, S, blocks=4, kt_size=4096, sub_size=256, cmp_chunk=256):
    G, Dh, nq, hpg = NSA_KV_GROUPS, NSA_HEAD_DIM, S // Q_BLOCK, NSA_HEADS_PER_GROUP
    kt_size = min(kt_size, S)
    blocks = min(blocks, nq)
    n_cmp, n_sel = S // CMP_STRIDE, S // SEL_BLOCK
    cmp_chunk = min(cmp_chunk, n_cmp)
    per_group = lambda shape: pl.BlockSpec((None, None) + shape, lambda b, g, i: (b, g, 0, 0))
    return pl.pallas_call(
        functools.partial(_nsa_body, blocks=blocks, kt_size=kt_size, sub_size=min(sub_size, kt_size),
                          cmp_chunk=cmp_chunk),
        grid=(B, G, nq // blocks),
        in_specs=[
            pl.BlockSpec((None, None, Dh, blocks * hpg * Q_BLOCK), lambda b, g, i: (b, g, 0, i)),
            per_group((n_cmp, Dh)), per_group((Dh, n_cmp)),
            per_group((S, Dh)), per_group((V_ROWS, S)),
            per_group((S, Dh)), per_group((V_ROWS, S)),
            pl.BlockSpec((None, blocks, GATE_ROWS, Q_BLOCK), lambda b, g, i: (b, i, 0, 0)),
        ],
        out_specs=pl.BlockSpec((None, blocks * Q_BLOCK, hpg * Dh), lambda b, g, i: (b, i, g)),
        out_shape=jax.ShapeDtypeStruct((B, S, NSA_HEADS * Dh), BF16),
        scratch_shapes=[
            pltpu.VMEM((n_cmp + 2 * IMP_PAD, Q_BLOCK), F32),
            pltpu.VMEM((n_sel, Q_BLOCK), F32),
            pltpu.VMEM((Dh, hpg * Q_BLOCK), F32),
            pltpu.VMEM((Dh, hpg * Q_BLOCK), F32),
            pltpu.VMEM((Dh, hpg * Q_BLOCK), F32),
        ],
        compiler_params=pltpu.CompilerParams(
            dimension_semantics=("parallel", "parallel", "arbitrary"), vmem_limit_bytes=VMEM_LIMIT),
        name="nsa_attention",
    )(qs, kcc, vcct, ks, vst, kw, vwt, gt)


def _merge_body(ya_ref, yb_ref, yc_ref, ga_ref, gb_ref, gc_ref, h_ref, wb_ref, wo_ref, o_ref):
    merged = jax.nn.sigmoid(ga_ref[...].astype(F32)) * _dot(ya_ref[...], wb_ref[0])
    merged = merged + jax.nn.sigmoid(gb_ref[...].astype(F32)) * _dot(yb_ref[...], wb_ref[1])
    merged = merged + jax.nn.sigmoid(gc_ref[...].astype(F32)) * _dot(yc_ref[...], wb_ref[2])
    o_ref[...] = h_ref[...] + _dot(merged.astype(BF16), wo_ref[...])


def _merge(ya, yb, yc, main, h, w_branch, w_out, tm=1024):
    T, D = h.shape
    y_spec = pl.BlockSpec((tm, 512), lambda i: (i, 0))
    gate_spec = lambda k: pl.BlockSpec((tm, D), lambda i: (i, MAIN_GM // D + k))
    return pl.pallas_call(
        _merge_body,
        grid=(T // tm,),
        in_specs=[
            y_spec, y_spec, y_spec, gate_spec(0), gate_spec(1), gate_spec(2),
            pl.BlockSpec((tm, D), lambda i: (i, 0)),
            pl.BlockSpec((3, 512, D), lambda i: (0, 0, 0)),
            pl.BlockSpec((D, D), lambda i: (0, 0)),
        ],
        out_specs=pl.BlockSpec((tm, D), lambda i: (i, 0)),
        out_shape=jax.ShapeDtypeStruct((T, D), F32),
        compiler_params=pltpu.CompilerParams(dimension_semantics=("parallel",), vmem_limit_bytes=VMEM_LIMIT),
        name="merge_out",
    )(ya, yb, yc, main, main, main, h, w_branch, w_out)


def _ffn_body(h_ref, gf_ref, w1_ref, w2_ref, gp_ref, wg_ref, p_ref, wp_ref, gl_ref, o_ref, f_scr, acc_scr, *, final_norm):
    j = pl.program_id(1)

    @pl.when(j == 0)
    def _():
        f_scr[...] = _rms(h_ref[...], gf_ref[...]).astype(BF16)
        acc_scr[...] = jnp.zeros_like(acc_scr)

    a = jnp.maximum(_dot(f_scr[...], w1_ref[...]), 0.0)
    acc_scr[...] += _dot((a * a).astype(BF16), w2_ref[...])

    @pl.when(j == pl.num_programs(1) - 1)
    def _():
        h2 = h_ref[...] + acc_scr[...]
        gate = jax.nn.sigmoid(_dot(_rms(h2, gp_ref[...]).astype(BF16), wg_ref[...]))
        h3 = h2 + gate * _dot(p_ref[...].astype(BF16), wp_ref[...])
        o_ref[...] = _rms(h3, gl_ref[...]) if final_norm else h3


def _ffn_ple(h, norm_ffn, w1, w2, norm_ple, w_gate, p, w_proj, norm_last, final_norm, tm=1024, tf=1024):
    T, D = h.shape
    tm = min(tm, T)
    Fdim = w1.shape[1]
    Pdim = p.shape[1]
    vec = pl.BlockSpec((1, D), lambda i, j: (0, 0))
    return pl.pallas_call(
        functools.partial(_ffn_body, final_norm=final_norm),
        grid=(T // tm, Fdim // tf),
        in_specs=[
            pl.BlockSpec((tm, D), lambda i, j: (i, 0)), vec,
            pl.BlockSpec((D, tf), lambda i, j: (0, j)),
            pl.BlockSpec((tf, D), lambda i, j: (j, 0)),
            vec,
            pl.BlockSpec((D, D), lambda i, j: (0, 0)),
            pl.BlockSpec((tm, Pdim), lambda i, j: (i, 0)),
            pl.BlockSpec((Pdim, D), lambda i, j: (0, 0)),
            vec,
        ],
        out_specs=pl.BlockSpec((tm, D), lambda i, j: (i, 0)),
        out_shape=jax.ShapeDtypeStruct((T, D), F32),
        scratch_shapes=[pltpu.VMEM((tm, D), BF16), pltpu.VMEM((tm, D), F32)],
        compiler_params=pltpu.CompilerParams(
            dimension_semantics=("parallel", "arbitrary"), vmem_limit_bytes=VMEM_LIMIT),
        name="ffn_ple",
    )(h, norm_ffn, w1, w2, norm_ple, w_gate, p, w_proj, norm_last)


def _split_w_in(w):
    parts, off = [], 0
    for sz in IN_SIZES:
        parts.append(w[:, off:off + sz])
        off += sz
    u, q_nsa, kv_nsa, g_nsa, q_gla, k_gla, v_gla, a_gla, r_gla, g_merge = parts
    D = w.shape[0]
    pad = lambda n: jnp.zeros((D, n), w.dtype)
    main = jnp.concatenate([u, q_gla, k_gla, v_gla, r_gla, g_merge, a_gla, pad(MAIN_WIDTH - MAIN_A - GLA_GATE_RANK)], axis=1)
    kv = [kv_nsa[:, s * LANES:(s + 1) * LANES] for s in range(6)]
    nsa = jnp.concatenate([q_nsa, kv[0], kv[2], kv[4], kv[1], kv[3], kv[5], g_nsa, pad(NSA_WIDTH - NSA_G - 24)], axis=1)
    return main.astype(BF16), nsa.astype(BF16)


def kernel(x, p, positions, norm_mix, w_in, pool_w, pool_scale, cmp_pos_k, cmp_w_k, cmp_pos_v, cmp_w_v, gla_w_gate, gla_b_gate, gla_norm, w_branch, w_out, norm_ffn, w_ff1, w_ff2, norm_ple, w_ple_gate, w_ple_proj, norm_final):
    B, S, D = x.shape
    depth = w_in.shape[0]
    T = B * S
    Dh = NSA_HEAD_DIM
    row = lambda v: v.reshape(1, -1).astype(F32)

    half = NSA_ROT_DIM // 2
    inv_freq = jnp.power(ROPE_THETA, -jnp.arange(half, dtype=F32) * (2.0 / NSA_ROT_DIM)).reshape(half, 1)
    pos_rows = positions.reshape(B, S // Q_BLOCK, 1, Q_BLOCK)
    rope_spread = _rope_spread_matrix()

    h = x.reshape(T, D)
    for i in range(depth):
        w_main, w_nsa = _split_w_in(w_in[i])
        main = _norm_proj(h, row(norm_mix[i]), w_main, BF16, tn=MAIN_WIDTH // 3).reshape(B, S, MAIN_WIDTH)
        nsa_proj = _norm_proj(h, row(norm_mix[i]), w_nsa, F32, tn=NSA_WIDTH).reshape(B, S, NSA_WIDTH)

        y_a = _pool_mixer(main, pool_w[i].astype(BF16), row(pool_scale[i]), B, S)

        wg = jnp.zeros((LANES, GLA_HEADS * GLA_DK), BF16).at[:GLA_GATE_RANK].set(gla_w_gate[i].astype(BF16))
        y_c = _gla_mixer(main, wg, row(gla_b_gate[i]), row(gla_norm[i]), B, S)

        qs, kc, vc, ks, kw, vst, vwt, gt = _nsa_prep(nsa_proj, pos_rows, inv_freq, rope_spread, B, S)
        kcc = _compress(kc, cmp_w_k[i].reshape(CMP_BLOCK, Dh, Dh).astype(BF16),
                        cmp_pos_k[i].reshape(CMP_BLOCK, 1, Dh).astype(F32), transpose_out=False)
        vcct = _compress(vc, cmp_w_v[i].reshape(CMP_BLOCK, Dh, Dh).astype(BF16),
                         cmp_pos_v[i].reshape(CMP_BLOCK, 1, Dh).astype(F32), transpose_out=True)
        y_b = _nsa_attention(qs, kcc, vcct, ks, vst, kw, vwt, gt, B, S)

        h = _merge(y_a.reshape(T, -1), y_b.reshape(T, -1), y_c.reshape(T, -1), main.reshape(T, MAIN_WIDTH), h,
                   w_branch[i].astype(BF16), w_out[i].astype(BF16))
        h = _ffn_ple(h, row(norm_ffn[i]), w_ff1[i].astype(BF16), w_ff2[i].astype(BF16), row(norm_ple[i]),
                     w_ple_gate[i].astype(BF16), p[i].reshape(T, -1), w_ple_proj[i].astype(BF16),
                     row(norm_final), final_norm=(i == depth - 1))
    return h.reshape(B, S, D)
```

```python
import functools

import jax
import jax.numpy as jnp
import numpy as np
from jax import lax
from jax.experimental import pallas as pl
from jax.experimental.pallas import tpu as pltpu

F32, BF16 = jnp.float32, jnp.bfloat16

EPS = 1e-6
ROPE_THETA = 500000.0
POOL_WINDOWS = (2, 4, 8, 16)
POOL_GROUP_DIM = 128
POOL_HALO = max(POOL_WINDOWS)
NSA_HEADS = 8
NSA_KV_GROUPS = 2
NSA_HEADS_PER_GROUP = NSA_HEADS // NSA_KV_GROUPS
NSA_HEAD_DIM = 64
NSA_ROT_DIM = 16
CMP_BLOCK = 32
CMP_STRIDE = 16
SEL_BLOCK = 64
SEL_TOPK = 16
WINDOW = 512
Q_BLOCK = 128
GLA_HEADS = 4
GLA_DK = 64
GLA_DV = 128
GLA_GATE_RANK = 16
GLA_TAU = 16.0
GLA_CHUNK = 64
IN_SIZES = (512, 512, 768, 24, 256, 256, 512, 16, 512, 3072)

LANES = 128
MASKED = -1e30
LOG2E = 1.4426950408889634
Q_SCALE = NSA_HEAD_DIM ** -0.5 * LOG2E
V_ROWS = NSA_HEAD_DIM + 16
IMP_PAD = 8
GATE_ROWS = 32
TOPK_ROW_STEP = 64
VMEM_LIMIT = 56 * 1024 * 1024

MAIN_U, MAIN_QG, MAIN_KG, MAIN_VG, MAIN_RG, MAIN_GM, MAIN_A, MAIN_WIDTH = 0, 512, 768, 1024, 1536, 2048, 5120, 5376
NSA_Q, NSA_KC, NSA_KS, NSA_KW, NSA_VC, NSA_VS, NSA_VW, NSA_G, NSA_WIDTH = 0, 512, 640, 768, 896, 1024, 1152, 1280, 1536


def _nt(a, b):
    return lax.dot_general(a, b, (((1,), (1,)), ((), ())), preferred_element_type=F32)


def _tn(a, b):
    return lax.dot_general(a, b, (((0,), (0,)), ((), ())), preferred_element_type=F32)


def _dot(a, b):
    return jnp.dot(a, b, preferred_element_type=F32)


def _rms(x, gain):
    return x * lax.rsqrt(jnp.mean(x * x, axis=-1, keepdims=True) + EPS) * gain


def _proj_body(h_ref, g_ref, w_ref, o_ref, a_scr):
    @pl.when(pl.program_id(1) == 0)
    def _():
        a_scr[...] = _rms(h_ref[...], g_ref[...]).astype(BF16)

    o_ref[...] = _dot(a_scr[...], w_ref[...]).astype(o_ref.dtype)


def _norm_proj(h, gain, w, out_dtype, tn, tm=1024):
    T, D = h.shape
    N = w.shape[1]
    tm = min(tm, T)
    return pl.pallas_call(
        _proj_body,
        grid=(T // tm, N // tn),
        in_specs=[
            pl.BlockSpec((tm, D), lambda i, j: (i, 0)),
            pl.BlockSpec((1, D), lambda i, j: (0, 0)),
            pl.BlockSpec((D, tn), lambda i, j: (0, j)),
        ],
        out_specs=pl.BlockSpec((tm, tn), lambda i, j: (i, j)),
        out_shape=jax.ShapeDtypeStruct((T, N), out_dtype),
        scratch_shapes=[pltpu.VMEM((tm, D), BF16)],
        compiler_params=pltpu.CompilerParams(
            dimension_semantics=("parallel", "arbitrary"), vmem_limit_bytes=VMEM_LIMIT),
        name="norm_proj",
    )(h, gain, w)


def _pool_body(u_ref, halo_ref, w_ref, sc_ref, o_ref, *, ts):
    i = pl.program_id(1)
    cur = u_ref[...].astype(F32)
    halo = jnp.where(i == 0, 0.0, halo_ref[...].astype(F32))
    t = i * ts + lax.broadcasted_iota(jnp.int32, (ts, 1), 0)
    for g, win in enumerate(POOL_WINDOWS):
        lo, hi = g * POOL_GROUP_DIM, (g + 1) * POOL_GROUP_DIM
        x = cur[:, lo:hi]
        e = jnp.concatenate([halo[:, lo:hi], x], axis=0)
        step = 1
        while step < win:
            e = e[step:] + e[:-step]
            step *= 2
        wsum = e[POOL_HALO - (win - 1):]
        cnt = jnp.minimum(t + 1, win).astype(F32)
        pooled = wsum / cnt - x
        y = _dot(pooled.astype(BF16), w_ref[g]) * sc_ref[:, lo:hi]
        o_ref[:, lo:hi] = y.astype(o_ref.dtype)


def _pool_mixer(main, pool_w, scale, B, S, ts=512):
    ts = min(ts, S)
    halo_blocks = ts // POOL_HALO
    return pl.pallas_call(
        functools.partial(_pool_body, ts=ts),
        grid=(B, S // ts),
        in_specs=[
            pl.BlockSpec((None, ts, 512), lambda b, i: (b, i, MAIN_U // 512)),
            pl.BlockSpec((None, POOL_HALO, 512), lambda b, i: (b, jnp.maximum(i * halo_blocks - 1, 0), MAIN_U // 512)),
            pl.BlockSpec((4, 128, 128), lambda b, i: (0, 0, 0)),
            pl.BlockSpec((1, 512), lambda b, i: (0, 0)),
        ],
        out_specs=pl.BlockSpec((None, ts, 512), lambda b, i: (b, i, 0)),
        out_shape=jax.ShapeDtypeStruct((B, S, 512), BF16),
        compiler_params=pltpu.CompilerParams(dimension_semantics=("parallel", "arbitrary")),
        name="pool_mixer",
    )(main, main, pool_w, scale)


def _gla_body(q_ref, k_ref, v_ref, r_ref, a_ref, wg_ref, bg_ref, ng_ref, o_ref, st_ref, *, tc):
    @pl.when(pl.program_id(1) == 0)
    def _():
        st_ref[...] = jnp.zeros_like(st_ref)

    C = GLA_CHUNK
    n_chunks = tc // C
    causal = lax.broadcasted_iota(jnp.int32, (C, C), 0) >= lax.broadcasted_iota(jnp.int32, (C, C), 1)
    width = GLA_HEADS * GLA_DK
    hk = [slice(h * GLA_DK, (h + 1) * GLA_DK) for h in range(GLA_HEADS)]
    hv = [slice(h * GLA_DV, (h + 1) * GLA_DV) for h in range(GLA_HEADS)]
    rows = [slice(c * C, (c + 1) * C) for c in range(n_chunks)]

    z = _dot(a_ref[...], wg_ref[...]) + bg_ref[...]
    b = jax.nn.log_sigmoid(z) * (1.0 / GLA_TAU)
    row_in_chunk = lax.broadcasted_iota(jnp.int32, (tc, 1), 0) % C
    step = 1
    while step < C:
        shifted = jnp.concatenate([jnp.zeros((step, width), F32), b[:-step]], axis=0)
        b = b + jnp.where(row_in_chunk >= step, shifted, 0.0)
        step *= 2
    b_last = [b[(c + 1) * C - 1:(c + 1) * C, :] for c in range(n_chunks)]
    b_last_rows = jnp.concatenate([jnp.broadcast_to(bl, (C, width)) for bl in b_last], axis=0)
    qf = q_ref[...].astype(F32) * (GLA_DK ** -0.5)
    kf = k_ref[...].astype(F32)
    q_s = (qf * jnp.exp(b)).astype(BF16)
    k_s = (kf * jnp.exp(-b)).astype(BF16)
    k_t = (kf * jnp.exp(b_last_rows - b)).astype(BF16)

    att = [[jnp.where(causal, _nt(q_s[rows[c], hk[h]], k_s[rows[c], hk[h]]), 0.0).astype(BF16)
            for h in range(GLA_HEADS)] for c in range(n_chunks)]
    o_intra = [[_dot(att[c][h], v_ref[rows[c], hv[h]]) for h in range(GLA_HEADS)] for c in range(n_chunks)]
    kv = [[_tn(v_ref[rows[c], hv[h]], k_t[rows[c], hk[h]]) for h in range(GLA_HEADS)] for c in range(n_chunks)]

    state_t = [st_ref[h] for h in range(GLA_HEADS)]
    for c in range(n_chunks):
        decay = jnp.exp(b_last[c])
        for h in range(GLA_HEADS):
            o = o_intra[c][h] + _nt(q_s[rows[c], hk[h]], state_t[h].astype(BF16))
            state_t[h] = state_t[h] * decay[:, hk[h]] + kv[c][h]
            o = _rms(o, ng_ref[...])
            r = r_ref[rows[c], hv[h]].astype(F32)
            o_ref[rows[c], hv[h]] = (o * (r * jax.nn.sigmoid(r))).astype(o_ref.dtype)
    for h in range(GLA_HEADS):
        st_ref[h] = state_t[h]


def _gla_mixer(main, w_gate, b_gate, norm_g, B, S, tc=512):
    tc = min(tc, S)
    return pl.pallas_call(
        functools.partial(_gla_body, tc=tc),
        grid=(B, S // tc),
        in_specs=[
            pl.BlockSpec((None, tc, 256), lambda b, i: (b, i, MAIN_QG // 256)),
            pl.BlockSpec((None, tc, 256), lambda b, i: (b, i, MAIN_KG // 256)),
            pl.BlockSpec((None, tc, 512), lambda b, i: (b, i, MAIN_VG // 512)),
            pl.BlockSpec((None, tc, 512), lambda b, i: (b, i, MAIN_RG // 512)),
            pl.BlockSpec((None, tc, 128), lambda b, i: (b, i, MAIN_A // 128)),
            pl.BlockSpec((128, 256), lambda b, i: (0, 0)),
            pl.BlockSpec((1, 256), lambda b, i: (0, 0)),
            pl.BlockSpec((1, 128), lambda b, i: (0, 0)),
        ],
        out_specs=pl.BlockSpec((None, tc, 512), lambda b, i: (b, i, 0)),
        out_shape=jax.ShapeDtypeStruct((B, S, 512), BF16),
        scratch_shapes=[pltpu.VMEM((GLA_HEADS, GLA_DV, GLA_DK), F32)],
        compiler_params=pltpu.CompilerParams(dimension_semantics=("parallel", "arbitrary")),
        name="gla_mixer",
    )(main, main, main, main, main, w_gate, b_gate, norm_g)


def _nsa_prep_body(x_ref, pos_ref, invf_ref, spread_ref, qs_ref, kc_ref, vc_ref, ks_ref, kw_ref, vst_ref, vwt_ref,
                   gt_ref, *, blocks):
    half = NSA_ROT_DIM // 2
    hpg = NSA_HEADS_PER_GROUP
    ones_row = (lax.broadcasted_iota(jnp.int32, (V_ROWS - NSA_HEAD_DIM, Q_BLOCK), 0) == 0).astype(BF16)

    def pieces(a):
        p1 = a.astype(BF16).astype(F32)
        p2 = (a - p1).astype(BF16).astype(F32)
        return [p1, p2, (a - p1 - p2).astype(BF16).astype(F32)]

    for blk in range(blocks):
        tok = slice(blk * Q_BLOCK, (blk + 1) * Q_BLOCK)
        ang = invf_ref[...] * pos_ref[blk].astype(F32)
        lhs = jnp.concatenate(
            pieces(jnp.cos(ang)) + pieces(jnp.sin(ang)) + [jnp.ones((2 * half, Q_BLOCK), F32)], axis=0)
        tables = _tn(lhs.astype(BF16), spread_ref[...])
        cs = tables[:, :LANES]
        s_lo = tables[:, LANES:2 * LANES]
        s_hi = tables[:, 2 * LANES:]

        def rope(x):
            return x * cs + pltpu.roll(x, LANES - half, 1) * s_lo + pltpu.roll(x, half, 1) * s_hi

        def split_groups(x, ref, dtype):
            ref[0, tok, :] = x[:, :NSA_HEAD_DIM].astype(dtype)
            ref[1, tok, :] = x[:, NSA_HEAD_DIM:].astype(dtype)

        for pair in range(NSA_HEADS // 2):
            cols = slice(NSA_Q + pair * LANES, NSA_Q + (pair + 1) * LANES)
            qr_t = (rope(x_ref[tok, cols]) * Q_SCALE).T
            for sub in range(2):
                head = 2 * pair + sub
                grp, hp = head // hpg, head % hpg
                col0 = (blk * hpg + hp) * Q_BLOCK
                qs_ref[grp, :, col0:col0 + Q_BLOCK] = qr_t[sub * NSA_HEAD_DIM:(sub + 1) * NSA_HEAD_DIM].astype(BF16)
        split_groups(rope(x_ref[tok, NSA_KC:NSA_KC + LANES]), kc_ref, F32)
        split_groups(rope(x_ref[tok, NSA_KS:NSA_KS + LANES]), ks_ref, BF16)
        split_groups(rope(x_ref[tok, NSA_KW:NSA_KW + LANES]), kw_ref, BF16)
        split_groups(x_ref[tok, NSA_VC:NSA_VC + LANES], vc_ref, F32)
        vst = x_ref[tok, NSA_VS:NSA_VS + LANES].T
        vwt = x_ref[tok, NSA_VW:NSA_VW + LANES].T
        for grp in range(NSA_KV_GROUPS):
            for ref, v in ((vst_ref, vst), (vwt_ref, vwt)):
                ref[grp, :NSA_HEAD_DIM, tok] = v[grp * NSA_HEAD_DIM:(grp + 1) * NSA_HEAD_DIM].astype(BF16)
                ref[grp, NSA_HEAD_DIM:, tok] = ones_row
        gt_ref[blk] = jax.nn.sigmoid(x_ref[tok, NSA_G:NSA_G + LANES]).T[:GATE_ROWS]


def _rope_spread_matrix():
    half = NSA_ROT_DIM // 2
    lane = np.arange(LANES)
    d = lane % NSA_HEAD_DIM
    m = np.zeros((8 * half, 3 * LANES), np.float32)
    for piece in range(3):
        for f in range(half):
            hit = d % half == f
            m[piece * half + f, :LANES] = hit & (d < NSA_ROT_DIM)
            m[(3 + piece) * half + f, LANES:2 * LANES] = -1.0 * (hit & (d < half))
            m[(3 + piece) * half + f, 2 * LANES:] = hit & (d >= half) & (d < NSA_ROT_DIM)
    m[6 * half, :LANES] = d >= NSA_ROT_DIM
    return jnp.asarray(m, BF16)


def _nsa_prep(nsa_proj, positions, inv_freq, rope_spread, B, S, blocks=4):
    G, Dh, nq = NSA_KV_GROUPS, NSA_HEAD_DIM, S // Q_BLOCK
    blocks = min(blocks, nq)
    ts = blocks * Q_BLOCK
    tok = lambda dtype: jax.ShapeDtypeStruct((B, G, S, Dh), dtype)
    tok_spec = pl.BlockSpec((None, G, ts, Dh), lambda b, i: (b, 0, i, 0))
    tr = jax.ShapeDtypeStruct((B, G, V_ROWS, S), BF16)
    tr_spec = pl.BlockSpec((None, G, V_ROWS, ts), lambda b, i: (b, 0, 0, i))
    return pl.pallas_call(
        functools.partial(_nsa_prep_body, blocks=blocks),
        grid=(B, nq // blocks),
        in_specs=[
            pl.BlockSpec((None, ts, NSA_WIDTH), lambda b, i: (b, i, 0)),
            pl.BlockSpec((None, blocks, 1, Q_BLOCK), lambda b, i: (b, i, 0, 0)),
            pl.BlockSpec(inv_freq.shape, lambda b, i: (0, 0)),
            pl.BlockSpec(rope_spread.shape, lambda b, i: (0, 0)),
        ],
        out_specs=[
            pl.BlockSpec((None, G, Dh, NSA_HEADS_PER_GROUP * ts), lambda b, i: (b, 0, 0, i)),
            tok_spec, tok_spec, tok_spec, tok_spec, tr_spec, tr_spec,
            pl.BlockSpec((None, blocks, GATE_ROWS, Q_BLOCK), lambda b, i: (b, i, 0, 0)),
        ],
        out_shape=[
            jax.ShapeDtypeStruct((B, G, Dh, NSA_HEADS_PER_GROUP * S), BF16),
            tok(F32), tok(F32), tok(BF16), tok(BF16), tr, tr,
            jax.ShapeDtypeStruct((B, nq, GATE_ROWS, Q_BLOCK), F32),
        ],
        compiler_params=pltpu.CompilerParams(dimension_semantics=("parallel", "parallel")),
        name="nsa_prep",
    )(nsa_proj, positions, inv_freq, rope_spread)


def _cmp_body(x_ref, w_ref, p_ref, o_ref, *, n_blk, transpose_out):
    half = CMP_BLOCK // 2
    first = jnp.zeros((n_blk, NSA_HEAD_DIM), F32)
    second = jnp.zeros((n_blk, NSA_HEAD_DIM), F32)
    for l in range(half):
        rows = x_ref[pl.ds(l, n_blk, stride=CMP_STRIDE), :]
        first = first + _dot((rows + p_ref[l]).astype(BF16), w_ref[l])
        second = second + _dot((rows + p_ref[half + l]).astype(BF16), w_ref[half + l])
    out = first + jnp.concatenate([second[1:], jnp.zeros((1, NSA_HEAD_DIM), F32)], axis=0)
    o_ref[...] = (out.T if transpose_out else out).astype(o_ref.dtype)


def _compress(x, w, pos, transpose_out):
    B, G, S, Dh = x.shape
    n_blk = S // CMP_STRIDE
    out_block = (None, None, Dh, n_blk) if transpose_out else (None, None, n_blk, Dh)
    out_shape = (B, G, Dh, n_blk) if transpose_out else (B, G, n_blk, Dh)
    return pl.pallas_call(
        functools.partial(_cmp_body, n_blk=n_blk, transpose_out=transpose_out),
        grid=(B, G),
        in_specs=[
            pl.BlockSpec((None, None, S, Dh), lambda b, g: (b, g, 0, 0)),
            pl.BlockSpec((CMP_BLOCK, Dh, Dh), lambda b, g: (0, 0, 0)),
            pl.BlockSpec((CMP_BLOCK, 1, Dh), lambda b, g: (0, 0, 0)),
        ],
        out_specs=pl.BlockSpec(out_block, lambda b, g: (b, g, 0, 0)),
        out_shape=jax.ShapeDtypeStruct(out_shape, BF16),
        compiler_params=pltpu.CompilerParams(
            dimension_semantics=("parallel", "parallel"), vmem_limit_bytes=VMEM_LIMIT),
        name="nsa_compress",
    )(x, w, pos)


def _nsa_body(*refs, blocks, **tiles):
    def one_block(qi, carry):
        _nsa_block(qi, *refs, blocks=blocks, **tiles)
        return carry

    lax.fori_loop(0, blocks, one_block, 0)


def _nsa_block(qi, q_ref, kcc_ref, vcct_ref, ks_ref, vst_ref, kw_ref, vwt_ref, gt_ref, o_ref,
               imp_scr, sel_scr, ocmp_scr, wins_scr, owin_scr, oslc_scr, *, blocks, kt_size, sub_size, cmp_chunk):
    grp = pl.program_id(1)
    t0 = (pl.program_id(2) * blocks + qi) * Q_BLOCK
    hpg = NSA_HEADS_PER_GROUP
    n_cmp = kcc_ref.shape[0]
    n_sel = sel_scr.shape[0]
    q_t = q_ref[:, pl.ds(pl.multiple_of(qi * (hpg * Q_BLOCK), hpg * Q_BLOCK), hpg * Q_BLOCK)]
    tq1 = t0 + lax.broadcasted_iota(jnp.int32, (1, Q_BLOCK), 1)
    heads = lambda x: jnp.concatenate([x] * hpg, axis=1)

    width = hpg * Q_BLOCK
    neg_inf = -jnp.inf
    eye = (lax.broadcasted_iota(jnp.int32, (Q_BLOCK, Q_BLOCK), 0)
           == lax.broadcasted_iota(jnp.int32, (Q_BLOCK, Q_BLOCK), 1)).astype(BF16)
    q_aug_t = jnp.concatenate([jnp.concatenate([eye] * hpg, axis=1), q_t], axis=0)

    ch = cmp_chunk
    n_chunks = (((t0 + Q_BLOCK - CMP_BLOCK) >> (CMP_STRIDE.bit_length() - 1)) + ch) // ch

    def compressed(n_sub):
        imp_scr[0:IMP_PAD, :] = jnp.zeros((IMP_PAD, Q_BLOCK), F32)
        imp_scr[IMP_PAD + n_sub * ch:, :] = jnp.zeros((n_cmp + IMP_PAD - n_sub * ch, Q_BLOCK), F32)
        biases = []
        for sub in range(n_sub):
            if sub >= n_sub - 2:
                cmp_end = (sub * ch + lax.broadcasted_iota(jnp.int32, (ch, 1), 0)) * CMP_STRIDE + (CMP_BLOCK - 1)
                biases.append(jnp.where(cmp_end <= tq1, 0.0, MASKED).astype(BF16))
            else:
                biases.append(jnp.zeros((ch, Q_BLOCK), BF16))
        k_aug = jnp.concatenate([jnp.concatenate(biases, axis=0), kcc_ref[0:n_sub * ch, :]], axis=1)
        scores = _dot(k_aug, q_aug_t)
        parts = []
        for sub in range(n_sub):
            s = scores[sub * ch:(sub + 1) * ch]
            m_sub = jnp.max(s, axis=0, keepdims=True)
            e = jnp.exp2(s - m_sub)
            parts.append((m_sub, jnp.sum(e, axis=0, keepdims=True),
                          _dot(vcct_ref[:, sub * ch:(sub + 1) * ch], e.astype(BF16)), e))
        m = parts[0][0]
        for m_sub, _, _, _ in parts[1:]:
            m = jnp.maximum(m, m_sub)
        scales = [jnp.exp2(m_sub - m) for m_sub, _, _, _ in parts]
        l = scales[0] * parts[0][1]
        acc = scales[0] * parts[0][2]
        for scale, (_, l_sub, acc_sub, _) in zip(scales[1:], parts[1:]):
            l = l + scale * l_sub
            acc = acc + scale * acc_sub
        inv_l = jnp.where(heads(tq1 >= CMP_BLOCK - 1), 1.0 / l, 0.0)
        ocmp_scr[...] = acc * inv_l
        for sub in range(n_sub):
            pc = parts[sub][3] * (scales[sub] * inv_l)
            p_sum = pc[:, :Q_BLOCK]
            for hp in range(1, hpg):
                p_sum = p_sum + pc[:, hp * Q_BLOCK:(hp + 1) * Q_BLOCK]
            imp_scr[IMP_PAD + sub * ch:IMP_PAD + (sub + 1) * ch, :] = p_sum

    for variant in range(n_cmp // ch):
        pl.when(n_chunks == variant + 1)(functools.partial(compressed, variant + 1))
    o_cmp = ocmp_scr[...]

    ratio = SEL_BLOCK // CMP_STRIDE
    sel_shift = SEL_BLOCK.bit_length() - 1
    cur = tq1 >> sel_shift
    n_forced = 3

    span = WINDOW + Q_BLOCK
    k_start = pl.multiple_of(jnp.maximum(t0 - WINDOW, 0), Q_BLOCK)

    def window_scores():
        diff = tq1 - (k_start + lax.broadcasted_iota(jnp.int32, (span, 1), 0))
        inside = pltpu.bitcast(diff, jnp.uint32) < jnp.uint32(WINDOW)
        bias = jnp.where(inside, 0.0, MASKED).astype(BF16)
        wins_scr[...] = _dot(jnp.concatenate([bias, kw_ref[pl.ds(k_start, span), :]], axis=1), q_aug_t)

    def window_values():
        sw = wins_scr[...]
        pw = jnp.exp2(sw - jnp.max(sw, axis=0, keepdims=True)).astype(BF16)
        acc_win = _dot(vwt_ref[:, pl.ds(k_start, span)], pw)
        owin_scr[...] = acc_win[:NSA_HEAD_DIM] / acc_win[NSA_HEAD_DIM:NSA_HEAD_DIM + 1]

    def select_blocks(rows):
        window_scores()
        imp = imp_scr[pl.ds(IMP_PAD - 1, rows, stride=ratio), :]
        for k in range(1, ratio + 1):
            imp = imp + imp_scr[pl.ds(IMP_PAD - 1 + k, rows, stride=ratio), :]
        j = lax.broadcasted_iota(jnp.int32, (rows, Q_BLOCK), 0)
        j_f = j.astype(F32)
        forced = (j == 0) | (j == cur) | (j == cur - 1)
        valid = j <= cur

        def pick(_, sc):
            best = jnp.max(sc, axis=0, keepdims=True)
            idx = jnp.min(jnp.where(sc == best, j_f, float(rows)), axis=0, keepdims=True)
            return jnp.where(j_f == idx, neg_inf, sc)

        left = lax.fori_loop(0, max(min(SEL_TOPK, n_sel) - n_forced, 0), pick,
                             jnp.where(valid & ~forced, imp, neg_inf), unroll=True)
        sel_scr[0:rows, :] = jnp.where(forced | (valid & (left == neg_inf)), 0.0, MASKED)
        if rows < n_sel:
            sel_scr[rows:n_sel, :] = jnp.full((n_sel - rows, Q_BLOCK), MASKED, F32)

    row_step = min(TOPK_ROW_STEP, n_sel)
    last_block = (t0 + Q_BLOCK - 1) >> sel_shift
    for variant in range(n_sel // row_step):
        pl.when(last_block // row_step == variant)(functools.partial(select_blocks, (variant + 1) * row_step))

    blocks_per_tile = kt_size // SEL_BLOCK

    def slc_tile(kt, carry, n_sub, causal_from, after_scores=None):
        m_run, acc = carry
        base = pl.multiple_of(kt * kt_size, kt_size)
        biases = []
        for sub in range(n_sub):
            blk0 = kt * blocks_per_tile + sub * (sub_size // SEL_BLOCK)
            bias = jnp.concatenate(
                [jnp.broadcast_to(sel_scr[pl.ds(blk0 + jb, 1), :], (SEL_BLOCK, Q_BLOCK))
                 for jb in range(sub_size // SEL_BLOCK)], axis=0)
            if sub >= causal_from:
                kpos = base + sub * sub_size + lax.broadcasted_iota(jnp.int32, (sub_size, 1), 0)
                bias = jnp.where(kpos <= tq1, bias, MASKED)
            biases.append(bias.astype(BF16))
        k_aug = jnp.concatenate([jnp.concatenate(biases, axis=0), ks_ref[pl.ds(base, n_sub * sub_size), :]], axis=1)
        scores = _dot(k_aug, q_aug_t)
        if after_scores is not None:
            after_scores()
        parts = []
        for sub in range(n_sub):
            sc = scores[sub * sub_size:(sub + 1) * sub_size]
            m_sub = jnp.max(sc, axis=0, keepdims=True)
            pe = jnp.exp2(sc - m_sub).astype(BF16)
            k0 = pl.multiple_of(base + sub * sub_size, sub_size)
            parts.append((m_sub, _dot(vst_ref[:, pl.ds(k0, sub_size)], pe)))
        m_new = m_run
        for m_sub, _ in parts:
            m_new = jnp.maximum(m_new, m_sub)
        acc = jnp.exp2(m_run - m_new) * acc
        for m_sub, acc_sub in parts:
            acc = acc + jnp.exp2(m_sub - m_new) * acc_sub
        return m_new, acc

    subs_per_tile = kt_size // sub_size
    n_full = t0 // kt_size
    carry = lax.fori_loop(
        0, n_full, functools.partial(slc_tile, n_sub=subs_per_tile, causal_from=subs_per_tile),
        (jnp.full((1, width), MASKED, F32), jnp.zeros((V_ROWS, width), F32)))
    tail_step = min(2, subs_per_tile)
    tail_len = (t0 - n_full * kt_size) // (tail_step * sub_size)
    for variant in range(subs_per_tile // tail_step):
        n_sub = tail_step * (variant + 1)

        @pl.when(tail_len == variant)
        def _(n_sub=n_sub):
            _, acc_slc = slc_tile(n_full, carry, n_sub=n_sub, causal_from=n_sub - tail_step,
                                  after_scores=window_values)
            oslc_scr[...] = acc_slc[:NSA_HEAD_DIM] / acc_slc[NSA_HEAD_DIM:NSA_HEAD_DIM + 1]

    o_slc = oslc_scr[...]
    o_win = owin_scr[...]

    for hp in range(hpg):
        cols = slice(hp * Q_BLOCK, (hp + 1) * Q_BLOCK)
        row = (grp * hpg + hp) * 3
        o_t = (gt_ref[qi, pl.ds(row, 1), :] * o_cmp[:, cols] + gt_ref[qi, pl.ds(row + 1, 1), :] * o_slc[:, cols]
               + gt_ref[qi, pl.ds(row + 2, 1), :] * o_win[:, cols])
        o_ref[pl.ds(pl.multiple_of(qi * Q_BLOCK, Q_BLOCK), Q_BLOCK), hp * NSA_HEAD_DIM:(hp + 1) * NSA_HEAD_DIM] = (
            o_t.T.astype(o_ref.dtype))


def _nsa_attention(qs, kcc, vcct, ks, vst, kw, vwt, gt, B, S, blocks=4, kt_size=4096, sub_size=256, cmp_chunk=256):
    G, Dh, nq, hpg = NSA_KV_GROUPS, NSA_HEAD_DIM, S // Q_BLOCK, NSA_HEADS_PER_GROUP
    kt_size = min(kt_size, S)
    blocks = min(blocks, nq)
    n_cmp, n_sel = S // CMP_STRIDE, S // SEL_BLOCK
    cmp_chunk = min(cmp_chunk, n_cmp)
    per_group = lambda shape: pl.BlockSpec((None, None) + shape, lambda b, g, i: (b, g, 0, 0))
    return pl.pallas_call(
        functools.partial(_nsa_body, blocks=blocks, kt_size=kt_size, sub_size=min(sub_size, kt_size),
                          cmp_chunk=cmp_chunk),
        grid=(B, G, nq // blocks),
        in_specs=[
            pl.BlockSpec((None, None, Dh, blocks * hpg * Q_BLOCK), lambda b, g, i: (b, g, 0, i)),
            per_group((n_cmp, Dh)), per_group((Dh, n_cmp)),
            per_group((S, Dh)), per_group((V_ROWS, S)),
            per_group((S, Dh)), per_group((V_ROWS, S)),
            pl.BlockSpec((None, blocks, GATE_ROWS, Q_BLOCK), lambda b, g, i: (b, i, 0, 0)),
        ],
        out_specs=pl.BlockSpec((None, blocks * Q_BLOCK, hpg * Dh), lambda b, g, i: (b, i, g)),
        out_shape=jax.ShapeDtypeStruct((B, S, NSA_HEADS * Dh), BF16),
        scratch_shapes=[
            pltpu.VMEM((n_cmp + 2 * IMP_PAD, Q_BLOCK), F32),
            pltpu.VMEM((n_sel, Q_BLOCK), F32),
            pltpu.VMEM((Dh, hpg * Q_BLOCK), F32),
            pltpu.VMEM((WINDOW + Q_BLOCK, hpg * Q_BLOCK), F32),
            pltpu.VMEM((Dh, hpg * Q_BLOCK), F32),
            pltpu.VMEM((Dh, hpg * Q_BLOCK), F32),
        ],
        compiler_params=pltpu.CompilerParams(
            dimension_semantics=("parallel", "parallel", "arbitrary"), vmem_limit_bytes=VMEM_LIMIT),
        name="nsa_attention",
    )(qs, kcc, vcct, ks, vst, kw, vwt, gt)


def _merge_body(ya_ref, yb_ref, yc_ref, ga_ref, gb_ref, gc_ref, h_ref, wb_ref, wo_ref, o_ref):
    merged = jax.nn.sigmoid(ga_ref[...].astype(F32)) * _dot(ya_ref[...], wb_ref[0])
    merged = merged + jax.nn.sigmoid(gb_ref[...].astype(F32)) * _dot(yb_ref[...], wb_ref[1])
    merged = merged + jax.nn.sigmoid(gc_ref[...].astype(F32)) * _dot(yc_ref[...], wb_ref[2])
    o_ref[...] = h_ref[...] + _dot(merged.astype(BF16), wo_ref[...])


def _merge(ya, yb, yc, main, h, w_branch, w_out, tm=1024):
    T, D = h.shape
    y_spec = pl.BlockSpec((tm, 512), lambda i: (i, 0))
    gate_spec = lambda k: pl.BlockSpec((tm, D), lambda i: (i, MAIN_GM // D + k))
    return pl.pallas_call(
        _merge_body,
        grid=(T // tm,),
        in_specs=[
            y_spec, y_spec, y_spec, gate_spec(0), gate_spec(1), gate_spec(2),
            pl.BlockSpec((tm, D), lambda i: (i, 0)),
            pl.BlockSpec((3, 512, D), lambda i: (0, 0, 0)),
            pl.BlockSpec((D, D), lambda i: (0, 0)),
        ],
        out_specs=pl.BlockSpec((tm, D), lambda i: (i, 0)),
        out_shape=jax.ShapeDtypeStruct((T, D), F32),
        compiler_params=pltpu.CompilerParams(dimension_semantics=("parallel",), vmem_limit_bytes=VMEM_LIMIT),
        name="merge_out",
    )(ya, yb, yc, main, main, main, h, w_branch, w_out)


def _ffn_body(h_ref, gf_ref, w1_ref, w2_ref, gp_ref, wg_ref, p_ref, wp_ref, gl_ref, o_ref, f_scr, acc_scr, *, final_norm):
    j = pl.program_id(1)

    @pl.when(j == 0)
    def _():
        f_scr[...] = _rms(h_ref[...], gf_ref[...]).astype(BF16)
        acc_scr[...] = jnp.zeros_like(acc_scr)

    a = jnp.maximum(_dot(f_scr[...], w1_ref[...]), 0.0)
    acc_scr[...] += _dot((a * a).astype(BF16), w2_ref[...])

    @pl.when(j == pl.num_programs(1) - 1)
    def _():
        h2 = h_ref[...] + acc_scr[...]
        gate = jax.nn.sigmoid(_dot(_rms(h2, gp_ref[...]).astype(BF16), wg_ref[...]))
        h3 = h2 + gate * _dot(p_ref[...].astype(BF16), wp_ref[...])
        o_ref[...] = _rms(h3, gl_ref[...]) if final_norm else h3


def _ffn_ple(h, norm_ffn, w1, w2, norm_ple, w_gate, p, w_proj, norm_last, final_norm, tm=1024, tf=1024):
    T, D = h.shape
    tm = min(tm, T)
    Fdim = w1.shape[1]
    Pdim = p.shape[1]
    vec = pl.BlockSpec((1, D), lambda i, j: (0, 0))
    return pl.pallas_call(
        functools.partial(_ffn_body, final_norm=final_norm),
        grid=(T // tm, Fdim // tf),
        in_specs=[
            pl.BlockSpec((tm, D), lambda i, j: (i, 0)), vec,
            pl.BlockSpec((D, tf), lambda i, j: (0, j)),
            pl.BlockSpec((tf, D), lambda i, j: (j, 0)),
            vec,
            pl.BlockSpec((D, D), lambda i, j: (0, 0)),
            pl.BlockSpec((tm, Pdim), lambda i, j: (i, 0)),
            pl.BlockSpec((Pdim, D), lambda i, j: (0, 0)),
            vec,
        ],
        out_specs=pl.BlockSpec((tm, D), lambda i, j: (i, 0)),
        out_shape=jax.ShapeDtypeStruct((T, D), F32),
        scratch_shapes=[pltpu.VMEM((tm, D), BF16), pltpu.VMEM((tm, D), F32)],
        compiler_params=pltpu.CompilerParams(
            dimension_semantics=("parallel", "arbitrary"), vmem_limit_bytes=VMEM_LIMIT),
        name="ffn_ple",
    )(h, norm_ffn, w1, w2, norm_ple, w_gate, p, w_proj, norm_last)


def _split_w_in(w):
    parts, off = [], 0
    for sz in IN_SIZES:
        parts.append(w[:, off:off + sz])
        off += sz
    u, q_nsa, kv_nsa, g_nsa, q_gla, k_gla, v_gla, a_gla, r_gla, g_merge = parts
    D = w.shape[0]
    pad = lambda n: jnp.zeros((D, n), w.dtype)
    main = jnp.concatenate([u, q_gla, k_gla, v_gla, r_gla, g_merge, a_gla, pad(MAIN_WIDTH - MAIN_A - GLA_GATE_RANK)], axis=1)
    kv = [kv_nsa[:, s * LANES:(s + 1) * LANES] for s in range(6)]
    nsa = jnp.concatenate([q_nsa, kv[0], kv[2], kv[4], kv[1], kv[3], kv[5], g_nsa, pad(NSA_WIDTH - NSA_G - 24)], axis=1)
    return main.astype(BF16), nsa.astype(BF16)


def kernel(x, p, positions, norm_mix, w_in, pool_w, pool_scale, cmp_pos_k, cmp_w_k, cmp_pos_v, cmp_w_v, gla_w_gate, gla_b_gate, gla_norm, w_branch, w_out, norm_ffn, w_ff1, w_ff2, norm_ple, w_ple_gate, w_ple_proj, norm_final):
    B, S, D = x.shape
    depth = w_in.shape[0]
    T = B * S
    Dh = NSA_HEAD_DIM
    row = lambda v: v.reshape(1, -1).astype(F32)

    half = NSA_ROT_DIM // 2
    inv_freq = jnp.power(ROPE_THETA, -jnp.arange(half, dtype=F32) * (2.0 / NSA_ROT_DIM)).reshape(half, 1)
    pos_rows = positions.reshape(B, S // Q_BLOCK, 1, Q_BLOCK)
    rope_spread = _rope_spread_matrix()

    h = x.reshape(T, D)
    for i in range(depth):
        w_main, w_nsa = _split_w_in(w_in[i])
        main = _norm_proj(h, row(norm_mix[i]), w_main, BF16, tn=MAIN_WIDTH // 3).reshape(B, S, MAIN_WIDTH)
        nsa_proj = _norm_proj(h, row(norm_mix[i]), w_nsa, F32, tn=NSA_WIDTH).reshape(B, S, NSA_WIDTH)

        y_a = _pool_mixer(main, pool_w[i].astype(BF16), row(pool_scale[i]), B, S)

        wg = jnp.zeros((LANES, GLA_HEADS * GLA_DK), BF16).at[:GLA_GATE_RANK].set(gla_w_gate[i].astype(BF16))
        y_c = _gla_mixer(main, wg, row(gla_b_gate[i]), row(gla_norm[i]), B, S)

        qs, kc, vc, ks, kw, vst, vwt, gt = _nsa_prep(nsa_proj, pos_rows, inv_freq, rope_spread, B, S)
        kcc = _compress(kc, cmp_w_k[i].reshape(CMP_BLOCK, Dh, Dh).astype(BF16),
                        cmp_pos_k[i].reshape(CMP_BLOCK, 1, Dh).astype(F32), transpose_out=False)
        vcct = _compress(vc, cmp_w_v[i].reshape(CMP_BLOCK, Dh, Dh).astype(BF16),
                         cmp_pos_v[i].reshape(CMP_BLOCK, 1, Dh).astype(F32), transpose_out=True)
        y_b = _nsa_attention(qs, kcc, vcct, ks, vst, kw, vwt, gt, B, S)

        h = _merge(y_a.reshape(T, -1), y_b.reshape(T, -1), y_c.reshape(T, -1), main.reshape(T, MAIN_WIDTH), h,
                   w_branch[i].astype(BF16), w_out[i].astype(BF16))
        h = _ffn_ple(h, row(norm_ffn[i]), w_ff1[i].astype(BF16), w_ff2[i].astype(BF16), row(norm_ple[i]),
                     w_ple_gate[i].astype(BF16), p[i].reshape(T, -1), w_ple_proj[i].astype(BF16),
                     row(norm_final), final_norm=(i == depth - 1))
    return h.reshape(B, S, D)
```

```python
import functools

import jax
import jax.numpy as jnp
import numpy as np
from jax import lax
from jax.experimental import pallas as pl
from jax.experimental.pallas import tpu as pltpu

F32, BF16 = jnp.float32, jnp.bfloat16

EPS = 1e-6
ROPE_THETA = 500000.0
POOL_WINDOWS = (2, 4, 8, 16)
POOL_GROUP_DIM = 128
POOL_HALO = max(POOL_WINDOWS)
NSA_HEADS = 8
NSA_KV_GROUPS = 2
NSA_HEADS_PER_GROUP = NSA_HEADS // NSA_KV_GROUPS
NSA_HEAD_DIM = 64
NSA_ROT_DIM = 16
CMP_BLOCK = 32
CMP_STRIDE = 16
SEL_BLOCK = 64
SEL_TOPK = 16
WINDOW = 512
Q_BLOCK = 128
GLA_HEADS = 4
GLA_DK = 64
GLA_DV = 128
GLA_GATE_RANK = 16
GLA_TAU = 16.0
GLA_CHUNK = 64
IN_SIZES = (512, 512, 768, 24, 256, 256, 512, 16, 512, 3072)

LANES = 128
MASKED = -1e30
LOG2E = 1.4426950408889634
Q_SCALE = NSA_HEAD_DIM ** -0.5 * LOG2E
V_ROWS = NSA_HEAD_DIM + 16
IMP_PAD = 8
GATE_ROWS = 32
TOPK_ROW_STEP = 64
VMEM_LIMIT = 56 * 1024 * 1024

MAIN_U, MAIN_QG, MAIN_KG, MAIN_VG, MAIN_RG, MAIN_GM, MAIN_A, MAIN_WIDTH = 0, 512, 768, 1024, 1536, 2048, 5120, 5376
NSA_Q, NSA_KC, NSA_KS, NSA_KW, NSA_VC, NSA_VS, NSA_VW, NSA_G, NSA_WIDTH = 0, 512, 640, 768, 896, 1024, 1152, 1280, 1536


def _nt(a, b):
    return lax.dot_general(a, b, (((1,), (1,)), ((), ())), preferred_element_type=F32)


def _tn(a, b):
    return lax.dot_general(a, b, (((0,), (0,)), ((), ())), preferred_element_type=F32)


def _dot(a, b):
    return jnp.dot(a, b, preferred_element_type=F32)


def _rms(x, gain):
    return x * lax.rsqrt(jnp.mean(x * x, axis=-1, keepdims=True) + EPS) * gain


def _proj_body(h_ref, g_ref, w_ref, o_ref, a_scr):
    @pl.when(pl.program_id(1) == 0)
    def _():
        a_scr[...] = _rms(h_ref[...], g_ref[...]).astype(BF16)

    o_ref[...] = _dot(a_scr[...], w_ref[...]).astype(o_ref.dtype)


def _norm_proj(h, gain, w, out_dtype, tn, tm=1024):
    T, D = h.shape
    N = w.shape[1]
    tm = min(tm, T)
    return pl.pallas_call(
        _proj_body,
        grid=(T // tm, N // tn),
        in_specs=[
            pl.BlockSpec((tm, D), lambda i, j: (i, 0)),
            pl.BlockSpec((1, D), lambda i, j: (0, 0)),
            pl.BlockSpec((D, tn), lambda i, j: (0, j)),
        ],
        out_specs=pl.BlockSpec((tm, tn), lambda i, j: (i, j)),
        out_shape=jax.ShapeDtypeStruct((T, N), out_dtype),
        scratch_shapes=[pltpu.VMEM((tm, D), BF16)],
        compiler_params=pltpu.CompilerParams(
            dimension_semantics=("parallel", "arbitrary"), vmem_limit_bytes=VMEM_LIMIT),
        name="norm_proj",
    )(h, gain, w)


def _pool_body(u_ref, halo_ref, w_ref, sc_ref, o_ref, *, ts):
    i = pl.program_id(1)
    cur = u_ref[...].astype(F32)
    halo = jnp.where(i == 0, 0.0, halo_ref[...].astype(F32))
    t = i * ts + lax.broadcasted_iota(jnp.int32, (ts, 1), 0)
    for g, win in enumerate(POOL_WINDOWS):
        lo, hi = g * POOL_GROUP_DIM, (g + 1) * POOL_GROUP_DIM
        x = cur[:, lo:hi]
        e = jnp.concatenate([halo[:, lo:hi], x], axis=0)
        step = 1
        while step < win:
            e = e[step:] + e[:-step]
            step *= 2
        wsum = e[POOL_HALO - (win - 1):]
        cnt = jnp.minimum(t + 1, win).astype(F32)
        pooled = wsum / cnt - x
        y = _dot(pooled.astype(BF16), w_ref[g]) * sc_ref[:, lo:hi]
        o_ref[:, lo:hi] = y.astype(o_ref.dtype)


def _pool_mixer(main, pool_w, scale, B, S, ts=512):
    ts = min(ts, S)
    halo_blocks = ts // POOL_HALO
    return pl.pallas_call(
        functools.partial(_pool_body, ts=ts),
        grid=(B, S // ts),
        in_specs=[
            pl.BlockSpec((None, ts, 512), lambda b, i: (b, i, MAIN_U // 512)),
            pl.BlockSpec((None, POOL_HALO, 512), lambda b, i: (b, jnp.maximum(i * halo_blocks - 1, 0), MAIN_U // 512)),
            pl.BlockSpec((4, 128, 128), lambda b, i: (0, 0, 0)),
            pl.BlockSpec((1, 512), lambda b, i: (0, 0)),
        ],
        out_specs=pl.BlockSpec((None, ts, 512), lambda b, i: (b, i, 0)),
        out_shape=jax.ShapeDtypeStruct((B, S, 512), BF16),
        compiler_params=pltpu.CompilerParams(dimension_semantics=("parallel", "arbitrary")),
        name="pool_mixer",
    )(main, main, pool_w, scale)


def _gla_body(q_ref, k_ref, v_ref, r_ref, a_ref, wg_ref, bg_ref, ng_ref, o_ref, st_ref, *, tc):
    @pl.when(pl.program_id(1) == 0)
    def _():
        st_ref[...] = jnp.zeros_like(st_ref)

    C = GLA_CHUNK
    n_chunks = tc // C
    causal = lax.broadcasted_iota(jnp.int32, (C, C), 0) >= lax.broadcasted_iota(jnp.int32, (C, C), 1)
    width = GLA_HEADS * GLA_DK
    hk = [slice(h * GLA_DK, (h + 1) * GLA_DK) for h in range(GLA_HEADS)]
    hv = [slice(h * GLA_DV, (h + 1) * GLA_DV) for h in range(GLA_HEADS)]
    rows = [slice(c * C, (c + 1) * C) for c in range(n_chunks)]

    z = _dot(a_ref[...], wg_ref[...]) + bg_ref[...]
    b = jax.nn.log_sigmoid(z) * (1.0 / GLA_TAU)
    row_in_chunk = lax.broadcasted_iota(jnp.int32, (tc, 1), 0) % C
    step = 1
    while step < C:
        shifted = jnp.concatenate([jnp.zeros((step, width), F32), b[:-step]], axis=0)
        b = b + jnp.where(row_in_chunk >= step, shifted, 0.0)
        step *= 2
    b_last = [b[(c + 1) * C - 1:(c + 1) * C, :] for c in range(n_chunks)]
    b_last_rows = jnp.concatenate([jnp.broadcast_to(bl, (C, width)) for bl in b_last], axis=0)
    qf = q_ref[...].astype(F32) * (GLA_DK ** -0.5)
    kf = k_ref[...].astype(F32)
    q_s = (qf * jnp.exp(b)).astype(BF16)
    k_s = (kf * jnp.exp(-b)).astype(BF16)
    k_t = (kf * jnp.exp(b_last_rows - b)).astype(BF16)

    att = [[jnp.where(causal, _nt(q_s[rows[c], hk[h]], k_s[rows[c], hk[h]]), 0.0).astype(BF16)
            for h in range(GLA_HEADS)] for c in range(n_chunks)]
    o_intra = [[_dot(att[c][h], v_ref[rows[c], hv[h]]) for h in range(GLA_HEADS)] for c in range(n_chunks)]
    kv = [[_tn(v_ref[rows[c], hv[h]], k_t[rows[c], hk[h]]) for h in range(GLA_HEADS)] for c in range(n_chunks)]

    state_t = [st_ref[h] for h in range(GLA_HEADS)]
    for c in range(n_chunks):
        decay = jnp.exp(b_last[c])
        for h in range(GLA_HEADS):
            o = o_intra[c][h] + _nt(q_s[rows[c], hk[h]], state_t[h].astype(BF16))
            state_t[h] = state_t[h] * decay[:, hk[h]] + kv[c][h]
            o = _rms(o, ng_ref[...])
            r = r_ref[rows[c], hv[h]].astype(F32)
            o_ref[rows[c], hv[h]] = (o * (r * jax.nn.sigmoid(r))).astype(o_ref.dtype)
    for h in range(GLA_HEADS):
        st_ref[h] = state_t[h]


def _gla_mixer(main, w_gate, b_gate, norm_g, B, S, tc=512):
    tc = min(tc, S)
    return pl.pallas_call(
        functools.partial(_gla_body, tc=tc),
        grid=(B, S // tc),
        in_specs=[
            pl.BlockSpec((None, tc, 256), lambda b, i: (b, i, MAIN_QG // 256)),
            pl.BlockSpec((None, tc, 256), lambda b, i: (b, i, MAIN_KG // 256)),
            pl.BlockSpec((None, tc, 512), lambda b, i: (b, i, MAIN_VG // 512)),
            pl.BlockSpec((None, tc, 512), lambda b, i: (b, i, MAIN_RG // 512)),
            pl.BlockSpec((None, tc, 128), lambda b, i: (b, i, MAIN_A // 128)),
            pl.BlockSpec((128, 256), lambda b, i: (0, 0)),
            pl.BlockSpec((1, 256), lambda b, i: (0, 0)),
            pl.BlockSpec((1, 128), lambda b, i: (0, 0)),
        ],
        out_specs=pl.BlockSpec((None, tc, 512), lambda b, i: (b, i, 0)),
        out_shape=jax.ShapeDtypeStruct((B, S, 512), BF16),
        scratch_shapes=[pltpu.VMEM((GLA_HEADS, GLA_DV, GLA_DK), F32)],
        compiler_params=pltpu.CompilerParams(dimension_semantics=("parallel", "arbitrary")),
        name="gla_mixer",
    )(main, main, main, main, main, w_gate, b_gate, norm_g)


def _nsa_prep_body(x_ref, pos_ref, invf_ref, spread_ref, qs_ref, kc_ref, vc_ref, ks_ref, kw_ref, vst_ref, vwt_ref,
                   gt_ref, *, blocks):
    half = NSA_ROT_DIM // 2
    hpg = NSA_HEADS_PER_GROUP
    ones_row = (lax.broadcasted_iota(jnp.int32, (V_ROWS - NSA_HEAD_DIM, Q_BLOCK), 0) == 0).astype(BF16)

    def pieces(a):
        p1 = a.astype(BF16).astype(F32)
        p2 = (a - p1).astype(BF16).astype(F32)
        return [p1, p2, (a - p1 - p2).astype(BF16).astype(F32)]

    for blk in range(blocks):
        tok = slice(blk * Q_BLOCK, (blk + 1) * Q_BLOCK)
        ang = invf_ref[...] * pos_ref[blk].astype(F32)
        lhs = jnp.concatenate(
            pieces(jnp.cos(ang)) + pieces(jnp.sin(ang)) + [jnp.ones((2 * half, Q_BLOCK), F32)], axis=0)
        tables = _tn(lhs.astype(BF16), spread_ref[...])
        cs = tables[:, :LANES]
        s_lo = tables[:, LANES:2 * LANES]
        s_hi = tables[:, 2 * LANES:]

        def rope(x):
            return x * cs + pltpu.roll(x, LANES - half, 1) * s_lo + pltpu.roll(x, half, 1) * s_hi

        def split_groups(x, ref, dtype):
            ref[0, tok, :] = x[:, :NSA_HEAD_DIM].astype(dtype)
            ref[1, tok, :] = x[:, NSA_HEAD_DIM:].astype(dtype)

        for pair in range(NSA_HEADS // 2):
            cols = slice(NSA_Q + pair * LANES, NSA_Q + (pair + 1) * LANES)
            qr_t = (rope(x_ref[tok, cols]) * Q_SCALE).T
            for sub in range(2):
                head = 2 * pair + sub
                grp, hp = head // hpg, head % hpg
                col0 = (blk * hpg + hp) * Q_BLOCK
                qs_ref[grp, :, col0:col0 + Q_BLOCK] = qr_t[sub * NSA_HEAD_DIM:(sub + 1) * NSA_HEAD_DIM].astype(BF16)
        split_groups(rope(x_ref[tok, NSA_KC:NSA_KC + LANES]), kc_ref, F32)
        split_groups(rope(x_ref[tok, NSA_KS:NSA_KS + LANES]), ks_ref, BF16)
        split_groups(rope(x_ref[tok, NSA_KW:NSA_KW + LANES]), kw_ref, BF16)
        split_groups(x_ref[tok, NSA_VC:NSA_VC + LANES], vc_ref, F32)
        vst = x_ref[tok, NSA_VS:NSA_VS + LANES].T
        vwt = x_ref[tok, NSA_VW:NSA_VW + LANES].T
        for grp in range(NSA_KV_GROUPS):
            for ref, v in ((vst_ref, vst), (vwt_ref, vwt)):
                ref[grp, :NSA_HEAD_DIM, tok] = v[grp * NSA_HEAD_DIM:(grp + 1) * NSA_HEAD_DIM].astype(BF16)
                ref[grp, NSA_HEAD_DIM:, tok] = ones_row
        gt_ref[blk] = jax.nn.sigmoid(x_ref[tok, NSA_G:NSA_G + LANES]).T[:GATE_ROWS]


def _rope_spread_matrix():
    half = NSA_ROT_DIM // 2
    lane = np.arange(LANES)
    d = lane % NSA_HEAD_DIM
    m = np.zeros((8 * half, 3 * LANES), np.float32)
    for piece in range(3):
        for f in range(half):
            hit = d % half == f
            m[piece * half + f, :LANES] = hit & (d < NSA_ROT_DIM)
            m[(3 + piece) * half + f, LANES:2 * LANES] = -1.0 * (hit & (d < half))
            m[(3 + piece) * half + f, 2 * LANES:] = hit & (d >= half) & (d < NSA_ROT_DIM)
    m[6 * half, :LANES] = d >= NSA_ROT_DIM
    return jnp.asarray(m, BF16)


def _nsa_prep(nsa_proj, positions, inv_freq, rope_spread, B, S, blocks=4):
    G, Dh, nq = NSA_KV_GROUPS, NSA_HEAD_DIM, S // Q_BLOCK
    blocks = min(blocks, nq)
    ts = blocks * Q_BLOCK
    tok = lambda dtype: jax.ShapeDtypeStruct((B, G, S, Dh), dtype)
    tok_spec = pl.BlockSpec((None, G, ts, Dh), lambda b, i: (b, 0, i, 0))
    tr = jax.ShapeDtypeStruct((B, G, V_ROWS, S), BF16)
    tr_spec = pl.BlockSpec((None, G, V_ROWS, ts), lambda b, i: (b, 0, 0, i))
    return pl.pallas_call(
        functools.partial(_nsa_prep_body, blocks=blocks),
        grid=(B, nq // blocks),
        in_specs=[
            pl.BlockSpec((None, ts, NSA_WIDTH), lambda b, i: (b, i, 0)),
            pl.BlockSpec((None, blocks, 1, Q_BLOCK), lambda b, i: (b, i, 0, 0)),
            pl.BlockSpec(inv_freq.shape, lambda b, i: (0, 0)),
            pl.BlockSpec(rope_spread.shape, lambda b, i: (0, 0)),
        ],
        out_specs=[
            pl.BlockSpec((None, G, Dh, NSA_HEADS_PER_GROUP * ts), lambda b, i: (b, 0, 0, i)),
            tok_spec, tok_spec, tok_spec, tok_spec, tr_spec, tr_spec,
            pl.BlockSpec((None, blocks, GATE_ROWS, Q_BLOCK), lambda b, i: (b, i, 0, 0)),
        ],
        out_shape=[
            jax.ShapeDtypeStruct((B, G, Dh, NSA_HEADS_PER_GROUP * S), BF16),
            tok(F32), tok(F32), tok(BF16), tok(BF16), tr, tr,
            jax.ShapeDtypeStruct((B, nq, GATE_ROWS, Q_BLOCK), F32),
        ],
        compiler_params=pltpu.CompilerParams(dimension_semantics=("parallel", "parallel")),
        name="nsa_prep",
    )(nsa_proj, positions, inv_freq, rope_spread)


def _cmp_body(x_ref, w_ref, p_ref, o_ref, *, n_blk, transpose_out):
    half = CMP_BLOCK // 2
    first = jnp.zeros((n_blk, NSA_HEAD_DIM), F32)
    second = jnp.zeros((n_blk, NSA_HEAD_DIM), F32)
    for l in range(half):
        rows = x_ref[pl.ds(l, n_blk, stride=CMP_STRIDE), :]
        first = first + _dot((rows + p_ref[l]).astype(BF16), w_ref[l])
        second = second + _dot((rows + p_ref[half + l]).astype(BF16), w_ref[half + l])
    out = first + jnp.concatenate([second[1:], jnp.zeros((1, NSA_HEAD_DIM), F32)], axis=0)
    o_ref[...] = (out.T if transpose_out else out).astype(o_ref.dtype)


def _compress(x, w, pos, transpose_out):
    B, G, S, Dh = x.shape
    n_blk = S // CMP_STRIDE
    out_block = (None, None, Dh, n_blk) if transpose_out else (None, None, n_blk, Dh)
    out_shape = (B, G, Dh, n_blk) if transpose_out else (B, G, n_blk, Dh)
    return pl.pallas_call(
        functools.partial(_cmp_body, n_blk=n_blk, transpose_out=transpose_out),
        grid=(B, G),
        in_specs=[
            pl.BlockSpec((None, None, S, Dh), lambda b, g: (b, g, 0, 0)),
            pl.BlockSpec((CMP_BLOCK, Dh, Dh), lambda b, g: (0, 0, 0)),
            pl.BlockSpec((CMP_BLOCK, 1, Dh), lambda b, g: (0, 0, 0)),
        ],
        out_specs=pl.BlockSpec(out_block, lambda b, g: (b, g, 0, 0)),
        out_shape=jax.ShapeDtypeStruct(out_shape, BF16),
        compiler_params=pltpu.CompilerParams(
            dimension_semantics=("parallel", "parallel"), vmem_limit_bytes=VMEM_LIMIT),
        name="nsa_compress",
    )(x, w, pos)


def _nsa_body(*refs, blocks, **tiles):
    def one_block(qi, carry):
        _nsa_block(qi, *refs, blocks=blocks, **tiles)
        return carry

    lax.fori_loop(0, blocks, one_block, 0)


def _nsa_block(qi, q_ref, kcc_ref, vcct_ref, ks_ref, vst_ref, kw_ref, vwt_ref, gt_ref, o_ref,
               imp_scr, sel_scr, ocmp_scr, wins_scr, owin_scr, oslc_scr, *, blocks, kt_size, sub_size, cmp_chunk):
    grp = pl.program_id(1)
    t0 = (pl.program_id(2) * blocks + qi) * Q_BLOCK
    hpg = NSA_HEADS_PER_GROUP
    n_cmp = kcc_ref.shape[0]
    n_sel = sel_scr.shape[0]
    q_t = q_ref[:, pl.ds(pl.multiple_of(qi * (hpg * Q_BLOCK), hpg * Q_BLOCK), hpg * Q_BLOCK)]
    tq1 = t0 + lax.broadcasted_iota(jnp.int32, (1, Q_BLOCK), 1)
    heads = lambda x: jnp.concatenate([x] * hpg, axis=1)

    width = hpg * Q_BLOCK
    neg_inf = -jnp.inf
    eye = (lax.broadcasted_iota(jnp.int32, (Q_BLOCK, Q_BLOCK), 0)
           == lax.broadcasted_iota(jnp.int32, (Q_BLOCK, Q_BLOCK), 1)).astype(BF16)
    q_aug_t = jnp.concatenate([jnp.concatenate([eye] * hpg, axis=1), q_t], axis=0)

    ch = cmp_chunk
    n_chunks = (((t0 + Q_BLOCK - CMP_BLOCK) >> (CMP_STRIDE.bit_length() - 1)) + ch) // ch

    def compressed(n_sub):
        imp_scr[0:IMP_PAD, :] = jnp.zeros((IMP_PAD, Q_BLOCK), F32)
        imp_scr[IMP_PAD + n_sub * ch:, :] = jnp.zeros((n_cmp + IMP_PAD - n_sub * ch, Q_BLOCK), F32)
        biases = []
        for sub in range(n_sub):
            if sub >= n_sub - 2:
                cmp_end = (sub * ch + lax.broadcasted_iota(jnp.int32, (ch, 1), 0)) * CMP_STRIDE + (CMP_BLOCK - 1)
                biases.append(jnp.where(cmp_end <= tq1, 0.0, MASKED).astype(BF16))
            else:
                biases.append(jnp.zeros((ch, Q_BLOCK), BF16))
        k_aug = jnp.concatenate([jnp.concatenate(biases, axis=0), kcc_ref[0:n_sub * ch, :]], axis=1)
        scores = _dot(k_aug, q_aug_t)
        parts = []
        for sub in range(n_sub):
            s = scores[sub * ch:(sub + 1) * ch]
            m_sub = jnp.max(s, axis=0, keepdims=True)
            e = jnp.exp2(s - m_sub)
            parts.append((m_sub, jnp.sum(e, axis=0, keepdims=True),
                          _dot(vcct_ref[:, sub * ch:(sub + 1) * ch], e.astype(BF16)), e))
        m = parts[0][0]
        for m_sub, _, _, _ in parts[1:]:
            m = jnp.maximum(m, m_sub)
        scales = [jnp.exp2(m_sub - m) for m_sub, _, _, _ in parts]
        l = scales[0] * parts[0][1]
        acc = scales[0] * parts[0][2]
        for scale, (_, l_sub, acc_sub, _) in zip(scales[1:], parts[1:]):
            l = l + scale * l_sub
            acc = acc + scale * acc_sub
        inv_l = jnp.where(heads(tq1 >= CMP_BLOCK - 1), 1.0 / l, 0.0)
        ocmp_scr[...] = acc * inv_l
        for sub in range(n_sub):
            pc = parts[sub][3] * (scales[sub] * inv_l)
            p_sum = pc[:, :Q_BLOCK]
            for hp in range(1, hpg):
                p_sum = p_sum + pc[:, hp * Q_BLOCK:(hp + 1) * Q_BLOCK]
            imp_scr[IMP_PAD + sub * ch:IMP_PAD + (sub + 1) * ch, :] = p_sum

    for variant in range(n_cmp // ch):
        pl.when(n_chunks == variant + 1)(functools.partial(compressed, variant + 1))
    o_cmp = ocmp_scr[...]

    ratio = SEL_BLOCK // CMP_STRIDE
    sel_shift = SEL_BLOCK.bit_length() - 1
    cur = tq1 >> sel_shift
    n_forced = 3

    span = WINDOW + Q_BLOCK
    k_start = pl.multiple_of(jnp.maximum(t0 - WINDOW, 0), Q_BLOCK)

    def window_scores():
        diff = tq1 - (k_start + lax.broadcasted_iota(jnp.int32, (span, 1), 0))
        inside = pltpu.bitcast(diff, jnp.uint32) < jnp.uint32(WINDOW)
        bias = jnp.where(inside, 0.0, MASKED).astype(BF16)
        wins_scr[...] = _dot(jnp.concatenate([bias, kw_ref[pl.ds(k_start, span), :]], axis=1), q_aug_t)

    def window_values():
        sw = wins_scr[...]
        pw = jnp.exp2(sw - jnp.max(sw, axis=0, keepdims=True)).astype(BF16)
        acc_win = _dot(vwt_ref[:, pl.ds(k_start, span)], pw)
        owin_scr[...] = acc_win[:NSA_HEAD_DIM] / acc_win[NSA_HEAD_DIM:NSA_HEAD_DIM + 1]

    def select_blocks(rows):
        window_scores()
        imp = imp_scr[pl.ds(IMP_PAD - 1, rows, stride=ratio), :]
        for k in range(1, ratio + 1):
            imp = imp + imp_scr[pl.ds(IMP_PAD - 1 + k, rows, stride=ratio), :]
        j = lax.broadcasted_iota(jnp.int32, (rows, Q_BLOCK), 0)
        j_f = j.astype(F32)
        forced = (j == 0) | (j == cur) | (j == cur - 1)
        valid = j <= cur

        def pick(_, sc):
            best = jnp.max(sc, axis=0, keepdims=True)
            idx = jnp.min(jnp.where(sc == best, j_f, float(rows)), axis=0, keepdims=True)
            return jnp.where(j_f == idx, neg_inf, sc)

        left = lax.fori_loop(0, max(min(SEL_TOPK, n_sel) - n_forced, 0), pick,
                             jnp.where(valid & ~forced, imp, neg_inf), unroll=True)
        sel_scr[0:rows, :] = jnp.where(forced | (valid & (left == neg_inf)), 0.0, MASKED)
        if rows < n_sel:
            sel_scr[rows:n_sel, :] = jnp.full((n_sel - rows, Q_BLOCK), MASKED, F32)

    row_step = min(TOPK_ROW_STEP, n_sel)
    last_block = (t0 + Q_BLOCK - 1) >> sel_shift
    for variant in range(n_sel // row_step):
        pl.when(last_block // row_step == variant)(functools.partial(select_blocks, (variant + 1) * row_step))

    blocks_per_tile = kt_size // SEL_BLOCK

    def slc_tile(kt, carry, n_sub, causal_from, after_scores=None):
        m_run, acc = carry
        base = pl.multiple_of(kt * kt_size, kt_size)
        biases = []
        for sub in range(n_sub):
            blk0 = kt * blocks_per_tile + sub * (sub_size // SEL_BLOCK)
            bias = jnp.concatenate(
                [jnp.broadcast_to(sel_scr[pl.ds(blk0 + jb, 1), :], (SEL_BLOCK, Q_BLOCK))
                 for jb in range(sub_size // SEL_BLOCK)], axis=0)
            if sub >= causal_from:
                kpos = base + sub * sub_size + lax.broadcasted_iota(jnp.int32, (sub_size, 1), 0)
                bias = jnp.where(kpos <= tq1, bias, MASKED)
            biases.append(bias.astype(BF16))
        k_aug = jnp.concatenate([jnp.concatenate(biases, axis=0), ks_ref[pl.ds(base, n_sub * sub_size), :]], axis=1)
        scores = _dot(k_aug, q_aug_t)
        if after_scores is not None:
            after_scores()
        parts = []
        for sub in range(n_sub):
            sc = scores[sub * sub_size:(sub + 1) * sub_size]
            m_sub = jnp.max(sc, axis=0, keepdims=True)
            pe = jnp.exp2(sc - m_sub).astype(BF16)
            k0 = pl.multiple_of(base + sub * sub_size, sub_size)
            parts.append((m_sub, _dot(vst_ref[:, pl.ds(k0, sub_size)], pe)))
        m_new = m_run
        for m_sub, _ in parts:
            m_new = jnp.maximum(m_new, m_sub)
        acc = jnp.exp2(m_run - m_new) * acc
        for m_sub, acc_sub in parts:
            acc = acc + jnp.exp2(m_sub - m_new) * acc_sub
        return m_new, acc

    subs_per_tile = kt_size // sub_size
    n_full = t0 // kt_size
    carry = lax.fori_loop(
        0, n_full, functools.partial(slc_tile, n_sub=subs_per_tile, causal_from=subs_per_tile),
        (jnp.full((1, width), MASKED, F32), jnp.zeros((V_ROWS, width), F32)))
    tail_step = min(2, subs_per_tile)
    tail_len = (t0 - n_full * kt_size) // (tail_step * sub_size)
    for variant in range(subs_per_tile // tail_step):
        n_sub = tail_step * (variant + 1)

        @pl.when(tail_len == variant)
        def _(n_sub=n_sub):
            _, acc_slc = slc_tile(n_full, carry, n_sub=n_sub, causal_from=n_sub - tail_step,
                                  after_scores=window_values)
            oslc_scr[...] = acc_slc[:NSA_HEAD_DIM] / acc_slc[NSA_HEAD_DIM:NSA_HEAD_DIM + 1]

    o_slc = oslc_scr[...]
    o_win = owin_scr[...]

    for hp in range(hpg):
        cols = slice(hp * Q_BLOCK, (hp + 1) * Q_BLOCK)
        row = (grp * hpg + hp) * 3
        o_t = (gt_ref[qi, pl.ds(row, 1), :] * o_cmp[:, cols] + gt_ref[qi, pl.ds(row + 1, 1), :] * o_slc[:, cols]
               + gt_ref[qi, pl.ds(row + 2, 1), :] * o_win[:, cols])
        o_ref[pl.ds(pl.multiple_of(qi * Q_BLOCK, Q_BLOCK), Q_BLOCK), hp * NSA_HEAD_DIM:(hp + 1) * NSA_HEAD_DIM] = (
            o_t.T.astype(o_ref.dtype))


def _nsa_attention(qs, kcc, vcct, ks, vst, kw, vwt, gt, B, S, blocks=4, kt_size=8192, sub_size=256, cmp_chunk=256):
    G, Dh, nq, hpg = NSA_KV_GROUPS, NSA_HEAD_DIM, S // Q_BLOCK, NSA_HEADS_PER_GROUP
    kt_size = min(kt_size, S)
    blocks = min(blocks, nq)
    n_cmp, n_sel = S // CMP_STRIDE, S // SEL_BLOCK
    cmp_chunk = min(cmp_chunk, n_cmp)
    per_group = lambda shape: pl.BlockSpec((None, None) + shape, lambda b, g, i: (b, g, 0, 0))
    return pl.pallas_call(
        functools.partial(_nsa_body, blocks=blocks, kt_size=kt_size, sub_size=min(sub_size, kt_size),
                          cmp_chunk=cmp_chunk),
        grid=(B, G, nq // blocks),
        in_specs=[
            pl.BlockSpec((None, None, Dh, blocks * hpg * Q_BLOCK), lambda b, g, i: (b, g, 0, i)),
            per_group((n_cmp, Dh)), per_group((Dh, n_cmp)),
            per_group((S, Dh)), per_group((V_ROWS, S)),
            per_group((S, Dh)), per_group((V_ROWS, S)),
            pl.BlockSpec((None, blocks, GATE_ROWS, Q_BLOCK), lambda b, g, i: (b, i, 0, 0)),
        ],
        out_specs=pl.BlockSpec((None, blocks * Q_BLOCK, hpg * Dh), lambda b, g, i: (b, i, g)),
        out_shape=jax.ShapeDtypeStruct((B, S, NSA_HEADS * Dh), BF16),
        scratch_shapes=[
            pltpu.VMEM((n_cmp + 2 * IMP_PAD, Q_BLOCK), F32),
            pltpu.VMEM((n_sel, Q_BLOCK), F32),
            pltpu.VMEM((Dh, hpg * Q_BLOCK), F32),
            pltpu.VMEM((WINDOW + Q_BLOCK, hpg * Q_BLOCK), F32),
            pltpu.VMEM((Dh, hpg * Q_BLOCK), F32),
            pltpu.VMEM((Dh, hpg * Q_BLOCK), F32),
        ],
        compiler_params=pltpu.CompilerParams(
            dimension_semantics=("parallel", "parallel", "arbitrary"), vmem_limit_bytes=VMEM_LIMIT),
        name="nsa_attention",
    )(qs, kcc, vcct, ks, vst, kw, vwt, gt)


def _merge_body(ya_ref, yb_ref, yc_ref, ga_ref, gb_ref, gc_ref, h_ref, wb_ref, wo_ref, o_ref):
    merged = jax.nn.sigmoid(ga_ref[...].astype(F32)) * _dot(ya_ref[...], wb_ref[0])
    merged = merged + jax.nn.sigmoid(gb_ref[...].astype(F32)) * _dot(yb_ref[...], wb_ref[1])
    merged = merged + jax.nn.sigmoid(gc_ref[...].astype(F32)) * _dot(yc_ref[...], wb_ref[2])
    o_ref[...] = h_ref[...] + _dot(merged.astype(BF16), wo_ref[...])


def _merge(ya, yb, yc, main, h, w_branch, w_out, tm=1024):
    T, D = h.shape
    y_spec = pl.BlockSpec((tm, 512), lambda i: (i, 0))
    gate_spec = lambda k: pl.BlockSpec((tm, D), lambda i: (i, MAIN_GM // D + k))
    return pl.pallas_call(
        _merge_body,
        grid=(T // tm,),
        in_specs=[
            y_spec, y_spec, y_spec, gate_spec(0), gate_spec(1), gate_spec(2),
            pl.BlockSpec((tm, D), lambda i: (i, 0)),
            pl.BlockSpec((3, 512, D), lambda i: (0, 0, 0)),
            pl.BlockSpec((D, D), lambda i: (0, 0)),
        ],
        out_specs=pl.BlockSpec((tm, D), lambda i: (i, 0)),
        out_shape=jax.ShapeDtypeStruct((T, D), F32),
        compiler_params=pltpu.CompilerParams(dimension_semantics=("parallel",), vmem_limit_bytes=VMEM_LIMIT),
        name="merge_out",
    )(ya, yb, yc, main, main, main, h, w_branch, w_out)


def _ffn_body(h_ref, gf_ref, w1_ref, w2_ref, gp_ref, wg_ref, p_ref, wp_ref, gl_ref, o_ref, f_scr, acc_scr, *, final_norm):
    j = pl.program_id(1)

    @pl.when(j == 0)
    def _():
        f_scr[...] = _rms(h_ref[...], gf_ref[...]).astype(BF16)
        acc_scr[...] = jnp.zeros_like(acc_scr)

    a = jnp.maximum(_dot(f_scr[...], w1_ref[...]), 0.0)
    acc_scr[...] += _dot((a * a).astype(BF16), w2_ref[...])

    @pl.when(j == pl.num_programs(1) - 1)
    def _():
        h2 = h_ref[...] + acc_scr[...]
        gate = jax.nn.sigmoid(_dot(_rms(h2, gp_ref[...]).astype(BF16), wg_ref[...]))
        h3 = h2 + gate * _dot(p_ref[...].astype(BF16), wp_ref[...])
        o_ref[...] = _rms(h3, gl_ref[...]) if final_norm else h3


def _ffn_ple(h, norm_ffn, w1, w2, norm_ple, w_gate, p, w_proj, norm_last, final_norm, tm=1024, tf=1024):
    T, D = h.shape
    tm = min(tm, T)
    Fdim = w1.shape[1]
    Pdim = p.shape[1]
    vec = pl.BlockSpec((1, D), lambda i, j: (0, 0))
    return pl.pallas_call(
        functools.partial(_ffn_body, final_norm=final_norm),
        grid=(T // tm, Fdim // tf),
        in_specs=[
            pl.BlockSpec((tm, D), lambda i, j: (i, 0)), vec,
            pl.BlockSpec((D, tf), lambda i, j: (0, j)),
            pl.BlockSpec((tf, D), lambda i, j: (j, 0)),
            vec,
            pl.BlockSpec((D, D), lambda i, j: (0, 0)),
            pl.BlockSpec((tm, Pdim), lambda i, j: (i, 0)),
            pl.BlockSpec((Pdim, D), lambda i, j: (0, 0)),
            vec,
        ],
        out_specs=pl.BlockSpec((tm, D), lambda i, j: (i, 0)),
        out_shape=jax.ShapeDtypeStruct((T, D), F32),
        scratch_shapes=[pltpu.VMEM((tm, D), BF16), pltpu.VMEM((tm, D), F32)],
        compiler_params=pltpu.CompilerParams(
            dimension_semantics=("parallel", "arbitrary"), vmem_limit_bytes=VMEM_LIMIT),
        name="ffn_ple",
    )(h, norm_ffn, w1, w2, norm_ple, w_gate, p, w_proj, norm_last)


def _split_w_in(w):
    parts, off = [], 0
    for sz in IN_SIZES:
        parts.append(w[:, off:off + sz])
        off += sz
    u, q_nsa, kv_nsa, g_nsa, q_gla, k_gla, v_gla, a_gla, r_gla, g_merge = parts
    D = w.shape[0]
    pad = lambda n: jnp.zeros((D, n), w.dtype)
    main = jnp.concatenate([u, q_gla, k_gla, v_gla, r_gla, g_merge, a_gla, pad(MAIN_WIDTH - MAIN_A - GLA_GATE_RANK)], axis=1)
    kv = [kv_nsa[:, s * LANES:(s + 1) * LANES] for s in range(6)]
    nsa = jnp.concatenate([q_nsa, kv[0], kv[2], kv[4], kv[1], kv[3], kv[5], g_nsa, pad(NSA_WIDTH - NSA_G - 24)], axis=1)
    return main.astype(BF16), nsa.astype(BF16)


def kernel(x, p, positions, norm_mix, w_in, pool_w, pool_scale, cmp_pos_k, cmp_w_k, cmp_pos_v, cmp_w_v, gla_w_gate, gla_b_gate, gla_norm, w_branch, w_out, norm_ffn, w_ff1, w_ff2, norm_ple, w_ple_gate, w_ple_proj, norm_final):
    B, S, D = x.shape
    depth = w_in.shape[0]
    T = B * S
    Dh = NSA_HEAD_DIM
    row = lambda v: v.reshape(1, -1).astype(F32)

    half = NSA_ROT_DIM // 2
    inv_freq = jnp.power(ROPE_THETA, -jnp.arange(half, dtype=F32) * (2.0 / NSA_ROT_DIM)).reshape(half, 1)
    pos_rows = positions.reshape(B, S // Q_BLOCK, 1, Q_BLOCK)
    rope_spread = _rope_spread_matrix()

    h = x.reshape(T, D)
    for i in range(depth):
        w_main, w_nsa = _split_w_in(w_in[i])
        main = _norm_proj(h, row(norm_mix[i]), w_main, BF16, tn=MAIN_WIDTH // 3).reshape(B, S, MAIN_WIDTH)
        nsa_proj = _norm_proj(h, row(norm_mix[i]), w_nsa, F32, tn=NSA_WIDTH).reshape(B, S, NSA_WIDTH)

        y_a = _pool_mixer(main, pool_w[i].astype(BF16), row(pool_scale[i]), B, S)

        wg = jnp.zeros((LANES, GLA_HEADS * GLA_DK), BF16).at[:GLA_GATE_RANK].set(gla_w_gate[i].astype(BF16))
        y_c = _gla_mixer(main, wg, row(gla_b_gate[i]), row(gla_norm[i]), B, S)

        qs, kc, vc, ks, kw, vst, vwt, gt = _nsa_prep(nsa_proj, pos_rows, inv_freq, rope_spread, B, S)
        kcc = _compress(kc, cmp_w_k[i].reshape(CMP_BLOCK, Dh, Dh).astype(BF16),
                        cmp_pos_k[i].reshape(CMP_BLOCK, 1, Dh).astype(F32), transpose_out=False)
        vcct = _compress(vc, cmp_w_v[i].reshape(CMP_BLOCK, Dh, Dh).astype(BF16),
                         cmp_pos_v[i].reshape(CMP_BLOCK, 1, Dh).astype(F32), transpose_out=True)
        y_b = _nsa_attention(qs, kcc, vcct, ks, vst, kw, vwt, gt, B, S)

        h = _merge(y_a.reshape(T, -1), y_b.reshape(T, -1), y_c.reshape(T, -1), main.reshape(T, MAIN_WIDTH), h,
                   w_branch[i].astype(BF16), w_out[i].astype(BF16))
        h = _ffn_ple(h, row(norm_ffn[i]), w_ff1[i].astype(BF16), w_ff2[i].astype(BF16), row(norm_ple[i]),
                     w_ple_gate[i].astype(BF16), p[i].reshape(T, -1), w_ple_proj[i].astype(BF16),
                     row(norm_final), final_norm=(i == depth - 1))
    return h.reshape(B, S, D)
```

```python
import functools

import jax
import jax.numpy as jnp
import numpy as np
from jax import lax
from jax.experimental import pallas as pl
from jax.experimental.pallas import tpu as pltpu

F32, BF16 = jnp.float32, jnp.bfloat16

EPS = 1e-6
ROPE_THETA = 500000.0
POOL_WINDOWS = (2, 4, 8, 16)
POOL_GROUP_DIM = 128
POOL_HALO = max(POOL_WINDOWS)
NSA_HEADS = 8
NSA_KV_GROUPS = 2
NSA_HEADS_PER_GROUP = NSA_HEADS // NSA_KV_GROUPS
NSA_HEAD_DIM = 64
NSA_ROT_DIM = 16
CMP_BLOCK = 32
CMP_STRIDE = 16
SEL_BLOCK = 64
SEL_TOPK = 16
WINDOW = 512
Q_BLOCK = 128
GLA_HEADS = 4
GLA_DK = 64
GLA_DV = 128
GLA_GATE_RANK = 16
GLA_TAU = 16.0
GLA_CHUNK = 64
IN_SIZES = (512, 512, 768, 24, 256, 256, 512, 16, 512, 3072)

LANES = 128
MASKED = -1e30
LOG2E = 1.4426950408889634
Q_SCALE = NSA_HEAD_DIM ** -0.5 * LOG2E
V_ROWS = NSA_HEAD_DIM + 16
IMP_PAD = 8
GATE_ROWS = 32
TOPK_ROW_STEP = 64
VMEM_LIMIT = 56 * 1024 * 1024

MAIN_U, MAIN_QG, MAIN_KG, MAIN_VG, MAIN_RG, MAIN_GM, MAIN_A, MAIN_WIDTH = 0, 512, 768, 1024, 1536, 2048, 5120, 5376
NSA_Q, NSA_KC, NSA_KS, NSA_KW, NSA_VC, NSA_VS, NSA_VW, NSA_G, NSA_WIDTH = 0, 512, 640, 768, 896, 1024, 1152, 1280, 1536


def _nt(a, b):
    return lax.dot_general(a, b, (((1,), (1,)), ((), ())), preferred_element_type=F32)


def _tn(a, b):
    return lax.dot_general(a, b, (((0,), (0,)), ((), ())), preferred_element_type=F32)


def _dot(a, b):
    return jnp.dot(a, b, preferred_element_type=F32)


def _rms(x, gain):
    return x * lax.rsqrt(jnp.mean(x * x, axis=-1, keepdims=True) + EPS) * gain


def _proj_body(h_ref, g_ref, w_ref, o_ref, a_scr):
    @pl.when(pl.program_id(1) == 0)
    def _():
        a_scr[...] = _rms(h_ref[...], g_ref[...]).astype(BF16)

    o_ref[...] = _dot(a_scr[...], w_ref[...]).astype(o_ref.dtype)


def _norm_proj(h, gain, w, out_dtype, tn, tm=1024):
    T, D = h.shape
    N = w.shape[1]
    tm = min(tm, T)
    return pl.pallas_call(
        _proj_body,
        grid=(T // tm, N // tn),
        in_specs=[
            pl.BlockSpec((tm, D), lambda i, j: (i, 0)),
            pl.BlockSpec((1, D), lambda i, j: (0, 0)),
            pl.BlockSpec((D, tn), lambda i, j: (0, j)),
        ],
        out_specs=pl.BlockSpec((tm, tn), lambda i, j: (i, j)),
        out_shape=jax.ShapeDtypeStruct((T, N), out_dtype),
        scratch_shapes=[pltpu.VMEM((tm, D), BF16)],
        compiler_params=pltpu.CompilerParams(
            dimension_semantics=("parallel", "arbitrary"), vmem_limit_bytes=VMEM_LIMIT),
        name="norm_proj",
    )(h, gain, w)


def _pool_body(u_ref, halo_ref, w_ref, sc_ref, o_ref, *, ts):
    i = pl.program_id(1)
    cur = u_ref[...].astype(F32)
    halo = jnp.where(i == 0, 0.0, halo_ref[...].astype(F32))
    t = i * ts + lax.broadcasted_iota(jnp.int32, (ts, 1), 0)
    for g, win in enumerate(POOL_WINDOWS):
        lo, hi = g * POOL_GROUP_DIM, (g + 1) * POOL_GROUP_DIM
        x = cur[:, lo:hi]
        e = jnp.concatenate([halo[:, lo:hi], x], axis=0)
        step = 1
        while step < win:
            e = e[step:] + e[:-step]
            step *= 2
        wsum = e[POOL_HALO - (win - 1):]
        cnt = jnp.minimum(t + 1, win).astype(F32)
        pooled = wsum / cnt - x
        y = _dot(pooled.astype(BF16), w_ref[g]) * sc_ref[:, lo:hi]
        o_ref[:, lo:hi] = y.astype(o_ref.dtype)


def _pool_mixer(main, pool_w, scale, B, S, ts=512):
    ts = min(ts, S)
    halo_blocks = ts // POOL_HALO
    return pl.pallas_call(
        functools.partial(_pool_body, ts=ts),
        grid=(B, S // ts),
        in_specs=[
            pl.BlockSpec((None, ts, 512), lambda b, i: (b, i, MAIN_U // 512)),
            pl.BlockSpec((None, POOL_HALO, 512), lambda b, i: (b, jnp.maximum(i * halo_blocks - 1, 0), MAIN_U // 512)),
            pl.BlockSpec((4, 128, 128), lambda b, i: (0, 0, 0)),
            pl.BlockSpec((1, 512), lambda b, i: (0, 0)),
        ],
        out_specs=pl.BlockSpec((None, ts, 512), lambda b, i: (b, i, 0)),
        out_shape=jax.ShapeDtypeStruct((B, S, 512), BF16),
        compiler_params=pltpu.CompilerParams(dimension_semantics=("parallel", "arbitrary")),
        name="pool_mixer",
    )(main, main, pool_w, scale)


def _gla_body(q_ref, k_ref, v_ref, r_ref, a_ref, wg_ref, bg_ref, ng_ref, o_ref, st_ref, *, tc):
    @pl.when(pl.program_id(1) == 0)
    def _():
        st_ref[...] = jnp.zeros_like(st_ref)

    C = GLA_CHUNK
    n_chunks = tc // C
    causal = lax.broadcasted_iota(jnp.int32, (C, C), 0) >= lax.broadcasted_iota(jnp.int32, (C, C), 1)
    width = GLA_HEADS * GLA_DK
    hk = [slice(h * GLA_DK, (h + 1) * GLA_DK) for h in range(GLA_HEADS)]
    hv = [slice(h * GLA_DV, (h + 1) * GLA_DV) for h in range(GLA_HEADS)]
    rows = [slice(c * C, (c + 1) * C) for c in range(n_chunks)]

    z = _dot(a_ref[...], wg_ref[...]) + bg_ref[...]
    b = jax.nn.log_sigmoid(z) * (1.0 / GLA_TAU)
    row_in_chunk = lax.broadcasted_iota(jnp.int32, (tc, 1), 0) % C
    step = 1
    while step < C:
        shifted = jnp.concatenate([jnp.zeros((step, width), F32), b[:-step]], axis=0)
        b = b + jnp.where(row_in_chunk >= step, shifted, 0.0)
        step *= 2
    b_last = [b[(c + 1) * C - 1:(c + 1) * C, :] for c in range(n_chunks)]
    b_last_rows = jnp.concatenate([jnp.broadcast_to(bl, (C, width)) for bl in b_last], axis=0)
    qf = q_ref[...].astype(F32) * (GLA_DK ** -0.5)
    kf = k_ref[...].astype(F32)
    q_s = (qf * jnp.exp(b)).astype(BF16)
    k_s = (kf * jnp.exp(-b)).astype(BF16)
    k_t = (kf * jnp.exp(b_last_rows - b)).astype(BF16)

    att = [[jnp.where(causal, _nt(q_s[rows[c], hk[h]], k_s[rows[c], hk[h]]), 0.0).astype(BF16)
            for h in range(GLA_HEADS)] for c in range(n_chunks)]
    o_intra = [[_dot(att[c][h], v_ref[rows[c], hv[h]]) for h in range(GLA_HEADS)] for c in range(n_chunks)]
    kv = [[_tn(v_ref[rows[c], hv[h]], k_t[rows[c], hk[h]]) for h in range(GLA_HEADS)] for c in range(n_chunks)]

    state_t = [st_ref[h] for h in range(GLA_HEADS)]
    for c in range(n_chunks):
        decay = jnp.exp(b_last[c])
        for h in range(GLA_HEADS):
            o = o_intra[c][h] + _nt(q_s[rows[c], hk[h]], state_t[h].astype(BF16))
            state_t[h] = state_t[h] * decay[:, hk[h]] + kv[c][h]
            o = _rms(o, ng_ref[...])
            r = r_ref[rows[c], hv[h]].astype(F32)
            o_ref[rows[c], hv[h]] = (o * (r * jax.nn.sigmoid(r))).astype(o_ref.dtype)
    for h in range(GLA_HEADS):
        st_ref[h] = state_t[h]


def _gla_mixer(main, w_gate, b_gate, norm_g, B, S, tc=512):
    tc = min(tc, S)
    return pl.pallas_call(
        functools.partial(_gla_body, tc=tc),
        grid=(B, S // tc),
        in_specs=[
            pl.BlockSpec((None, tc, 256), lambda b, i: (b, i, MAIN_QG // 256)),
            pl.BlockSpec((None, tc, 256), lambda b, i: (b, i, MAIN_KG // 256)),
            pl.BlockSpec((None, tc, 512), lambda b, i: (b, i, MAIN_VG // 512)),
            pl.BlockSpec((None, tc, 512), lambda b, i: (b, i, MAIN_RG // 512)),
            pl.BlockSpec((None, tc, 128), lambda b, i: (b, i, MAIN_A // 128)),
            pl.BlockSpec((128, 256), lambda b, i: (0, 0)),
            pl.BlockSpec((1, 256), lambda b, i: (0, 0)),
            pl.BlockSpec((1, 128), lambda b, i: (0, 0)),
        ],
        out_specs=pl.BlockSpec((None, tc, 512), lambda b, i: (b, i, 0)),
        out_shape=jax.ShapeDtypeStruct((B, S, 512), BF16),
        scratch_shapes=[pltpu.VMEM((GLA_HEADS, GLA_DV, GLA_DK), F32)],
        compiler_params=pltpu.CompilerParams(dimension_semantics=("parallel", "arbitrary")),
        name="gla_mixer",
    )(main, main, main, main, main, w_gate, b_gate, norm_g)


def _nsa_prep_body(x_ref, pos_ref, invf_ref, spread_ref, qs_ref, kc_ref, vc_ref, ks_ref, kw_ref, vst_ref, vwt_ref,
                   gt_ref, *, blocks):
    half = NSA_ROT_DIM // 2
    hpg = NSA_HEADS_PER_GROUP
    ones_row = (lax.broadcasted_iota(jnp.int32, (V_ROWS - NSA_HEAD_DIM, Q_BLOCK), 0) == 0).astype(BF16)

    def pieces(a):
        p1 = a.astype(BF16).astype(F32)
        p2 = (a - p1).astype(BF16).astype(F32)
        return [p1, p2, (a - p1 - p2).astype(BF16).astype(F32)]

    for blk in range(blocks):
        tok = slice(blk * Q_BLOCK, (blk + 1) * Q_BLOCK)
        ang = invf_ref[...] * pos_ref[blk].astype(F32)
        lhs = jnp.concatenate(
            pieces(jnp.cos(ang)) + pieces(jnp.sin(ang)) + [jnp.ones((2 * half, Q_BLOCK), F32)], axis=0)
        tables = _tn(lhs.astype(BF16), spread_ref[...])
        cs = tables[:, :LANES]
        s_lo = tables[:, LANES:2 * LANES]
        s_hi = tables[:, 2 * LANES:]

        def rope(x):
            return x * cs + pltpu.roll(x, LANES - half, 1) * s_lo + pltpu.roll(x, half, 1) * s_hi

        def split_groups(x, ref, dtype):
            ref[0, tok, :] = x[:, :NSA_HEAD_DIM].astype(dtype)
            ref[1, tok, :] = x[:, NSA_HEAD_DIM:].astype(dtype)

        for pair in range(NSA_HEADS // 2):
            cols = slice(NSA_Q + pair * LANES, NSA_Q + (pair + 1) * LANES)
            qr_t = (rope(x_ref[tok, cols]) * Q_SCALE).T
            for sub in range(2):
                head = 2 * pair + sub
                grp, hp = head // hpg, head % hpg
                col0 = (blk * hpg + hp) * Q_BLOCK
                qs_ref[grp, :, col0:col0 + Q_BLOCK] = qr_t[sub * NSA_HEAD_DIM:(sub + 1) * NSA_HEAD_DIM].astype(BF16)
        split_groups(rope(x_ref[tok, NSA_KC:NSA_KC + LANES]), kc_ref, F32)
        split_groups(rope(x_ref[tok, NSA_KS:NSA_KS + LANES]), ks_ref, BF16)
        split_groups(rope(x_ref[tok, NSA_KW:NSA_KW + LANES]), kw_ref, BF16)
        split_groups(x_ref[tok, NSA_VC:NSA_VC + LANES], vc_ref, F32)
        vst = x_ref[tok, NSA_VS:NSA_VS + LANES].T
        vwt = x_ref[tok, NSA_VW:NSA_VW + LANES].T
        for grp in range(NSA_KV_GROUPS):
            for ref, v in ((vst_ref, vst), (vwt_ref, vwt)):
                ref[grp, :NSA_HEAD_DIM, tok] = v[grp * NSA_HEAD_DIM:(grp + 1) * NSA_HEAD_DIM].astype(BF16)
                ref[grp, NSA_HEAD_DIM:, tok] = ones_row
        gt_ref[blk] = jax.nn.sigmoid(x_ref[tok, NSA_G:NSA_G + LANES]).T[:GATE_ROWS]


def _rope_spread_matrix():
    half = NSA_ROT_DIM // 2
    lane = np.arange(LANES)
    d = lane % NSA_HEAD_DIM
    m = np.zeros((8 * half, 3 * LANES), np.float32)
    for piece in range(3):
        for f in range(half):
            hit = d % half == f
            m[piece * half + f, :LANES] = hit & (d < NSA_ROT_DIM)
            m[(3 + piece) * half + f, LANES:2 * LANES] = -1.0 * (hit & (d < half))
            m[(3 + piece) * half + f, 2 * LANES:] = hit & (d >= half) & (d < NSA_ROT_DIM)
    m[6 * half, :LANES] = d >= NSA_ROT_DIM
    return jnp.asarray(m, BF16)


def _nsa_prep(nsa_proj, positions, inv_freq, rope_spread, B, S, blocks=4):
    G, Dh, nq = NSA_KV_GROUPS, NSA_HEAD_DIM, S // Q_BLOCK
    blocks = min(blocks, nq)
    ts = blocks * Q_BLOCK
    tok = lambda dtype: jax.ShapeDtypeStruct((B, G, S, Dh), dtype)
    tok_spec = pl.BlockSpec((None, G, ts, Dh), lambda b, i: (b, 0, i, 0))
    tr = jax.ShapeDtypeStruct((B, G, V_ROWS, S), BF16)
    tr_spec = pl.BlockSpec((None, G, V_ROWS, ts), lambda b, i: (b, 0, 0, i))
    return pl.pallas_call(
        functools.partial(_nsa_prep_body, blocks=blocks),
        grid=(B, nq // blocks),
        in_specs=[
            pl.BlockSpec((None, ts, NSA_WIDTH), lambda b, i: (b, i, 0)),
            pl.BlockSpec((None, blocks, 1, Q_BLOCK), lambda b, i: (b, i, 0, 0)),
            pl.BlockSpec(inv_freq.shape, lambda b, i: (0, 0)),
            pl.BlockSpec(rope_spread.shape, lambda b, i: (0, 0)),
        ],
        out_specs=[
            pl.BlockSpec((None, G, Dh, NSA_HEADS_PER_GROUP * ts), lambda b, i: (b, 0, 0, i)),
            tok_spec, tok_spec, tok_spec, tok_spec, tr_spec, tr_spec,
            pl.BlockSpec((None, blocks, GATE_ROWS, Q_BLOCK), lambda b, i: (b, i, 0, 0)),
        ],
        out_shape=[
            jax.ShapeDtypeStruct((B, G, Dh, NSA_HEADS_PER_GROUP * S), BF16),
            tok(F32), tok(F32), tok(BF16), tok(BF16), tr, tr,
            jax.ShapeDtypeStruct((B, nq, GATE_ROWS, Q_BLOCK), F32),
        ],
        compiler_params=pltpu.CompilerParams(dimension_semantics=("parallel", "parallel")),
        name="nsa_prep",
    )(nsa_proj, positions, inv_freq, rope_spread)


def _cmp_body(x_ref, w_ref, p_ref, o_ref, *, n_blk, transpose_out):
    half = CMP_BLOCK // 2
    first = jnp.zeros((n_blk, NSA_HEAD_DIM), F32)
    second = jnp.zeros((n_blk, NSA_HEAD_DIM), F32)
    for l in range(half):
        rows = x_ref[pl.ds(l, n_blk, stride=CMP_STRIDE), :]
        first = first + _dot((rows + p_ref[l]).astype(BF16), w_ref[l])
        second = second + _dot((rows + p_ref[half + l]).astype(BF16), w_ref[half + l])
    out = first + jnp.concatenate([second[1:], jnp.zeros((1, NSA_HEAD_DIM), F32)], axis=0)
    o_ref[...] = (out.T if transpose_out else out).astype(o_ref.dtype)


def _compress(x, w, pos, transpose_out):
    B, G, S, Dh = x.shape
    n_blk = S // CMP_STRIDE
    out_block = (None, None, Dh, n_blk) if transpose_out else (None, None, n_blk, Dh)
    out_shape = (B, G, Dh, n_blk) if transpose_out else (B, G, n_blk, Dh)
    return pl.pallas_call(
        functools.partial(_cmp_body, n_blk=n_blk, transpose_out=transpose_out),
        grid=(B, G),
        in_specs=[
            pl.BlockSpec((None, None, S, Dh), lambda b, g: (b, g, 0, 0)),
            pl.BlockSpec((CMP_BLOCK, Dh, Dh), lambda b, g: (0, 0, 0)),
            pl.BlockSpec((CMP_BLOCK, 1, Dh), lambda b, g: (0, 0, 0)),
        ],
        out_specs=pl.BlockSpec(out_block, lambda b, g: (b, g, 0, 0)),
        out_shape=jax.ShapeDtypeStruct(out_shape, BF16),
        compiler_params=pltpu.CompilerParams(
            dimension_semantics=("parallel", "parallel"), vmem_limit_bytes=VMEM_LIMIT),
        name="nsa_compress",
    )(x, w, pos)


def _nsa_body(*refs, blocks, **tiles):
    def one_block(qi, carry):
        _nsa_block(qi, *refs, blocks=blocks, **tiles)
        return carry

    lax.fori_loop(0, blocks, one_block, 0)


def _nsa_block(qi, q_ref, kcc_ref, vcct_ref, ks_ref, vst_ref, kw_ref, vwt_ref, gt_ref, o_ref,
               imp_scr, sel_scr, ocmp_scr, wins_scr, owin_scr, oslc_scr, *, blocks, kt_size, sub_size, cmp_chunk):
    grp = pl.program_id(1)
    t0 = (pl.program_id(2) * blocks + qi) * Q_BLOCK
    hpg = NSA_HEADS_PER_GROUP
    n_cmp = kcc_ref.shape[0]
    n_sel = sel_scr.shape[0]
    q_t = q_ref[:, pl.ds(pl.multiple_of(qi * (hpg * Q_BLOCK), hpg * Q_BLOCK), hpg * Q_BLOCK)]
    tq1 = t0 + lax.broadcasted_iota(jnp.int32, (1, Q_BLOCK), 1)
    heads = lambda x: jnp.concatenate([x] * hpg, axis=1)

    width = hpg * Q_BLOCK
    neg_inf = -jnp.inf
    eye = (lax.broadcasted_iota(jnp.int32, (Q_BLOCK, Q_BLOCK), 0)
           == lax.broadcasted_iota(jnp.int32, (Q_BLOCK, Q_BLOCK), 1)).astype(BF16)
    q_aug_t = jnp.concatenate([jnp.concatenate([eye] * hpg, axis=1), q_t], axis=0)

    ch = cmp_chunk
    n_chunks = (((t0 + Q_BLOCK - CMP_BLOCK) >> (CMP_STRIDE.bit_length() - 1)) + ch) // ch

    def compressed(n_sub):
        imp_scr[0:IMP_PAD, :] = jnp.zeros((IMP_PAD, Q_BLOCK), F32)
        imp_scr[IMP_PAD + n_sub * ch:, :] = jnp.zeros((n_cmp + IMP_PAD - n_sub * ch, Q_BLOCK), F32)
        biases = []
        for sub in range(n_sub):
            if sub >= n_sub - 2:
                cmp_end = (sub * ch + lax.broadcasted_iota(jnp.int32, (ch, 1), 0)) * CMP_STRIDE + (CMP_BLOCK - 1)
                biases.append(jnp.where(cmp_end <= tq1, 0.0, MASKED).astype(BF16))
            else:
                biases.append(jnp.zeros((ch, Q_BLOCK), BF16))
        k_aug = jnp.concatenate([jnp.concatenate(biases, axis=0), kcc_ref[0:n_sub * ch, :]], axis=1)
        scores = _dot(k_aug, q_aug_t)
        parts = []
        for sub in range(n_sub):
            s = scores[sub * ch:(sub + 1) * ch]
            m_sub = jnp.max(s, axis=0, keepdims=True)
            e = jnp.exp2(s - m_sub)
            parts.append((m_sub, jnp.sum(e, axis=0, keepdims=True),
                          _dot(vcct_ref[:, sub * ch:(sub + 1) * ch], e.astype(BF16)), e))
        m = parts[0][0]
        for m_sub, _, _, _ in parts[1:]:
            m = jnp.maximum(m, m_sub)
        scales = [jnp.exp2(m_sub - m) for m_sub, _, _, _ in parts]
        l = scales[0] * parts[0][1]
        acc = scales[0] * parts[0][2]
        for scale, (_, l_sub, acc_sub, _) in zip(scales[1:], parts[1:]):
            l = l + scale * l_sub
            acc = acc + scale * acc_sub
        inv_l = jnp.where(heads(tq1 >= CMP_BLOCK - 1), 1.0 / l, 0.0)
        ocmp_scr[...] = acc * inv_l
        for sub in range(n_sub):
            pc = parts[sub][3] * (scales[sub] * inv_l)
            p_sum = pc[:, :Q_BLOCK]
            for hp in range(1, hpg):
                p_sum = p_sum + pc[:, hp * Q_BLOCK:(hp + 1) * Q_BLOCK]
            imp_scr[IMP_PAD + sub * ch:IMP_PAD + (sub + 1) * ch, :] = p_sum

    for variant in range(n_cmp // ch):
        pl.when(n_chunks == variant + 1)(functools.partial(compressed, variant + 1))
    o_cmp = ocmp_scr[...]

    ratio = SEL_BLOCK // CMP_STRIDE
    sel_shift = SEL_BLOCK.bit_length() - 1
    cur = tq1 >> sel_shift
    n_forced = 3

    span = WINDOW + Q_BLOCK
    k_start = pl.multiple_of(jnp.maximum(t0 - WINDOW, 0), Q_BLOCK)

    def window_scores():
        diff = tq1 - (k_start + lax.broadcasted_iota(jnp.int32, (span, 1), 0))
        inside = pltpu.bitcast(diff, jnp.uint32) < jnp.uint32(WINDOW)
        bias = jnp.where(inside, 0.0, MASKED).astype(BF16)
        wins_scr[...] = _dot(jnp.concatenate([bias, kw_ref[pl.ds(k_start, span), :]], axis=1), q_aug_t)

    def window_values():
        sw = wins_scr[...]
        pw = jnp.exp2(sw - jnp.max(sw, axis=0, keepdims=True)).astype(BF16)
        acc_win = _dot(vwt_ref[:, pl.ds(k_start, span)], pw)
        owin_scr[...] = acc_win[:NSA_HEAD_DIM] / acc_win[NSA_HEAD_DIM:NSA_HEAD_DIM + 1]

    def select_blocks(rows):
        window_scores()
        imp = imp_scr[pl.ds(IMP_PAD - 1, rows, stride=ratio), :]
        for k in range(1, ratio + 1):
            imp = imp + imp_scr[pl.ds(IMP_PAD - 1 + k, rows, stride=ratio), :]
        j = lax.broadcasted_iota(jnp.int32, (rows, Q_BLOCK), 0)
        j_f = j.astype(F32)
        forced = (j == 0) | (j == cur) | (j == cur - 1)
        valid = j <= cur

        def pick(_, sc):
            best = jnp.max(sc, axis=0, keepdims=True)
            idx = jnp.min(jnp.where(sc == best, j_f, float(rows)), axis=0, keepdims=True)
            return jnp.where(j_f == idx, neg_inf, sc)

        left = lax.fori_loop(0, max(min(SEL_TOPK, n_sel) - n_forced, 0), pick,
                             jnp.where(valid & ~forced, imp, neg_inf), unroll=True)
        sel_scr[0:rows, :] = jnp.where(forced | (valid & (left == neg_inf)), 0.0, MASKED)
        if rows < n_sel:
            sel_scr[rows:n_sel, :] = jnp.full((n_sel - rows, Q_BLOCK), MASKED, F32)

    row_step = min(TOPK_ROW_STEP, n_sel)
    last_block = (t0 + Q_BLOCK - 1) >> sel_shift
    for variant in range(n_sel // row_step):
        pl.when(last_block // row_step == variant)(functools.partial(select_blocks, (variant + 1) * row_step))

    blocks_per_tile = kt_size // SEL_BLOCK

    def slc_tile(kt, carry, n_sub, causal_from, after_scores=None):
        m_run, acc = carry
        base = pl.multiple_of(kt * kt_size, kt_size)
        biases = []
        for sub in range(n_sub):
            blk0 = kt * blocks_per_tile + sub * (sub_size // SEL_BLOCK)
            bias = jnp.concatenate(
                [jnp.broadcast_to(sel_scr[pl.ds(blk0 + jb, 1), :], (SEL_BLOCK, Q_BLOCK))
                 for jb in range(sub_size // SEL_BLOCK)], axis=0)
            if sub >= causal_from:
                kpos = base + sub * sub_size + lax.broadcasted_iota(jnp.int32, (sub_size, 1), 0)
                bias = jnp.where(kpos <= tq1, bias, MASKED)
            biases.append(bias.astype(BF16))
        k_aug = jnp.concatenate([jnp.concatenate(biases, axis=0), ks_ref[pl.ds(base, n_sub * sub_size), :]], axis=1)
        scores = _dot(k_aug, q_aug_t)
        if after_scores is not None:
            after_scores()
        parts = []
        for sub in range(n_sub):
            sc = scores[sub * sub_size:(sub + 1) * sub_size]
            m_sub = jnp.max(sc, axis=0, keepdims=True)
            pe = jnp.exp2(sc - m_sub).astype(BF16)
            k0 = pl.multiple_of(base + sub * sub_size, sub_size)
            parts.append((m_sub, _dot(vst_ref[:, pl.ds(k0, sub_size)], pe)))
        m_new = m_run
        for m_sub, _ in parts:
            m_new = jnp.maximum(m_new, m_sub)
        acc = jnp.exp2(m_run - m_new) * acc
        for m_sub, acc_sub in parts:
            acc = acc + jnp.exp2(m_sub - m_new) * acc_sub
        return m_new, acc

    subs_per_tile = kt_size // sub_size
    n_full = t0 // kt_size
    carry = lax.fori_loop(
        0, n_full, functools.partial(slc_tile, n_sub=subs_per_tile, causal_from=subs_per_tile),
        (jnp.full((1, width), MASKED, F32), jnp.zeros((V_ROWS, width), F32)))
    tail_step = min(4, subs_per_tile)
    tail_len = (t0 - n_full * kt_size) // (tail_step * sub_size)
    for variant in range(subs_per_tile // tail_step):
        n_sub = tail_step * (variant + 1)

        @pl.when(tail_len == variant)
        def _(n_sub=n_sub):
            _, acc_slc = slc_tile(n_full, carry, n_sub=n_sub, causal_from=n_sub - tail_step,
                                  after_scores=window_values)
            oslc_scr[...] = acc_slc[:NSA_HEAD_DIM] / acc_slc[NSA_HEAD_DIM:NSA_HEAD_DIM + 1]

    o_slc = oslc_scr[...]
    o_win = owin_scr[...]

    for hp in range(hpg):
        cols = slice(hp * Q_BLOCK, (hp + 1) * Q_BLOCK)
        row = (grp * hpg + hp) * 3
        o_t = (gt_ref[qi, pl.ds(row, 1), :] * o_cmp[:, cols] + gt_ref[qi, pl.ds(row + 1, 1), :] * o_slc[:, cols]
               + gt_ref[qi, pl.ds(row + 2, 1), :] * o_win[:, cols])
        o_ref[pl.ds(pl.multiple_of(qi * Q_BLOCK, Q_BLOCK), Q_BLOCK), hp * NSA_HEAD_DIM:(hp + 1) * NSA_HEAD_DIM] = (
            o_t.T.astype(o_ref.dtype))


def _nsa_attention(qs, kcc, vcct, ks, vst, kw, vwt, gt, B, S, blocks=4, kt_size=4096, sub_size=256, cmp_chunk=256):
    G, Dh, nq, hpg = NSA_KV_GROUPS, NSA_HEAD_DIM, S // Q_BLOCK, NSA_HEADS_PER_GROUP
    kt_size = min(kt_size, S)
    blocks = min(blocks, nq)
    n_cmp, n_sel = S // CMP_STRIDE, S // SEL_BLOCK
    cmp_chunk = min(cmp_chunk, n_cmp)
    per_group = lambda shape: pl.BlockSpec((None, None) + shape, lambda b, g, i: (b, g, 0, 0))
    return pl.pallas_call(
        functools.partial(_nsa_body, blocks=blocks, kt_size=kt_size, sub_size=min(sub_size, kt_size),
                          cmp_chunk=cmp_chunk),
        grid=(B, G, nq // blocks),
        in_specs=[
            pl.BlockSpec((None, None, Dh, blocks * hpg * Q_BLOCK), lambda b, g, i: (b, g, 0, i)),
            per_group((n_cmp, Dh)), per_group((Dh, n_cmp)),
            per_group((S, Dh)), per_group((V_ROWS, S)),
            per_group((S, Dh)), per_group((V_ROWS, S)),
            pl.BlockSpec((None, blocks, GATE_ROWS, Q_BLOCK), lambda b, g, i: (b, i, 0, 0)),
        ],
        out_specs=pl.BlockSpec((None, blocks * Q_BLOCK, hpg * Dh), lambda b, g, i: (b, i, g)),
        out_shape=jax.ShapeDtypeStruct((B, S, NSA_HEADS * Dh), BF16),
        scratch_shapes=[
            pltpu.VMEM((n_cmp + 2 * IMP_PAD, Q_BLOCK), F32),
            pltpu.VMEM((n_sel, Q_BLOCK), F32),
            pltpu.VMEM((Dh, hpg * Q_BLOCK), F32),
            pltpu.VMEM((WINDOW + Q_BLOCK, hpg * Q_BLOCK), F32),
            pltpu.VMEM((Dh, hpg * Q_BLOCK), F32),
            pltpu.VMEM((Dh, hpg * Q_BLOCK), F32),
        ],
        compiler_params=pltpu.CompilerParams(
            dimension_semantics=("parallel", "parallel", "arbitrary"), vmem_limit_bytes=VMEM_LIMIT),
        name="nsa_attention",
    )(qs, kcc, vcct, ks, vst, kw, vwt, gt)


def _merge_body(ya_ref, yb_ref, yc_ref, ga_ref, gb_ref, gc_ref, h_ref, wb_ref, wo_ref, o_ref):
    merged = jax.nn.sigmoid(ga_ref[...].astype(F32)) * _dot(ya_ref[...], wb_ref[0])
    merged = merged + jax.nn.sigmoid(gb_ref[...].astype(F32)) * _dot(yb_ref[...], wb_ref[1])
    merged = merged + jax.nn.sigmoid(gc_ref[...].astype(F32)) * _dot(yc_ref[...], wb_ref[2])
    o_ref[...] = h_ref[...] + _dot(merged.astype(BF16), wo_ref[...])


def _merge(ya, yb, yc, main, h, w_branch, w_out, tm=1024):
    T, D = h.shape
    y_spec = pl.BlockSpec((tm, 512), lambda i: (i, 0))
    gate_spec = lambda k: pl.BlockSpec((tm, D), lambda i: (i, MAIN_GM // D + k))
    return pl.pallas_call(
        _merge_body,
        grid=(T // tm,),
        in_specs=[
            y_spec, y_spec, y_spec, gate_spec(0), gate_spec(1), gate_spec(2),
            pl.BlockSpec((tm, D), lambda i: (i, 0)),
            pl.BlockSpec((3, 512, D), lambda i: (0, 0, 0)),
            pl.BlockSpec((D, D), lambda i: (0, 0)),
        ],
        out_specs=pl.BlockSpec((tm, D), lambda i: (i, 0)),
        out_shape=jax.ShapeDtypeStruct((T, D), F32),
        compiler_params=pltpu.CompilerParams(dimension_semantics=("parallel",), vmem_limit_bytes=VMEM_LIMIT),
        name="merge_out",
    )(ya, yb, yc, main, main, main, h, w_branch, w_out)


def _ffn_body(h_ref, gf_ref, w1_ref, w2_ref, gp_ref, wg_ref, p_ref, wp_ref, gl_ref, o_ref, f_scr, acc_scr, *, final_norm):
    j = pl.program_id(1)

    @pl.when(j == 0)
    def _():
        f_scr[...] = _rms(h_ref[...], gf_ref[...]).astype(BF16)
        acc_scr[...] = jnp.zeros_like(acc_scr)

    a = jnp.maximum(_dot(f_scr[...], w1_ref[...]), 0.0)
    acc_scr[...] += _dot((a * a).astype(BF16), w2_ref[...])

    @pl.when(j == pl.num_programs(1) - 1)
    def _():
        h2 = h_ref[...] + acc_scr[...]
        gate = jax.nn.sigmoid(_dot(_rms(h2, gp_ref[...]).astype(BF16), wg_ref[...]))
        h3 = h2 + gate * _dot(p_ref[...].astype(BF16), wp_ref[...])
        o_ref[...] = _rms(h3, gl_ref[...]) if final_norm else h3


def _ffn_ple(h, norm_ffn, w1, w2, norm_ple, w_gate, p, w_proj, norm_last, final_norm, tm=1024, tf=1024):
    T, D = h.shape
    tm = min(tm, T)
    Fdim = w1.shape[1]
    Pdim = p.shape[1]
    vec = pl.BlockSpec((1, D), lambda i, j: (0, 0))
    return pl.pallas_call(
        functools.partial(_ffn_body, final_norm=final_norm),
        grid=(T // tm, Fdim // tf),
        in_specs=[
            pl.BlockSpec((tm, D), lambda i, j: (i, 0)), vec,
            pl.BlockSpec((D, tf), lambda i, j: (0, j)),
            pl.BlockSpec((tf, D), lambda i, j: (j, 0)),
            vec,
            pl.BlockSpec((D, D), lambda i, j: (0, 0)),
            pl.BlockSpec((tm, Pdim), lambda i, j: (i, 0)),
            pl.BlockSpec((Pdim, D), lambda i, j: (0, 0)),
            vec,
        ],
        out_specs=pl.BlockSpec((tm, D), lambda i, j: (i, 0)),
        out_shape=jax.ShapeDtypeStruct((T, D), F32),
        scratch_shapes=[pltpu.VMEM((tm, D), BF16), pltpu.VMEM((tm, D), F32)],
        compiler_params=pltpu.CompilerParams(
            dimension_semantics=("parallel", "arbitrary"), vmem_limit_bytes=VMEM_LIMIT),
        name="ffn_ple",
    )(h, norm_ffn, w1, w2, norm_ple, w_gate, p, w_proj, norm_last)


def _split_w_in(w):
    parts, off = [], 0
    for sz in IN_SIZES:
        parts.append(w[:, off:off + sz])
        off += sz
    u, q_nsa, kv_nsa, g_nsa, q_gla, k_gla, v_gla, a_gla, r_gla, g_merge = parts
    D = w.shape[0]
    pad = lambda n: jnp.zeros((D, n), w.dtype)
    main = jnp.concatenate([u, q_gla, k_gla, v_gla, r_gla, g_merge, a_gla, pad(MAIN_WIDTH - MAIN_A - GLA_GATE_RANK)], axis=1)
    kv = [kv_nsa[:, s * LANES:(s + 1) * LANES] for s in range(6)]
    nsa = jnp.concatenate([q_nsa, kv[0], kv[2], kv[4], kv[1], kv[3], kv[5], g_nsa, pad(NSA_WIDTH - NSA_G - 24)], axis=1)
    return main.astype(BF16), nsa.astype(BF16)


def kernel(x, p, positions, norm_mix, w_in, pool_w, pool_scale, cmp_pos_k, cmp_w_k, cmp_pos_v, cmp_w_v, gla_w_gate, gla_b_gate, gla_norm, w_branch, w_out, norm_ffn, w_ff1, w_ff2, norm_ple, w_ple_gate, w_ple_proj, norm_final):
    B, S, D = x.shape
    depth = w_in.shape[0]
    T = B * S
    Dh = NSA_HEAD_DIM
    row = lambda v: v.reshape(1, -1).astype(F32)

    half = NSA_ROT_DIM // 2
    inv_freq = jnp.power(ROPE_THETA, -jnp.arange(half, dtype=F32) * (2.0 / NSA_ROT_DIM)).reshape(half, 1)
    pos_rows = positions.reshape(B, S // Q_BLOCK, 1, Q_BLOCK)
    rope_spread = _rope_spread_matrix()

    h = x.reshape(T, D)
    for i in range(depth):
        w_main, w_nsa = _split_w_in(w_in[i])
        main = _norm_proj(h, row(norm_mix[i]), w_main, BF16, tn=MAIN_WIDTH // 3).reshape(B, S, MAIN_WIDTH)
        nsa_proj = _norm_proj(h, row(norm_mix[i]), w_nsa, F32, tn=NSA_WIDTH).reshape(B, S, NSA_WIDTH)

        y_a = _pool_mixer(main, pool_w[i].astype(BF16), row(pool_scale[i]), B, S)

        wg = jnp.zeros((LANES, GLA_HEADS * GLA_DK), BF16).at[:GLA_GATE_RANK].set(gla_w_gate[i].astype(BF16))
        y_c = _gla_mixer(main, wg, row(gla_b_gate[i]), row(gla_norm[i]), B, S)

        qs, kc, vc, ks, kw, vst, vwt, gt = _nsa_prep(nsa_proj, pos_rows, inv_freq, rope_spread, B, S)
        kcc = _compress(kc, cmp_w_k[i].reshape(CMP_BLOCK, Dh, Dh).astype(BF16),
                        cmp_pos_k[i].reshape(CMP_BLOCK, 1, Dh).astype(F32), transpose_out=False)
        vcct = _compress(vc, cmp_w_v[i].reshape(CMP_BLOCK, Dh, Dh).astype(BF16),
                         cmp_pos_v[i].reshape(CMP_BLOCK, 1, Dh).astype(F32), transpose_out=True)
        y_b = _nsa_attention(qs, kcc, vcct, ks, vst, kw, vwt, gt, B, S)

        h = _merge(y_a.reshape(T, -1), y_b.reshape(T, -1), y_c.reshape(T, -1), main.reshape(T, MAIN_WIDTH), h,
                   w_branch[i].astype(BF16), w_out[i].astype(BF16))
        h = _ffn_ple(h, row(norm_ffn[i]), w_ff1[i].astype(BF16), w_ff2[i].astype(BF16), row(norm_ple[i]),
                     w_ple_gate[i].astype(BF16), p[i].reshape(T, -1), w_ple_proj[i].astype(BF16),
                     row(norm_final), final_norm=(i == depth - 1))
    return h.reshape(B, S, D)
```

```python
import functools

import jax
import jax.numpy as jnp
import numpy as np
from jax import lax
from jax.experimental import pallas as pl
from jax.experimental.pallas import tpu as pltpu

F32, BF16 = jnp.float32, jnp.bfloat16

EPS = 1e-6
ROPE_THETA = 500000.0
POOL_WINDOWS = (2, 4, 8, 16)
POOL_GROUP_DIM = 128
POOL_HALO = max(POOL_WINDOWS)
NSA_HEADS = 8
NSA_KV_GROUPS = 2
NSA_HEADS_PER_GROUP = NSA_HEADS // NSA_KV_GROUPS
NSA_HEAD_DIM = 64
NSA_ROT_DIM = 16
CMP_BLOCK = 32
CMP_STRIDE = 16
SEL_BLOCK = 64
SEL_TOPK = 16
WINDOW = 512
Q_BLOCK = 128
GLA_HEADS = 4
GLA_DK = 64
GLA_DV = 128
GLA_GATE_RANK = 16
GLA_TAU = 16.0
GLA_CHUNK = 64
IN_SIZES = (512, 512, 768, 24, 256, 256, 512, 16, 512, 3072)

LANES = 128
MASKED = -1e30
LOG2E = 1.4426950408889634
Q_SCALE = NSA_HEAD_DIM ** -0.5 * LOG2E
V_ROWS = NSA_HEAD_DIM + 16
IMP_PAD = 8
GATE_ROWS = 32
TOPK_ROW_STEP = 64
VMEM_LIMIT = 56 * 1024 * 1024

MAIN_U, MAIN_QG, MAIN_KG, MAIN_VG, MAIN_RG, MAIN_GM, MAIN_A, MAIN_WIDTH = 0, 512, 768, 1024, 1536, 2048, 5120, 5376
NSA_Q, NSA_KC, NSA_KS, NSA_KW, NSA_VC, NSA_VS, NSA_VW, NSA_G, NSA_WIDTH = 0, 512, 640, 768, 896, 1024, 1152, 1280, 1536


def _nt(a, b):
    return lax.dot_general(a, b, (((1,), (1,)), ((), ())), preferred_element_type=F32)


def _tn(a, b):
    return lax.dot_general(a, b, (((0,), (0,)), ((), ())), preferred_element_type=F32)


def _dot(a, b):
    return jnp.dot(a, b, preferred_element_type=F32)


def _rms(x, gain):
    return x * lax.rsqrt(jnp.mean(x * x, axis=-1, keepdims=True) + EPS) * gain


def _proj_body(h_ref, g_ref, w_ref, o_ref, a_scr):
    @pl.when(pl.program_id(1) == 0)
    def _():
        a_scr[...] = _rms(h_ref[...], g_ref[...]).astype(BF16)

    o_ref[...] = _dot(a_scr[...], w_ref[...]).astype(o_ref.dtype)


def _norm_proj(h, gain, w, out_dtype, tn, tm=1024):
    T, D = h.shape
    N = w.shape[1]
    tm = min(tm, T)
    return pl.pallas_call(
        _proj_body,
        grid=(T // tm, N // tn),
        in_specs=[
            pl.BlockSpec((tm, D), lambda i, j: (i, 0)),
            pl.BlockSpec((1, D), lambda i, j: (0, 0)),
            pl.BlockSpec((D, tn), lambda i, j: (0, j)),
        ],
        out_specs=pl.BlockSpec((tm, tn), lambda i, j: (i, j)),
        out_shape=jax.ShapeDtypeStruct((T, N), out_dtype),
        scratch_shapes=[pltpu.VMEM((tm, D), BF16)],
        compiler_params=pltpu.CompilerParams(
            dimension_semantics=("parallel", "arbitrary"), vmem_limit_bytes=VMEM_LIMIT),
        name="norm_proj",
    )(h, gain, w)


def _pool_tile(cur, halo, t, w_ref, sc_ref):
    outs = []
    for g, win in enumerate(POOL_WINDOWS):
        lo, hi = g * POOL_GROUP_DIM, (g + 1) * POOL_GROUP_DIM
        x = cur[:, lo:hi]
        e = jnp.concatenate([halo[:, lo:hi], x], axis=0)
        step = 1
        while step < win:
            e = e[step:] + e[:-step]
            step *= 2
        wsum = e[POOL_HALO - (win - 1):]
        cnt = jnp.minimum(t + 1, win).astype(F32)
        pooled = wsum / cnt - x
        outs.append((_dot(pooled.astype(BF16), w_ref[g]) * sc_ref[:, lo:hi]).astype(BF16))
    return jnp.concatenate(outs, axis=1)


def _gla_body(q_ref, k_ref, v_ref, r_ref, a_ref, wg_ref, bg_ref, ng_ref, o_ref, st_ref, *, tc):
    @pl.when(pl.program_id(1) == 0)
    def _():
        st_ref[...] = jnp.zeros_like(st_ref)

    C = GLA_CHUNK
    n_chunks = tc // C
    causal = lax.broadcasted_iota(jnp.int32, (C, C), 0) >= lax.broadcasted_iota(jnp.int32, (C, C), 1)
    width = GLA_HEADS * GLA_DK
    hk = [slice(h * GLA_DK, (h + 1) * GLA_DK) for h in range(GLA_HEADS)]
    hv = [slice(h * GLA_DV, (h + 1) * GLA_DV) for h in range(GLA_HEADS)]
    rows = [slice(c * C, (c + 1) * C) for c in range(n_chunks)]

    z = _dot(a_ref[...], wg_ref[...]) + bg_ref[...]
    b = jax.nn.log_sigmoid(z) * (1.0 / GLA_TAU)
    row_in_chunk = lax.broadcasted_iota(jnp.int32, (tc, 1), 0) % C
    step = 1
    while step < C:
        shifted = jnp.concatenate([jnp.zeros((step, width), F32), b[:-step]], axis=0)
        b = b + jnp.where(row_in_chunk >= step, shifted, 0.0)
        step *= 2
    b_last = [b[(c + 1) * C - 1:(c + 1) * C, :] for c in range(n_chunks)]
    b_last_rows = jnp.concatenate([jnp.broadcast_to(bl, (C, width)) for bl in b_last], axis=0)
    qf = q_ref[...].astype(F32) * (GLA_DK ** -0.5)
    kf = k_ref[...].astype(F32)
    q_s = (qf * jnp.exp(b)).astype(BF16)
    k_s = (kf * jnp.exp(-b)).astype(BF16)
    k_t = (kf * jnp.exp(b_last_rows - b)).astype(BF16)

    att = [[jnp.where(causal, _nt(q_s[rows[c], hk[h]], k_s[rows[c], hk[h]]), 0.0).astype(BF16)
            for h in range(GLA_HEADS)] for c in range(n_chunks)]
    o_intra = [[_dot(att[c][h], v_ref[rows[c], hv[h]]) for h in range(GLA_HEADS)] for c in range(n_chunks)]
    kv = [[_tn(v_ref[rows[c], hv[h]], k_t[rows[c], hk[h]]) for h in range(GLA_HEADS)] for c in range(n_chunks)]

    state_t = [st_ref[h] for h in range(GLA_HEADS)]
    for c in range(n_chunks):
        decay = jnp.exp(b_last[c])
        for h in range(GLA_HEADS):
            o = o_intra[c][h] + _nt(q_s[rows[c], hk[h]], state_t[h].astype(BF16))
            state_t[h] = state_t[h] * decay[:, hk[h]] + kv[c][h]
            o = _rms(o, ng_ref[...])
            r = r_ref[rows[c], hv[h]].astype(F32)
            o_ref[rows[c], hv[h]] = (o * (r * jax.nn.sigmoid(r))).astype(o_ref.dtype)
    for h in range(GLA_HEADS):
        st_ref[h] = state_t[h]


def _gla_mixer(main, w_gate, b_gate, norm_g, B, S, tc=512):
    tc = min(tc, S)
    return pl.pallas_call(
        functools.partial(_gla_body, tc=tc),
        grid=(B, S // tc),
        in_specs=[
            pl.BlockSpec((None, tc, 256), lambda b, i: (b, i, MAIN_QG // 256)),
            pl.BlockSpec((None, tc, 256), lambda b, i: (b, i, MAIN_KG // 256)),
            pl.BlockSpec((None, tc, 512), lambda b, i: (b, i, MAIN_VG // 512)),
            pl.BlockSpec((None, tc, 512), lambda b, i: (b, i, MAIN_RG // 512)),
            pl.BlockSpec((None, tc, 128), lambda b, i: (b, i, MAIN_A // 128)),
            pl.BlockSpec((128, 256), lambda b, i: (0, 0)),
            pl.BlockSpec((1, 256), lambda b, i: (0, 0)),
            pl.BlockSpec((1, 128), lambda b, i: (0, 0)),
        ],
        out_specs=pl.BlockSpec((None, tc, 512), lambda b, i: (b, i, 0)),
        out_shape=jax.ShapeDtypeStruct((B, S, 512), BF16),
        scratch_shapes=[pltpu.VMEM((GLA_HEADS, GLA_DV, GLA_DK), F32)],
        compiler_params=pltpu.CompilerParams(dimension_semantics=("parallel", "arbitrary")),
        name="gla_mixer",
    )(main, main, main, main, main, w_gate, b_gate, norm_g)


def _nsa_prep_body(x_ref, pos_ref, invf_ref, spread_ref, qs_ref, kc_ref, vc_ref, ks_ref, kw_ref, vst_ref, vwt_ref,
                   gt_ref, *, blocks):
    half = NSA_ROT_DIM // 2
    hpg = NSA_HEADS_PER_GROUP
    ones_row = (lax.broadcasted_iota(jnp.int32, (V_ROWS - NSA_HEAD_DIM, Q_BLOCK), 0) == 0).astype(BF16)

    def pieces(a):
        p1 = a.astype(BF16).astype(F32)
        p2 = (a - p1).astype(BF16).astype(F32)
        return [p1, p2, (a - p1 - p2).astype(BF16).astype(F32)]

    for blk in range(blocks):
        tok = slice(blk * Q_BLOCK, (blk + 1) * Q_BLOCK)
        ang = invf_ref[...] * pos_ref[blk].astype(F32)
        lhs = jnp.concatenate(
            pieces(jnp.cos(ang)) + pieces(jnp.sin(ang)) + [jnp.ones((2 * half, Q_BLOCK), F32)], axis=0)
        tables = _tn(lhs.astype(BF16), spread_ref[...])
        cs = tables[:, :LANES]
        s_lo = tables[:, LANES:2 * LANES]
        s_hi = tables[:, 2 * LANES:]

        def rope(x):
            return x * cs + pltpu.roll(x, LANES - half, 1) * s_lo + pltpu.roll(x, half, 1) * s_hi

        def split_groups(x, ref, dtype):
            ref[0, tok, :] = x[:, :NSA_HEAD_DIM].astype(dtype)
            ref[1, tok, :] = x[:, NSA_HEAD_DIM:].astype(dtype)

        for pair in range(NSA_HEADS // 2):
            cols = slice(NSA_Q + pair * LANES, NSA_Q + (pair + 1) * LANES)
            qr_t = (rope(x_ref[tok, cols]) * Q_SCALE).T
            for sub in range(2):
                head = 2 * pair + sub
                grp, hp = head // hpg, head % hpg
                col0 = (blk * hpg + hp) * Q_BLOCK
                qs_ref[grp, :, col0:col0 + Q_BLOCK] = qr_t[sub * NSA_HEAD_DIM:(sub + 1) * NSA_HEAD_DIM].astype(BF16)
        split_groups(rope(x_ref[tok, NSA_KC:NSA_KC + LANES]), kc_ref, F32)
        split_groups(rope(x_ref[tok, NSA_KS:NSA_KS + LANES]), ks_ref, BF16)
        split_groups(rope(x_ref[tok, NSA_KW:NSA_KW + LANES]), kw_ref, BF16)
        split_groups(x_ref[tok, NSA_VC:NSA_VC + LANES], vc_ref, F32)
        vst = x_ref[tok, NSA_VS:NSA_VS + LANES].T
        vwt = x_ref[tok, NSA_VW:NSA_VW + LANES].T
        for grp in range(NSA_KV_GROUPS):
            for ref, v in ((vst_ref, vst), (vwt_ref, vwt)):
                ref[grp, :NSA_HEAD_DIM, tok] = v[grp * NSA_HEAD_DIM:(grp + 1) * NSA_HEAD_DIM].astype(BF16)
                ref[grp, NSA_HEAD_DIM:, tok] = ones_row
        gt_ref[blk] = jax.nn.sigmoid(x_ref[tok, NSA_G:NSA_G + LANES]).T[:GATE_ROWS]


def _rope_spread_matrix():
    half = NSA_ROT_DIM // 2
    lane = np.arange(LANES)
    d = lane % NSA_HEAD_DIM
    m = np.zeros((8 * half, 3 * LANES), np.float32)
    for piece in range(3):
        for f in range(half):
            hit = d % half == f
            m[piece * half + f, :LANES] = hit & (d < NSA_ROT_DIM)
            m[(3 + piece) * half + f, LANES:2 * LANES] = -1.0 * (hit & (d < half))
            m[(3 + piece) * half + f, 2 * LANES:] = hit & (d >= half) & (d < NSA_ROT_DIM)
    m[6 * half, :LANES] = d >= NSA_ROT_DIM
    return jnp.asarray(m, BF16)


def _nsa_prep(nsa_proj, positions, inv_freq, rope_spread, B, S, blocks=4):
    G, Dh, nq = NSA_KV_GROUPS, NSA_HEAD_DIM, S // Q_BLOCK
    blocks = min(blocks, nq)
    ts = blocks * Q_BLOCK
    tok = lambda dtype: jax.ShapeDtypeStruct((B, G, S, Dh), dtype)
    tok_spec = pl.BlockSpec((None, G, ts, Dh), lambda b, i: (b, 0, i, 0))
    tr = jax.ShapeDtypeStruct((B, G, V_ROWS, S), BF16)
    tr_spec = pl.BlockSpec((None, G, V_ROWS, ts), lambda b, i: (b, 0, 0, i))
    return pl.pallas_call(
        functools.partial(_nsa_prep_body, blocks=blocks),
        grid=(B, nq // blocks),
        in_specs=[
            pl.BlockSpec((None, ts, NSA_WIDTH), lambda b, i: (b, i, 0)),
            pl.BlockSpec((None, blocks, 1, Q_BLOCK), lambda b, i: (b, i, 0, 0)),
            pl.BlockSpec(inv_freq.shape, lambda b, i: (0, 0)),
            pl.BlockSpec(rope_spread.shape, lambda b, i: (0, 0)),
        ],
        out_specs=[
            pl.BlockSpec((None, G, Dh, NSA_HEADS_PER_GROUP * ts), lambda b, i: (b, 0, 0, i)),
            tok_spec, tok_spec, tok_spec, tok_spec, tr_spec, tr_spec,
            pl.BlockSpec((None, blocks, GATE_ROWS, Q_BLOCK), lambda b, i: (b, i, 0, 0)),
        ],
        out_shape=[
            jax.ShapeDtypeStruct((B, G, Dh, NSA_HEADS_PER_GROUP * S), BF16),
            tok(F32), tok(F32), tok(BF16), tok(BF16), tr, tr,
            jax.ShapeDtypeStruct((B, nq, GATE_ROWS, Q_BLOCK), F32),
        ],
        compiler_params=pltpu.CompilerParams(dimension_semantics=("parallel", "parallel")),
        name="nsa_prep",
    )(nsa_proj, positions, inv_freq, rope_spread)


def _cmp_body(x_ref, w_ref, p_ref, o_ref, *, n_blk, transpose_out):
    half = CMP_BLOCK // 2
    first = jnp.zeros((n_blk, NSA_HEAD_DIM), F32)
    second = jnp.zeros((n_blk, NSA_HEAD_DIM), F32)
    for l in range(half):
        rows = x_ref[pl.ds(l, n_blk, stride=CMP_STRIDE), :]
        first = first + _dot((rows + p_ref[l]).astype(BF16), w_ref[l])
        second = second + _dot((rows + p_ref[half + l]).astype(BF16), w_ref[half + l])
    out = first + jnp.concatenate([second[1:], jnp.zeros((1, NSA_HEAD_DIM), F32)], axis=0)
    o_ref[...] = (out.T if transpose_out else out).astype(o_ref.dtype)


def _compress(x, w, pos, transpose_out):
    B, G, S, Dh = x.shape
    n_blk = S // CMP_STRIDE
    out_block = (None, None, Dh, n_blk) if transpose_out else (None, None, n_blk, Dh)
    out_shape = (B, G, Dh, n_blk) if transpose_out else (B, G, n_blk, Dh)
    return pl.pallas_call(
        functools.partial(_cmp_body, n_blk=n_blk, transpose_out=transpose_out),
        grid=(B, G),
        in_specs=[
            pl.BlockSpec((None, None, S, Dh), lambda b, g: (b, g, 0, 0)),
            pl.BlockSpec((CMP_BLOCK, Dh, Dh), lambda b, g: (0, 0, 0)),
            pl.BlockSpec((CMP_BLOCK, 1, Dh), lambda b, g: (0, 0, 0)),
        ],
        out_specs=pl.BlockSpec(out_block, lambda b, g: (b, g, 0, 0)),
        out_shape=jax.ShapeDtypeStruct(out_shape, BF16),
        compiler_params=pltpu.CompilerParams(
            dimension_semantics=("parallel", "parallel"), vmem_limit_bytes=VMEM_LIMIT),
        name="nsa_compress",
    )(x, w, pos)


def _nsa_body(*refs, blocks, **tiles):
    def one_block(qi, carry):
        _nsa_block(qi, *refs, blocks=blocks, **tiles)
        return carry

    lax.fori_loop(0, blocks, one_block, 0)


def _nsa_block(qi, q_ref, kcc_ref, vcct_ref, ks_ref, vst_ref, kw_ref, vwt_ref, gt_ref, o_ref,
               imp_scr, sel_scr, ocmp_scr, wins_scr, owin_scr, oslc_scr, *, blocks, kt_size, sub_size, cmp_chunk):
    grp = pl.program_id(1)
    t0 = (pl.program_id(2) * blocks + qi) * Q_BLOCK
    hpg = NSA_HEADS_PER_GROUP
    n_cmp = kcc_ref.shape[0]
    n_sel = sel_scr.shape[0]
    q_t = q_ref[:, pl.ds(pl.multiple_of(qi * (hpg * Q_BLOCK), hpg * Q_BLOCK), hpg * Q_BLOCK)]
    tq1 = t0 + lax.broadcasted_iota(jnp.int32, (1, Q_BLOCK), 1)
    heads = lambda x: jnp.concatenate([x] * hpg, axis=1)

    width = hpg * Q_BLOCK
    neg_inf = -jnp.inf
    eye = (lax.broadcasted_iota(jnp.int32, (Q_BLOCK, Q_BLOCK), 0)
           == lax.broadcasted_iota(jnp.int32, (Q_BLOCK, Q_BLOCK), 1)).astype(BF16)
    q_aug_t = jnp.concatenate([jnp.concatenate([eye] * hpg, axis=1), q_t], axis=0)

    ch = cmp_chunk
    n_chunks = (((t0 + Q_BLOCK - CMP_BLOCK) >> (CMP_STRIDE.bit_length() - 1)) + ch) // ch

    def compressed(n_sub):
        imp_scr[0:IMP_PAD, :] = jnp.zeros((IMP_PAD, Q_BLOCK), F32)
        imp_scr[IMP_PAD + n_sub * ch:, :] = jnp.zeros((n_cmp + IMP_PAD - n_sub * ch, Q_BLOCK), F32)
        biases = []
        for sub in range(n_sub):
            if sub >= n_sub - 2:
                cmp_end = (sub * ch + lax.broadcasted_iota(jnp.int32, (ch, 1), 0)) * CMP_STRIDE + (CMP_BLOCK - 1)
                biases.append(jnp.where(cmp_end <= tq1, 0.0, MASKED).astype(BF16))
            else:
                biases.append(jnp.zeros((ch, Q_BLOCK), BF16))
        k_aug = jnp.concatenate([jnp.concatenate(biases, axis=0), kcc_ref[0:n_sub * ch, :]], axis=1)
        scores = _dot(k_aug, q_aug_t)
        parts = []
        for sub in range(n_sub):
            s = scores[sub * ch:(sub + 1) * ch]
            m_sub = jnp.max(s, axis=0, keepdims=True)
            e = jnp.exp2(s - m_sub)
            parts.append((m_sub, jnp.sum(e, axis=0, keepdims=True),
                          _dot(vcct_ref[:, sub * ch:(sub + 1) * ch], e.astype(BF16)), e))
        m = parts[0][0]
        for m_sub, _, _, _ in parts[1:]:
            m = jnp.maximum(m, m_sub)
        scales = [jnp.exp2(m_sub - m) for m_sub, _, _, _ in parts]
        l = scales[0] * parts[0][1]
        acc = scales[0] * parts[0][2]
        for scale, (_, l_sub, acc_sub, _) in zip(scales[1:], parts[1:]):
            l = l + scale * l_sub
            acc = acc + scale * acc_sub
        inv_l = jnp.where(heads(tq1 >= CMP_BLOCK - 1), 1.0 / l, 0.0)
        ocmp_scr[...] = acc * inv_l
        for sub in range(n_sub):
            pc = parts[sub][3] * (scales[sub] * inv_l)
            p_sum = pc[:, :Q_BLOCK]
            for hp in range(1, hpg):
                p_sum = p_sum + pc[:, hp * Q_BLOCK:(hp + 1) * Q_BLOCK]
            imp_scr[IMP_PAD + sub * ch:IMP_PAD + (sub + 1) * ch, :] = p_sum

    for variant in range(n_cmp // ch):
        pl.when(n_chunks == variant + 1)(functools.partial(compressed, variant + 1))
    o_cmp = ocmp_scr[...]

    ratio = SEL_BLOCK // CMP_STRIDE
    sel_shift = SEL_BLOCK.bit_length() - 1
    cur = tq1 >> sel_shift
    n_forced = 3

    span = WINDOW + Q_BLOCK
    k_start = pl.multiple_of(jnp.maximum(t0 - WINDOW, 0), Q_BLOCK)

    def window_scores():
        diff = tq1 - (k_start + lax.broadcasted_iota(jnp.int32, (span, 1), 0))
        inside = pltpu.bitcast(diff, jnp.uint32) < jnp.uint32(WINDOW)
        bias = jnp.where(inside, 0.0, MASKED).astype(BF16)
        wins_scr[...] = _dot(jnp.concatenate([bias, kw_ref[pl.ds(k_start, span), :]], axis=1), q_aug_t)

    def window_values():
        sw = wins_scr[...]
        pw = jnp.exp2(sw - jnp.max(sw, axis=0, keepdims=True)).astype(BF16)
        acc_win = _dot(vwt_ref[:, pl.ds(k_start, span)], pw)
        owin_scr[...] = acc_win[:NSA_HEAD_DIM] / acc_win[NSA_HEAD_DIM:NSA_HEAD_DIM + 1]

    def select_blocks(rows):
        window_scores()
        imp = imp_scr[pl.ds(IMP_PAD - 1, rows, stride=ratio), :]
        for k in range(1, ratio + 1):
            imp = imp + imp_scr[pl.ds(IMP_PAD - 1 + k, rows, stride=ratio), :]
        j = lax.broadcasted_iota(jnp.int32, (rows, Q_BLOCK), 0)
        j_f = j.astype(F32)
        forced = (j == 0) | (j == cur) | (j == cur - 1)
        valid = j <= cur

        def pick(_, sc):
            best = jnp.max(sc, axis=0, keepdims=True)
            idx = jnp.min(jnp.where(sc == best, j_f, float(rows)), axis=0, keepdims=True)
            return jnp.where(j_f == idx, neg_inf, sc)

        left = lax.fori_loop(0, max(min(SEL_TOPK, n_sel) - n_forced, 0), pick,
                             jnp.where(valid & ~forced, imp, neg_inf), unroll=True)
        sel_scr[0:rows, :] = jnp.where(forced | (valid & (left == neg_inf)), 0.0, MASKED)
        if rows < n_sel:
            sel_scr[rows:n_sel, :] = jnp.full((n_sel - rows, Q_BLOCK), MASKED, F32)

    row_step = min(TOPK_ROW_STEP, n_sel)
    last_block = (t0 + Q_BLOCK - 1) >> sel_shift
    for variant in range(n_sel // row_step):
        pl.when(last_block // row_step == variant)(functools.partial(select_blocks, (variant + 1) * row_step))

    blocks_per_tile = kt_size // SEL_BLOCK

    def slc_tile(kt, carry, n_sub, causal_from, after_scores=None):
        m_run, acc = carry
        base = pl.multiple_of(kt * kt_size, kt_size)
        biases = []
        for sub in range(n_sub):
            blk0 = kt * blocks_per_tile + sub * (sub_size // SEL_BLOCK)
            bias = jnp.concatenate(
                [jnp.broadcast_to(sel_scr[pl.ds(blk0 + jb, 1), :], (SEL_BLOCK, Q_BLOCK))
                 for jb in range(sub_size // SEL_BLOCK)], axis=0)
            if sub >= causal_from:
                kpos = base + sub * sub_size + lax.broadcasted_iota(jnp.int32, (sub_size, 1), 0)
                bias = jnp.where(kpos <= tq1, bias, MASKED)
            biases.append(bias.astype(BF16))
        k_aug = jnp.concatenate([jnp.concatenate(biases, axis=0), ks_ref[pl.ds(base, n_sub * sub_size), :]], axis=1)
        scores = _dot(k_aug, q_aug_t)
        if after_scores is not None:
            after_scores()
        parts = []
        for sub in range(n_sub):
            sc = scores[sub * sub_size:(sub + 1) * sub_size]
            m_sub = jnp.max(sc, axis=0, keepdims=True)
            pe = jnp.exp2(sc - m_sub).astype(BF16)
            k0 = pl.multiple_of(base + sub * sub_size, sub_size)
            parts.append((m_sub, _dot(vst_ref[:, pl.ds(k0, sub_size)], pe)))
        m_new = m_run
        for m_sub, _ in parts:
            m_new = jnp.maximum(m_new, m_sub)
        acc = jnp.exp2(m_run - m_new) * acc
        for m_sub, acc_sub in parts:
            acc = acc + jnp.exp2(m_sub - m_new) * acc_sub
        return m_new, acc

    subs_per_tile = kt_size // sub_size
    n_full = t0 // kt_size
    carry = lax.fori_loop(
        0, n_full, functools.partial(slc_tile, n_sub=subs_per_tile, causal_from=subs_per_tile),
        (jnp.full((1, width), MASKED, F32), jnp.zeros((V_ROWS, width), F32)))
    tail_step = min(2, subs_per_tile)
    tail_len = (t0 - n_full * kt_size) // (tail_step * sub_size)
    for variant in range(subs_per_tile // tail_step):
        n_sub = tail_step * (variant + 1)

        @pl.when(tail_len == variant)
        def _(n_sub=n_sub):
            _, acc_slc = slc_tile(n_full, carry, n_sub=n_sub, causal_from=n_sub - tail_step,
                                  after_scores=window_values)
            oslc_scr[...] = acc_slc[:NSA_HEAD_DIM] / acc_slc[NSA_HEAD_DIM:NSA_HEAD_DIM + 1]

    o_slc = oslc_scr[...]
    o_win = owin_scr[...]

    for hp in range(hpg):
        cols = slice(hp * Q_BLOCK, (hp + 1) * Q_BLOCK)
        row = (grp * hpg + hp) * 3
        o_t = (gt_ref[qi, pl.ds(row, 1), :] * o_cmp[:, cols] + gt_ref[qi, pl.ds(row + 1, 1), :] * o_slc[:, cols]
               + gt_ref[qi, pl.ds(row + 2, 1), :] * o_win[:, cols])
        o_ref[pl.ds(pl.multiple_of(qi * Q_BLOCK, Q_BLOCK), Q_BLOCK), hp * NSA_HEAD_DIM:(hp + 1) * NSA_HEAD_DIM] = (
            o_t.T.astype(o_ref.dtype))


def _nsa_attention(qs, kcc, vcct, ks, vst, kw, vwt, gt, B, S, blocks=4, kt_size=4096, sub_size=256, cmp_chunk=256):
    G, Dh, nq, hpg = NSA_KV_GROUPS, NSA_HEAD_DIM, S // Q_BLOCK, NSA_HEADS_PER_GROUP
    kt_size = min(kt_size, S)
    blocks = min(blocks, nq)
    n_cmp, n_sel = S // CMP_STRIDE, S // SEL_BLOCK
    cmp_chunk = min(cmp_chunk, n_cmp)
    per_group = lambda shape: pl.BlockSpec((None, None) + shape, lambda b, g, i: (b, g, 0, 0))
    return pl.pallas_call(
        functools.partial(_nsa_body, blocks=blocks, kt_size=kt_size, sub_size=min(sub_size, kt_size),
                          cmp_chunk=cmp_chunk),
        grid=(B, G, nq // blocks),
        in_specs=[
            pl.BlockSpec((None, None, Dh, blocks * hpg * Q_BLOCK), lambda b, g, i: (b, g, 0, i)),
            per_group((n_cmp, Dh)), per_group((Dh, n_cmp)),
            per_group((S, Dh)), per_group((V_ROWS, S)),
            per_group((S, Dh)), per_group((V_ROWS, S)),
            pl.BlockSpec((None, blocks, GATE_ROWS, Q_BLOCK), lambda b, g, i: (b, i, 0, 0)),
        ],
        out_specs=pl.BlockSpec((None, blocks * Q_BLOCK, hpg * Dh), lambda b, g, i: (b, i, g)),
        out_shape=jax.ShapeDtypeStruct((B, S, NSA_HEADS * Dh), BF16),
        scratch_shapes=[
            pltpu.VMEM((n_cmp + 2 * IMP_PAD, Q_BLOCK), F32),
            pltpu.VMEM((n_sel, Q_BLOCK), F32),
            pltpu.VMEM((Dh, hpg * Q_BLOCK), F32),
            pltpu.VMEM((WINDOW + Q_BLOCK, hpg * Q_BLOCK), F32),
            pltpu.VMEM((Dh, hpg * Q_BLOCK), F32),
            pltpu.VMEM((Dh, hpg * Q_BLOCK), F32),
        ],
        compiler_params=pltpu.CompilerParams(
            dimension_semantics=("parallel", "parallel", "arbitrary"), vmem_limit_bytes=VMEM_LIMIT),
        name="nsa_attention",
    )(qs, kcc, vcct, ks, vst, kw, vwt, gt)


def _merge_body(u_ref, halo_ref, pw_ref, ps_ref, yb_ref, yc_ref, ga_ref, gb_ref, gc_ref, h_ref, wb_ref, wo_ref, o_ref,
                *, tm, tiles_per_seq):
    tile_in_seq = lax.rem(pl.program_id(0), tiles_per_seq)
    halo = jnp.where(tile_in_seq == 0, 0.0, halo_ref[...].astype(F32))
    t = tile_in_seq * tm + lax.broadcasted_iota(jnp.int32, (tm, 1), 0)
    ya = _pool_tile(u_ref[...].astype(F32), halo, t, pw_ref, ps_ref)
    merged = jax.nn.sigmoid(ga_ref[...].astype(F32)) * _dot(ya, wb_ref[0])
    merged = merged + jax.nn.sigmoid(gb_ref[...].astype(F32)) * _dot(yb_ref[...], wb_ref[1])
    merged = merged + jax.nn.sigmoid(gc_ref[...].astype(F32)) * _dot(yc_ref[...], wb_ref[2])
    o_ref[...] = h_ref[...] + _dot(merged.astype(BF16), wo_ref[...])


def _merge(main, pool_w, pool_scale, yb, yc, h, w_branch, w_out, S, tm=1024):
    T, D = h.shape
    tm = min(tm, S)
    halo_blocks = tm // POOL_HALO
    y_spec = pl.BlockSpec((tm, 512), lambda i: (i, 0))
    gate_spec = lambda k: pl.BlockSpec((tm, D), lambda i: (i, MAIN_GM // D + k))
    return pl.pallas_call(
        functools.partial(_merge_body, tm=tm, tiles_per_seq=S // tm),
        grid=(T // tm,),
        in_specs=[
            pl.BlockSpec((tm, 512), lambda i: (i, MAIN_U // 512)),
            pl.BlockSpec((POOL_HALO, 512), lambda i: (jnp.maximum(i * halo_blocks - 1, 0), MAIN_U // 512)),
            pl.BlockSpec((len(POOL_WINDOWS), POOL_GROUP_DIM, POOL_GROUP_DIM), lambda i: (0, 0, 0)),
            pl.BlockSpec((1, 512), lambda i: (0, 0)),
            y_spec, y_spec, gate_spec(0), gate_spec(1), gate_spec(2),
            pl.BlockSpec((tm, D), lambda i: (i, 0)),
            pl.BlockSpec((3, 512, D), lambda i: (0, 0, 0)),
            pl.BlockSpec((D, D), lambda i: (0, 0)),
        ],
        out_specs=pl.BlockSpec((tm, D), lambda i: (i, 0)),
        out_shape=jax.ShapeDtypeStruct((T, D), F32),
        compiler_params=pltpu.CompilerParams(dimension_semantics=("parallel",), vmem_limit_bytes=VMEM_LIMIT),
        name="merge_out",
    )(main, main, pool_w, pool_scale, yb, yc, main, main, main, h, w_branch, w_out)


def _ffn_body(h_ref, gf_ref, w1_ref, w2_ref, gp_ref, wg_ref, p_ref, wp_ref, gl_ref, o_ref, f_scr, acc_scr, *, final_norm):
    j = pl.program_id(1)

    @pl.when(j == 0)
    def _():
        f_scr[...] = _rms(h_ref[...], gf_ref[...]).astype(BF16)
        acc_scr[...] = jnp.zeros_like(acc_scr)

    a = jnp.maximum(_dot(f_scr[...], w1_ref[...]), 0.0)
    acc_scr[...] += _dot((a * a).astype(BF16), w2_ref[...])

    @pl.when(j == pl.num_programs(1) - 1)
    def _():
        h2 = h_ref[...] + acc_scr[...]
        gate = jax.nn.sigmoid(_dot(_rms(h2, gp_ref[...]).astype(BF16), wg_ref[...]))
        h3 = h2 + gate * _dot(p_ref[...].astype(BF16), wp_ref[...])
        o_ref[...] = _rms(h3, gl_ref[...]) if final_norm else h3


def _ffn_ple(h, norm_ffn, w1, w2, norm_ple, w_gate, p, w_proj, norm_last, final_norm, tm=1024, tf=1024):
    T, D = h.shape
    tm = min(tm, T)
    Fdim = w1.shape[1]
    Pdim = p.shape[1]
    vec = pl.BlockSpec((1, D), lambda i, j: (0, 0))
    return pl.pallas_call(
        functools.partial(_ffn_body, final_norm=final_norm),
        grid=(T // tm, Fdim // tf),
        in_specs=[
            pl.BlockSpec((tm, D), lambda i, j: (i, 0)), vec,
            pl.BlockSpec((D, tf), lambda i, j: (0, j)),
            pl.BlockSpec((tf, D), lambda i, j: (j, 0)),
            vec,
            pl.BlockSpec((D, D), lambda i, j: (0, 0)),
            pl.BlockSpec((tm, Pdim), lambda i, j: (i, 0)),
            pl.BlockSpec((Pdim, D), lambda i, j: (0, 0)),
            vec,
        ],
        out_specs=pl.BlockSpec((tm, D), lambda i, j: (i, 0)),
        out_shape=jax.ShapeDtypeStruct((T, D), F32),
        scratch_shapes=[pltpu.VMEM((tm, D), BF16), pltpu.VMEM((tm, D), F32)],
        compiler_params=pltpu.CompilerParams(
            dimension_semantics=("parallel", "arbitrary"), vmem_limit_bytes=VMEM_LIMIT),
        name="ffn_ple",
    )(h, norm_ffn, w1, w2, norm_ple, w_gate, p, w_proj, norm_last)


def _split_w_in(w):
    parts, off = [], 0
    for sz in IN_SIZES:
        parts.append(w[:, off:off + sz])
        off += sz
    u, q_nsa, kv_nsa, g_nsa, q_gla, k_gla, v_gla, a_gla, r_gla, g_merge = parts
    D = w.shape[0]
    pad = lambda n: jnp.zeros((D, n), w.dtype)
    main = jnp.concatenate([u, q_gla, k_gla, v_gla, r_gla, g_merge, a_gla, pad(MAIN_WIDTH - MAIN_A - GLA_GATE_RANK)], axis=1)
    kv = [kv_nsa[:, s * LANES:(s + 1) * LANES] for s in range(6)]
    nsa = jnp.concatenate([q_nsa, kv[0], kv[2], kv[4], kv[1], kv[3], kv[5], g_nsa, pad(NSA_WIDTH - NSA_G - 24)], axis=1)
    return main.astype(BF16), nsa.astype(BF16)


def kernel(x, p, positions, norm_mix, w_in, pool_w, pool_scale, cmp_pos_k, cmp_w_k, cmp_pos_v, cmp_w_v, gla_w_gate, gla_b_gate, gla_norm, w_branch, w_out, norm_ffn, w_ff1, w_ff2, norm_ple, w_ple_gate, w_ple_proj, norm_final):
    B, S, D = x.shape
    depth = w_in.shape[0]
    T = B * S
    Dh = NSA_HEAD_DIM
    row = lambda v: v.reshape(1, -1).astype(F32)

    half = NSA_ROT_DIM // 2
    inv_freq = jnp.power(ROPE_THETA, -jnp.arange(half, dtype=F32) * (2.0 / NSA_ROT_DIM)).reshape(half, 1)
    pos_rows = positions.reshape(B, S // Q_BLOCK, 1, Q_BLOCK)
    rope_spread = _rope_spread_matrix()

    h = x.reshape(T, D)
    for i in range(depth):
        w_main, w_nsa = _split_w_in(w_in[i])
        main = _norm_proj(h, row(norm_mix[i]), w_main, BF16, tn=MAIN_WIDTH // 3).reshape(B, S, MAIN_WIDTH)
        nsa_proj = _norm_proj(h, row(norm_mix[i]), w_nsa, F32, tn=NSA_WIDTH).reshape(B, S, NSA_WIDTH)

        wg = jnp.zeros((LANES, GLA_HEADS * GLA_DK), BF16).at[:GLA_GATE_RANK].set(gla_w_gate[i].astype(BF16))
        y_c = _gla_mixer(main, wg, row(gla_b_gate[i]), row(gla_norm[i]), B, S)

        qs, kc, vc, ks, kw, vst, vwt, gt = _nsa_prep(nsa_proj, pos_rows, inv_freq, rope_spread, B, S)
        kcc = _compress(kc, cmp_w_k[i].reshape(CMP_BLOCK, Dh, Dh).astype(BF16),
                        cmp_pos_k[i].reshape(CMP_BLOCK, 1, Dh).astype(F32), transpose_out=False)
        vcct = _compress(vc, cmp_w_v[i].reshape(CMP_BLOCK, Dh, Dh).astype(BF16),
                         cmp_pos_v[i].reshape(CMP_BLOCK, 1, Dh).astype(F32), transpose_out=True)
        y_b = _nsa_attention(qs, kcc, vcct, ks, vst, kw, vwt, gt, B, S)

        h = _merge(main.reshape(T, MAIN_WIDTH), pool_w[i].astype(BF16), row(pool_scale[i]), y_b.reshape(T, -1),
                   y_c.reshape(T, -1), h, w_branch[i].astype(BF16), w_out[i].astype(BF16), S)
        h = _ffn_ple(h, row(norm_ffn[i]), w_ff1[i].astype(BF16), w_ff2[i].astype(BF16), row(norm_ple[i]),
                     w_ple_gate[i].astype(BF16), p[i].reshape(T, -1), w_ple_proj[i].astype(BF16),
                     row(norm_final), final_norm=(i == depth - 1))
    return h.reshape(B, S, D)
```

```python
import functools

import jax
import jax.numpy as jnp
import numpy as np
from jax import lax
from jax.experimental import pallas as pl
from jax.experimental.pallas import tpu as pltpu

F32, BF16 = jnp.float32, jnp.bfloat16

EPS = 1e-6
ROPE_THETA = 500000.0
POOL_WINDOWS = (2, 4, 8, 16)
POOL_GROUP_DIM = 128
POOL_HALO = max(POOL_WINDOWS)
NSA_HEADS = 8
NSA_KV_GROUPS = 2
NSA_HEADS_PER_GROUP = NSA_HEADS // NSA_KV_GROUPS
NSA_HEAD_DIM = 64
NSA_ROT_DIM = 16
CMP_BLOCK = 32
CMP_STRIDE = 16
SEL_BLOCK = 64
SEL_TOPK = 16
WINDOW = 512
Q_BLOCK = 128
GLA_HEADS = 4
GLA_DK = 64
GLA_DV = 128
GLA_GATE_RANK = 16
GLA_TAU = 16.0
GLA_CHUNK = 64
IN_SIZES = (512, 512, 768, 24, 256, 256, 512, 16, 512, 3072)

LANES = 128
MASKED = -1e30
LOG2E = 1.4426950408889634
Q_SCALE = NSA_HEAD_DIM ** -0.5 * LOG2E
V_ROWS = NSA_HEAD_DIM + 16
IMP_PAD = 8
GATE_ROWS = 32
TOPK_ROW_STEP = 64
VMEM_LIMIT = 56 * 1024 * 1024

MAIN_U, MAIN_QG, MAIN_KG, MAIN_VG, MAIN_RG, MAIN_GM, MAIN_A, MAIN_WIDTH = 0, 512, 768, 1024, 1536, 2048, 5120, 5376
NSA_Q, NSA_KC, NSA_KS, NSA_KW, NSA_VC, NSA_VS, NSA_VW, NSA_G, NSA_WIDTH = 0, 512, 640, 768, 896, 1024, 1152, 1280, 1536


def _nt(a, b):
    return lax.dot_general(a, b, (((1,), (1,)), ((), ())), preferred_element_type=F32)


def _tn(a, b):
    return lax.dot_general(a, b, (((0,), (0,)), ((), ())), preferred_element_type=F32)


def _dot(a, b):
    return jnp.dot(a, b, preferred_element_type=F32)


def _rms(x, gain):
    return x * lax.rsqrt(jnp.mean(x * x, axis=-1, keepdims=True) + EPS) * gain


def _proj_body(h_ref, g_ref, w_ref, o_ref, a_scr):
    @pl.when(pl.program_id(1) == 0)
    def _():
        a_scr[...] = _rms(h_ref[...], g_ref[...]).astype(BF16)

    o_ref[...] = _dot(a_scr[...], w_ref[...]).astype(o_ref.dtype)


def _norm_proj(h, gain, w, out_dtype, tn, tm=1024):
    T, D = h.shape
    N = w.shape[1]
    tm = min(tm, T)
    return pl.pallas_call(
        _proj_body,
        grid=(T // tm, N // tn),
        in_specs=[
            pl.BlockSpec((tm, D), lambda i, j: (i, 0)),
            pl.BlockSpec((1, D), lambda i, j: (0, 0)),
            pl.BlockSpec((D, tn), lambda i, j: (0, j)),
        ],
        out_specs=pl.BlockSpec((tm, tn), lambda i, j: (i, j)),
        out_shape=jax.ShapeDtypeStruct((T, N), out_dtype),
        scratch_shapes=[pltpu.VMEM((tm, D), BF16)],
        compiler_params=pltpu.CompilerParams(
            dimension_semantics=("parallel", "arbitrary"), vmem_limit_bytes=VMEM_LIMIT),
        name="norm_proj",
    )(h, gain, w)


def _pool_tile(cur, halo, t, w_ref, sc_ref):
    outs = []
    for g, win in enumerate(POOL_WINDOWS):
        lo, hi = g * POOL_GROUP_DIM, (g + 1) * POOL_GROUP_DIM
        x = cur[:, lo:hi]
        e = jnp.concatenate([halo[:, lo:hi], x], axis=0)
        step = 1
        while step < win:
            e = e[step:] + e[:-step]
            step *= 2
        wsum = e[POOL_HALO - (win - 1):]
        cnt = jnp.minimum(t + 1, win).astype(F32)
        pooled = wsum / cnt - x
        outs.append((_dot(pooled.astype(BF16), w_ref[g]) * sc_ref[:, lo:hi]).astype(BF16))
    return jnp.concatenate(outs, axis=1)


def _gla_body(q_ref, k_ref, v_ref, r_ref, a_ref, wg_ref, bg_ref, ng_ref, o_ref, st_ref, *, tc):
    @pl.when(pl.program_id(1) == 0)
    def _():
        st_ref[...] = jnp.zeros_like(st_ref)

    C = GLA_CHUNK
    n_chunks = tc // C
    causal = lax.broadcasted_iota(jnp.int32, (C, C), 0) >= lax.broadcasted_iota(jnp.int32, (C, C), 1)
    width = GLA_HEADS * GLA_DK
    hk = [slice(h * GLA_DK, (h + 1) * GLA_DK) for h in range(GLA_HEADS)]
    hv = [slice(h * GLA_DV, (h + 1) * GLA_DV) for h in range(GLA_HEADS)]
    rows = [slice(c * C, (c + 1) * C) for c in range(n_chunks)]

    z = _dot(a_ref[...], wg_ref[...]) + bg_ref[...]
    b = jax.nn.log_sigmoid(z) * (1.0 / GLA_TAU)
    row_in_chunk = lax.broadcasted_iota(jnp.int32, (tc, 1), 0) % C
    step = 1
    while step < C:
        shifted = jnp.concatenate([jnp.zeros((step, width), F32), b[:-step]], axis=0)
        b = b + jnp.where(row_in_chunk >= step, shifted, 0.0)
        step *= 2
    b_last = [b[(c + 1) * C - 1:(c + 1) * C, :] for c in range(n_chunks)]
    b_last_rows = jnp.concatenate([jnp.broadcast_to(bl, (C, width)) for bl in b_last], axis=0)
    qf = q_ref[...].astype(F32) * (GLA_DK ** -0.5)
    kf = k_ref[...].astype(F32)
    q_s = (qf * jnp.exp(b)).astype(BF16)
    k_s = (kf * jnp.exp(-b)).astype(BF16)
    k_t = (kf * jnp.exp(b_last_rows - b)).astype(BF16)

    att = [[jnp.where(causal, _nt(q_s[rows[c], hk[h]], k_s[rows[c], hk[h]]), 0.0).astype(BF16)
            for h in range(GLA_HEADS)] for c in range(n_chunks)]
    o_intra = [[_dot(att[c][h], v_ref[rows[c], hv[h]]) for h in range(GLA_HEADS)] for c in range(n_chunks)]
    kv = [[_tn(v_ref[rows[c], hv[h]], k_t[rows[c], hk[h]]) for h in range(GLA_HEADS)] for c in range(n_chunks)]

    state_t = [st_ref[h] for h in range(GLA_HEADS)]
    for c in range(n_chunks):
        decay = jnp.exp(b_last[c])
        for h in range(GLA_HEADS):
            o = o_intra[c][h] + _nt(q_s[rows[c], hk[h]], state_t[h].astype(BF16))
            state_t[h] = state_t[h] * decay[:, hk[h]] + kv[c][h]
            o = _rms(o, ng_ref[...])
            r = r_ref[rows[c], hv[h]].astype(F32)
            o_ref[rows[c], hv[h]] = (o * (r * jax.nn.sigmoid(r))).astype(o_ref.dtype)
    for h in range(GLA_HEADS):
        st_ref[h] = state_t[h]


def _gla_mixer(main, w_gate, b_gate, norm_g, B, S, tc=512):
    tc = min(tc, S)
    return pl.pallas_call(
        functools.partial(_gla_body, tc=tc),
        grid=(B, S // tc),
        in_specs=[
            pl.BlockSpec((None, tc, 256), lambda b, i: (b, i, MAIN_QG // 256)),
            pl.BlockSpec((None, tc, 256), lambda b, i: (b, i, MAIN_KG // 256)),
            pl.BlockSpec((None, tc, 512), lambda b, i: (b, i, MAIN_VG // 512)),
            pl.BlockSpec((None, tc, 512), lambda b, i: (b, i, MAIN_RG // 512)),
            pl.BlockSpec((None, tc, 128), lambda b, i: (b, i, MAIN_A // 128)),
            pl.BlockSpec((128, 256), lambda b, i: (0, 0)),
            pl.BlockSpec((1, 256), lambda b, i: (0, 0)),
            pl.BlockSpec((1, 128), lambda b, i: (0, 0)),
        ],
        out_specs=pl.BlockSpec((None, tc, 512), lambda b, i: (b, i, 0)),
        out_shape=jax.ShapeDtypeStruct((B, S, 512), BF16),
        scratch_shapes=[pltpu.VMEM((GLA_HEADS, GLA_DV, GLA_DK), F32)],
        compiler_params=pltpu.CompilerParams(dimension_semantics=("parallel", "arbitrary")),
        name="gla_mixer",
    )(main, main, main, main, main, w_gate, b_gate, norm_g)


def _nsa_prep_body(x_ref, pos_ref, invf_ref, spread_ref, qs_ref, kc_ref, vc_ref, ks_ref, kw_ref, vst_ref, vwt_ref,
                   gt_ref, *, blocks):
    half = NSA_ROT_DIM // 2
    hpg = NSA_HEADS_PER_GROUP
    ones_row = (lax.broadcasted_iota(jnp.int32, (V_ROWS - NSA_HEAD_DIM, Q_BLOCK), 0) == 0).astype(BF16)

    def pieces(a):
        p1 = a.astype(BF16).astype(F32)
        p2 = (a - p1).astype(BF16).astype(F32)
        return [p1, p2, (a - p1 - p2).astype(BF16).astype(F32)]

    for blk in range(blocks):
        tok = slice(blk * Q_BLOCK, (blk + 1) * Q_BLOCK)
        ang = invf_ref[...] * pos_ref[blk].astype(F32)
        lhs = jnp.concatenate(
            pieces(jnp.cos(ang)) + pieces(jnp.sin(ang)) + [jnp.ones((2 * half, Q_BLOCK), F32)], axis=0)
        tables = _tn(lhs.astype(BF16), spread_ref[...])
        cs = tables[:, :LANES]
        s_lo = tables[:, LANES:2 * LANES]
        s_hi = tables[:, 2 * LANES:]

        def rope(x):
            return x * cs + pltpu.roll(x, LANES - half, 1) * s_lo + pltpu.roll(x, half, 1) * s_hi

        def split_groups(x, ref, dtype):
            ref[0, tok, :] = x[:, :NSA_HEAD_DIM].astype(dtype)
            ref[1, tok, :] = x[:, NSA_HEAD_DIM:].astype(dtype)

        for pair in range(NSA_HEADS // 2):
            cols = slice(NSA_Q + pair * LANES, NSA_Q + (pair + 1) * LANES)
            qr_t = (rope(x_ref[tok, cols]) * Q_SCALE).T
            for sub in range(2):
                head = 2 * pair + sub
                grp, hp = head // hpg, head % hpg
                col0 = (blk * hpg + hp) * Q_BLOCK
                qs_ref[grp, :, col0:col0 + Q_BLOCK] = qr_t[sub * NSA_HEAD_DIM:(sub + 1) * NSA_HEAD_DIM].astype(BF16)
        split_groups(rope(x_ref[tok, NSA_KC:NSA_KC + LANES]), kc_ref, F32)
        split_groups(rope(x_ref[tok, NSA_KS:NSA_KS + LANES]), ks_ref, BF16)
        split_groups(rope(x_ref[tok, NSA_KW:NSA_KW + LANES]), kw_ref, BF16)
        split_groups(x_ref[tok, NSA_VC:NSA_VC + LANES], vc_ref, F32)
        vst = x_ref[tok, NSA_VS:NSA_VS + LANES].T
        vwt = x_ref[tok, NSA_VW:NSA_VW + LANES].T
        for grp in range(NSA_KV_GROUPS):
            for ref, v in ((vst_ref, vst), (vwt_ref, vwt)):
                ref[grp, :NSA_HEAD_DIM, tok] = v[grp * NSA_HEAD_DIM:(grp + 1) * NSA_HEAD_DIM].astype(BF16)
                ref[grp, NSA_HEAD_DIM:, tok] = ones_row
        gt_ref[blk] = jax.nn.sigmoid(x_ref[tok, NSA_G:NSA_G + LANES]).T[:GATE_ROWS]


def _rope_spread_matrix():
    half = NSA_ROT_DIM // 2
    lane = np.arange(LANES)
    d = lane % NSA_HEAD_DIM
    m = np.zeros((8 * half, 3 * LANES), np.float32)
    for piece in range(3):
        for f in range(half):
            hit = d % half == f
            m[piece * half + f, :LANES] = hit & (d < NSA_ROT_DIM)
            m[(3 + piece) * half + f, LANES:2 * LANES] = -1.0 * (hit & (d < half))
            m[(3 + piece) * half + f, 2 * LANES:] = hit & (d >= half) & (d < NSA_ROT_DIM)
    m[6 * half, :LANES] = d >= NSA_ROT_DIM
    return jnp.asarray(m, BF16)


def _nsa_prep(nsa_proj, positions, inv_freq, rope_spread, B, S, blocks=4):
    G, Dh, nq = NSA_KV_GROUPS, NSA_HEAD_DIM, S // Q_BLOCK
    blocks = min(blocks, nq)
    ts = blocks * Q_BLOCK
    tok = lambda dtype: jax.ShapeDtypeStruct((B, G, S, Dh), dtype)
    tok_spec = pl.BlockSpec((None, G, ts, Dh), lambda b, i: (b, 0, i, 0))
    tr = jax.ShapeDtypeStruct((B, G, V_ROWS, S), BF16)
    tr_spec = pl.BlockSpec((None, G, V_ROWS, ts), lambda b, i: (b, 0, 0, i))
    return pl.pallas_call(
        functools.partial(_nsa_prep_body, blocks=blocks),
        grid=(B, nq // blocks),
        in_specs=[
            pl.BlockSpec((None, ts, NSA_WIDTH), lambda b, i: (b, i, 0)),
            pl.BlockSpec((None, blocks, 1, Q_BLOCK), lambda b, i: (b, i, 0, 0)),
            pl.BlockSpec(inv_freq.shape, lambda b, i: (0, 0)),
            pl.BlockSpec(rope_spread.shape, lambda b, i: (0, 0)),
        ],
        out_specs=[
            pl.BlockSpec((None, G, Dh, NSA_HEADS_PER_GROUP * ts), lambda b, i: (b, 0, 0, i)),
            tok_spec, tok_spec, tok_spec, tok_spec, tr_spec, tr_spec,
            pl.BlockSpec((None, blocks, GATE_ROWS, Q_BLOCK), lambda b, i: (b, i, 0, 0)),
        ],
        out_shape=[
            jax.ShapeDtypeStruct((B, G, Dh, NSA_HEADS_PER_GROUP * S), BF16),
            tok(F32), tok(F32), tok(BF16), tok(BF16), tr, tr,
            jax.ShapeDtypeStruct((B, nq, GATE_ROWS, Q_BLOCK), F32),
        ],
        compiler_params=pltpu.CompilerParams(dimension_semantics=("parallel", "parallel")),
        name="nsa_prep",
    )(nsa_proj, positions, inv_freq, rope_spread)


def _cmp_body(x_ref, w_ref, p_ref, o_ref, *, n_blk, transpose_out):
    half = CMP_BLOCK // 2
    first = jnp.zeros((n_blk, NSA_HEAD_DIM), F32)
    second = jnp.zeros((n_blk, NSA_HEAD_DIM), F32)
    for l in range(half):
        rows = x_ref[pl.ds(l, n_blk, stride=CMP_STRIDE), :]
        first = first + _dot((rows + p_ref[l]).astype(BF16), w_ref[l])
        second = second + _dot((rows + p_ref[half + l]).astype(BF16), w_ref[half + l])
    out = first + jnp.concatenate([second[1:], jnp.zeros((1, NSA_HEAD_DIM), F32)], axis=0)
    o_ref[...] = (out.T if transpose_out else out).astype(o_ref.dtype)


def _compress(x, w, pos, transpose_out):
    B, G, S, Dh = x.shape
    n_blk = S // CMP_STRIDE
    out_block = (None, None, Dh, n_blk) if transpose_out else (None, None, n_blk, Dh)
    out_shape = (B, G, Dh, n_blk) if transpose_out else (B, G, n_blk, Dh)
    return pl.pallas_call(
        functools.partial(_cmp_body, n_blk=n_blk, transpose_out=transpose_out),
        grid=(B, G),
        in_specs=[
            pl.BlockSpec((None, None, S, Dh), lambda b, g: (b, g, 0, 0)),
            pl.BlockSpec((CMP_BLOCK, Dh, Dh), lambda b, g: (0, 0, 0)),
            pl.BlockSpec((CMP_BLOCK, 1, Dh), lambda b, g: (0, 0, 0)),
        ],
        out_specs=pl.BlockSpec(out_block, lambda b, g: (b, g, 0, 0)),
        out_shape=jax.ShapeDtypeStruct(out_shape, BF16),
        compiler_params=pltpu.CompilerParams(
            dimension_semantics=("parallel", "parallel"), vmem_limit_bytes=VMEM_LIMIT),
        name="nsa_compress",
    )(x, w, pos)


def _nsa_body(*refs, blocks, **tiles):
    def one_block(qi, carry):
        _nsa_block(qi, *refs, blocks=blocks, **tiles)
        return carry

    lax.fori_loop(0, blocks, one_block, 0)


def _nsa_block(qi, q_ref, kcc_ref, vcct_ref, ks_ref, vst_ref, kw_ref, vwt_ref, gt_ref, o_ref,
               imp_scr, sel_scr, ocmp_scr, wins_scr, owin_scr, oslc_scr, *, blocks, kt_size, sub_size, cmp_chunk):
    grp = pl.program_id(1)
    t0 = (pl.program_id(2) * blocks + qi) * Q_BLOCK
    hpg = NSA_HEADS_PER_GROUP
    n_cmp = kcc_ref.shape[0]
    n_sel = sel_scr.shape[0]
    q_t = q_ref[:, pl.ds(pl.multiple_of(qi * (hpg * Q_BLOCK), hpg * Q_BLOCK), hpg * Q_BLOCK)]
    tq1 = t0 + lax.broadcasted_iota(jnp.int32, (1, Q_BLOCK), 1)
    heads = lambda x: jnp.concatenate([x] * hpg, axis=1)

    width = hpg * Q_BLOCK
    neg_inf = -jnp.inf
    eye = (lax.broadcasted_iota(jnp.int32, (Q_BLOCK, Q_BLOCK), 0)
           == lax.broadcasted_iota(jnp.int32, (Q_BLOCK, Q_BLOCK), 1)).astype(BF16)
    q_aug_t = jnp.concatenate([jnp.concatenate([eye] * hpg, axis=1), q_t], axis=0)

    ch = cmp_chunk
    n_chunks = (((t0 + Q_BLOCK - CMP_BLOCK) >> (CMP_STRIDE.bit_length() - 1)) + ch) // ch

    def compressed(n_sub):
        imp_scr[0:IMP_PAD, :] = jnp.zeros((IMP_PAD, Q_BLOCK), F32)
        imp_scr[IMP_PAD + n_sub * ch:, :] = jnp.zeros((n_cmp + IMP_PAD - n_sub * ch, Q_BLOCK), F32)
        biases = []
        for sub in range(n_sub):
            if sub >= n_sub - 2:
                cmp_end = (sub * ch + lax.broadcasted_iota(jnp.int32, (ch, 1), 0)) * CMP_STRIDE + (CMP_BLOCK - 1)
                biases.append(jnp.where(cmp_end <= tq1, 0.0, MASKED).astype(BF16))
            else:
                biases.append(jnp.zeros((ch, Q_BLOCK), BF16))
        k_aug = jnp.concatenate([jnp.concatenate(biases, axis=0), kcc_ref[0:n_sub * ch, :]], axis=1)
        scores = _dot(k_aug, q_aug_t)
        parts = []
        for sub in range(n_sub):
            s = scores[sub * ch:(sub + 1) * ch]
            m_sub = jnp.max(s, axis=0, keepdims=True)
            e = jnp.exp2(s - m_sub)
            parts.append((m_sub, jnp.sum(e, axis=0, keepdims=True),
                          _dot(vcct_ref[:, sub * ch:(sub + 1) * ch], e.astype(BF16)), e))
        m = parts[0][0]
        for m_sub, _, _, _ in parts[1:]:
            m = jnp.maximum(m, m_sub)
        scales = [jnp.exp2(m_sub - m) for m_sub, _, _, _ in parts]
        l = scales[0] * parts[0][1]
        acc = scales[0] * parts[0][2]
        for scale, (_, l_sub, acc_sub, _) in zip(scales[1:], parts[1:]):
            l = l + scale * l_sub
            acc = acc + scale * acc_sub
        inv_l = jnp.where(heads(tq1 >= CMP_BLOCK - 1), 1.0 / l, 0.0)
        ocmp_scr[...] = acc * inv_l
        for sub in range(n_sub):
            pc = parts[sub][3] * (scales[sub] * inv_l)
            p_sum = pc[:, :Q_BLOCK]
            for hp in range(1, hpg):
                p_sum = p_sum + pc[:, hp * Q_BLOCK:(hp + 1) * Q_BLOCK]
            imp_scr[IMP_PAD + sub * ch:IMP_PAD + (sub + 1) * ch, :] = p_sum

    for variant in range(n_cmp // ch):
        pl.when(n_chunks == variant + 1)(functools.partial(compressed, variant + 1))
    o_cmp = ocmp_scr[...]

    ratio = SEL_BLOCK // CMP_STRIDE
    sel_shift = SEL_BLOCK.bit_length() - 1
    cur = tq1 >> sel_shift
    n_forced = 3

    span = WINDOW + Q_BLOCK
    k_start = pl.multiple_of(jnp.maximum(t0 - WINDOW, 0), Q_BLOCK)

    def window_scores():
        diff = tq1 - (k_start + lax.broadcasted_iota(jnp.int32, (span, 1), 0))
        inside = pltpu.bitcast(diff, jnp.uint32) < jnp.uint32(WINDOW)
        bias = jnp.where(inside, 0.0, MASKED).astype(BF16)
        wins_scr[...] = _dot(jnp.concatenate([bias, kw_ref[pl.ds(k_start, span), :]], axis=1), q_aug_t)

    def window_values():
        sw = wins_scr[...]
        pw = jnp.exp2(sw - jnp.max(sw, axis=0, keepdims=True)).astype(BF16)
        acc_win = _dot(vwt_ref[:, pl.ds(k_start, span)], pw)
        owin_scr[...] = acc_win[:NSA_HEAD_DIM] / acc_win[NSA_HEAD_DIM:NSA_HEAD_DIM + 1]

    def select_blocks(rows):
        window_scores()
        imp = imp_scr[pl.ds(IMP_PAD - 1, rows, stride=ratio), :]
        for k in range(1, ratio + 1):
            imp = imp + imp_scr[pl.ds(IMP_PAD - 1 + k, rows, stride=ratio), :]
        j = lax.broadcasted_iota(jnp.int32, (rows, Q_BLOCK), 0)
        j_f = j.astype(F32)
        forced = (j == 0) | (j == cur) | (j == cur - 1)
        valid = j <= cur

        def pick(_, sc):
            best = jnp.max(sc, axis=0, keepdims=True)
            idx = jnp.min(jnp.where(sc == best, j_f, float(rows)), axis=0, keepdims=True)
            return jnp.where(j_f == idx, neg_inf, sc)

        left = lax.fori_loop(0, max(min(SEL_TOPK, n_sel) - n_forced, 0), pick,
                             jnp.where(valid & ~forced, imp, neg_inf), unroll=True)
        sel_scr[0:rows, :] = jnp.where(forced | (valid & (left == neg_inf)), 0.0, MASKED)
        if rows < n_sel:
            sel_scr[rows:n_sel, :] = jnp.full((n_sel - rows, Q_BLOCK), MASKED, F32)

    row_step = min(TOPK_ROW_STEP, n_sel)
    last_block = (t0 + Q_BLOCK - 1) >> sel_shift
    for variant in range(n_sel // row_step):
        pl.when(last_block // row_step == variant)(functools.partial(select_blocks, (variant + 1) * row_step))

    blocks_per_tile = kt_size // SEL_BLOCK

    def slc_tile(kt, carry, n_sub, causal_from, after_scores=None):
        m_run, acc = carry
        base = pl.multiple_of(kt * kt_size, kt_size)
        biases = []
        for sub in range(n_sub):
            blk0 = kt * blocks_per_tile + sub * (sub_size // SEL_BLOCK)
            bias = jnp.concatenate(
                [jnp.broadcast_to(sel_scr[pl.ds(blk0 + jb, 1), :], (SEL_BLOCK, Q_BLOCK))
                 for jb in range(sub_size // SEL_BLOCK)], axis=0)
            if sub >= causal_from:
                kpos = base + sub * sub_size + lax.broadcasted_iota(jnp.int32, (sub_size, 1), 0)
                bias = jnp.where(kpos <= tq1, bias, MASKED)
            biases.append(bias.astype(BF16))
        k_aug = jnp.concatenate([jnp.concatenate(biases, axis=0), ks_ref[pl.ds(base, n_sub * sub_size), :]], axis=1)
        scores = _dot(k_aug, q_aug_t)
        if after_scores is not None:
            after_scores()
        parts = []
        for sub in range(n_sub):
            sc = scores[sub * sub_size:(sub + 1) * sub_size]
            m_sub = jnp.max(sc, axis=0, keepdims=True)
            pe = jnp.exp2(sc - m_sub).astype(BF16)
            k0 = pl.multiple_of(base + sub * sub_size, sub_size)
            parts.append((m_sub, _dot(vst_ref[:, pl.ds(k0, sub_size)], pe)))
        m_new = m_run
        for m_sub, _ in parts:
            m_new = jnp.maximum(m_new, m_sub)
        acc = jnp.exp2(m_run - m_new) * acc
        for m_sub, acc_sub in parts:
            acc = acc + jnp.exp2(m_sub - m_new) * acc_sub
        return m_new, acc

    subs_per_tile = kt_size // sub_size
    n_full = t0 // kt_size
    carry = lax.fori_loop(
        0, n_full, functools.partial(slc_tile, n_sub=subs_per_tile, causal_from=subs_per_tile),
        (jnp.full((1, width), MASKED, F32), jnp.zeros((V_ROWS, width), F32)))
    tail_step = min(2, subs_per_tile)
    tail_len = (t0 - n_full * kt_size) // (tail_step * sub_size)
    for variant in range(subs_per_tile // tail_step):
        n_sub = tail_step * (variant + 1)

        @pl.when(tail_len == variant)
        def _(n_sub=n_sub):
            _, acc_slc = slc_tile(n_full, carry, n_sub=n_sub, causal_from=n_sub - tail_step,
                                  after_scores=window_values)
            oslc_scr[...] = acc_slc[:NSA_HEAD_DIM] / acc_slc[NSA_HEAD_DIM:NSA_HEAD_DIM + 1]

    o_slc = oslc_scr[...]
    o_win = owin_scr[...]

    for hp in range(hpg):
        cols = slice(hp * Q_BLOCK, (hp + 1) * Q_BLOCK)
        row = (grp * hpg + hp) * 3
        o_t = (gt_ref[qi, pl.ds(row, 1), :] * o_cmp[:, cols] + gt_ref[qi, pl.ds(row + 1, 1), :] * o_slc[:, cols]
               + gt_ref[qi, pl.ds(row + 2, 1), :] * o_win[:, cols])
        o_ref[pl.ds(pl.multiple_of(qi * Q_BLOCK, Q_BLOCK), Q_BLOCK), hp * NSA_HEAD_DIM:(hp + 1) * NSA_HEAD_DIM] = (
            o_t.T.astype(o_ref.dtype))


def _nsa_attention(qs, kcc, vcct, ks, vst, kw, vwt, gt, B, S, blocks=4, kt_size=4096, sub_size=256, cmp_chunk=256):
    G, Dh, nq, hpg = NSA_KV_GROUPS, NSA_HEAD_DIM, S // Q_BLOCK, NSA_HEADS_PER_GROUP
    kt_size = min(kt_size, S)
    blocks = min(blocks, nq)
    n_cmp, n_sel = S // CMP_STRIDE, S // SEL_BLOCK
    cmp_chunk = min(cmp_chunk, n_cmp)
    per_group = lambda shape: pl.BlockSpec((None, None) + shape, lambda b, g, i: (b, g, 0, 0))
    return pl.pallas_call(
        functools.partial(_nsa_body, blocks=blocks, kt_size=kt_size, sub_size=min(sub_size, kt_size),
                          cmp_chunk=cmp_chunk),
        grid=(B, G, nq // blocks),
        in_specs=[
            pl.BlockSpec((None, None, Dh, blocks * hpg * Q_BLOCK), lambda b, g, i: (b, g, 0, i)),
            per_group((n_cmp, Dh)), per_group((Dh, n_cmp)),
            per_group((S, Dh)), per_group((V_ROWS, S)),
            per_group((S, Dh)), per_group((V_ROWS, S)),
            pl.BlockSpec((None, blocks, GATE_ROWS, Q_BLOCK), lambda b, g, i: (b, i, 0, 0)),
        ],
        out_specs=pl.BlockSpec((None, blocks * Q_BLOCK, hpg * Dh), lambda b, g, i: (b, i, g)),
        out_shape=jax.ShapeDtypeStruct((B, S, NSA_HEADS * Dh), BF16),
        scratch_shapes=[
            pltpu.VMEM((n_cmp + 2 * IMP_PAD, Q_BLOCK), F32),
            pltpu.VMEM((n_sel, Q_BLOCK), F32),
            pltpu.VMEM((Dh, hpg * Q_BLOCK), F32),
            pltpu.VMEM((WINDOW + Q_BLOCK, hpg * Q_BLOCK), F32),
            pltpu.VMEM((Dh, hpg * Q_BLOCK), F32),
            pltpu.VMEM((Dh, hpg * Q_BLOCK), F32),
        ],
        compiler_params=pltpu.CompilerParams(
            dimension_semantics=("parallel", "parallel", "arbitrary"), vmem_limit_bytes=VMEM_LIMIT),
        name="nsa_attention",
    )(qs, kcc, vcct, ks, vst, kw, vwt, gt)


def _merge_body(u_ref, halo_ref, pw_ref, ps_ref, yb_ref, yc_ref, ga_ref, gb_ref, gc_ref, h_ref, wb_ref, wo_ref, o_ref,
                *, tm, tiles_per_seq):
    tile_in_seq = lax.rem(pl.program_id(0), tiles_per_seq)
    halo = jnp.where(tile_in_seq == 0, 0.0, halo_ref[...].astype(F32))
    t = tile_in_seq * tm + lax.broadcasted_iota(jnp.int32, (tm, 1), 0)
    ya = _pool_tile(u_ref[...].astype(F32), halo, t, pw_ref, ps_ref)
    merged = jax.nn.sigmoid(ga_ref[...].astype(F32)) * _dot(ya, wb_ref[0])
    merged = merged + jax.nn.sigmoid(gb_ref[...].astype(F32)) * _dot(yb_ref[...], wb_ref[1])
    merged = merged + jax.nn.sigmoid(gc_ref[...].astype(F32)) * _dot(yc_ref[...], wb_ref[2])
    o_ref[...] = h_ref[...] + _dot(merged.astype(BF16), wo_ref[...])


def _merge(main, pool_w, pool_scale, yb, yc, h, w_branch, w_out, S, tm=1024):
    T, D = h.shape
    tm = min(tm, S)
    halo_blocks = tm // POOL_HALO
    y_spec = pl.BlockSpec((tm, 512), lambda i: (i, 0))
    gate_spec = lambda k: pl.BlockSpec((tm, D), lambda i: (i, MAIN_GM // D + k))
    return pl.pallas_call(
        functools.partial(_merge_body, tm=tm, tiles_per_seq=S // tm),
        grid=(T // tm,),
        in_specs=[
            pl.BlockSpec((tm, 512), lambda i: (i, MAIN_U // 512)),
            pl.BlockSpec((POOL_HALO, 512), lambda i: (jnp.maximum(i * halo_blocks - 1, 0), MAIN_U // 512)),
            pl.BlockSpec((len(POOL_WINDOWS), POOL_GROUP_DIM, POOL_GROUP_DIM), lambda i: (0, 0, 0)),
            pl.BlockSpec((1, 512), lambda i: (0, 0)),
            y_spec, y_spec, gate_spec(0), gate_spec(1), gate_spec(2),
            pl.BlockSpec((tm, D), lambda i: (i, 0)),
            pl.BlockSpec((3, 512, D), lambda i: (0, 0, 0)),
            pl.BlockSpec((D, D), lambda i: (0, 0)),
        ],
        out_specs=pl.BlockSpec((tm, D), lambda i: (i, 0)),
        out_shape=jax.ShapeDtypeStruct((T, D), F32),
        compiler_params=pltpu.CompilerParams(dimension_semantics=("parallel",), vmem_limit_bytes=VMEM_LIMIT),
        name="merge_out",
    )(main, main, pool_w, pool_scale, yb, yc, main, main, main, h, w_branch, w_out)


def _ffn_body(h_ref, gf_ref, w1_ref, w2_ref, gp_ref, wg_ref, p_ref, wp_ref, gl_ref, o_ref, f_scr, acc_scr, *, final_norm):
    j = pl.program_id(1)

    @pl.when(j == 0)
    def _():
        f_scr[...] = _rms(h_ref[...], gf_ref[...]).astype(BF16)
        acc_scr[...] = jnp.zeros_like(acc_scr)

    a = jnp.maximum(_dot(f_scr[...], w1_ref[...]), 0.0)
    acc_scr[...] += _dot((a * a).astype(BF16), w2_ref[...])

    @pl.when(j == pl.num_programs(1) - 1)
    def _():
        h2 = h_ref[...] + acc_scr[...]
        gate = jax.nn.sigmoid(_dot(_rms(h2, gp_ref[...]).astype(BF16), wg_ref[...]))
        h3 = h2 + gate * _dot(p_ref[...].astype(BF16), wp_ref[...])
        o_ref[...] = _rms(h3, gl_ref[...]) if final_norm else h3


def _ffn_ple(h, norm_ffn, w1, w2, norm_ple, w_gate, p, w_proj, norm_last, final_norm, tm=1024, tf=1024):
    T, D = h.shape
    tm = min(tm, T)
    Fdim = w1.shape[1]
    Pdim = p.shape[1]
    vec = pl.BlockSpec((1, D), lambda i, j: (0, 0))
    return pl.pallas_call(
        functools.partial(_ffn_body, final_norm=final_norm),
        grid=(T // tm, Fdim // tf),
        in_specs=[
            pl.BlockSpec((tm, D), lambda i, j: (i, 0)), vec,
            pl.BlockSpec((D, tf), lambda i, j: (0, j)),
            pl.BlockSpec((tf, D), lambda i, j: (j, 0)),
            vec,
            pl.BlockSpec((D, D), lambda i, j: (0, 0)),
            pl.BlockSpec((tm, Pdim), lambda i, j: (i, 0)),
            pl.BlockSpec((Pdim, D), lambda i, j: (0, 0)),
            vec,
        ],
        out_specs=pl.BlockSpec((tm, D), lambda i, j: (i, 0)),
        out_shape=jax.ShapeDtypeStruct((T, D), F32),
        scratch_shapes=[pltpu.VMEM((tm, D), BF16), pltpu.VMEM((tm, D), F32)],
        compiler_params=pltpu.CompilerParams(
            dimension_semantics=("parallel", "arbitrary"), vmem_limit_bytes=VMEM_LIMIT),
        name="ffn_ple",
    )(h, norm_ffn, w1, w2, norm_ple, w_gate, p, w_proj, norm_last)


def _split_w_in(w):
    w = w.astype(BF16)
    parts, off = [], 0
    for sz in IN_SIZES:
        parts.append(w[:, off:off + sz])
        off += sz
    u, q_nsa, kv_nsa, g_nsa, q_gla, k_gla, v_gla, a_gla, r_gla, g_merge = parts
    D = w.shape[0]
    pad = lambda n: jnp.zeros((D, n), w.dtype)
    main = jnp.concatenate([u, q_gla, k_gla, v_gla, r_gla, g_merge, a_gla, pad(MAIN_WIDTH - MAIN_A - GLA_GATE_RANK)], axis=1)
    kv = [kv_nsa[:, s * LANES:(s + 1) * LANES] for s in range(6)]
    nsa = jnp.concatenate([q_nsa, kv[0], kv[2], kv[4], kv[1], kv[3], kv[5], g_nsa, pad(NSA_WIDTH - NSA_G - 24)], axis=1)
    return main.astype(BF16), nsa.astype(BF16)


def kernel(x, p, positions, norm_mix, w_in, pool_w, pool_scale, cmp_pos_k, cmp_w_k, cmp_pos_v, cmp_w_v, gla_w_gate, gla_b_gate, gla_norm, w_branch, w_out, norm_ffn, w_ff1, w_ff2, norm_ple, w_ple_gate, w_ple_proj, norm_final):
    B, S, D = x.shape
    depth = w_in.shape[0]
    T = B * S
    Dh = NSA_HEAD_DIM
    row = lambda v: v.reshape(1, -1).astype(F32)

    half = NSA_ROT_DIM // 2
    inv_freq = jnp.power(ROPE_THETA, -jnp.arange(half, dtype=F32) * (2.0 / NSA_ROT_DIM)).reshape(half, 1)
    pos_rows = positions.reshape(B, S // Q_BLOCK, 1, Q_BLOCK)
    rope_spread = _rope_spread_matrix()

    h = x.reshape(T, D)
    for i in range(depth):
        w_main, w_nsa = _split_w_in(w_in[i])
        main = _norm_proj(h, row(norm_mix[i]), w_main, BF16, tn=MAIN_WIDTH // 3).reshape(B, S, MAIN_WIDTH)
        nsa_proj = _norm_proj(h, row(norm_mix[i]), w_nsa, F32, tn=NSA_WIDTH).reshape(B, S, NSA_WIDTH)

        wg = jnp.zeros((LANES, GLA_HEADS * GLA_DK), BF16).at[:GLA_GATE_RANK].set(gla_w_gate[i].astype(BF16))
        y_c = _gla_mixer(main, wg, row(gla_b_gate[i]), row(gla_norm[i]), B, S)

        qs, kc, vc, ks, kw, vst, vwt, gt = _nsa_prep(nsa_proj, pos_rows, inv_freq, rope_spread, B, S)
        kcc = _compress(kc, cmp_w_k[i].reshape(CMP_BLOCK, Dh, Dh).astype(BF16),
                        cmp_pos_k[i].reshape(CMP_BLOCK, 1, Dh).astype(F32), transpose_out=False)
        vcct = _compress(vc, cmp_w_v[i].reshape(CMP_BLOCK, Dh, Dh).astype(BF16),
                         cmp_pos_v[i].reshape(CMP_BLOCK, 1, Dh).astype(F32), transpose_out=True)
        y_b = _nsa_attention(qs, kcc, vcct, ks, vst, kw, vwt, gt, B, S)

        h = _merge(main.reshape(T, MAIN_WIDTH), pool_w[i].astype(BF16), row(pool_scale[i]), y_b.reshape(T, -1),
                   y_c.reshape(T, -1), h, w_branch[i].astype(BF16), w_out[i].astype(BF16), S)
        h = _ffn_ple(h, row(norm_ffn[i]), w_ff1[i].astype(BF16), w_ff2[i].astype(BF16), row(norm_ple[i]),
                     w_ple_gate[i].astype(BF16), p[i].reshape(T, -1), w_ple_proj[i].astype(BF16),
                     row(norm_final), final_norm=(i == depth - 1))
    return h.reshape(B, S, D)
```

```python
import functools

import jax
import jax.numpy as jnp
import numpy as np
from jax import lax
from jax.experimental import pallas as pl
from jax.experimental.pallas import tpu as pltpu

F32, BF16 = jnp.float32, jnp.bfloat16

EPS = 1e-6
ROPE_THETA = 500000.0
POOL_WINDOWS = (2, 4, 8, 16)
POOL_GROUP_DIM = 128
POOL_HALO = max(POOL_WINDOWS)
NSA_HEADS = 8
NSA_KV_GROUPS = 2
NSA_HEADS_PER_GROUP = NSA_HEADS // NSA_KV_GROUPS
NSA_HEAD_DIM = 64
NSA_ROT_DIM = 16
CMP_BLOCK = 32
CMP_STRIDE = 16
SEL_BLOCK = 64
SEL_TOPK = 16
WINDOW = 512
Q_BLOCK = 128
GLA_HEADS = 4
GLA_DK = 64
GLA_DV = 128
GLA_GATE_RANK = 16
GLA_TAU = 16.0
GLA_CHUNK = 64
IN_SIZES = (512, 512, 768, 24, 256, 256, 512, 16, 512, 3072)

LANES = 128
MASKED = -1e30
LOG2E = 1.4426950408889634
Q_SCALE = NSA_HEAD_DIM ** -0.5 * LOG2E
V_ROWS = NSA_HEAD_DIM + 16
IMP_PAD = 8
GATE_ROWS = 32
TOPK_ROW_STEP = 64
VMEM_LIMIT = 56 * 1024 * 1024

MAIN_U, MAIN_QG, MAIN_KG, MAIN_VG, MAIN_RG, MAIN_GM, MAIN_A, MAIN_WIDTH = 0, 512, 768, 1024, 1536, 2048, 5120, 5376
NSA_Q, NSA_KC, NSA_KS, NSA_KW, NSA_VC, NSA_VS, NSA_VW, NSA_G, NSA_WIDTH = 0, 512, 640, 768, 896, 1024, 1152, 1280, 1536


def _nt(a, b):
    return lax.dot_general(a, b, (((1,), (1,)), ((), ())), preferred_element_type=F32)


def _tn(a, b):
    return lax.dot_general(a, b, (((0,), (0,)), ((), ())), preferred_element_type=F32)


def _dot(a, b):
    return jnp.dot(a, b, preferred_element_type=F32)


def _rms(x, gain):
    return x * lax.rsqrt(jnp.mean(x * x, axis=-1, keepdims=True) + EPS) * gain


def _proj_body(h_ref, g_ref, w_ref, o_ref, a_scr):
    @pl.when(pl.program_id(1) == 0)
    def _():
        a_scr[...] = _rms(h_ref[...], g_ref[...]).astype(BF16)

    o_ref[...] = _dot(a_scr[...], w_ref[...]).astype(o_ref.dtype)


def _norm_proj(h, gain, w, out_dtype, tn, tm=1024):
    T, D = h.shape
    N = w.shape[1]
    tm = min(tm, T)
    return pl.pallas_call(
        _proj_body,
        grid=(T // tm, N // tn),
        in_specs=[
            pl.BlockSpec((tm, D), lambda i, j: (i, 0)),
            pl.BlockSpec((1, D), lambda i, j: (0, 0)),
            pl.BlockSpec((D, tn), lambda i, j: (0, j)),
        ],
        out_specs=pl.BlockSpec((tm, tn), lambda i, j: (i, j)),
        out_shape=jax.ShapeDtypeStruct((T, N), out_dtype),
        scratch_shapes=[pltpu.VMEM((tm, D), BF16)],
        compiler_params=pltpu.CompilerParams(
            dimension_semantics=("parallel", "arbitrary"), vmem_limit_bytes=VMEM_LIMIT),
        name="norm_proj",
    )(h, gain, w)


def _pool_tile(cur, halo, t, w_ref, sc_ref):
    outs = []
    for g, win in enumerate(POOL_WINDOWS):
        lo, hi = g * POOL_GROUP_DIM, (g + 1) * POOL_GROUP_DIM
        x = cur[:, lo:hi]
        e = jnp.concatenate([halo[:, lo:hi], x], axis=0)
        step = 1
        while step < win:
            e = e[step:] + e[:-step]
            step *= 2
        wsum = e[POOL_HALO - (win - 1):]
        cnt = jnp.minimum(t + 1, win).astype(F32)
        pooled = wsum / cnt - x
        outs.append((_dot(pooled.astype(BF16), w_ref[g]) * sc_ref[:, lo:hi]).astype(BF16))
    return jnp.concatenate(outs, axis=1)


def _gla_body(q_ref, k_ref, v_ref, r_ref, a_ref, wg_ref, bg_ref, ng_ref, o_ref, st_ref, *, tc):
    @pl.when(pl.program_id(1) == 0)
    def _():
        st_ref[...] = jnp.zeros_like(st_ref)

    C = GLA_CHUNK
    n_chunks = tc // C
    causal = lax.broadcasted_iota(jnp.int32, (C, C), 0) >= lax.broadcasted_iota(jnp.int32, (C, C), 1)
    width = GLA_HEADS * GLA_DK
    hk = [slice(h * GLA_DK, (h + 1) * GLA_DK) for h in range(GLA_HEADS)]
    hv = [slice(h * GLA_DV, (h + 1) * GLA_DV) for h in range(GLA_HEADS)]
    rows = [slice(c * C, (c + 1) * C) for c in range(n_chunks)]

    z = _dot(a_ref[...], wg_ref[...]) + bg_ref[...]
    b = jax.nn.log_sigmoid(z) * (1.0 / GLA_TAU)
    row_in_chunk = lax.broadcasted_iota(jnp.int32, (tc, 1), 0) % C
    step = 1
    while step < C:
        shifted = jnp.concatenate([jnp.zeros((step, width), F32), b[:-step]], axis=0)
        b = b + jnp.where(row_in_chunk >= step, shifted, 0.0)
        step *= 2
    b_last = [b[(c + 1) * C - 1:(c + 1) * C, :] for c in range(n_chunks)]
    b_last_rows = jnp.concatenate([jnp.broadcast_to(bl, (C, width)) for bl in b_last], axis=0)
    qf = q_ref[...].astype(F32) * (GLA_DK ** -0.5)
    kf = k_ref[...].astype(F32)
    q_s = (qf * jnp.exp(b)).astype(BF16)
    k_s = (kf * jnp.exp(-b)).astype(BF16)
    k_t = (kf * jnp.exp(b_last_rows - b)).astype(BF16)

    att = [[jnp.where(causal, _nt(q_s[rows[c], hk[h]], k_s[rows[c], hk[h]]), 0.0).astype(BF16)
            for h in range(GLA_HEADS)] for c in range(n_chunks)]
    o_intra = [[_dot(att[c][h], v_ref[rows[c], hv[h]]) for h in range(GLA_HEADS)] for c in range(n_chunks)]
    kv = [[_tn(v_ref[rows[c], hv[h]], k_t[rows[c], hk[h]]) for h in range(GLA_HEADS)] for c in range(n_chunks)]

    state_t = [st_ref[h] for h in range(GLA_HEADS)]
    for c in range(n_chunks):
        decay = jnp.exp(b_last[c])
        for h in range(GLA_HEADS):
            o = o_intra[c][h] + _nt(q_s[rows[c], hk[h]], state_t[h].astype(BF16))
            state_t[h] = state_t[h] * decay[:, hk[h]] + kv[c][h]
            o = _rms(o, ng_ref[...])
            r = r_ref[rows[c], hv[h]].astype(F32)
            o_ref[rows[c], hv[h]] = (o * (r * jax.nn.sigmoid(r))).astype(o_ref.dtype)
    for h in range(GLA_HEADS):
        st_ref[h] = state_t[h]


def _gla_mixer(main, w_gate, b_gate, norm_g, B, S, tc=512):
    tc = min(tc, S)
    return pl.pallas_call(
        functools.partial(_gla_body, tc=tc),
        grid=(B, S // tc),
        in_specs=[
            pl.BlockSpec((None, tc, 256), lambda b, i: (b, i, MAIN_QG // 256)),
            pl.BlockSpec((None, tc, 256), lambda b, i: (b, i, MAIN_KG // 256)),
            pl.BlockSpec((None, tc, 512), lambda b, i: (b, i, MAIN_VG // 512)),
            pl.BlockSpec((None, tc, 512), lambda b, i: (b, i, MAIN_RG // 512)),
            pl.BlockSpec((None, tc, 128), lambda b, i: (b, i, MAIN_A // 128)),
            pl.BlockSpec((128, 256), lambda b, i: (0, 0)),
            pl.BlockSpec((1, 256), lambda b, i: (0, 0)),
            pl.BlockSpec((1, 128), lambda b, i: (0, 0)),
        ],
        out_specs=pl.BlockSpec((None, tc, 512), lambda b, i: (b, i, 0)),
        out_shape=jax.ShapeDtypeStruct((B, S, 512), BF16),
        scratch_shapes=[pltpu.VMEM((GLA_HEADS, GLA_DV, GLA_DK), F32)],
        compiler_params=pltpu.CompilerParams(dimension_semantics=("parallel", "arbitrary")),
        name="gla_mixer",
    )(main, main, main, main, main, w_gate, b_gate, norm_g)


def _nsa_prep_body(x_ref, pos_ref, invf_ref, spread_ref, swap_ref, qs_ref, kc_ref, vc_ref, ks_ref, kw_ref, vst_ref,
                   vwt_ref, gt_ref, *, blocks):
    half = NSA_ROT_DIM // 2
    hpg = NSA_HEADS_PER_GROUP
    ones_row = (lax.broadcasted_iota(jnp.int32, (V_ROWS - NSA_HEAD_DIM, Q_BLOCK), 0) == 0).astype(BF16)

    def pieces(a):
        p1 = a.astype(BF16).astype(F32)
        p2 = (a - p1).astype(BF16).astype(F32)
        return [p1, p2, (a - p1 - p2).astype(BF16).astype(F32)]

    for blk in range(blocks):
        tok = slice(blk * Q_BLOCK, (blk + 1) * Q_BLOCK)
        ang = invf_ref[...] * pos_ref[blk].astype(F32)
        lhs = jnp.concatenate(
            pieces(jnp.cos(ang)) + pieces(jnp.sin(ang)) + [jnp.ones((2 * half, Q_BLOCK), F32)], axis=0)
        tables = _tn(lhs.astype(BF16), spread_ref[...])
        cs = tables[:, :LANES]
        sn = tables[:, LANES:2 * LANES] + tables[:, 2 * LANES:]

        def rope(x):
            partner = _dot(jnp.concatenate([p.astype(BF16) for p in pieces(x)], axis=1), swap_ref[...])
            return x * cs + partner * sn

        def split_groups(x, ref, dtype):
            ref[0, tok, :] = x[:, :NSA_HEAD_DIM].astype(dtype)
            ref[1, tok, :] = x[:, NSA_HEAD_DIM:].astype(dtype)

        for pair in range(NSA_HEADS // 2):
            cols = slice(NSA_Q + pair * LANES, NSA_Q + (pair + 1) * LANES)
            qr_t = (rope(x_ref[tok, cols]) * Q_SCALE).T
            for sub in range(2):
                head = 2 * pair + sub
                grp, hp = head // hpg, head % hpg
                col0 = (blk * hpg + hp) * Q_BLOCK
                qs_ref[grp, :, col0:col0 + Q_BLOCK] = qr_t[sub * NSA_HEAD_DIM:(sub + 1) * NSA_HEAD_DIM].astype(BF16)
        split_groups(rope(x_ref[tok, NSA_KC:NSA_KC + LANES]), kc_ref, F32)
        split_groups(rope(x_ref[tok, NSA_KS:NSA_KS + LANES]), ks_ref, BF16)
        split_groups(rope(x_ref[tok, NSA_KW:NSA_KW + LANES]), kw_ref, BF16)
        split_groups(x_ref[tok, NSA_VC:NSA_VC + LANES], vc_ref, F32)
        vst = x_ref[tok, NSA_VS:NSA_VS + LANES].T
        vwt = x_ref[tok, NSA_VW:NSA_VW + LANES].T
        for grp in range(NSA_KV_GROUPS):
            for ref, v in ((vst_ref, vst), (vwt_ref, vwt)):
                ref[grp, :NSA_HEAD_DIM, tok] = v[grp * NSA_HEAD_DIM:(grp + 1) * NSA_HEAD_DIM].astype(BF16)
                ref[grp, NSA_HEAD_DIM:, tok] = ones_row
        gt_ref[blk] = jax.nn.sigmoid(x_ref[tok, NSA_G:NSA_G + LANES]).T[:GATE_ROWS]


def _rope_spread_matrix():
    half = NSA_ROT_DIM // 2
    lane = np.arange(LANES)
    d = lane % NSA_HEAD_DIM
    m = np.zeros((8 * half, 3 * LANES), np.float32)
    for piece in range(3):
        for f in range(half):
            hit = d % half == f
            m[piece * half + f, :LANES] = hit & (d < NSA_ROT_DIM)
            m[(3 + piece) * half + f, LANES:2 * LANES] = -1.0 * (hit & (d < half))
            m[(3 + piece) * half + f, 2 * LANES:] = hit & (d >= half) & (d < NSA_ROT_DIM)
    m[6 * half, :LANES] = d >= NSA_ROT_DIM
    return jnp.asarray(m, BF16)


def _rope_swap_matrix():
    half = NSA_ROT_DIM // 2
    lane = np.arange(LANES)
    d = lane % NSA_HEAD_DIM
    partner = np.where(d < half, lane + half, lane - half)
    m = np.zeros((LANES, LANES), np.float32)
    rotary = d < NSA_ROT_DIM
    m[partner[rotary], lane[rotary]] = 1.0
    return jnp.asarray(np.concatenate([m, m, m], axis=0), BF16)


def _nsa_prep(nsa_proj, positions, inv_freq, rope_spread, rope_swap, B, S, blocks=4):
    G, Dh, nq = NSA_KV_GROUPS, NSA_HEAD_DIM, S // Q_BLOCK
    blocks = min(blocks, nq)
    ts = blocks * Q_BLOCK
    tok = lambda dtype: jax.ShapeDtypeStruct((B, G, S, Dh), dtype)
    tok_spec = pl.BlockSpec((None, G, ts, Dh), lambda b, i: (b, 0, i, 0))
    tr = jax.ShapeDtypeStruct((B, G, V_ROWS, S), BF16)
    tr_spec = pl.BlockSpec((None, G, V_ROWS, ts), lambda b, i: (b, 0, 0, i))
    return pl.pallas_call(
        functools.partial(_nsa_prep_body, blocks=blocks),
        grid=(B, nq // blocks),
        in_specs=[
            pl.BlockSpec((None, ts, NSA_WIDTH), lambda b, i: (b, i, 0)),
            pl.BlockSpec((None, blocks, 1, Q_BLOCK), lambda b, i: (b, i, 0, 0)),
            pl.BlockSpec(inv_freq.shape, lambda b, i: (0, 0)),
            pl.BlockSpec(rope_spread.shape, lambda b, i: (0, 0)),
            pl.BlockSpec(rope_swap.shape, lambda b, i: (0, 0)),
        ],
        out_specs=[
            pl.BlockSpec((None, G, Dh, NSA_HEADS_PER_GROUP * ts), lambda b, i: (b, 0, 0, i)),
            tok_spec, tok_spec, tok_spec, tok_spec, tr_spec, tr_spec,
            pl.BlockSpec((None, blocks, GATE_ROWS, Q_BLOCK), lambda b, i: (b, i, 0, 0)),
        ],
        out_shape=[
            jax.ShapeDtypeStruct((B, G, Dh, NSA_HEADS_PER_GROUP * S), BF16),
            tok(F32), tok(F32), tok(BF16), tok(BF16), tr, tr,
            jax.ShapeDtypeStruct((B, nq, GATE_ROWS, Q_BLOCK), F32),
        ],
        compiler_params=pltpu.CompilerParams(dimension_semantics=("parallel", "parallel")),
        name="nsa_prep",
    )(nsa_proj, positions, inv_freq, rope_spread, rope_swap)


def _cmp_body(x_ref, w_ref, p_ref, o_ref, *, n_blk, transpose_out):
    half = CMP_BLOCK // 2
    first = jnp.zeros((n_blk, NSA_HEAD_DIM), F32)
    second = jnp.zeros((n_blk, NSA_HEAD_DIM), F32)
    for l in range(half):
        rows = x_ref[pl.ds(l, n_blk, stride=CMP_STRIDE), :]
        first = first + _dot((rows + p_ref[l]).astype(BF16), w_ref[l])
        second = second + _dot((rows + p_ref[half + l]).astype(BF16), w_ref[half + l])
    out = first + jnp.concatenate([second[1:], jnp.zeros((1, NSA_HEAD_DIM), F32)], axis=0)
    o_ref[...] = (out.T if transpose_out else out).astype(o_ref.dtype)


def _compress(x, w, pos, transpose_out):
    B, G, S, Dh = x.shape
    n_blk = S // CMP_STRIDE
    out_block = (None, None, Dh, n_blk) if transpose_out else (None, None, n_blk, Dh)
    out_shape = (B, G, Dh, n_blk) if transpose_out else (B, G, n_blk, Dh)
    return pl.pallas_call(
        functools.partial(_cmp_body, n_blk=n_blk, transpose_out=transpose_out),
        grid=(B, G),
        in_specs=[
            pl.BlockSpec((None, None, S, Dh), lambda b, g: (b, g, 0, 0)),
            pl.BlockSpec((CMP_BLOCK, Dh, Dh), lambda b, g: (0, 0, 0)),
            pl.BlockSpec((CMP_BLOCK, 1, Dh), lambda b, g: (0, 0, 0)),
        ],
        out_specs=pl.BlockSpec(out_block, lambda b, g: (b, g, 0, 0)),
        out_shape=jax.ShapeDtypeStruct(out_shape, BF16),
        compiler_params=pltpu.CompilerParams(
            dimension_semantics=("parallel", "parallel"), vmem_limit_bytes=VMEM_LIMIT),
        name="nsa_compress",
    )(x, w, pos)


def _nsa_body(*refs, blocks, **tiles):
    def one_block(qi, carry):
        _nsa_block(qi, *refs, blocks=blocks, **tiles)
        return carry

    lax.fori_loop(0, blocks, one_block, 0)


def _nsa_block(qi, q_ref, kcc_ref, vcct_ref, ks_ref, vst_ref, kw_ref, vwt_ref, gt_ref, o_ref,
               imp_scr, sel_scr, ocmp_scr, wins_scr, owin_scr, oslc_scr, *, blocks, kt_size, sub_size, cmp_chunk):
    grp = pl.program_id(1)
    t0 = (pl.program_id(2) * blocks + qi) * Q_BLOCK
    hpg = NSA_HEADS_PER_GROUP
    n_cmp = kcc_ref.shape[0]
    n_sel = sel_scr.shape[0]
    q_t = q_ref[:, pl.ds(pl.multiple_of(qi * (hpg * Q_BLOCK), hpg * Q_BLOCK), hpg * Q_BLOCK)]
    tq1 = t0 + lax.broadcasted_iota(jnp.int32, (1, Q_BLOCK), 1)
    heads = lambda x: jnp.concatenate([x] * hpg, axis=1)

    width = hpg * Q_BLOCK
    neg_inf = -jnp.inf
    eye = (lax.broadcasted_iota(jnp.int32, (Q_BLOCK, Q_BLOCK), 0)
           == lax.broadcasted_iota(jnp.int32, (Q_BLOCK, Q_BLOCK), 1)).astype(BF16)
    q_aug_t = jnp.concatenate([jnp.concatenate([eye] * hpg, axis=1), q_t], axis=0)

    ch = cmp_chunk
    n_chunks = (((t0 + Q_BLOCK - CMP_BLOCK) >> (CMP_STRIDE.bit_length() - 1)) + ch) // ch

    def compressed(n_sub):
        imp_scr[0:IMP_PAD, :] = jnp.zeros((IMP_PAD, Q_BLOCK), F32)
        imp_scr[IMP_PAD + n_sub * ch:, :] = jnp.zeros((n_cmp + IMP_PAD - n_sub * ch, Q_BLOCK), F32)
        biases = []
        for sub in range(n_sub):
            if sub >= n_sub - 2:
                cmp_end = (sub * ch + lax.broadcasted_iota(jnp.int32, (ch, 1), 0)) * CMP_STRIDE + (CMP_BLOCK - 1)
                biases.append(jnp.where(cmp_end <= tq1, 0.0, MASKED).astype(BF16))
            else:
                biases.append(jnp.zeros((ch, Q_BLOCK), BF16))
        k_aug = jnp.concatenate([jnp.concatenate(biases, axis=0), kcc_ref[0:n_sub * ch, :]], axis=1)
        scores = _dot(k_aug, q_aug_t)
        parts = []
        for sub in range(n_sub):
            s = scores[sub * ch:(sub + 1) * ch]
            m_sub = jnp.max(s, axis=0, keepdims=True)
            e = jnp.exp2(s - m_sub)
            parts.append((m_sub, jnp.sum(e, axis=0, keepdims=True),
                          _dot(vcct_ref[:, sub * ch:(sub + 1) * ch], e.astype(BF16)), e))
        m = parts[0][0]
        for m_sub, _, _, _ in parts[1:]:
            m = jnp.maximum(m, m_sub)
        scales = [jnp.exp2(m_sub - m) for m_sub, _, _, _ in parts]
        l = scales[0] * parts[0][1]
        acc = scales[0] * parts[0][2]
        for scale, (_, l_sub, acc_sub, _) in zip(scales[1:], parts[1:]):
            l = l + scale * l_sub
            acc = acc + scale * acc_sub
        inv_l = jnp.where(heads(tq1 >= CMP_BLOCK - 1), 1.0 / l, 0.0)
        ocmp_scr[...] = acc * inv_l
        for sub in range(n_sub):
            pc = parts[sub][3] * (scales[sub] * inv_l)
            p_sum = pc[:, :Q_BLOCK]
            for hp in range(1, hpg):
                p_sum = p_sum + pc[:, hp * Q_BLOCK:(hp + 1) * Q_BLOCK]
            imp_scr[IMP_PAD + sub * ch:IMP_PAD + (sub + 1) * ch, :] = p_sum

    for variant in range(n_cmp // ch):
        pl.when(n_chunks == variant + 1)(functools.partial(compressed, variant + 1))
    o_cmp = ocmp_scr[...]

    ratio = SEL_BLOCK // CMP_STRIDE
    sel_shift = SEL_BLOCK.bit_length() - 1
    cur = tq1 >> sel_shift
    n_forced = 3

    span = WINDOW + Q_BLOCK
    k_start = pl.multiple_of(jnp.maximum(t0 - WINDOW, 0), Q_BLOCK)

    def window_scores():
        diff = tq1 - (k_start + lax.broadcasted_iota(jnp.int32, (span, 1), 0))
        inside = pltpu.bitcast(diff, jnp.uint32) < jnp.uint32(WINDOW)
        bias = jnp.where(inside, 0.0, MASKED).astype(BF16)
        wins_scr[...] = _dot(jnp.concatenate([bias, kw_ref[pl.ds(k_start, span), :]], axis=1), q_aug_t)

    def window_values():
        sw = wins_scr[...]
        pw = jnp.exp2(sw - jnp.max(sw, axis=0, keepdims=True)).astype(BF16)
        acc_win = _dot(vwt_ref[:, pl.ds(k_start, span)], pw)
        owin_scr[...] = acc_win[:NSA_HEAD_DIM] / acc_win[NSA_HEAD_DIM:NSA_HEAD_DIM + 1]

    def select_blocks(rows):
        window_scores()
        imp = imp_scr[pl.ds(IMP_PAD - 1, rows, stride=ratio), :]
        for k in range(1, ratio + 1):
            imp = imp + imp_scr[pl.ds(IMP_PAD - 1 + k, rows, stride=ratio), :]
        j = lax.broadcasted_iota(jnp.int32, (rows, Q_BLOCK), 0)
        j_f = j.astype(F32)
        forced = (j == 0) | (j == cur) | (j == cur - 1)
        valid = j <= cur

        def pick(_, sc):
            best = jnp.max(sc, axis=0, keepdims=True)
            idx = jnp.min(jnp.where(sc == best, j_f, float(rows)), axis=0, keepdims=True)
            return jnp.where(j_f == idx, neg_inf, sc)

        left = lax.fori_loop(0, max(min(SEL_TOPK, n_sel) - n_forced, 0), pick,
                             jnp.where(valid & ~forced, imp, neg_inf), unroll=True)
        sel_scr[0:rows, :] = jnp.where(forced | (valid & (left == neg_inf)), 0.0, MASKED)
        if rows < n_sel:
            sel_scr[rows:n_sel, :] = jnp.full((n_sel - rows, Q_BLOCK), MASKED, F32)

    row_step = min(TOPK_ROW_STEP, n_sel)
    last_block = (t0 + Q_BLOCK - 1) >> sel_shift
    for variant in range(n_sel // row_step):
        pl.when(last_block // row_step == variant)(functools.partial(select_blocks, (variant + 1) * row_step))

    blocks_per_tile = kt_size // SEL_BLOCK

    def slc_tile(kt, carry, n_sub, causal_from, after_scores=None):
        m_run, acc = carry
        base = pl.multiple_of(kt * kt_size, kt_size)
        biases = []
        for sub in range(n_sub):
            blk0 = kt * blocks_per_tile + sub * (sub_size // SEL_BLOCK)
            bias = jnp.concatenate(
                [jnp.broadcast_to(sel_scr[pl.ds(blk0 + jb, 1), :], (SEL_BLOCK, Q_BLOCK))
                 for jb in range(sub_size // SEL_BLOCK)], axis=0)
            if sub >= causal_from:
                kpos = base + sub * sub_size + lax.broadcasted_iota(jnp.int32, (sub_size, 1), 0)
                bias = jnp.where(kpos <= tq1, bias, MASKED)
            biases.append(bias.astype(BF16))
        k_aug = jnp.concatenate([jnp.concatenate(biases, axis=0), ks_ref[pl.ds(base, n_sub * sub_size), :]], axis=1)
        scores = _dot(k_aug, q_aug_t)
        if after_scores is not None:
            after_scores()
        parts = []
        for sub in range(n_sub):
            sc = scores[sub * sub_size:(sub + 1) * sub_size]
            m_sub = jnp.max(sc, axis=0, keepdims=True)
            pe = jnp.exp2(sc - m_sub).astype(BF16)
            k0 = pl.multiple_of(base + sub * sub_size, sub_size)
            parts.append((m_sub, _dot(vst_ref[:, pl.ds(k0, sub_size)], pe)))
        m_new = m_run
        for m_sub, _ in parts:
            m_new = jnp.maximum(m_new, m_sub)
        acc = jnp.exp2(m_run - m_new) * acc
        for m_sub, acc_sub in parts:
            acc = acc + jnp.exp2(m_sub - m_new) * acc_sub
        return m_new, acc

    subs_per_tile = kt_size // sub_size
    n_full = t0 // kt_size
    carry = lax.fori_loop(
        0, n_full, functools.partial(slc_tile, n_sub=subs_per_tile, causal_from=subs_per_tile),
        (jnp.full((1, width), MASKED, F32), jnp.zeros((V_ROWS, width), F32)))
    tail_step = min(2, subs_per_tile)
    tail_len = (t0 - n_full * kt_size) // (tail_step * sub_size)
    for variant in range(subs_per_tile // tail_step):
        n_sub = tail_step * (variant + 1)

        @pl.when(tail_len == variant)
        def _(n_sub=n_sub):
            _, acc_slc = slc_tile(n_full, carry, n_sub=n_sub, causal_from=n_sub - tail_step,
                                  after_scores=window_values)
            oslc_scr[...] = acc_slc[:NSA_HEAD_DIM] / acc_slc[NSA_HEAD_DIM:NSA_HEAD_DIM + 1]

    o_slc = oslc_scr[...]
    o_win = owin_scr[...]

    for hp in range(hpg):
        cols = slice(hp * Q_BLOCK, (hp + 1) * Q_BLOCK)
        row = (grp * hpg + hp) * 3
        o_t = (gt_ref[qi, pl.ds(row, 1), :] * o_cmp[:, cols] + gt_ref[qi, pl.ds(row + 1, 1), :] * o_slc[:, cols]
               + gt_ref[qi, pl.ds(row + 2, 1), :] * o_win[:, cols])
        o_ref[pl.ds(pl.multiple_of(qi * Q_BLOCK, Q_BLOCK), Q_BLOCK), hp * NSA_HEAD_DIM:(hp + 1) * NSA_HEAD_DIM] = (
            o_t.T.astype(o_ref.dtype))


def _nsa_attention(qs, kcc, vcct, ks, vst, kw, vwt, gt, B, S, blocks=4, kt_size=4096, sub_size=256, cmp_chunk=256):
    G, Dh, nq, hpg = NSA_KV_GROUPS, NSA_HEAD_DIM, S // Q_BLOCK, NSA_HEADS_PER_GROUP
    kt_size = min(kt_size, S)
    blocks = min(blocks, nq)
    n_cmp, n_sel = S // CMP_STRIDE, S // SEL_BLOCK
    cmp_chunk = min(cmp_chunk, n_cmp)
    per_group = lambda shape: pl.BlockSpec((None, None) + shape, lambda b, g, i: (b, g, 0, 0))
    return pl.pallas_call(
        functools.partial(_nsa_body, blocks=blocks, kt_size=kt_size, sub_size=min(sub_size, kt_size),
                          cmp_chunk=cmp_chunk),
        grid=(B, G, nq // blocks),
        in_specs=[
            pl.BlockSpec((None, None, Dh, blocks * hpg * Q_BLOCK), lambda b, g, i: (b, g, 0, i)),
            per_group((n_cmp, Dh)), per_group((Dh, n_cmp)),
            per_group((S, Dh)), per_group((V_ROWS, S)),
            per_group((S, Dh)), per_group((V_ROWS, S)),
            pl.BlockSpec((None, blocks, GATE_ROWS, Q_BLOCK), lambda b, g, i: (b, i, 0, 0)),
        ],
        out_specs=pl.BlockSpec((None, blocks * Q_BLOCK, hpg * Dh), lambda b, g, i: (b, i, g)),
        out_shape=jax.ShapeDtypeStruct((B, S, NSA_HEADS * Dh), BF16),
        scratch_shapes=[
            pltpu.VMEM((n_cmp + 2 * IMP_PAD, Q_BLOCK), F32),
            pltpu.VMEM((n_sel, Q_BLOCK), F32),
            pltpu.VMEM((Dh, hpg * Q_BLOCK), F32),
            pltpu.VMEM((WINDOW + Q_BLOCK, hpg * Q_BLOCK), F32),
            pltpu.VMEM((Dh, hpg * Q_BLOCK), F32),
            pltpu.VMEM((Dh, hpg * Q_BLOCK), F32),
        ],
        compiler_params=pltpu.CompilerParams(
            dimension_semantics=("parallel", "parallel", "arbitrary"), vmem_limit_bytes=VMEM_LIMIT),
        name="nsa_attention",
    )(qs, kcc, vcct, ks, vst, kw, vwt, gt)


def _merge_body(u_ref, halo_ref, pw_ref, ps_ref, yb_ref, yc_ref, ga_ref, gb_ref, gc_ref, h_ref, wb_ref, wo_ref, o_ref,
                *, tm, tiles_per_seq):
    tile_in_seq = lax.rem(pl.program_id(0), tiles_per_seq)
    halo = jnp.where(tile_in_seq == 0, 0.0, halo_ref[...].astype(F32))
    t = tile_in_seq * tm + lax.broadcasted_iota(jnp.int32, (tm, 1), 0)
    ya = _pool_tile(u_ref[...].astype(F32), halo, t, pw_ref, ps_ref)
    merged = jax.nn.sigmoid(ga_ref[...].astype(F32)) * _dot(ya, wb_ref[0])
    merged = merged + jax.nn.sigmoid(gb_ref[...].astype(F32)) * _dot(yb_ref[...], wb_ref[1])
    merged = merged + jax.nn.sigmoid(gc_ref[...].astype(F32)) * _dot(yc_ref[...], wb_ref[2])
    o_ref[...] = h_ref[...] + _dot(merged.astype(BF16), wo_ref[...])


def _merge(main, pool_w, pool_scale, yb, yc, h, w_branch, w_out, S, tm=1024):
    T, D = h.shape
    tm = min(tm, S)
    halo_blocks = tm // POOL_HALO
    y_spec = pl.BlockSpec((tm, 512), lambda i: (i, 0))
    gate_spec = lambda k: pl.BlockSpec((tm, D), lambda i: (i, MAIN_GM // D + k))
    return pl.pallas_call(
        functools.partial(_merge_body, tm=tm, tiles_per_seq=S // tm),
        grid=(T // tm,),
        in_specs=[
            pl.BlockSpec((tm, 512), lambda i: (i, MAIN_U // 512)),
            pl.BlockSpec((POOL_HALO, 512), lambda i: (jnp.maximum(i * halo_blocks - 1, 0), MAIN_U // 512)),
            pl.BlockSpec((len(POOL_WINDOWS), POOL_GROUP_DIM, POOL_GROUP_DIM), lambda i: (0, 0, 0)),
            pl.BlockSpec((1, 512), lambda i: (0, 0)),
            y_spec, y_spec, gate_spec(0), gate_spec(1), gate_spec(2),
            pl.BlockSpec((tm, D), lambda i: (i, 0)),
            pl.BlockSpec((3, 512, D), lambda i: (0, 0, 0)),
            pl.BlockSpec((D, D), lambda i: (0, 0)),
        ],
        out_specs=pl.BlockSpec((tm, D), lambda i: (i, 0)),
        out_shape=jax.ShapeDtypeStruct((T, D), F32),
        compiler_params=pltpu.CompilerParams(dimension_semantics=("parallel",), vmem_limit_bytes=VMEM_LIMIT),
        name="merge_out",
    )(main, main, pool_w, pool_scale, yb, yc, main, main, main, h, w_branch, w_out)


def _ffn_body(h_ref, gf_ref, w1_ref, w2_ref, gp_ref, wg_ref, p_ref, wp_ref, gl_ref, o_ref, f_scr, acc_scr, *, final_norm):
    j = pl.program_id(1)

    @pl.when(j == 0)
    def _():
        f_scr[...] = _rms(h_ref[...], gf_ref[...]).astype(BF16)
        acc_scr[...] = jnp.zeros_like(acc_scr)

    a = jnp.maximum(_dot(f_scr[...], w1_ref[...]), 0.0)
    acc_scr[...] += _dot((a * a).astype(BF16), w2_ref[...])

    @pl.when(j == pl.num_programs(1) - 1)
    def _():
        h2 = h_ref[...] + acc_scr[...]
        gate = jax.nn.sigmoid(_dot(_rms(h2, gp_ref[...]).astype(BF16), wg_ref[...]))
        h3 = h2 + gate * _dot(p_ref[...].astype(BF16), wp_ref[...])
        o_ref[...] = _rms(h3, gl_ref[...]) if final_norm else h3


def _ffn_ple(h, norm_ffn, w1, w2, norm_ple, w_gate, p, w_proj, norm_last, final_norm, tm=1024, tf=1024):
    T, D = h.shape
    tm = min(tm, T)
    Fdim = w1.shape[1]
    Pdim = p.shape[1]
    vec = pl.BlockSpec((1, D), lambda i, j: (0, 0))
    return pl.pallas_call(
        functools.partial(_ffn_body, final_norm=final_norm),
        grid=(T // tm, Fdim // tf),
        in_specs=[
            pl.BlockSpec((tm, D), lambda i, j: (i, 0)), vec,
            pl.BlockSpec((D, tf), lambda i, j: (0, j)),
            pl.BlockSpec((tf, D), lambda i, j: (j, 0)),
            vec,
            pl.BlockSpec((D, D), lambda i, j: (0, 0)),
            pl.BlockSpec((tm, Pdim), lambda i, j: (i, 0)),
            pl.BlockSpec((Pdim, D), lambda i, j: (0, 0)),
            vec,
        ],
        out_specs=pl.BlockSpec((tm, D), lambda i, j: (i, 0)),
        out_shape=jax.ShapeDtypeStruct((T, D), F32),
        scratch_shapes=[pltpu.VMEM((tm, D), BF16), pltpu.VMEM((tm, D), F32)],
        compiler_params=pltpu.CompilerParams(
            dimension_semantics=("parallel", "arbitrary"), vmem_limit_bytes=VMEM_LIMIT),
        name="ffn_ple",
    )(h, norm_ffn, w1, w2, norm_ple, w_gate, p, w_proj, norm_last)


def _split_w_in(w):
    parts, off = [], 0
    for sz in IN_SIZES:
        parts.append(w[:, off:off + sz])
        off += sz
    u, q_nsa, kv_nsa, g_nsa, q_gla, k_gla, v_gla, a_gla, r_gla, g_merge = parts
    D = w.shape[0]
    pad = lambda n: jnp.zeros((D, n), w.dtype)
    main = jnp.concatenate([u, q_gla, k_gla, v_gla, r_gla, g_merge, a_gla, pad(MAIN_WIDTH - MAIN_A - GLA_GATE_RANK)], axis=1)
    kv = [kv_nsa[:, s * LANES:(s + 1) * LANES] for s in range(6)]
    nsa = jnp.concatenate([q_nsa, kv[0], kv[2], kv[4], kv[1], kv[3], kv[5], g_nsa, pad(NSA_WIDTH - NSA_G - 24)], axis=1)
    return main.astype(BF16), nsa.astype(BF16)


def kernel(x, p, positions, norm_mix, w_in, pool_w, pool_scale, cmp_pos_k, cmp_w_k, cmp_pos_v, cmp_w_v, gla_w_gate, gla_b_gate, gla_norm, w_branch, w_out, norm_ffn, w_ff1, w_ff2, norm_ple, w_ple_gate, w_ple_proj, norm_final):
    B, S, D = x.shape
    depth = w_in.shape[0]
    T = B * S
    Dh = NSA_HEAD_DIM
    row = lambda v: v.reshape(1, -1).astype(F32)

    half = NSA_ROT_DIM // 2
    inv_freq = jnp.power(ROPE_THETA, -jnp.arange(half, dtype=F32) * (2.0 / NSA_ROT_DIM)).reshape(half, 1)
    pos_rows = positions.reshape(B, S // Q_BLOCK, 1, Q_BLOCK)
    rope_spread, rope_swap = _rope_spread_matrix(), _rope_swap_matrix()

    h = x.reshape(T, D)
    for i in range(depth):
        w_main, w_nsa = _split_w_in(w_in[i])
        main = _norm_proj(h, row(norm_mix[i]), w_main, BF16, tn=MAIN_WIDTH // 3).reshape(B, S, MAIN_WIDTH)
        nsa_proj = _norm_proj(h, row(norm_mix[i]), w_nsa, F32, tn=NSA_WIDTH).reshape(B, S, NSA_WIDTH)

        wg = jnp.zeros((LANES, GLA_HEADS * GLA_DK), BF16).at[:GLA_GATE_RANK].set(gla_w_gate[i].astype(BF16))
        y_c = _gla_mixer(main, wg, row(gla_b_gate[i]), row(gla_norm[i]), B, S)

        qs, kc, vc, ks, kw, vst, vwt, gt = _nsa_prep(nsa_proj, pos_rows, inv_freq, rope_spread, rope_swap, B, S)
        kcc = _compress(kc, cmp_w_k[i].reshape(CMP_BLOCK, Dh, Dh).astype(BF16),
                        cmp_pos_k[i].reshape(CMP_BLOCK, 1, Dh).astype(F32), transpose_out=False)
        vcct = _compress(vc, cmp_w_v[i].reshape(CMP_BLOCK, Dh, Dh).astype(BF16),
                         cmp_pos_v[i].reshape(CMP_BLOCK, 1, Dh).astype(F32), transpose_out=True)
        y_b = _nsa_attention(qs, kcc, vcct, ks, vst, kw, vwt, gt, B, S)

        h = _merge(main.reshape(T, MAIN_WIDTH), pool_w[i].astype(BF16), row(pool_scale[i]), y_b.reshape(T, -1),
                   y_c.reshape(T, -1), h, w_branch[i].astype(BF16), w_out[i].astype(BF16), S)
        h = _ffn_ple(h, row(norm_ffn[i]), w_ff1[i].astype(BF16), w_ff2[i].astype(BF16), row(norm_ple[i]),
                     w_ple_gate[i].astype(BF16), p[i].reshape(T, -1), w_ple_proj[i].astype(BF16),
                     row(norm_final), final_norm=(i == depth - 1))
    return h.reshape(B, S, D)
```

```python
import functools

import jax
import jax.numpy as jnp
import numpy as np
from jax import lax
from jax.experimental import pallas as pl
from jax.experimental.pallas import tpu as pltpu

F32, BF16 = jnp.float32, jnp.bfloat16

EPS = 1e-6
ROPE_THETA = 500000.0
POOL_WINDOWS = (2, 4, 8, 16)
POOL_GROUP_DIM = 128
POOL_HALO = max(POOL_WINDOWS)
NSA_HEADS = 8
NSA_KV_GROUPS = 2
NSA_HEADS_PER_GROUP = NSA_HEADS // NSA_KV_GROUPS
NSA_HEAD_DIM = 64
NSA_ROT_DIM = 16
CMP_BLOCK = 32
CMP_STRIDE = 16
SEL_BLOCK = 64
SEL_TOPK = 16
WINDOW = 512
Q_BLOCK = 128
GLA_HEADS = 4
GLA_DK = 64
GLA_DV = 128
GLA_GATE_RANK = 16
GLA_TAU = 16.0
GLA_CHUNK = 64
IN_SIZES = (512, 512, 768, 24, 256, 256, 512, 16, 512, 3072)

LANES = 128
MASKED = -1e30
LOG2E = 1.4426950408889634
Q_SCALE = NSA_HEAD_DIM ** -0.5 * LOG2E
V_ROWS = NSA_HEAD_DIM + 16
IMP_PAD = 8
GATE_ROWS = 32
TOPK_ROW_STEP = 64
VMEM_LIMIT = 56 * 1024 * 1024

MAIN_U, MAIN_QG, MAIN_KG, MAIN_VG, MAIN_RG, MAIN_GM, MAIN_A, MAIN_WIDTH = 0, 512, 768, 1024, 1536, 2048, 5120, 5376
NSA_Q, NSA_KC, NSA_KS, NSA_KW, NSA_VC, NSA_VS, NSA_VW, NSA_G, NSA_WIDTH = 0, 512, 640, 768, 896, 1024, 1152, 1280, 1536


def _nt(a, b):
    return lax.dot_general(a, b, (((1,), (1,)), ((), ())), preferred_element_type=F32)


def _tn(a, b):
    return lax.dot_general(a, b, (((0,), (0,)), ((), ())), preferred_element_type=F32)


def _dot(a, b):
    return jnp.dot(a, b, preferred_element_type=F32)


def _rms(x, gain):
    return x * lax.rsqrt(jnp.mean(x * x, axis=-1, keepdims=True) + EPS) * gain


def _proj_body(h_ref, g_ref, w_ref, o_ref, a_scr):
    @pl.when(pl.program_id(1) == 0)
    def _():
        a_scr[...] = _rms(h_ref[...], g_ref[...]).astype(BF16)

    o_ref[...] = _dot(a_scr[...], w_ref[...]).astype(o_ref.dtype)


def _norm_proj(h, gain, w, out_dtype, tn, tm=1024):
    T, D = h.shape
    N = w.shape[1]
    tm = min(tm, T)
    return pl.pallas_call(
        _proj_body,
        grid=(T // tm, N // tn),
        in_specs=[
            pl.BlockSpec((tm, D), lambda i, j: (i, 0)),
            pl.BlockSpec((1, D), lambda i, j: (0, 0)),
            pl.BlockSpec((D, tn), lambda i, j: (0, j)),
        ],
        out_specs=pl.BlockSpec((tm, tn), lambda i, j: (i, j)),
        out_shape=jax.ShapeDtypeStruct((T, N), out_dtype),
        scratch_shapes=[pltpu.VMEM((tm, D), BF16)],
        compiler_params=pltpu.CompilerParams(
            dimension_semantics=("parallel", "arbitrary"), vmem_limit_bytes=VMEM_LIMIT),
        name="norm_proj",
    )(h, gain, w)


def _pool_tile(cur, halo, t, w_ref, sc_ref):
    outs = []
    for g, win in enumerate(POOL_WINDOWS):
        lo, hi = g * POOL_GROUP_DIM, (g + 1) * POOL_GROUP_DIM
        x = cur[:, lo:hi]
        e = jnp.concatenate([halo[:, lo:hi], x], axis=0)
        step = 1
        while step < win:
            e = e[step:] + e[:-step]
            step *= 2
        wsum = e[POOL_HALO - (win - 1):]
        cnt = jnp.minimum(t + 1, win).astype(F32)
        pooled = wsum / cnt - x
        outs.append((_dot(pooled.astype(BF16), w_ref[g]) * sc_ref[:, lo:hi]).astype(BF16))
    return jnp.concatenate(outs, axis=1)


def _gla_body(q_ref, k_ref, v_ref, r_ref, a_ref, wg_ref, bg_ref, ng_ref, o_ref, st_ref, *, tc):
    @pl.when(pl.program_id(1) == 0)
    def _():
        st_ref[...] = jnp.zeros_like(st_ref)

    C = GLA_CHUNK
    n_chunks = tc // C
    causal = lax.broadcasted_iota(jnp.int32, (C, C), 0) >= lax.broadcasted_iota(jnp.int32, (C, C), 1)
    width = GLA_HEADS * GLA_DK
    hk = [slice(h * GLA_DK, (h + 1) * GLA_DK) for h in range(GLA_HEADS)]
    hv = [slice(h * GLA_DV, (h + 1) * GLA_DV) for h in range(GLA_HEADS)]
    rows = [slice(c * C, (c + 1) * C) for c in range(n_chunks)]

    z = _dot(a_ref[...], wg_ref[...]) + bg_ref[...]
    b = jax.nn.log_sigmoid(z) * (1.0 / GLA_TAU)
    row_in_chunk = lax.broadcasted_iota(jnp.int32, (tc, 1), 0) % C
    step = 1
    while step < C:
        shifted = jnp.concatenate([jnp.zeros((step, width), F32), b[:-step]], axis=0)
        b = b + jnp.where(row_in_chunk >= step, shifted, 0.0)
        step *= 2
    b_last = [b[(c + 1) * C - 1:(c + 1) * C, :] for c in range(n_chunks)]
    b_last_rows = jnp.concatenate([jnp.broadcast_to(bl, (C, width)) for bl in b_last], axis=0)
    qf = q_ref[...].astype(F32) * (GLA_DK ** -0.5)
    kf = k_ref[...].astype(F32)
    q_s = (qf * jnp.exp(b)).astype(BF16)
    k_s = (kf * jnp.exp(-b)).astype(BF16)
    k_t = (kf * jnp.exp(b_last_rows - b)).astype(BF16)

    att = [[jnp.where(causal, _nt(q_s[rows[c], hk[h]], k_s[rows[c], hk[h]]), 0.0).astype(BF16)
            for h in range(GLA_HEADS)] for c in range(n_chunks)]
    o_intra = [[_dot(att[c][h], v_ref[rows[c], hv[h]]) for h in range(GLA_HEADS)] for c in range(n_chunks)]
    kv = [[_tn(v_ref[rows[c], hv[h]], k_t[rows[c], hk[h]]) for h in range(GLA_HEADS)] for c in range(n_chunks)]

    state_t = [st_ref[h] for h in range(GLA_HEADS)]
    for c in range(n_chunks):
        decay = jnp.exp(b_last[c])
        for h in range(GLA_HEADS):
            o = o_intra[c][h] + _nt(q_s[rows[c], hk[h]], state_t[h].astype(BF16))
            state_t[h] = state_t[h] * decay[:, hk[h]] + kv[c][h]
            o = _rms(o, ng_ref[...])
            r = r_ref[rows[c], hv[h]].astype(F32)
            o_ref[rows[c], hv[h]] = (o * (r * jax.nn.sigmoid(r))).astype(o_ref.dtype)
    for h in range(GLA_HEADS):
        st_ref[h] = state_t[h]


def _gla_mixer(main, w_gate, b_gate, norm_g, B, S, tc=512):
    tc = min(tc, S)
    return pl.pallas_call(
        functools.partial(_gla_body, tc=tc),
        grid=(B, S // tc),
        in_specs=[
            pl.BlockSpec((None, tc, 256), lambda b, i: (b, i, MAIN_QG // 256)),
            pl.BlockSpec((None, tc, 256), lambda b, i: (b, i, MAIN_KG // 256)),
            pl.BlockSpec((None, tc, 512), lambda b, i: (b, i, MAIN_VG // 512)),
            pl.BlockSpec((None, tc, 512), lambda b, i: (b, i, MAIN_RG // 512)),
            pl.BlockSpec((None, tc, 128), lambda b, i: (b, i, MAIN_A // 128)),
            pl.BlockSpec((128, 256), lambda b, i: (0, 0)),
            pl.BlockSpec((1, 256), lambda b, i: (0, 0)),
            pl.BlockSpec((1, 128), lambda b, i: (0, 0)),
        ],
        out_specs=pl.BlockSpec((None, tc, 512), lambda b, i: (b, i, 0)),
        out_shape=jax.ShapeDtypeStruct((B, S, 512), BF16),
        scratch_shapes=[pltpu.VMEM((GLA_HEADS, GLA_DV, GLA_DK), F32)],
        compiler_params=pltpu.CompilerParams(dimension_semantics=("parallel", "arbitrary")),
        name="gla_mixer",
    )(main, main, main, main, main, w_gate, b_gate, norm_g)


def _nsa_prep_body(x_ref, pos_ref, invf_ref, spread_ref, swap_ref, qs_ref, kc_ref, vc_ref, ks_ref, kw_ref, vst_ref,
                   vwt_ref, gt_ref, *, blocks):
    half = NSA_ROT_DIM // 2
    hpg = NSA_HEADS_PER_GROUP
    ones_row = (lax.broadcasted_iota(jnp.int32, (V_ROWS - NSA_HEAD_DIM, Q_BLOCK), 0) == 0).astype(BF16)

    def pieces(a):
        p1 = a.astype(BF16).astype(F32)
        p2 = (a - p1).astype(BF16).astype(F32)
        return [p1, p2, (a - p1 - p2).astype(BF16).astype(F32)]

    for blk in range(blocks):
        tok = slice(blk * Q_BLOCK, (blk + 1) * Q_BLOCK)
        ang = invf_ref[...] * pos_ref[blk].astype(F32)
        lhs = jnp.concatenate(
            pieces(jnp.cos(ang)) + pieces(jnp.sin(ang)) + [jnp.ones((2 * half, Q_BLOCK), F32)], axis=0)
        tables = _tn(lhs.astype(BF16), spread_ref[...])
        cs = tables[:, :LANES]
        sn = tables[:, LANES:2 * LANES] + tables[:, 2 * LANES:]

        def rope(x):
            partner = _dot(jnp.concatenate([p.astype(BF16) for p in pieces(x)], axis=1), swap_ref[...])
            return x * cs + partner * sn

        def split_groups(x, ref, dtype):
            ref[0, tok, :] = x[:, :NSA_HEAD_DIM].astype(dtype)
            ref[1, tok, :] = x[:, NSA_HEAD_DIM:].astype(dtype)

        for pair in range(NSA_HEADS // 2):
            cols = slice(NSA_Q + pair * LANES, NSA_Q + (pair + 1) * LANES)
            qr_t = (rope(x_ref[tok, cols]) * Q_SCALE).T
            for sub in range(2):
                head = 2 * pair + sub
                grp, hp = head // hpg, head % hpg
                col0 = (blk * hpg + hp) * Q_BLOCK
                qs_ref[grp, :, col0:col0 + Q_BLOCK] = qr_t[sub * NSA_HEAD_DIM:(sub + 1) * NSA_HEAD_DIM].astype(BF16)
        split_groups(rope(x_ref[tok, NSA_KC:NSA_KC + LANES]), kc_ref, F32)
        split_groups(rope(x_ref[tok, NSA_KS:NSA_KS + LANES]), ks_ref, BF16)
        split_groups(rope(x_ref[tok, NSA_KW:NSA_KW + LANES]), kw_ref, BF16)
        split_groups(x_ref[tok, NSA_VC:NSA_VC + LANES], vc_ref, F32)
        vst = x_ref[tok, NSA_VS:NSA_VS + LANES].T
        vwt = x_ref[tok, NSA_VW:NSA_VW + LANES].T
        for grp in range(NSA_KV_GROUPS):
            for ref, v in ((vst_ref, vst), (vwt_ref, vwt)):
                ref[grp, :NSA_HEAD_DIM, tok] = v[grp * NSA_HEAD_DIM:(grp + 1) * NSA_HEAD_DIM].astype(BF16)
                ref[grp, NSA_HEAD_DIM:, tok] = ones_row
        gt_ref[blk] = jax.nn.sigmoid(x_ref[tok, NSA_G:NSA_G + LANES]).T[:GATE_ROWS]


def _rope_spread_matrix():
    half = NSA_ROT_DIM // 2
    lane = np.arange(LANES)
    d = lane % NSA_HEAD_DIM
    m = np.zeros((8 * half, 3 * LANES), np.float32)
    for piece in range(3):
        for f in range(half):
            hit = d % half == f
            m[piece * half + f, :LANES] = hit & (d < NSA_ROT_DIM)
            m[(3 + piece) * half + f, LANES:2 * LANES] = -1.0 * (hit & (d < half))
            m[(3 + piece) * half + f, 2 * LANES:] = hit & (d >= half) & (d < NSA_ROT_DIM)
    m[6 * half, :LANES] = d >= NSA_ROT_DIM
    return jnp.asarray(m, BF16)


def _rope_swap_matrix():
    half = NSA_ROT_DIM // 2
    lane = np.arange(LANES)
    d = lane % NSA_HEAD_DIM
    partner = np.where(d < half, lane + half, lane - half)
    m = np.zeros((LANES, LANES), np.float32)
    rotary = d < NSA_ROT_DIM
    m[partner[rotary], lane[rotary]] = 1.0
    return jnp.asarray(np.concatenate([m, m, m], axis=0), BF16)


def _nsa_prep(nsa_proj, positions, inv_freq, rope_spread, rope_swap, B, S, blocks=4):
    G, Dh, nq = NSA_KV_GROUPS, NSA_HEAD_DIM, S // Q_BLOCK
    blocks = min(blocks, nq)
    ts = blocks * Q_BLOCK
    tok = lambda dtype: jax.ShapeDtypeStruct((B, G, S, Dh), dtype)
    tok_spec = pl.BlockSpec((None, G, ts, Dh), lambda b, i: (b, 0, i, 0))
    tr = jax.ShapeDtypeStruct((B, G, V_ROWS, S), BF16)
    tr_spec = pl.BlockSpec((None, G, V_ROWS, ts), lambda b, i: (b, 0, 0, i))
    return pl.pallas_call(
        functools.partial(_nsa_prep_body, blocks=blocks),
        grid=(B, nq // blocks),
        in_specs=[
            pl.BlockSpec((None, ts, NSA_WIDTH), lambda b, i: (b, i, 0)),
            pl.BlockSpec((None, blocks, 1, Q_BLOCK), lambda b, i: (b, i, 0, 0)),
            pl.BlockSpec(inv_freq.shape, lambda b, i: (0, 0)),
            pl.BlockSpec(rope_spread.shape, lambda b, i: (0, 0)),
            pl.BlockSpec(rope_swap.shape, lambda b, i: (0, 0)),
        ],
        out_specs=[
            pl.BlockSpec((None, G, Dh, NSA_HEADS_PER_GROUP * ts), lambda b, i: (b, 0, 0, i)),
            tok_spec, tok_spec, tok_spec, tok_spec, tr_spec, tr_spec,
            pl.BlockSpec((None, blocks, GATE_ROWS, Q_BLOCK), lambda b, i: (b, i, 0, 0)),
        ],
        out_shape=[
            jax.ShapeDtypeStruct((B, G, Dh, NSA_HEADS_PER_GROUP * S), BF16),
            tok(F32), tok(F32), tok(BF16), tok(BF16), tr, tr,
            jax.ShapeDtypeStruct((B, nq, GATE_ROWS, Q_BLOCK), F32),
        ],
        compiler_params=pltpu.CompilerParams(dimension_semantics=("parallel", "parallel")),
        name="nsa_prep",
    )(nsa_proj, positions, inv_freq, rope_spread, rope_swap)


def _cmp_body(x_ref, w_ref, p_ref, o_ref, *, n_blk, transpose_out):
    half = CMP_BLOCK // 2
    first = jnp.zeros((n_blk, NSA_HEAD_DIM), F32)
    second = jnp.zeros((n_blk, NSA_HEAD_DIM), F32)
    for l in range(half):
        rows = x_ref[pl.ds(l, n_blk, stride=CMP_STRIDE), :]
        first = first + _dot((rows + p_ref[l]).astype(BF16), w_ref[l])
        second = second + _dot((rows + p_ref[half + l]).astype(BF16), w_ref[half + l])
    out = first + jnp.concatenate([second[1:], jnp.zeros((1, NSA_HEAD_DIM), F32)], axis=0)
    o_ref[...] = (out.T if transpose_out else out).astype(o_ref.dtype)


def _compress(x, w, pos, transpose_out):
    B, G, S, Dh = x.shape
    n_blk = S // CMP_STRIDE
    out_block = (None, None, Dh, n_blk) if transpose_out else (None, None, n_blk, Dh)
    out_shape = (B, G, Dh, n_blk) if transpose_out else (B, G, n_blk, Dh)
    return pl.pallas_call(
        functools.partial(_cmp_body, n_blk=n_blk, transpose_out=transpose_out),
        grid=(B, G),
        in_specs=[
            pl.BlockSpec((None, None, S, Dh), lambda b, g: (b, g, 0, 0)),
            pl.BlockSpec((CMP_BLOCK, Dh, Dh), lambda b, g: (0, 0, 0)),
            pl.BlockSpec((CMP_BLOCK, 1, Dh), lambda b, g: (0, 0, 0)),
        ],
        out_specs=pl.BlockSpec(out_block, lambda b, g: (b, g, 0, 0)),
        out_shape=jax.ShapeDtypeStruct(out_shape, BF16),
        compiler_params=pltpu.CompilerParams(
            dimension_semantics=("parallel", "parallel"), vmem_limit_bytes=VMEM_LIMIT),
        name="nsa_compress",
    )(x, w, pos)


def _nsa_body(*refs, blocks, **tiles):
    def one_block(qi, carry):
        _nsa_block(qi, *refs, blocks=blocks, **tiles)
        return carry

    lax.fori_loop(0, blocks, one_block, 0)


def _nsa_block(qi, q_ref, kcc_ref, vcct_ref, ks_ref, vst_ref, kw_ref, vwt_ref, gt_ref, o_ref,
               imp_scr, sel_scr, ocmp_scr, wins_scr, owin_scr, oslc_scr, *, blocks, kt_size, sub_size, cmp_chunk):
    grp = pl.program_id(1)
    t0 = (pl.program_id(2) * blocks + qi) * Q_BLOCK
    hpg = NSA_HEADS_PER_GROUP
    n_cmp = kcc_ref.shape[0]
    n_sel = sel_scr.shape[0]
    q_t = q_ref[:, pl.ds(pl.multiple_of(qi * (hpg * Q_BLOCK), hpg * Q_BLOCK), hpg * Q_BLOCK)]
    tq1 = t0 + lax.broadcasted_iota(jnp.int32, (1, Q_BLOCK), 1)
    heads = lambda x: jnp.concatenate([x] * hpg, axis=1)

    width = hpg * Q_BLOCK
    neg_inf = -jnp.inf
    eye = (lax.broadcasted_iota(jnp.int32, (Q_BLOCK, Q_BLOCK), 0)
           == lax.broadcasted_iota(jnp.int32, (Q_BLOCK, Q_BLOCK), 1)).astype(BF16)
    q_aug_t = jnp.concatenate([jnp.concatenate([eye] * hpg, axis=1), q_t], axis=0)

    ch = cmp_chunk
    n_chunks = (((t0 + Q_BLOCK - CMP_BLOCK) >> (CMP_STRIDE.bit_length() - 1)) + ch) // ch

    def compressed(n_sub):
        imp_scr[0:IMP_PAD, :] = jnp.zeros((IMP_PAD, Q_BLOCK), F32)
        imp_scr[IMP_PAD + n_sub * ch:, :] = jnp.zeros((n_cmp + IMP_PAD - n_sub * ch, Q_BLOCK), F32)
        biases = []
        for sub in range(n_sub):
            if sub >= n_sub - 2:
                cmp_end = (sub * ch + lax.broadcasted_iota(jnp.int32, (ch, 1), 0)) * CMP_STRIDE + (CMP_BLOCK - 1)
                biases.append(jnp.where(cmp_end <= tq1, 0.0, MASKED).astype(BF16))
            else:
                biases.append(jnp.zeros((ch, Q_BLOCK), BF16))
        k_aug = jnp.concatenate([jnp.concatenate(biases, axis=0), kcc_ref[0:n_sub * ch, :]], axis=1)
        scores = _dot(k_aug, q_aug_t)
        parts = []
        for sub in range(n_sub):
            s = scores[sub * ch:(sub + 1) * ch]
            m_sub = jnp.max(s, axis=0, keepdims=True)
            e = jnp.exp2(s - m_sub)
            parts.append((m_sub, jnp.sum(e, axis=0, keepdims=True),
                          _dot(vcct_ref[:, sub * ch:(sub + 1) * ch], e.astype(BF16)), e))
        m = parts[0][0]
        for m_sub, _, _, _ in parts[1:]:
            m = jnp.maximum(m, m_sub)
        scales = [jnp.exp2(m_sub - m) for m_sub, _, _, _ in parts]
        l = scales[0] * parts[0][1]
        acc = scales[0] * parts[0][2]
        for scale, (_, l_sub, acc_sub, _) in zip(scales[1:], parts[1:]):
            l = l + scale * l_sub
            acc = acc + scale * acc_sub
        inv_l = jnp.where(heads(tq1 >= CMP_BLOCK - 1), 1.0 / l, 0.0)
        ocmp_scr[...] = acc * inv_l
        for sub in range(n_sub):
            pc = parts[sub][3] * (scales[sub] * inv_l)
            p_sum = pc[:, :Q_BLOCK]
            for hp in range(1, hpg):
                p_sum = p_sum + pc[:, hp * Q_BLOCK:(hp + 1) * Q_BLOCK]
            imp_scr[IMP_PAD + sub * ch:IMP_PAD + (sub + 1) * ch, :] = p_sum

    for variant in range(n_cmp // ch):
        pl.when(n_chunks == variant + 1)(functools.partial(compressed, variant + 1))
    o_cmp = ocmp_scr[...]

    ratio = SEL_BLOCK // CMP_STRIDE
    sel_shift = SEL_BLOCK.bit_length() - 1
    cur = tq1 >> sel_shift
    n_forced = 3

    span = WINDOW + Q_BLOCK
    k_start = pl.multiple_of(jnp.maximum(t0 - WINDOW, 0), Q_BLOCK)

    def window_scores():
        diff = tq1 - (k_start + lax.broadcasted_iota(jnp.int32, (span, 1), 0))
        inside = pltpu.bitcast(diff, jnp.uint32) < jnp.uint32(WINDOW)
        bias = jnp.where(inside, 0.0, MASKED).astype(BF16)
        wins_scr[...] = _dot(jnp.concatenate([bias, kw_ref[pl.ds(k_start, span), :]], axis=1), q_aug_t)

    def window_values():
        sw = wins_scr[...]
        pw = jnp.exp2(sw - jnp.max(sw, axis=0, keepdims=True)).astype(BF16)
        acc_win = _dot(vwt_ref[:, pl.ds(k_start, span)], pw)
        owin_scr[...] = acc_win[:NSA_HEAD_DIM] / acc_win[NSA_HEAD_DIM:NSA_HEAD_DIM + 1]

    def select_blocks(rows):
        window_scores()
        imp = imp_scr[pl.ds(IMP_PAD - 1, rows, stride=ratio), :]
        for k in range(1, ratio + 1):
            imp = imp + imp_scr[pl.ds(IMP_PAD - 1 + k, rows, stride=ratio), :]
        j = lax.broadcasted_iota(jnp.int32, (rows, Q_BLOCK), 0)
        j_f = j.astype(F32)
        forced = (j == 0) | (j == cur) | (j == cur - 1)
        valid = j <= cur

        def pick(_, sc):
            best = jnp.max(sc, axis=0, keepdims=True)
            idx = jnp.min(jnp.where(sc == best, j_f, float(rows)), axis=0, keepdims=True)
            return jnp.where(j_f == idx, neg_inf, sc)

        left = lax.fori_loop(0, max(min(SEL_TOPK, n_sel) - n_forced, 0), pick,
                             jnp.where(valid & ~forced, imp, neg_inf), unroll=True)
        sel_scr[0:rows, :] = jnp.where(forced | (valid & (left == neg_inf)), 0.0, MASKED)
        if rows < n_sel:
            sel_scr[rows:n_sel, :] = jnp.full((n_sel - rows, Q_BLOCK), MASKED, F32)

    row_step = min(TOPK_ROW_STEP, n_sel)
    last_block = (t0 + Q_BLOCK - 1) >> sel_shift
    for variant in range(n_sel // row_step):
        pl.when(last_block // row_step == variant)(functools.partial(select_blocks, (variant + 1) * row_step))

    blocks_per_tile = kt_size // SEL_BLOCK

    def slc_tile(kt, carry, n_sub, causal_from, after_scores=None):
        m_run, acc = carry
        base = pl.multiple_of(kt * kt_size, kt_size)
        biases = []
        for sub in range(n_sub):
            blk0 = kt * blocks_per_tile + sub * (sub_size // SEL_BLOCK)
            bias = jnp.concatenate(
                [jnp.broadcast_to(sel_scr[pl.ds(blk0 + jb, 1), :], (SEL_BLOCK, Q_BLOCK))
                 for jb in range(sub_size // SEL_BLOCK)], axis=0)
            if sub >= causal_from:
                kpos = base + sub * sub_size + lax.broadcasted_iota(jnp.int32, (sub_size, 1), 0)
                bias = jnp.where(kpos <= tq1, bias, MASKED)
            biases.append(bias.astype(BF16))
        k_aug = jnp.concatenate([jnp.concatenate(biases, axis=0), ks_ref[pl.ds(base, n_sub * sub_size), :]], axis=1)
        scores = _dot(k_aug, q_aug_t)
        if after_scores is not None:
            after_scores()
        parts = []
        for sub in range(n_sub):
            k0 = pl.multiple_of(base + sub * sub_size, sub_size)
            v_t = vst_ref[:, pl.ds(k0, sub_size)]
            m_halves, acc_halves = [], []
            for half in range(2):
                cols = slice(half * (width // 2), (half + 1) * (width // 2))
                sc = scores[sub * sub_size:(sub + 1) * sub_size, cols]
                m_half = jnp.max(sc, axis=0, keepdims=True)
                m_halves.append(m_half)
                acc_halves.append(_dot(v_t, jnp.exp2(sc - m_half).astype(BF16)))
            parts.append((jnp.concatenate(m_halves, axis=1), jnp.concatenate(acc_halves, axis=1)))
        m_new = m_run
        for m_sub, _ in parts:
            m_new = jnp.maximum(m_new, m_sub)
        acc = jnp.exp2(m_run - m_new) * acc
        for m_sub, acc_sub in parts:
            acc = acc + jnp.exp2(m_sub - m_new) * acc_sub
        return m_new, acc

    subs_per_tile = kt_size // sub_size
    n_full = t0 // kt_size
    carry = lax.fori_loop(
        0, n_full, functools.partial(slc_tile, n_sub=subs_per_tile, causal_from=subs_per_tile),
        (jnp.full((1, width), MASKED, F32), jnp.zeros((V_ROWS, width), F32)))
    tail_step = min(2, subs_per_tile)
    tail_len = (t0 - n_full * kt_size) // (tail_step * sub_size)
    for variant in range(subs_per_tile // tail_step):
        n_sub = tail_step * (variant + 1)

        @pl.when(tail_len == variant)
        def _(n_sub=n_sub):
            _, acc_slc = slc_tile(n_full, carry, n_sub=n_sub, causal_from=n_sub - tail_step,
                                  after_scores=window_values)
            oslc_scr[...] = acc_slc[:NSA_HEAD_DIM] / acc_slc[NSA_HEAD_DIM:NSA_HEAD_DIM + 1]

    o_slc = oslc_scr[...]
    o_win = owin_scr[...]

    for hp in range(hpg):
        cols = slice(hp * Q_BLOCK, (hp + 1) * Q_BLOCK)
        row = (grp * hpg + hp) * 3
        o_t = (gt_ref[qi, pl.ds(row, 1), :] * o_cmp[:, cols] + gt_ref[qi, pl.ds(row + 1, 1), :] * o_slc[:, cols]
               + gt_ref[qi, pl.ds(row + 2, 1), :] * o_win[:, cols])
        o_ref[pl.ds(pl.multiple_of(qi * Q_BLOCK, Q_BLOCK), Q_BLOCK), hp * NSA_HEAD_DIM:(hp + 1) * NSA_HEAD_DIM] = (
            o_t.T.astype(o_ref.dtype))


def _nsa_attention(qs, kcc, vcct, ks, vst, kw, vwt, gt, B, S, blocks=4, kt_size=4096, sub_size=256, cmp_chunk=256):
    G, Dh, nq, hpg = NSA_KV_GROUPS, NSA_HEAD_DIM, S // Q_BLOCK, NSA_HEADS_PER_GROUP
    kt_size = min(kt_size, S)
    blocks = min(blocks, nq)
    n_cmp, n_sel = S // CMP_STRIDE, S // SEL_BLOCK
    cmp_chunk = min(cmp_chunk, n_cmp)
    per_group = lambda shape: pl.BlockSpec((None, None) + shape, lambda b, g, i: (b, g, 0, 0))
    return pl.pallas_call(
        functools.partial(_nsa_body, blocks=blocks, kt_size=kt_size, sub_size=min(sub_size, kt_size),
                          cmp_chunk=cmp_chunk),
        grid=(B, G, nq // blocks),
        in_specs=[
            pl.BlockSpec((None, None, Dh, blocks * hpg * Q_BLOCK), lambda b, g, i: (b, g, 0, i)),
            per_group((n_cmp, Dh)), per_group((Dh, n_cmp)),
            per_group((S, Dh)), per_group((V_ROWS, S)),
            per_group((S, Dh)), per_group((V_ROWS, S)),
            pl.BlockSpec((None, blocks, GATE_ROWS, Q_BLOCK), lambda b, g, i: (b, i, 0, 0)),
        ],
        out_specs=pl.BlockSpec((None, blocks * Q_BLOCK, hpg * Dh), lambda b, g, i: (b, i, g)),
        out_shape=jax.ShapeDtypeStruct((B, S, NSA_HEADS * Dh), BF16),
        scratch_shapes=[
            pltpu.VMEM((n_cmp + 2 * IMP_PAD, Q_BLOCK), F32),
            pltpu.VMEM((n_sel, Q_BLOCK), F32),
            pltpu.VMEM((Dh, hpg * Q_BLOCK), F32),
            pltpu.VMEM((WINDOW + Q_BLOCK, hpg * Q_BLOCK), F32),
            pltpu.VMEM((Dh, hpg * Q_BLOCK), F32),
            pltpu.VMEM((Dh, hpg * Q_BLOCK), F32),
        ],
        compiler_params=pltpu.CompilerParams(
            dimension_semantics=("parallel", "parallel", "arbitrary"), vmem_limit_bytes=VMEM_LIMIT),
        name="nsa_attention",
    )(qs, kcc, vcct, ks, vst, kw, vwt, gt)


def _merge_body(u_ref, halo_ref, pw_ref, ps_ref, yb_ref, yc_ref, ga_ref, gb_ref, gc_ref, h_ref, wb_ref, wo_ref, o_ref,
                *, tm, tiles_per_seq):
    tile_in_seq = lax.rem(pl.program_id(0), tiles_per_seq)
    halo = jnp.where(tile_in_seq == 0, 0.0, halo_ref[...].astype(F32))
    t = tile_in_seq * tm + lax.broadcasted_iota(jnp.int32, (tm, 1), 0)
    ya = _pool_tile(u_ref[...].astype(F32), halo, t, pw_ref, ps_ref)
    merged = jax.nn.sigmoid(ga_ref[...].astype(F32)) * _dot(ya, wb_ref[0])
    merged = merged + jax.nn.sigmoid(gb_ref[...].astype(F32)) * _dot(yb_ref[...], wb_ref[1])
    merged = merged + jax.nn.sigmoid(gc_ref[...].astype(F32)) * _dot(yc_ref[...], wb_ref[2])
    o_ref[...] = h_ref[...] + _dot(merged.astype(BF16), wo_ref[...])


def _merge(main, pool_w, pool_scale, yb, yc, h, w_branch, w_out, S, tm=1024):
    T, D = h.shape
    tm = min(tm, S)
    halo_blocks = tm // POOL_HALO
    y_spec = pl.BlockSpec((tm, 512), lambda i: (i, 0))
    gate_spec = lambda k: pl.BlockSpec((tm, D), lambda i: (i, MAIN_GM // D + k))
    return pl.pallas_call(
        functools.partial(_merge_body, tm=tm, tiles_per_seq=S // tm),
        grid=(T // tm,),
        in_specs=[
            pl.BlockSpec((tm, 512), lambda i: (i, MAIN_U // 512)),
            pl.BlockSpec((POOL_HALO, 512), lambda i: (jnp.maximum(i * halo_blocks - 1, 0), MAIN_U // 512)),
            pl.BlockSpec((len(POOL_WINDOWS), POOL_GROUP_DIM, POOL_GROUP_DIM), lambda i: (0, 0, 0)),
            pl.BlockSpec((1, 512), lambda i: (0, 0)),
            y_spec, y_spec, gate_spec(0), gate_spec(1), gate_spec(2),
            pl.BlockSpec((tm, D), lambda i: (i, 0)),
            pl.BlockSpec((3, 512, D), lambda i: (0, 0, 0)),
            pl.BlockSpec((D, D), lambda i: (0, 0)),
        ],
        out_specs=pl.BlockSpec((tm, D), lambda i: (i, 0)),
        out_shape=jax.ShapeDtypeStruct((T, D), F32),
        compiler_params=pltpu.CompilerParams(dimension_semantics=("parallel",), vmem_limit_bytes=VMEM_LIMIT),
        name="merge_out",
    )(main, main, pool_w, pool_scale, yb, yc, main, main, main, h, w_branch, w_out)


def _ffn_body(h_ref, gf_ref, w1_ref, w2_ref, gp_ref, wg_ref, p_ref, wp_ref, gl_ref, o_ref, f_scr, acc_scr, *, final_norm):
    j = pl.program_id(1)

    @pl.when(j == 0)
    def _():
        f_scr[...] = _rms(h_ref[...], gf_ref[...]).astype(BF16)
        acc_scr[...] = jnp.zeros_like(acc_scr)

    a = jnp.maximum(_dot(f_scr[...], w1_ref[...]), 0.0)
    acc_scr[...] += _dot((a * a).astype(BF16), w2_ref[...])

    @pl.when(j == pl.num_programs(1) - 1)
    def _():
        h2 = h_ref[...] + acc_scr[...]
        gate = jax.nn.sigmoid(_dot(_rms(h2, gp_ref[...]).astype(BF16), wg_ref[...]))
        h3 = h2 + gate * _dot(p_ref[...].astype(BF16), wp_ref[...])
        o_ref[...] = _rms(h3, gl_ref[...]) if final_norm else h3


def _ffn_ple(h, norm_ffn, w1, w2, norm_ple, w_gate, p, w_proj, norm_last, final_norm, tm=1024, tf=1024):
    T, D = h.shape
    tm = min(tm, T)
    Fdim = w1.shape[1]
    Pdim = p.shape[1]
    vec = pl.BlockSpec((1, D), lambda i, j: (0, 0))
    return pl.pallas_call(
        functools.partial(_ffn_body, final_norm=final_norm),
        grid=(T // tm, Fdim // tf),
        in_specs=[
            pl.BlockSpec((tm, D), lambda i, j: (i, 0)), vec,
            pl.BlockSpec((D, tf), lambda i, j: (0, j)),
            pl.BlockSpec((tf, D), lambda i, j: (j, 0)),
            vec,
            pl.BlockSpec((D, D), lambda i, j: (0, 0)),
            pl.BlockSpec((tm, Pdim), lambda i, j: (i, 0)),
            pl.BlockSpec((Pdim, D), lambda i, j: (0, 0)),
            vec,
        ],
        out_specs=pl.BlockSpec((tm, D), lambda i, j: (i, 0)),
        out_shape=jax.ShapeDtypeStruct((T, D), F32),
        scratch_shapes=[pltpu.VMEM((tm, D), BF16), pltpu.VMEM((tm, D), F32)],
        compiler_params=pltpu.CompilerParams(
            dimension_semantics=("parallel", "arbitrary"), vmem_limit_bytes=VMEM_LIMIT),
        name="ffn_ple",
    )(h, norm_ffn, w1, w2, norm_ple, w_gate, p, w_proj, norm_last)


def _split_w_in(w):
    parts, off = [], 0
    for sz in IN_SIZES:
        parts.append(w[:, off:off + sz])
        off += sz
    u, q_nsa, kv_nsa, g_nsa, q_gla, k_gla, v_gla, a_gla, r_gla, g_merge = parts
    D = w.shape[0]
    pad = lambda n: jnp.zeros((D, n), w.dtype)
    main = jnp.concatenate([u, q_gla, k_gla, v_gla, r_gla, g_merge, a_gla, pad(MAIN_WIDTH - MAIN_A - GLA_GATE_RANK)], axis=1)
    kv = [kv_nsa[:, s * LANES:(s + 1) * LANES] for s in range(6)]
    nsa = jnp.concatenate([q_nsa, kv[0], kv[2], kv[4], kv[1], kv[3], kv[5], g_nsa, pad(NSA_WIDTH - NSA_G - 24)], axis=1)
    return main.astype(BF16), nsa.astype(BF16)


def kernel(x, p, positions, norm_mix, w_in, pool_w, pool_scale, cmp_pos_k, cmp_w_k, cmp_pos_v, cmp_w_v, gla_w_gate, gla_b_gate, gla_norm, w_branch, w_out, norm_ffn, w_ff1, w_ff2, norm_ple, w_ple_gate, w_ple_proj, norm_final):
    B, S, D = x.shape
    depth = w_in.shape[0]
    T = B * S
    Dh = NSA_HEAD_DIM
    row = lambda v: v.reshape(1, -1).astype(F32)

    half = NSA_ROT_DIM // 2
    inv_freq = jnp.power(ROPE_THETA, -jnp.arange(half, dtype=F32) * (2.0 / NSA_ROT_DIM)).reshape(half, 1)
    pos_rows = positions.reshape(B, S // Q_BLOCK, 1, Q_BLOCK)
    rope_spread, rope_swap = _rope_spread_matrix(), _rope_swap_matrix()

    h = x.reshape(T, D)
    for i in range(depth):
        w_main, w_nsa = _split_w_in(w_in[i])
        main = _norm_proj(h, row(norm_mix[i]), w_main, BF16, tn=MAIN_WIDTH // 3).reshape(B, S, MAIN_WIDTH)
        nsa_proj = _norm_proj(h, row(norm_mix[i]), w_nsa, F32, tn=NSA_WIDTH).reshape(B, S, NSA_WIDTH)

        wg = jnp.zeros((LANES, GLA_HEADS * GLA_DK), BF16).at[:GLA_GATE_RANK].set(gla_w_gate[i].astype(BF16))
        y_c = _gla_mixer(main, wg, row(gla_b_gate[i]), row(gla_norm[i]), B, S)

        qs, kc, vc, ks, kw, vst, vwt, gt = _nsa_prep(nsa_proj, pos_rows, inv_freq, rope_spread, rope_swap, B, S)
        kcc = _compress(kc, cmp_w_k[i].reshape(CMP_BLOCK, Dh, Dh).astype(BF16),
                        cmp_pos_k[i].reshape(CMP_BLOCK, 1, Dh).astype(F32), transpose_out=False)
        vcct = _compress(vc, cmp_w_v[i].reshape(CMP_BLOCK, Dh, Dh).astype(BF16),
                         cmp_pos_v[i].reshape(CMP_BLOCK, 1, Dh).astype(F32), transpose_out=True)
        y_b = _nsa_attention(qs, kcc, vcct, ks, vst, kw, vwt, gt, B, S)

        h = _merge(main.reshape(T, MAIN_WIDTH), pool_w[i].astype(BF16), row(pool_scale[i]), y_b.reshape(T, -1),
                   y_c.reshape(T, -1), h, w_branch[i].astype(BF16), w_out[i].astype(BF16), S)
        h = _ffn_ple(h, row(norm_ffn[i]), w_ff1[i].astype(BF16), w_ff2[i].astype(BF16), row(norm_ple[i]),
                     w_ple_gate[i].astype(BF16), p[i].reshape(T, -1), w_ple_proj[i].astype(BF16),
                     row(norm_final), final_norm=(i == depth - 1))
    return h.reshape(B, S, D)
```

```python
import functools

import jax
import jax.numpy as jnp
import numpy as np
from jax import lax
from jax.experimental import pallas as pl
from jax.experimental.pallas import tpu as pltpu

F32, BF16 = jnp.float32, jnp.bfloat16

EPS = 1e-6
ROPE_THETA = 500000.0
POOL_WINDOWS = (2, 4, 8, 16)
POOL_GROUP_DIM = 128
POOL_HALO = max(POOL_WINDOWS)
NSA_HEADS = 8
NSA_KV_GROUPS = 2
NSA_HEADS_PER_GROUP = NSA_HEADS // NSA_KV_GROUPS
NSA_HEAD_DIM = 64
NSA_ROT_DIM = 16
CMP_BLOCK = 32
CMP_STRIDE = 16
SEL_BLOCK = 64
SEL_TOPK = 16
WINDOW = 512
Q_BLOCK = 128
GLA_HEADS = 4
GLA_DK = 64
GLA_DV = 128
GLA_GATE_RANK = 16
GLA_TAU = 16.0
GLA_CHUNK = 64
IN_SIZES = (512, 512, 768, 24, 256, 256, 512, 16, 512, 3072)

LANES = 128
MASKED = -1e30
LOG2E = 1.4426950408889634
Q_SCALE = NSA_HEAD_DIM ** -0.5 * LOG2E
V_ROWS = NSA_HEAD_DIM + 16
IMP_PAD = 8
GATE_ROWS = 32
TOPK_ROW_STEP = 64
VMEM_LIMIT = 56 * 1024 * 1024

MAIN_U, MAIN_QG, MAIN_KG, MAIN_VG, MAIN_RG, MAIN_GM, MAIN_A, MAIN_WIDTH = 0, 512, 768, 1024, 1536, 2048, 5120, 5376
NSA_Q, NSA_KC, NSA_KS, NSA_KW, NSA_VC, NSA_VS, NSA_VW, NSA_G, NSA_WIDTH = 0, 512, 640, 768, 896, 1024, 1152, 1280, 1536


def _nt(a, b):
    return lax.dot_general(a, b, (((1,), (1,)), ((), ())), preferred_element_type=F32)


def _tn(a, b):
    return lax.dot_general(a, b, (((0,), (0,)), ((), ())), preferred_element_type=F32)


def _dot(a, b):
    return jnp.dot(a, b, preferred_element_type=F32)


def _rms(x, gain):
    return x * lax.rsqrt(jnp.mean(x * x, axis=-1, keepdims=True) + EPS) * gain


def _proj_body(h_ref, g_ref, w_ref, o_ref, a_scr):
    @pl.when(pl.program_id(1) == 0)
    def _():
        a_scr[...] = _rms(h_ref[...], g_ref[...]).astype(BF16)

    o_ref[...] = _dot(a_scr[...], w_ref[...]).astype(o_ref.dtype)


def _norm_proj(h, gain, w, out_dtype, tn, tm=1024):
    T, D = h.shape
    N = w.shape[1]
    tm = min(tm, T)
    return pl.pallas_call(
        _proj_body,
        grid=(T // tm, N // tn),
        in_specs=[
            pl.BlockSpec((tm, D), lambda i, j: (i, 0)),
            pl.BlockSpec((1, D), lambda i, j: (0, 0)),
            pl.BlockSpec((D, tn), lambda i, j: (0, j)),
        ],
        out_specs=pl.BlockSpec((tm, tn), lambda i, j: (i, j)),
        out_shape=jax.ShapeDtypeStruct((T, N), out_dtype),
        scratch_shapes=[pltpu.VMEM((tm, D), BF16)],
        compiler_params=pltpu.CompilerParams(
            dimension_semantics=("parallel", "arbitrary"), vmem_limit_bytes=VMEM_LIMIT),
        name="norm_proj",
    )(h, gain, w)


def _pool_tile(cur, halo, t, w_ref, sc_ref):
    outs = []
    for g, win in enumerate(POOL_WINDOWS):
        lo, hi = g * POOL_GROUP_DIM, (g + 1) * POOL_GROUP_DIM
        x = cur[:, lo:hi]
        e = jnp.concatenate([halo[:, lo:hi], x], axis=0)
        step = 1
        while step < win:
            e = e[step:] + e[:-step]
            step *= 2
        wsum = e[POOL_HALO - (win - 1):]
        cnt = jnp.minimum(t + 1, win).astype(F32)
        pooled = wsum / cnt - x
        outs.append((_dot(pooled.astype(BF16), w_ref[g]) * sc_ref[:, lo:hi]).astype(BF16))
    return jnp.concatenate(outs, axis=1)


def _gla_body(q_ref, k_ref, v_ref, r_ref, a_ref, wg_ref, bg_ref, ng_ref, o_ref, st_ref, *, tc):
    @pl.when(pl.program_id(1) == 0)
    def _():
        st_ref[...] = jnp.zeros_like(st_ref)

    C = GLA_CHUNK
    n_chunks = tc // C
    causal = lax.broadcasted_iota(jnp.int32, (C, C), 0) >= lax.broadcasted_iota(jnp.int32, (C, C), 1)
    width = GLA_HEADS * GLA_DK
    hk = [slice(h * GLA_DK, (h + 1) * GLA_DK) for h in range(GLA_HEADS)]
    hv = [slice(h * GLA_DV, (h + 1) * GLA_DV) for h in range(GLA_HEADS)]
    rows = [slice(c * C, (c + 1) * C) for c in range(n_chunks)]

    z = _dot(a_ref[...], wg_ref[...]) + bg_ref[...]
    b = jax.nn.log_sigmoid(z) * (1.0 / GLA_TAU)
    row_in_chunk = lax.broadcasted_iota(jnp.int32, (tc, 1), 0) % C
    step = 1
    while step < C:
        shifted = jnp.concatenate([jnp.zeros((step, width), F32), b[:-step]], axis=0)
        b = b + jnp.where(row_in_chunk >= step, shifted, 0.0)
        step *= 2
    b_last = [b[(c + 1) * C - 1:(c + 1) * C, :] for c in range(n_chunks)]
    b_last_rows = jnp.concatenate([jnp.broadcast_to(bl, (C, width)) for bl in b_last], axis=0)
    qf = q_ref[...].astype(F32) * (GLA_DK ** -0.5)
    kf = k_ref[...].astype(F32)
    q_s = (qf * jnp.exp(b)).astype(BF16)
    k_s = (kf * jnp.exp(-b)).astype(BF16)
    k_t = (kf * jnp.exp(b_last_rows - b)).astype(BF16)

    att = [[jnp.where(causal, _nt(q_s[rows[c], hk[h]], k_s[rows[c], hk[h]]), 0.0).astype(BF16)
            for h in range(GLA_HEADS)] for c in range(n_chunks)]
    o_intra = [[_dot(att[c][h], v_ref[rows[c], hv[h]]) for h in range(GLA_HEADS)] for c in range(n_chunks)]
    kv = [[_tn(v_ref[rows[c], hv[h]], k_t[rows[c], hk[h]]) for h in range(GLA_HEADS)] for c in range(n_chunks)]

    state_t = [st_ref[h] for h in range(GLA_HEADS)]
    for c in range(n_chunks):
        decay = jnp.exp(b_last[c])
        for h in range(GLA_HEADS):
            o = o_intra[c][h] + _nt(q_s[rows[c], hk[h]], state_t[h].astype(BF16))
            state_t[h] = state_t[h] * decay[:, hk[h]] + kv[c][h]
            o = _rms(o, ng_ref[...])
            r = r_ref[rows[c], hv[h]].astype(F32)
            o_ref[rows[c], hv[h]] = (o * (r * jax.nn.sigmoid(r))).astype(o_ref.dtype)
    for h in range(GLA_HEADS):
        st_ref[h] = state_t[h]


def _gla_mixer(main, w_gate, b_gate, norm_g, B, S, tc=512):
    tc = min(tc, S)
    return pl.pallas_call(
        functools.partial(_gla_body, tc=tc),
        grid=(B, S // tc),
        in_specs=[
            pl.BlockSpec((None, tc, 256), lambda b, i: (b, i, MAIN_QG // 256)),
            pl.BlockSpec((None, tc, 256), lambda b, i: (b, i, MAIN_KG // 256)),
            pl.BlockSpec((None, tc, 512), lambda b, i: (b, i, MAIN_VG // 512)),
            pl.BlockSpec((None, tc, 512), lambda b, i: (b, i, MAIN_RG // 512)),
            pl.BlockSpec((None, tc, 128), lambda b, i: (b, i, MAIN_A // 128)),
            pl.BlockSpec((128, 256), lambda b, i: (0, 0)),
            pl.BlockSpec((1, 256), lambda b, i: (0, 0)),
            pl.BlockSpec((1, 128), lambda b, i: (0, 0)),
        ],
        out_specs=pl.BlockSpec((None, tc, 512), lambda b, i: (b, i, 0)),
        out_shape=jax.ShapeDtypeStruct((B, S, 512), BF16),
        scratch_shapes=[pltpu.VMEM((GLA_HEADS, GLA_DV, GLA_DK), F32)],
        compiler_params=pltpu.CompilerParams(dimension_semantics=("parallel", "arbitrary")),
        name="gla_mixer",
    )(main, main, main, main, main, w_gate, b_gate, norm_g)


def _nsa_prep_body(x_ref, pos_ref, invf_ref, spread_ref, swap_ref, qs_ref, kc_ref, vc_ref, ks_ref, kw_ref, vst_ref,
                   vwt_ref, gt_ref, *, blocks):
    half = NSA_ROT_DIM // 2
    hpg = NSA_HEADS_PER_GROUP
    ones_row = (lax.broadcasted_iota(jnp.int32, (V_ROWS - NSA_HEAD_DIM, Q_BLOCK), 0) == 0).astype(BF16)

    def pieces(a):
        p1 = a.astype(BF16).astype(F32)
        p2 = (a - p1).astype(BF16).astype(F32)
        return [p1, p2, (a - p1 - p2).astype(BF16).astype(F32)]

    for blk in range(blocks):
        tok = slice(blk * Q_BLOCK, (blk + 1) * Q_BLOCK)
        ang = invf_ref[...] * pos_ref[blk].astype(F32)
        lhs = jnp.concatenate(
            pieces(jnp.cos(ang)) + pieces(jnp.sin(ang)) + [jnp.ones((2 * half, Q_BLOCK), F32)], axis=0)
        tables = _tn(lhs.astype(BF16), spread_ref[...])
        cs = tables[:, :LANES]
        sn = tables[:, LANES:2 * LANES] + tables[:, 2 * LANES:]

        def rope(x):
            partner = _dot(jnp.concatenate([p.astype(BF16) for p in pieces(x)], axis=1), swap_ref[...])
            return x * cs + partner * sn

        def split_groups(x, ref, dtype):
            ref[0, tok, :] = x[:, :NSA_HEAD_DIM].astype(dtype)
            ref[1, tok, :] = x[:, NSA_HEAD_DIM:].astype(dtype)

        for pair in range(NSA_HEADS // 2):
            cols = slice(NSA_Q + pair * LANES, NSA_Q + (pair + 1) * LANES)
            qr_t = (rope(x_ref[tok, cols]) * Q_SCALE).T
            for sub in range(2):
                head = 2 * pair + sub
                grp, hp = head // hpg, head % hpg
                col0 = (blk * hpg + hp) * Q_BLOCK
                qs_ref[grp, :, col0:col0 + Q_BLOCK] = qr_t[sub * NSA_HEAD_DIM:(sub + 1) * NSA_HEAD_DIM].astype(BF16)
        split_groups(rope(x_ref[tok, NSA_KC:NSA_KC + LANES]), kc_ref, F32)
        split_groups(rope(x_ref[tok, NSA_KS:NSA_KS + LANES]), ks_ref, BF16)
        split_groups(rope(x_ref[tok, NSA_KW:NSA_KW + LANES]), kw_ref, BF16)
        split_groups(x_ref[tok, NSA_VC:NSA_VC + LANES], vc_ref, F32)
        vst = x_ref[tok, NSA_VS:NSA_VS + LANES].T
        vwt = x_ref[tok, NSA_VW:NSA_VW + LANES].T
        for grp in range(NSA_KV_GROUPS):
            for ref, v in ((vst_ref, vst), (vwt_ref, vwt)):
                ref[grp, :NSA_HEAD_DIM, tok] = v[grp * NSA_HEAD_DIM:(grp + 1) * NSA_HEAD_DIM].astype(BF16)
                ref[grp, NSA_HEAD_DIM:, tok] = ones_row
        gt_ref[blk] = jax.nn.sigmoid(x_ref[tok, NSA_G:NSA_G + LANES]).T[:GATE_ROWS]


def _rope_spread_matrix():
    half = NSA_ROT_DIM // 2
    lane = np.arange(LANES)
    d = lane % NSA_HEAD_DIM
    m = np.zeros((8 * half, 3 * LANES), np.float32)
    for piece in range(3):
        for f in range(half):
            hit = d % half == f
            m[piece * half + f, :LANES] = hit & (d < NSA_ROT_DIM)
            m[(3 + piece) * half + f, LANES:2 * LANES] = -1.0 * (hit & (d < half))
            m[(3 + piece) * half + f, 2 * LANES:] = hit & (d >= half) & (d < NSA_ROT_DIM)
    m[6 * half, :LANES] = d >= NSA_ROT_DIM
    return jnp.asarray(m, BF16)


def _rope_swap_matrix():
    half = NSA_ROT_DIM // 2
    lane = np.arange(LANES)
    d = lane % NSA_HEAD_DIM
    partner = np.where(d < half, lane + half, lane - half)
    m = np.zeros((LANES, LANES), np.float32)
    rotary = d < NSA_ROT_DIM
    m[partner[rotary], lane[rotary]] = 1.0
    return jnp.asarray(np.concatenate([m, m, m], axis=0), BF16)


def _nsa_prep(nsa_proj, positions, inv_freq, rope_spread, rope_swap, B, S, blocks=4):
    G, Dh, nq = NSA_KV_GROUPS, NSA_HEAD_DIM, S // Q_BLOCK
    blocks = min(blocks, nq)
    ts = blocks * Q_BLOCK
    tok = lambda dtype: jax.ShapeDtypeStruct((B, G, S, Dh), dtype)
    tok_spec = pl.BlockSpec((None, G, ts, Dh), lambda b, i: (b, 0, i, 0))
    tr = jax.ShapeDtypeStruct((B, G, V_ROWS, S), BF16)
    tr_spec = pl.BlockSpec((None, G, V_ROWS, ts), lambda b, i: (b, 0, 0, i))
    return pl.pallas_call(
        functools.partial(_nsa_prep_body, blocks=blocks),
        grid=(B, nq // blocks),
        in_specs=[
            pl.BlockSpec((None, ts, NSA_WIDTH), lambda b, i: (b, i, 0)),
            pl.BlockSpec((None, blocks, 1, Q_BLOCK), lambda b, i: (b, i, 0, 0)),
            pl.BlockSpec(inv_freq.shape, lambda b, i: (0, 0)),
            pl.BlockSpec(rope_spread.shape, lambda b, i: (0, 0)),
            pl.BlockSpec(rope_swap.shape, lambda b, i: (0, 0)),
        ],
        out_specs=[
            pl.BlockSpec((None, G, Dh, NSA_HEADS_PER_GROUP * ts), lambda b, i: (b, 0, 0, i)),
            tok_spec, tok_spec, tok_spec, tok_spec, tr_spec, tr_spec,
            pl.BlockSpec((None, blocks, GATE_ROWS, Q_BLOCK), lambda b, i: (b, i, 0, 0)),
        ],
        out_shape=[
            jax.ShapeDtypeStruct((B, G, Dh, NSA_HEADS_PER_GROUP * S), BF16),
            tok(F32), tok(F32), tok(BF16), tok(BF16), tr, tr,
            jax.ShapeDtypeStruct((B, nq, GATE_ROWS, Q_BLOCK), F32),
        ],
        compiler_params=pltpu.CompilerParams(dimension_semantics=("parallel", "parallel")),
        name="nsa_prep",
    )(nsa_proj, positions, inv_freq, rope_spread, rope_swap)


def _cmp_body(x_ref, w_ref, p_ref, o_ref, *, n_blk, transpose_out):
    half = CMP_BLOCK // 2
    first = jnp.zeros((n_blk, NSA_HEAD_DIM), F32)
    second = jnp.zeros((n_blk, NSA_HEAD_DIM), F32)
    for l in range(half):
        rows = x_ref[pl.ds(l, n_blk, stride=CMP_STRIDE), :]
        first = first + _dot((rows + p_ref[l]).astype(BF16), w_ref[l])
        second = second + _dot((rows + p_ref[half + l]).astype(BF16), w_ref[half + l])
    out = first + jnp.concatenate([second[1:], jnp.zeros((1, NSA_HEAD_DIM), F32)], axis=0)
    o_ref[...] = (out.T if transpose_out else out).astype(o_ref.dtype)


def _compress(x, w, pos, transpose_out):
    B, G, S, Dh = x.shape
    n_blk = S // CMP_STRIDE
    out_block = (None, None, Dh, n_blk) if transpose_out else (None, None, n_blk, Dh)
    out_shape = (B, G, Dh, n_blk) if transpose_out else (B, G, n_blk, Dh)
    return pl.pallas_call(
        functools.partial(_cmp_body, n_blk=n_blk, transpose_out=transpose_out),
        grid=(B, G),
        in_specs=[
            pl.BlockSpec((None, None, S, Dh), lambda b, g: (b, g, 0, 0)),
            pl.BlockSpec((CMP_BLOCK, Dh, Dh), lambda b, g: (0, 0, 0)),
            pl.BlockSpec((CMP_BLOCK, 1, Dh), lambda b, g: (0, 0, 0)),
        ],
        out_specs=pl.BlockSpec(out_block, lambda b, g: (b, g, 0, 0)),
        out_shape=jax.ShapeDtypeStruct(out_shape, BF16),
        compiler_params=pltpu.CompilerParams(
            dimension_semantics=("parallel", "parallel"), vmem_limit_bytes=VMEM_LIMIT),
        name="nsa_compress",
    )(x, w, pos)


def _nsa_body(*refs, blocks, **tiles):
    def one_block(qi, carry):
        _nsa_block(qi, *refs, blocks=blocks, **tiles)
        return carry

    lax.fori_loop(0, blocks, one_block, 0)


def _nsa_block(qi, q_ref, kcc_ref, vcct_ref, ks_ref, vst_ref, kw_ref, vwt_ref, gt_ref, o_ref,
               imp_scr, sel_scr, ocmp_scr, wins_scr, owin_scr, oslc_scr, *, blocks, kt_size, sub_size, cmp_chunk):
    grp = pl.program_id(1)
    t0 = (pl.program_id(2) * blocks + qi) * Q_BLOCK
    hpg = NSA_HEADS_PER_GROUP
    n_cmp = kcc_ref.shape[0]
    n_sel = sel_scr.shape[0]
    q_t = q_ref[:, pl.ds(pl.multiple_of(qi * (hpg * Q_BLOCK), hpg * Q_BLOCK), hpg * Q_BLOCK)]
    tq1 = t0 + lax.broadcasted_iota(jnp.int32, (1, Q_BLOCK), 1)
    heads = lambda x: jnp.concatenate([x] * hpg, axis=1)

    width = hpg * Q_BLOCK
    neg_inf = -jnp.inf
    eye = (lax.broadcasted_iota(jnp.int32, (Q_BLOCK, Q_BLOCK), 0)
           == lax.broadcasted_iota(jnp.int32, (Q_BLOCK, Q_BLOCK), 1)).astype(BF16)
    q_aug_t = jnp.concatenate([jnp.concatenate([eye] * hpg, axis=1), q_t], axis=0)

    ch = cmp_chunk
    n_chunks = (((t0 + Q_BLOCK - CMP_BLOCK) >> (CMP_STRIDE.bit_length() - 1)) + ch) // ch

    def compressed(n_sub):
        imp_scr[0:IMP_PAD, :] = jnp.zeros((IMP_PAD, Q_BLOCK), F32)
        imp_scr[IMP_PAD + n_sub * ch:, :] = jnp.zeros((n_cmp + IMP_PAD - n_sub * ch, Q_BLOCK), F32)
        biases = []
        for sub in range(n_sub):
            if sub >= n_sub - 2:
                cmp_end = (sub * ch + lax.broadcasted_iota(jnp.int32, (ch, 1), 0)) * CMP_STRIDE + (CMP_BLOCK - 1)
                biases.append(jnp.where(cmp_end <= tq1, 0.0, MASKED).astype(BF16))
            else:
                biases.append(jnp.zeros((ch, Q_BLOCK), BF16))
        k_aug = jnp.concatenate([jnp.concatenate(biases, axis=0), kcc_ref[0:n_sub * ch, :]], axis=1)
        scores = _dot(k_aug, q_aug_t)
        parts = []
        for sub in range(n_sub):
            s = scores[sub * ch:(sub + 1) * ch]
            m_sub = jnp.max(s, axis=0, keepdims=True)
            e = jnp.exp2(s - m_sub)
            parts.append((m_sub, jnp.sum(e, axis=0, keepdims=True),
                          _dot(vcct_ref[:, sub * ch:(sub + 1) * ch], e.astype(BF16)), e))
        m = parts[0][0]
        for m_sub, _, _, _ in parts[1:]:
            m = jnp.maximum(m, m_sub)
        scales = [jnp.exp2(m_sub - m) for m_sub, _, _, _ in parts]
        l = scales[0] * parts[0][1]
        acc = scales[0] * parts[0][2]
        for scale, (_, l_sub, acc_sub, _) in zip(scales[1:], parts[1:]):
            l = l + scale * l_sub
            acc = acc + scale * acc_sub
        inv_l = jnp.where(heads(tq1 >= CMP_BLOCK - 1), 1.0 / l, 0.0)
        ocmp_scr[...] = acc * inv_l
        for sub in range(n_sub):
            pc = parts[sub][3] * (scales[sub] * inv_l)
            p_sum = pc[:, :Q_BLOCK]
            for hp in range(1, hpg):
                p_sum = p_sum + pc[:, hp * Q_BLOCK:(hp + 1) * Q_BLOCK]
            imp_scr[IMP_PAD + sub * ch:IMP_PAD + (sub + 1) * ch, :] = p_sum

    for variant in range(n_cmp // ch):
        pl.when(n_chunks == variant + 1)(functools.partial(compressed, variant + 1))
    o_cmp = ocmp_scr[...]

    ratio = SEL_BLOCK // CMP_STRIDE
    sel_shift = SEL_BLOCK.bit_length() - 1
    cur = tq1 >> sel_shift
    n_forced = 3

    span = WINDOW + Q_BLOCK
    k_start = pl.multiple_of(jnp.maximum(t0 - WINDOW, 0), Q_BLOCK)

    def window_scores():
        diff = tq1 - (k_start + lax.broadcasted_iota(jnp.int32, (span, 1), 0))
        inside = pltpu.bitcast(diff, jnp.uint32) < jnp.uint32(WINDOW)
        bias = jnp.where(inside, 0.0, MASKED).astype(BF16)
        wins_scr[...] = _dot(jnp.concatenate([bias, kw_ref[pl.ds(k_start, span), :]], axis=1), q_aug_t)

    def window_values():
        v_t = vwt_ref[:, pl.ds(k_start, span)]
        for half in range(2):
            cols = slice(half * (width // 2), (half + 1) * (width // 2))
            sw = wins_scr[:, cols]
            pw = jnp.exp2(sw - jnp.max(sw, axis=0, keepdims=True)).astype(BF16)
            acc_win = _dot(v_t, pw)
            owin_scr[:, cols] = acc_win[:NSA_HEAD_DIM] / acc_win[NSA_HEAD_DIM:NSA_HEAD_DIM + 1]

    def select_blocks(rows):
        window_scores()
        imp = imp_scr[pl.ds(IMP_PAD - 1, rows, stride=ratio), :]
        for k in range(1, ratio + 1):
            imp = imp + imp_scr[pl.ds(IMP_PAD - 1 + k, rows, stride=ratio), :]
        j = lax.broadcasted_iota(jnp.int32, (rows, Q_BLOCK), 0)
        j_f = j.astype(F32)
        forced = (j == 0) | (j == cur) | (j == cur - 1)
        valid = j <= cur

        def pick(_, sc):
            best = jnp.max(sc, axis=0, keepdims=True)
            idx = jnp.min(jnp.where(sc == best, j_f, float(rows)), axis=0, keepdims=True)
            return jnp.where(j_f == idx, neg_inf, sc)

        left = lax.fori_loop(0, max(min(SEL_TOPK, n_sel) - n_forced, 0), pick,
                             jnp.where(valid & ~forced, imp, neg_inf), unroll=True)
        sel_scr[0:rows, :] = jnp.where(forced | (valid & (left == neg_inf)), 0.0, MASKED)
        if rows < n_sel:
            sel_scr[rows:n_sel, :] = jnp.full((n_sel - rows, Q_BLOCK), MASKED, F32)

    row_step = min(TOPK_ROW_STEP, n_sel)
    last_block = (t0 + Q_BLOCK - 1) >> sel_shift
    for variant in range(n_sel // row_step):
        pl.when(last_block // row_step == variant)(functools.partial(select_blocks, (variant + 1) * row_step))

    blocks_per_tile = kt_size // SEL_BLOCK

    def slc_tile(kt, carry, n_sub, causal_from, after_scores=None):
        m_run, acc = carry
        base = pl.multiple_of(kt * kt_size, kt_size)
        biases = []
        for sub in range(n_sub):
            blk0 = kt * blocks_per_tile + sub * (sub_size // SEL_BLOCK)
            bias = jnp.concatenate(
                [jnp.broadcast_to(sel_scr[pl.ds(blk0 + jb, 1), :], (SEL_BLOCK, Q_BLOCK))
                 for jb in range(sub_size // SEL_BLOCK)], axis=0)
            if sub >= causal_from:
                kpos = base + sub * sub_size + lax.broadcasted_iota(jnp.int32, (sub_size, 1), 0)
                bias = jnp.where(kpos <= tq1, bias, MASKED)
            biases.append(bias.astype(BF16))
        k_aug = jnp.concatenate([jnp.concatenate(biases, axis=0), ks_ref[pl.ds(base, n_sub * sub_size), :]], axis=1)
        scores = _dot(k_aug, q_aug_t)
        if after_scores is not None:
            after_scores()
        parts = []
        for sub in range(n_sub):
            k0 = pl.multiple_of(base + sub * sub_size, sub_size)
            v_t = vst_ref[:, pl.ds(k0, sub_size)]
            m_halves, acc_halves = [], []
            for half in range(2):
                cols = slice(half * (width // 2), (half + 1) * (width // 2))
                sc = scores[sub * sub_size:(sub + 1) * sub_size, cols]
                m_half = jnp.max(sc, axis=0, keepdims=True)
                m_halves.append(m_half)
                acc_halves.append(_dot(v_t, jnp.exp2(sc - m_half).astype(BF16)))
            parts.append((jnp.concatenate(m_halves, axis=1), jnp.concatenate(acc_halves, axis=1)))
        m_new = m_run
        for m_sub, _ in parts:
            m_new = jnp.maximum(m_new, m_sub)
        acc = jnp.exp2(m_run - m_new) * acc
        for m_sub, acc_sub in parts:
            acc = acc + jnp.exp2(m_sub - m_new) * acc_sub
        return m_new, acc

    subs_per_tile = kt_size // sub_size
    n_full = t0 // kt_size
    carry = lax.fori_loop(
        0, n_full, functools.partial(slc_tile, n_sub=subs_per_tile, causal_from=subs_per_tile),
        (jnp.full((1, width), MASKED, F32), jnp.zeros((V_ROWS, width), F32)))
    tail_step = min(2, subs_per_tile)
    tail_len = (t0 - n_full * kt_size) // (tail_step * sub_size)
    for variant in range(subs_per_tile // tail_step):
        n_sub = tail_step * (variant + 1)

        @pl.when(tail_len == variant)
        def _(n_sub=n_sub):
            _, acc_slc = slc_tile(n_full, carry, n_sub=n_sub, causal_from=n_sub - tail_step,
                                  after_scores=window_values)
            oslc_scr[...] = acc_slc[:NSA_HEAD_DIM] / acc_slc[NSA_HEAD_DIM:NSA_HEAD_DIM + 1]

    o_slc = oslc_scr[...]
    o_win = owin_scr[...]

    for hp in range(hpg):
        cols = slice(hp * Q_BLOCK, (hp + 1) * Q_BLOCK)
        row = (grp * hpg + hp) * 3
        o_t = (gt_ref[qi, pl.ds(row, 1), :] * o_cmp[:, cols] + gt_ref[qi, pl.ds(row + 1, 1), :] * o_slc[:, cols]
               + gt_ref[qi, pl.ds(row + 2, 1), :] * o_win[:, cols])
        o_ref[pl.ds(pl.multiple_of(qi * Q_BLOCK, Q_BLOCK), Q_BLOCK), hp * NSA_HEAD_DIM:(hp + 1) * NSA_HEAD_DIM] = (
            o_t.T.astype(o_ref.dtype))


def _nsa_attention(qs, kcc, vcct, ks, vst, kw, vwt, gt, B, S, blocks=4, kt_size=4096, sub_size=256, cmp_chunk=256):
    G, Dh, nq, hpg = NSA_KV_GROUPS, NSA_HEAD_DIM, S // Q_BLOCK, NSA_HEADS_PER_GROUP
    kt_size = min(kt_size, S)
    blocks = min(blocks, nq)
    n_cmp, n_sel = S // CMP_STRIDE, S // SEL_BLOCK
    cmp_chunk = min(cmp_chunk, n_cmp)
    per_group = lambda shape: pl.BlockSpec((None, None) + shape, lambda b, g, i: (b, g, 0, 0))
    return pl.pallas_call(
        functools.partial(_nsa_body, blocks=blocks, kt_size=kt_size, sub_size=min(sub_size, kt_size),
                          cmp_chunk=cmp_chunk),
        grid=(B, G, nq // blocks),
        in_specs=[
            pl.BlockSpec((None, None, Dh, blocks * hpg * Q_BLOCK), lambda b, g, i: (b, g, 0, i)),
            per_group((n_cmp, Dh)), per_group((Dh, n_cmp)),
            per_group((S, Dh)), per_group((V_ROWS, S)),
            per_group((S, Dh)), per_group((V_ROWS, S)),
            pl.BlockSpec((None, blocks, GATE_ROWS, Q_BLOCK), lambda b, g, i: (b, i, 0, 0)),
        ],
        out_specs=pl.BlockSpec((None, blocks * Q_BLOCK, hpg * Dh), lambda b, g, i: (b, i, g)),
        out_shape=jax.ShapeDtypeStruct((B, S, NSA_HEADS * Dh), BF16),
        scratch_shapes=[
            pltpu.VMEM((n_cmp + 2 * IMP_PAD, Q_BLOCK), F32),
            pltpu.VMEM((n_sel, Q_BLOCK), F32),
            pltpu.VMEM((Dh, hpg * Q_BLOCK), F32),
            pltpu.VMEM((WINDOW + Q_BLOCK, hpg * Q_BLOCK), F32),
            pltpu.VMEM((Dh, hpg * Q_BLOCK), F32),
            pltpu.VMEM((Dh, hpg * Q_BLOCK), F32),
        ],
        compiler_params=pltpu.CompilerParams(
            dimension_semantics=("parallel", "parallel", "arbitrary"), vmem_limit_bytes=VMEM_LIMIT),
        name="nsa_attention",
    )(qs, kcc, vcct, ks, vst, kw, vwt, gt)


def _merge_body(u_ref, halo_ref, pw_ref, ps_ref, yb_ref, yc_ref, ga_ref, gb_ref, gc_ref, h_ref, wb_ref, wo_ref, o_ref,
                *, tm, tiles_per_seq):
    tile_in_seq = lax.rem(pl.program_id(0), tiles_per_seq)
    halo = jnp.where(tile_in_seq == 0, 0.0, halo_ref[...].astype(F32))
    t = tile_in_seq * tm + lax.broadcasted_iota(jnp.int32, (tm, 1), 0)
    ya = _pool_tile(u_ref[...].astype(F32), halo, t, pw_ref, ps_ref)
    merged = jax.nn.sigmoid(ga_ref[...].astype(F32)) * _dot(ya, wb_ref[0])
    merged = merged + jax.nn.sigmoid(gb_ref[...].astype(F32)) * _dot(yb_ref[...], wb_ref[1])
    merged = merged + jax.nn.sigmoid(gc_ref[...].astype(F32)) * _dot(yc_ref[...], wb_ref[2])
    o_ref[...] = h_ref[...] + _dot(merged.astype(BF16), wo_ref[...])


def _merge(main, pool_w, pool_scale, yb, yc, h, w_branch, w_out, S, tm=1024):
    T, D = h.shape
    tm = min(tm, S)
    halo_blocks = tm // POOL_HALO
    y_spec = pl.BlockSpec((tm, 512), lambda i: (i, 0))
    gate_spec = lambda k: pl.BlockSpec((tm, D), lambda i: (i, MAIN_GM // D + k))
    return pl.pallas_call(
        functools.partial(_merge_body, tm=tm, tiles_per_seq=S // tm),
        grid=(T // tm,),
        in_specs=[
            pl.BlockSpec((tm, 512), lambda i: (i, MAIN_U // 512)),
            pl.BlockSpec((POOL_HALO, 512), lambda i: (jnp.maximum(i * halo_blocks - 1, 0), MAIN_U // 512)),
            pl.BlockSpec((len(POOL_WINDOWS), POOL_GROUP_DIM, POOL_GROUP_DIM), lambda i: (0, 0, 0)),
            pl.BlockSpec((1, 512), lambda i: (0, 0)),
            y_spec, y_spec, gate_spec(0), gate_spec(1), gate_spec(2),
            pl.BlockSpec((tm, D), lambda i: (i, 0)),
            pl.BlockSpec((3, 512, D), lambda i: (0, 0, 0)),
            pl.BlockSpec((D, D), lambda i: (0, 0)),
        ],
        out_specs=pl.BlockSpec((tm, D), lambda i: (i, 0)),
        out_shape=jax.ShapeDtypeStruct((T, D), F32),
        compiler_params=pltpu.CompilerParams(dimension_semantics=("parallel",), vmem_limit_bytes=VMEM_LIMIT),
        name="merge_out",
    )(main, main, pool_w, pool_scale, yb, yc, main, main, main, h, w_branch, w_out)


def _ffn_body(h_ref, gf_ref, w1_ref, w2_ref, gp_ref, wg_ref, p_ref, wp_ref, gl_ref, o_ref, f_scr, acc_scr, *, final_norm):
    j = pl.program_id(1)

    @pl.when(j == 0)
    def _():
        f_scr[...] = _rms(h_ref[...], gf_ref[...]).astype(BF16)
        acc_scr[...] = jnp.zeros_like(acc_scr)

    a = jnp.maximum(_dot(f_scr[...], w1_ref[...]), 0.0)
    acc_scr[...] += _dot((a * a).astype(BF16), w2_ref[...])

    @pl.when(j == pl.num_programs(1) - 1)
    def _():
        h2 = h_ref[...] + acc_scr[...]
        gate = jax.nn.sigmoid(_dot(_rms(h2, gp_ref[...]).astype(BF16), wg_ref[...]))
        h3 = h2 + gate * _dot(p_ref[...].astype(BF16), wp_ref[...])
        o_ref[...] = _rms(h3, gl_ref[...]) if final_norm else h3


def _ffn_ple(h, norm_ffn, w1, w2, norm_ple, w_gate, p, w_proj, norm_last, final_norm, tm=1024, tf=1024):
    T, D = h.shape
    tm = min(tm, T)
    Fdim = w1.shape[1]
    Pdim = p.shape[1]
    vec = pl.BlockSpec((1, D), lambda i, j: (0, 0))
    return pl.pallas_call(
        functools.partial(_ffn_body, final_norm=final_norm),
        grid=(T // tm, Fdim // tf),
        in_specs=[
            pl.BlockSpec((tm, D), lambda i, j: (i, 0)), vec,
            pl.BlockSpec((D, tf), lambda i, j: (0, j)),
            pl.BlockSpec((tf, D), lambda i, j: (j, 0)),
            vec,
            pl.BlockSpec((D, D), lambda i, j: (0, 0)),
            pl.BlockSpec((tm, Pdim), lambda i, j: (i, 0)),
            pl.BlockSpec((Pdim, D), lambda i, j: (0, 0)),
            vec,
        ],
        out_specs=pl.BlockSpec((tm, D), lambda i, j: (i, 0)),
        out_shape=jax.ShapeDtypeStruct((T, D), F32),
        scratch_shapes=[pltpu.VMEM((tm, D), BF16), pltpu.VMEM((tm, D), F32)],
        compiler_params=pltpu.CompilerParams(
            dimension_semantics=("parallel", "arbitrary"), vmem_limit_bytes=VMEM_LIMIT),
        name="ffn_ple",
    )(h, norm_ffn, w1, w2, norm_ple, w_gate, p, w_proj, norm_last)


def _split_w_in(w):
    parts, off = [], 0
    for sz in IN_SIZES:
        parts.append(w[:, off:off + sz])
        off += sz
    u, q_nsa, kv_nsa, g_nsa, q_gla, k_gla, v_gla, a_gla, r_gla, g_merge = parts
    D = w.shape[0]
    pad = lambda n: jnp.zeros((D, n), w.dtype)
    main = jnp.concatenate([u, q_gla, k_gla, v_gla, r_gla, g_merge, a_gla, pad(MAIN_WIDTH - MAIN_A - GLA_GATE_RANK)], axis=1)
    kv = [kv_nsa[:, s * LANES:(s + 1) * LANES] for s in range(6)]
    nsa = jnp.concatenate([q_nsa, kv[0], kv[2], kv[4], kv[1], kv[3], kv[5], g_nsa, pad(NSA_WIDTH - NSA_G - 24)], axis=1)
    return main.astype(BF16), nsa.astype(BF16)


def kernel(x, p, positions, norm_mix, w_in, pool_w, pool_scale, cmp_pos_k, cmp_w_k, cmp_pos_v, cmp_w_v, gla_w_gate, gla_b_gate, gla_norm, w_branch, w_out, norm_ffn, w_ff1, w_ff2, norm_ple, w_ple_gate, w_ple_proj, norm_final):
    B, S, D = x.shape
    depth = w_in.shape[0]
    T = B * S
    Dh = NSA_HEAD_DIM
    row = lambda v: v.reshape(1, -1).astype(F32)

    half = NSA_ROT_DIM // 2
    inv_freq = jnp.power(ROPE_THETA, -jnp.arange(half, dtype=F32) * (2.0 / NSA_ROT_DIM)).reshape(half, 1)
    pos_rows = positions.reshape(B, S // Q_BLOCK, 1, Q_BLOCK)
    rope_spread, rope_swap = _rope_spread_matrix(), _rope_swap_matrix()

    h = x.reshape(T, D)
    for i in range(depth):
        w_main, w_nsa = _split_w_in(w_in[i])
        main = _norm_proj(h, row(norm_mix[i]), w_main, BF16, tn=MAIN_WIDTH // 3).reshape(B, S, MAIN_WIDTH)
        nsa_proj = _norm_proj(h, row(norm_mix[i]), w_nsa, F32, tn=NSA_WIDTH).reshape(B, S, NSA_WIDTH)

        wg = jnp.zeros((LANES, GLA_HEADS * GLA_DK), BF16).at[:GLA_GATE_RANK].set(gla_w_gate[i].astype(BF16))
        y_c = _gla_mixer(main, wg, row(gla_b_gate[i]), row(gla_norm[i]), B, S)

        qs, kc, vc, ks, kw, vst, vwt, gt = _nsa_prep(nsa_proj, pos_rows, inv_freq, rope_spread, rope_swap, B, S)
        kcc = _compress(kc, cmp_w_k[i].reshape(CMP_BLOCK, Dh, Dh).astype(BF16),
                        cmp_pos_k[i].reshape(CMP_BLOCK, 1, Dh).astype(F32), transpose_out=False)
        vcct = _compress(vc, cmp_w_v[i].reshape(CMP_BLOCK, Dh, Dh).astype(BF16),
                         cmp_pos_v[i].reshape(CMP_BLOCK, 1, Dh).astype(F32), transpose_out=True)
        y_b = _nsa_attention(qs, kcc, vcct, ks, vst, kw, vwt, gt, B, S)

        h = _merge(main.reshape(T, MAIN_WIDTH), pool_w[i].astype(BF16), row(pool_scale[i]), y_b.reshape(T, -1),
                   y_c.reshape(T, -1), h, w_branch[i].astype(BF16), w_out[i].astype(BF16), S)
        h = _ffn_ple(h, row(norm_ffn[i]), w_ff1[i].astype(BF16), w_ff2[i].astype(BF16), row(norm_ple[i]),
                     w_ple_gate[i].astype(BF16), p[i].reshape(T, -1), w_ple_proj[i].astype(BF16),
                     row(norm_final), final_norm=(i == depth - 1))
    return h.reshape(B, S, D)
```
